```python
import jax, jax.numpy as jnp
from jax import lax
import numpy as np

D_MODEL = 1024
BATCH = 2
SEQ = 8192
DEPTH = 1

CHUNK = 64
MIX_WIDTH = D_MODEL
SGU_WIDTH = MIX_WIDTH // 2
SGU_HEADS = 8
SGU_HEAD_DIM = SGU_WIDTH // SGU_HEADS
SGU_BLOCK = 128
RWKV_WIDTH = MIX_WIDTH - SGU_WIDTH
RWKV_HEAD_DIM = 64
RWKV_HEADS = RWKV_WIDTH // RWKV_HEAD_DIM
DECAY_LORA = 64
ICLR_LORA = 64
GATE_LORA = 128
LNX_EPS = 64e-5
IN_WIDTH = 2 * SGU_WIDTH + 3 * RWKV_WIDTH
N_EXPERTS = 32
TOP_K = 4
D_EXPERT = D_MODEL
SWIGLU_ALPHA = 1.702
SWIGLU_LIMIT = 7.0
MOE_BLOCK = 128
RMS_EPS = 1e-5
LN_EPS = 1e-5

kernel_name = "hymba_sgu_rwkv7_moe_block"


def rms_norm(x, g):
    xf = x.astype(jnp.float32)
    y = xf * lax.rsqrt(jnp.mean(xf * xf, axis=-1, keepdims=True) + RMS_EPS)
    return (y * g.astype(jnp.float32)).astype(x.dtype)


def token_shift(z):
    return jnp.pad(z, ((0, 0), (1, 0), (0, 0)))[:, :-1]


def shift_mix(z, mu):
    return z + (token_shift(z) - z) * mu


def chunk_causal_mask(n):
    c = jnp.arange(n) // CHUNK
    return c[None, :] <= c[:, None]


def spatial_gating(z, ln_g, ln_b, w_s, b_s):
    b, s, _ = z.shape
    u, v = jnp.split(z, 2, axis=-1)
    vf = v.astype(jnp.float32)
    mu = jnp.mean(vf, axis=-1, keepdims=True)
    var = jnp.mean(jnp.square(vf - mu), axis=-1, keepdims=True)
    v = ((vf - mu) * lax.rsqrt(var + LN_EPS) * ln_g.astype(jnp.float32)
         + ln_b.astype(jnp.float32)).astype(z.dtype)
    v = v.reshape(b, s // SGU_BLOCK, SGU_BLOCK, SGU_HEADS, SGU_HEAD_DIM)
    w = jnp.where(chunk_causal_mask(SGU_BLOCK)[None], w_s, jnp.zeros_like(w_s))
    sv = jnp.einsum('hij,bnjhc->bnihc', w, v) + jnp.transpose(b_s)[None, None, :, :, None]
    return u * sv.reshape(b, s, SGU_WIDTH).astype(u.dtype)


def rwkv7_time_mix(hn, p_r, p_k, p_v, mu_rkv, mu_wag, w0, w1, w2, a0, a1, a2,
                   g1, g2, k_k, k_a, r_k, lnx_g, lnx_b):
    f32 = jnp.float32
    b, s, _ = hn.shape
    H, N = RWKV_HEADS, RWKV_HEAD_DIM
    hd = (b, s, H, N)
    r = shift_mix(p_r, mu_rkv[0]).astype(f32).reshape(hd)
    k = shift_mix(p_k, mu_rkv[1]).astype(f32).reshape(hd)
    v = shift_mix(p_v, mu_rkv[2]).astype(f32).reshape(hd)
    xw = shift_mix(hn, mu_wag[0])
    xa = shift_mix(hn, mu_wag[1])
    xg = shift_mix(hn, mu_wag[2])
    w = -jax.nn.softplus(-(w0 + jnp.tanh(xw @ w1) @ w2).astype(f32)) - 0.5
    decay = jnp.exp(-jnp.exp(w)).reshape(hd)
    iclr = jax.nn.sigmoid((a0 + (xa @ a1) @ a2).astype(f32)).reshape(hd)
    gate = (jax.nn.sigmoid(xg @ g1) @ g2).astype(f32)
    kk = k * k_k.astype(f32).reshape(H, N)
    kk = kk / jnp.maximum(jnp.sqrt(jnp.sum(kk * kk, axis=-1, keepdims=True)), 1e-12)
    k = k * (1.0 + (iclr - 1.0) * k_a.astype(f32).reshape(H, N))

    def step(state, inp):
        r_t, w_t, k_t, v_t, a_t, b_t = inp
        sa = jnp.einsum('bhvk,bhk->bhv', state, a_t)
        state = (state * w_t[:, :, None, :] + sa[..., None] * b_t[:, :, None, :]
                 + v_t[..., None] * k_t[:, :, None, :])
        return state, jnp.einsum('bhvk,bhk->bhv', state, r_t)

    seq_first = lambda z: jnp.moveaxis(z, 1, 0)
    xs = (seq_first(r), seq_first(decay), seq_first(k), seq_first(v),
          seq_first(-kk), seq_first(kk * iclr))
    state0 = jnp.zeros((b, H, N, N), f32)
    _, y = lax.scan(step, state0, xs)
    y = jnp.moveaxis(y, 0, 1)
    mu = jnp.mean(y, axis=-1, keepdims=True)
    var = jnp.mean(jnp.square(y - mu), axis=-1, keepdims=True)
    y = ((y - mu) * lax.rsqrt(var + LNX_EPS) * lnx_g.astype(f32).reshape(H, N)
         + lnx_b.astype(f32).reshape(H, N))
    y = y + jnp.sum(r * k * r_k.astype(f32), axis=-1, keepdims=True) * v
    return (y.reshape(b, s, RWKV_WIDTH) * gate).astype(hn.dtype)


def clamped_swiglu(hid):
    x_glu, x_lin = hid[..., ::2], hid[..., 1::2]
    x_glu = jnp.minimum(x_glu, SWIGLU_LIMIT)
    x_lin = jnp.clip(x_lin, -SWIGLU_LIMIT, SWIGLU_LIMIT)
    return x_glu * jax.nn.sigmoid(SWIGLU_ALPHA * x_glu) * (x_lin + 1.0)


def moe_ffn(h, router_w, router_b, w1, b1, w2, b2):
    b, s, d = h.shape
    t = b * s
    hf = h.reshape(t, d)
    logits = (hf @ router_w + router_b).astype(jnp.float32)
    top_val, top_idx = lax.top_k(logits, TOP_K)
    gates = jax.nn.softmax(top_val, axis=-1)
    n_assign = t * TOP_K
    e_flat = top_idx.reshape(-1).astype(jnp.int32)
    tok_flat = (jnp.arange(n_assign, dtype=jnp.int32) // TOP_K)
    gate_flat = gates.reshape(-1)
    order = jnp.argsort(e_flat)
    e_sorted = e_flat[order]
    counts = jnp.bincount(e_flat, length=N_EXPERTS).astype(jnp.int32)
    padded = (counts + MOE_BLOCK - 1) // MOE_BLOCK * MOE_BLOCK
    start = jnp.cumsum(counts) - counts
    pad_end = jnp.cumsum(padded)
    pad_start = pad_end - padded
    dest = pad_start[e_sorted] + jnp.arange(n_assign, dtype=jnp.int32) - start[e_sorted]
    n_blocks = -(-n_assign // MOE_BLOCK) + N_EXPERTS
    n_rows = n_blocks * MOE_BLOCK
    row_tok = jnp.full((n_rows,), t, jnp.int32).at[dest].set(tok_flat[order])
    row_gate = jnp.zeros((n_rows,), jnp.float32).at[dest].set(gate_flat[order])
    block_start = jnp.arange(n_blocks, dtype=jnp.int32) * MOE_BLOCK
    block_expert = jnp.minimum(jnp.searchsorted(pad_end, block_start, side='right'),
                               N_EXPERTS - 1).astype(jnp.int32)
    hpad = jnp.concatenate([hf, jnp.zeros((1, d), hf.dtype)], axis=0)
    xin = hpad[row_tok].reshape(n_blocks, MOE_BLOCK, d)

    def expert_block(args):
        xb, e = args
        hid = xb @ w1[e] + b1[e]
        return clamped_swiglu(hid) @ w2[e] + b2[e]

    yout = lax.map(expert_block, (xin, block_expert)).reshape(n_rows, d)
    y = jnp.zeros((t + 1, d), jnp.float32).at[row_tok].add(
        yout.astype(jnp.float32) * row_gate[:, None])
    return y[:t].reshape(b, s, d).astype(h.dtype)


def setup_inputs(seed: int = 0) -> dict:
    key = jax.random.key(seed)
    ks = iter(jax.random.split(key, 40))
    L, D = DEPTH, D_MODEL
    nrm = lambda shape, scale: jax.random.normal(next(ks), shape, jnp.float32) * scale
    gain = lambda shape: 1.0 + nrm(shape, 0.05)
    uni = lambda shape, lo, hi: jax.random.uniform(next(ks), shape, jnp.float32, lo, hi)
    return {
        "x": nrm((BATCH, SEQ, D), 1.0),
        "norm1_g": gain((L, D)),
        "w_in": nrm((L, D, IN_WIDTH), D ** -0.5),
        "sgu_ln_g": gain((L, SGU_WIDTH)),
        "sgu_ln_b": nrm((L, SGU_WIDTH), 0.05),
        "sgu_w": nrm((L, SGU_HEADS, SGU_BLOCK, SGU_BLOCK), SGU_BLOCK ** -0.5),
        "sgu_b": 1.0 + nrm((L, SGU_HEADS, SGU_BLOCK), 0.1),
        "mu_rkv": uni((L, 3, RWKV_WIDTH), 0.0, 1.0),
        "mu_wag": uni((L, 3, D), 0.0, 1.0),
        "decay_w0": uni((L, RWKV_WIDTH), -5.5, 1.0),
        "decay_w1": nrm((L, D, DECAY_LORA), D ** -0.5),
        "decay_w2": nrm((L, DECAY_LORA, RWKV_WIDTH), 0.5 * DECAY_LORA ** -0.5),
        "iclr_a0": nrm((L, RWKV_WIDTH), 0.1),
        "iclr_a1": nrm((L, D, ICLR_LORA), D ** -0.5),
        "iclr_a2": nrm((L, ICLR_LORA, RWKV_WIDTH), ICLR_LORA ** -0.5),
        "gate_g1": nrm((L, D, GATE_LORA), D ** -0.5),
        "gate_g2": nrm((L, GATE_LORA, RWKV_WIDTH), GATE_LORA ** -0.5),
        "k_k": 0.85 + nrm((L, RWKV_WIDTH), 0.05),
        "k_a": gain((L, RWKV_WIDTH)),
        "r_k": nrm((L, RWKV_HEADS, RWKV_HEAD_DIM), 0.1),
        "lnx_g": gain((L, RWKV_WIDTH)),
        "lnx_b": nrm((L, RWKV_WIDTH), 0.05),
        "w_out": nrm((L, MIX_WIDTH, D), MIX_WIDTH ** -0.5),
        "norm2_g": gain((L, D)),
        "router_w": nrm((L, D, N_EXPERTS), D ** -0.5),
        "router_b": nrm((L, N_EXPERTS), 0.01),
        "moe_w1": nrm((L, N_EXPERTS, D, 2 * D_EXPERT), D ** -0.5),
        "moe_b1": nrm((L, N_EXPERTS, 2 * D_EXPERT), 0.01),
        "moe_w2": nrm((L, N_EXPERTS, D_EXPERT, D), D_EXPERT ** -0.5),
        "moe_b2": nrm((L, N_EXPERTS, D), 0.01),
        "final_g": gain((D,)),
    }


def reference(x, norm1_g, w_in, sgu_ln_g, sgu_ln_b, sgu_w, sgu_b, mu_rkv, mu_wag,
              decay_w0, decay_w1, decay_w2, iclr_a0, iclr_a1, iclr_a2, gate_g1, gate_g2,
              k_k, k_a, r_k, lnx_g, lnx_b, w_out, norm2_g, router_w, router_b,
              moe_w1, moe_b1, moe_w2, moe_b2, final_g):
    split_at = [2 * SGU_WIDTH, 2 * SGU_WIDTH + RWKV_WIDTH, 2 * SGU_WIDTH + 2 * RWKV_WIDTH]
    h = x
    for l in range(DEPTH):
        hn = rms_norm(h, norm1_g[l])
        proj = hn @ w_in[l]
        z_a, p_r, p_k, p_v = jnp.split(proj, split_at, axis=-1)
        y_a = spatial_gating(jax.nn.gelu(z_a, approximate=False),
                             sgu_ln_g[l], sgu_ln_b[l], sgu_w[l], sgu_b[l])
        y_b = rwkv7_time_mix(hn, p_r, p_k, p_v, mu_rkv[l], mu_wag[l],
                             decay_w0[l], decay_w1[l], decay_w2[l],
                             iclr_a0[l], iclr_a1[l], iclr_a2[l], gate_g1[l], gate_g2[l],
                             k_k[l], k_a[l], r_k[l], lnx_g[l], lnx_b[l])
        h = h + jnp.concatenate([y_a, y_b], axis=-1) @ w_out[l]
        h = h + moe_ffn(rms_norm(h, norm2_g[l]), router_w[l], router_b[l],
                        moe_w1[l], moe_b1[l], moe_w2[l], moe_b2[l])
    return rms_norm(h, final_g)
```

```python
import functools

import jax
import jax.numpy as jnp
from jax import lax
from jax.experimental import pallas as pl
from jax.experimental.pallas import tpu as pltpu

F32 = jnp.float32
BF16 = jnp.bfloat16
I32 = jnp.int32

RMS_EPS = 1e-5
LN_EPS = 1e-5
LNX_EPS = 64e-5
CHUNK = 64
SGU_BLOCK = 128
HEAD = 64
PAIR = 2 * HEAD
N_EXPERTS = 32
TOP_K = 4
SWIGLU_ALPHA = 1.702
SWIGLU_LIMIT = 7.0

V7X_VMEM_LIMIT = 56 * 1024 * 1024

TM_PROJ = 512
TM_SGU = 512
TT_RWKV = 256
TR_ROUTE = 512
BM_MOE = 256
TD_DISPATCH = 256
TC_COMBINE = 256


def _dot(a, b):
    return jnp.dot(a, b, preferred_element_type=F32)


def _dot_nt(a, b):
    return lax.dot_general(a, b, (((1,), (1,)), ((), ())), preferred_element_type=F32)


def _split(x):
    hi = x.astype(BF16)
    lo = (x - hi.astype(F32)).astype(BF16)
    return hi, lo


def _dot3(a, b):
    ah, al = _split(a)
    bh, bl = _split(b)
    return _dot(ah, bh) + _dot(al, bh) + _dot(ah, bl)


def _dot3_nt(a, b):
    ah, al = _split(a)
    bh, bl = _split(b)
    return _dot_nt(ah, bh) + _dot_nt(al, bh) + _dot_nt(ah, bl)


def _head_sum(x, bd):
    hi, lo = _split(x)
    return _dot(hi, bd) + _dot(lo, bd)


def _params(sem, vmem=V7X_VMEM_LIMIT):
    return pltpu.CompilerParams(dimension_semantics=sem, vmem_limit_bytes=vmem)


def _in_proj_kernel(x_ref, g_ref, win_ref, wl_ref, mucat_ref, proj_ref, lora_ref, wl_scr):
    @pl.when(pl.program_id(0) == 0)
    def _():
        wl = wl_ref[...]
        mu = mucat_ref[...]
        n = wl.shape[1]
        wl_scr[:, 0:n] = (wl * (1.0 - mu)).astype(BF16)
        wl_scr[:, n:2 * n] = (wl * mu).astype(BF16)

    x = x_ref[...]
    hn = x * lax.rsqrt(jnp.mean(x * x, axis=-1, keepdims=True) + RMS_EPS) * g_ref[...]
    hb = hn.astype(BF16)
    proj_ref[...] = _dot(hb, win_ref[...])
    lora_ref[...] = _dot(hb, wl_scr[...])


def _in_proj(x2, g, win_bf, wl, mucat):
    t, d = x2.shape
    n_in = win_bf.shape[1]
    n_l = wl.shape[1]
    tm = TM_PROJ
    return pl.pallas_call(
        _in_proj_kernel,
        grid=(t // tm,),
        in_specs=[
            pl.BlockSpec((tm, d), lambda i: (i, 0)),
            pl.BlockSpec((1, d), lambda i: (0, 0)),
            pl.BlockSpec((d, n_in), lambda i: (0, 0)),
            pl.BlockSpec((d, n_l), lambda i: (0, 0)),
            pl.BlockSpec((d, n_l), lambda i: (0, 0)),
        ],
        out_specs=[
            pl.BlockSpec((tm, n_in), lambda i: (i, 0)),
            pl.BlockSpec((tm, 2 * n_l), lambda i: (i, 0)),
        ],
        out_shape=[
            jax.ShapeDtypeStruct((t, n_in), F32),
            jax.ShapeDtypeStruct((t, 2 * n_l), F32),
        ],
        scratch_shapes=[pltpu.VMEM((d, 2 * n_l), BF16)],
        compiler_params=_params(("arbitrary",)),
        name="in_proj",
    )(x2, g, win_bf, wl, mucat)


def _sgu_kernel(z_ref, lng_ref, lnb_ref, w_ref, bias_ref, o_ref, wm_scr):
    n_heads = w_ref.shape[0]

    @pl.when(pl.program_id(0) == 0)
    def _():
        qi = lax.broadcasted_iota(I32, (SGU_BLOCK, SGU_BLOCK), 0) // CHUNK
        kj = lax.broadcasted_iota(I32, (SGU_BLOCK, SGU_BLOCK), 1) // CHUNK
        for h in range(n_heads):
            wm_scr[h] = jnp.where(kj <= qi, w_ref[h], 0.0).astype(BF16)

    z = z_ref[...]
    tm, two_w = z.shape
    width = two_w // 2
    gz = 0.5 * z * (1.0 + lax.erf(z * (2.0 ** -0.5)))
    u = gz[:, :width]
    v = gz[:, width:]
    mu = jnp.mean(v, axis=-1, keepdims=True)
    vc = v - mu
    var = jnp.mean(vc * vc, axis=-1, keepdims=True)
    vn = vc * lax.rsqrt(var + LN_EPS) * lng_ref[...] + lnb_ref[...]
    lane = lax.broadcasted_iota(I32, (SGU_BLOCK, PAIR), 1)
    is_lo = lane < HEAD
    bias = bias_ref[...]
    for blk in range(tm // SGU_BLOCK):
        rows = slice(blk * SGU_BLOCK, (blk + 1) * SGU_BLOCK)
        for p in range(width // PAIR):
            cols = slice(p * PAIR, (p + 1) * PAIR)
            vp = vn[rows, cols]
            lo = jnp.where(is_lo, vp, 0.0).astype(BF16)
            hi = jnp.where(is_lo, 0.0, vp).astype(BF16)
            sv = _dot(wm_scr[2 * p], lo) + _dot(wm_scr[2 * p + 1], hi)
            o_ref[rows, cols] = u[rows, cols] * (sv + bias[:, cols])


def _sgu(proj, ln_g, ln_b, w_s, bias2d):
    t = proj.shape[0]
    width = ln_g.shape[1]
    n_heads = w_s.shape[0]
    tm = TM_SGU
    return pl.pallas_call(
        _sgu_kernel,
        grid=(t // tm,),
        in_specs=[
            pl.BlockSpec((tm, 2 * width), lambda i: (i, 0)),
            pl.BlockSpec((1, width), lambda i: (0, 0)),
            pl.BlockSpec((1, width), lambda i: (0, 0)),
            pl.BlockSpec((n_heads, SGU_BLOCK, SGU_BLOCK), lambda i: (0, 0, 0)),
            pl.BlockSpec((SGU_BLOCK, width), lambda i: (0, 0)),
        ],
        out_specs=pl.BlockSpec((tm, width), lambda i: (i, 0)),
        out_shape=jax.ShapeDtypeStruct((t, width), F32),
        scratch_shapes=[pltpu.VMEM((n_heads, SGU_BLOCK, SGU_BLOCK), BF16)],
        compiler_params=_params(("arbitrary",)),
        name="sgu",
    )(proj, ln_g, ln_b, w_s, bias2d)


def _rwkv_prep_kernel(seq_len, pr_ref, pk_ref, pv_ref, ppr_ref, ppk_ref, ppv_ref, lo_ref, plo_ref,
                      mu_ref, pvec_ref, w2_ref, a2_ref, g2_ref,
                      rp_ref, yq_ref, bonus_ref, gate_ref, g_ref, h_ref):
    tt = pr_ref.shape[0]
    n_chunks = tt // CHUNK
    i = pl.program_id(0)
    keep = jnp.where((i * tt) % seq_len == 0, 0.0, 1.0)

    def shift(x, prev):
        rolled = pltpu.roll(x, 1, 0)
        first = prev[7:8, :] * keep
        rowc = lax.broadcasted_iota(I32, x.shape, 0)
        return jnp.where(rowc == 0, first, rolled)

    mu = mu_ref[...]

    def shift_mix(ref, pref, m):
        x = ref[...]
        return x + (shift(x, pref[...]) - x) * m

    r = shift_mix(pr_ref, ppr_ref, mu[0:1])
    k = shift_mix(pk_ref, ppk_ref, mu[1:2])
    v = shift_mix(pv_ref, ppv_ref, mu[2:3])

    lo = lo_ref[...]
    plo = plo_ref[...]
    half = lo.shape[1] // 2
    l_all = lo[:, :half] + shift(lo[:, half:], plo[:, half:])
    l_wa = l_all[:, :PAIR]
    l_g = l_all[:, PAIR:]

    pvec = pvec_ref[...]
    w0, a0, k_k, k_a, r_k = pvec[0:1], pvec[1:2], pvec[2:3], pvec[3:4], pvec[4:5]

    dw = _dot(jnp.tanh(l_wa).astype(BF16), w2_ref[...].astype(BF16))
    ia = _dot(l_wa.astype(BF16), a2_ref[...].astype(BF16))
    gate_ref[...] = _dot(jax.nn.sigmoid(l_g).astype(BF16), g2_ref[...].astype(BF16))

    zneg = -(w0 + dw)
    softplus = jnp.maximum(zneg, 0.0) + jnp.log(1.0 + jnp.exp(-jnp.abs(zneg)))
    logw = -jnp.exp(-softplus - 0.5)
    iclr = jax.nn.sigmoid(a0 + ia)

    lane_r = lax.broadcasted_iota(I32, (PAIR, PAIR), 0)
    lane_c = lax.broadcasted_iota(I32, (PAIR, PAIR), 1)
    bd = jnp.where((lane_r // HEAD) == (lane_c // HEAD), 1.0, 0.0).astype(BF16)

    kk = k * k_k
    kk = kk / jnp.maximum(jnp.sqrt(_head_sum(kk * kk, bd)), 1e-12)
    k2 = k * (1.0 + (iclr - 1.0) * k_a)
    a = -kk
    b = kk * iclr
    bonus_ref[...] = _head_sum(r * k2 * r_k, bd) * v

    rc = lax.broadcasted_iota(I32, (tt, PAIR), 0) % CHUNK
    cl = logw
    s = 1
    while s < CHUNK:
        cl = cl + jnp.where(rc >= s, pltpu.roll(cl, s, 0), 0.0)
        s *= 2
    at_all = a * jnp.exp(cl - logw)
    rt_all = r * jnp.exp(cl)
    w_inv = jnp.exp(-cl)
    bt_all = b * w_inv
    kt_all = k2 * w_inv

    lane = lax.broadcasted_iota(I32, (CHUNK, PAIR), 1)
    is_lo = lane < HEAD

    def stack(x):
        return jnp.concatenate([jnp.where(is_lo, x, 0.0), jnp.where(is_lo, 0.0, x)], axis=0)

    n2 = 2 * PAIR
    ri = lax.broadcasted_iota(I32, (n2, n2), 0)
    ci = lax.broadcasted_iota(I32, (n2, n2), 1)
    same_head = ((ri // CHUNK) % 2) == ((ci // CHUNK) % 2)
    t_i = ri % CHUNK
    s_i = ci % CHUNK
    tri = same_head & (s_i < t_i + ri // PAIR)
    eye = lane_r == lane_c

    for c in range(n_chunks):
        rows = slice(c * CHUNK, (c + 1) * CHUNK)
        cl_c = cl[rows]
        last = cl_c[CHUNK - 1:CHUNK, :]
        w_rem = jnp.exp(last - cl_c)
        a_s = stack(at_all[rows])
        r_s = stack(rt_all[rows])
        b_s = stack(bt_all[rows])
        k_s = stack(kt_all[rows])
        v_s = stack(v[rows])
        bh_s = stack(b[rows] * w_rem)
        kh_s = stack(k2[rows] * w_rem)

        lhs = jnp.concatenate([a_s, r_s], axis=0).astype(BF16)
        rhs = jnp.concatenate([b_s, k_s], axis=0).astype(BF16)
        a_all = jnp.where(tri, _dot_nt(lhs, rhs), 0.0)
        n_k = a_all[:PAIR, :PAIR]
        a_ak = a_all[:PAIR, PAIR:].astype(BF16)
        a_rb = a_all[PAIR:, :PAIR].astype(BF16)
        a_rk = a_all[PAIR:, PAIR:].astype(BF16)
        v_sb = v_s.astype(BF16)

        x = jnp.concatenate([a_s, _dot(a_ak, v_sb)], axis=1)
        steps = CHUNK.bit_length() - 1
        for it in range(steps):
            nb = n_k.astype(BF16)
            x = x + _dot(nb, x.astype(BF16))
            if it + 1 < steps:
                n_k = _dot(nb, nb)
        xb = x.astype(BF16)
        ry = jnp.concatenate([r_s, _dot(a_rk, v_sb)], axis=1) + _dot(a_rb, xb)
        rp_ref[rows, :] = ry[:CHUNK, :PAIR] + ry[CHUNK:, :PAIR]
        yq_ref[rows, :] = ry[:CHUNK, PAIR:] + ry[CHUNK:, PAIR:]

        p_t = x[:, :PAIR].T.astype(BF16)
        g_ref[c, 0] = jnp.where(eye, jnp.exp(last), 0.0) + _dot(p_t, bh_s.astype(BF16))
        qv_t = jnp.concatenate([x[:, PAIR:], v_s], axis=0).T.astype(BF16)
        hbd = _dot(qv_t, jnp.concatenate([bh_s, kh_s], axis=0).astype(BF16))
        h_ref[c, 0] = hbd[:HEAD] + hbd[HEAD:]


def _rwkv_prep(proj, lora, mu_rkv, pvec, w2pad, a2pad, g2, seq_len, col0):
    t = proj.shape[0]
    width = mu_rkv.shape[1]
    n_pairs = width // PAIR
    tt = TT_RWKV
    nl = lora.shape[1]
    cb = col0 // PAIR
    wb = width // PAIR

    def prev_rows(i):
        return jnp.maximum(i * (tt // 8) - 1, 0)

    def tok(c):
        return pl.BlockSpec((tt, PAIR), lambda i, p: (i, c + p))

    def prev(c):
        return pl.BlockSpec((8, PAIR), lambda i, p: (prev_rows(i), c + p))

    out_tok = pl.BlockSpec((tt, PAIR), lambda i, p: (i, p))
    tok_shape = jax.ShapeDtypeStruct((t, width), F32)
    return pl.pallas_call(
        functools.partial(_rwkv_prep_kernel, seq_len),
        grid=(t // tt, n_pairs),
        in_specs=[
            tok(cb), tok(cb + wb), tok(cb + 2 * wb),
            prev(cb), prev(cb + wb), prev(cb + 2 * wb),
            pl.BlockSpec((tt, nl), lambda i, p: (i, 0)),
            pl.BlockSpec((8, nl), lambda i, p: (prev_rows(i), 0)),
            pl.BlockSpec((3, PAIR), lambda i, p: (0, p)),
            pl.BlockSpec((8, PAIR), lambda i, p: (0, p)),
            pl.BlockSpec((PAIR, PAIR), lambda i, p: (0, p)),
            pl.BlockSpec((PAIR, PAIR), lambda i, p: (0, p)),
            pl.BlockSpec((PAIR, PAIR), lambda i, p: (0, p)),
        ],
        out_specs=[
            out_tok, out_tok, out_tok, out_tok,
            pl.BlockSpec((tt // CHUNK, 1, PAIR, PAIR), lambda i, p: (i, p, 0, 0)),
            pl.BlockSpec((tt // CHUNK, 1, HEAD, PAIR), lambda i, p: (i, p, 0, 0)),
        ],
        out_shape=[
            tok_shape, tok_shape, tok_shape, tok_shape,
            jax.ShapeDtypeStruct((t // CHUNK, n_pairs, PAIR, PAIR), F32),
            jax.ShapeDtypeStruct((t // CHUNK, n_pairs, HEAD, PAIR), F32),
        ],
        compiler_params=_params(("arbitrary", "arbitrary")),
        name="rwkv_prep",
    )(proj, proj, proj, proj, proj, proj, lora, lora, mu_rkv, pvec, w2pad, a2pad, g2)


def _rwkv_scan_kernel(rp_ref, yq_ref, bonus_ref, gate_ref, g_ref, h_ref, pvec_ref, o_ref, s_scr):
    tt = rp_ref.shape[0]

    @pl.when(pl.program_id(2) == 0)
    def _():
        s_scr[...] = jnp.zeros_like(s_scr)

    pvec = pvec_ref[...]
    ln_g, ln_b = pvec[5:6], pvec[6:7]
    lane = lax.broadcasted_iota(I32, (HEAD, PAIR), 1)
    is_lo = lane < HEAD
    lane_r = lax.broadcasted_iota(I32, (PAIR, PAIR), 0)
    lane_c = lax.broadcasted_iota(I32, (PAIR, PAIR), 1)
    bd = jnp.where((lane_r // HEAD) == (lane_c // HEAD), 1.0, 0.0).astype(BF16)

    for c in range(tt // CHUNK):
        rows = slice(c * CHUNK, (c + 1) * CHUNK)
        s0 = s_scr[...]
        s_st = jnp.concatenate([jnp.where(is_lo, s0, 0.0), jnp.where(is_lo, 0.0, s0)], axis=0)
        y = _dot3_nt(rp_ref[rows, :], s_st) + yq_ref[rows, :]
        s_scr[...] = _dot3(s0, g_ref[c, 0]) + h_ref[c, 0]
        mu = _head_sum(y, bd) * (1.0 / HEAD)
        yc = y - mu
        var = _head_sum(yc * yc, bd) * (1.0 / HEAD)
        yn = yc * lax.rsqrt(var + LNX_EPS) * ln_g + ln_b
        o_ref[rows, :] = (yn + bonus_ref[rows, :]) * gate_ref[rows, :]


def _rwkv_scan(rp, yq, bonus, gate, g, h, pvec, batch):
    t, width = rp.shape
    n_pairs = width // PAIR
    tt = TT_RWKV
    nt = t // batch // tt
    tok = pl.BlockSpec((tt, PAIR), lambda b, p, i: (b * nt + i, p))
    return pl.pallas_call(
        _rwkv_scan_kernel,
        grid=(batch, n_pairs, nt),
        in_specs=[
            tok, tok, tok, tok,
            pl.BlockSpec((tt // CHUNK, 1, PAIR, PAIR), lambda b, p, i: (b * nt + i, p, 0, 0)),
            pl.BlockSpec((tt // CHUNK, 1, HEAD, PAIR), lambda b, p, i: (b * nt + i, p, 0, 0)),
            pl.BlockSpec((8, PAIR), lambda b, p, i: (0, p)),
        ],
        out_specs=tok,
        out_shape=jax.ShapeDtypeStruct((t, width), F32),
        scratch_shapes=[pltpu.VMEM((HEAD, PAIR), F32)],
        compiler_params=_params(("arbitrary", "arbitrary", "arbitrary")),
        name="rwkv_scan",
    )(rp, yq, bonus, gate, g, h, pvec)


def _out_proj_kernel(x_ref, ya_ref, yb_ref, wo_ref, g_ref, rwt_ref, rb_ref, h1_ref, hn_ref, lg_ref):
    wa = ya_ref.shape[1]
    h1 = (x_ref[...] + _dot(ya_ref[...].astype(BF16), wo_ref[0:wa, :])
          + _dot(yb_ref[...].astype(BF16), wo_ref[wa:, :]))
    h1_ref[...] = h1
    hn = h1 * lax.rsqrt(jnp.mean(h1 * h1, axis=-1, keepdims=True) + RMS_EPS) * g_ref[...]
    hn_ref[...] = hn
    lg_ref[...] = _dot3_nt(rwt_ref[...], hn) + rb_ref[...]


def _out_proj(x2, ya, yb, wo_bf, g2, rwt, rb):
    t, d = x2.shape
    wa = ya.shape[1]
    wb = yb.shape[1]
    ne = rwt.shape[0]
    tm = TM_PROJ
    return pl.pallas_call(
        _out_proj_kernel,
        grid=(t // tm,),
        in_specs=[
            pl.BlockSpec((tm, d), lambda i: (i, 0)),
            pl.BlockSpec((tm, wa), lambda i: (i, 0)),
            pl.BlockSpec((tm, wb), lambda i: (i, 0)),
            pl.BlockSpec((wa + wb, d), lambda i: (0, 0)),
            pl.BlockSpec((1, d), lambda i: (0, 0)),
            pl.BlockSpec((ne, d), lambda i: (0, 0)),
            pl.BlockSpec((ne, 1), lambda i: (0, 0)),
        ],
        out_specs=[
            pl.BlockSpec((tm, d), lambda i: (i, 0)),
            pl.BlockSpec((tm, d), lambda i: (i, 0)),
            pl.BlockSpec((ne, tm), lambda i: (0, i)),
        ],
        out_shape=[
            jax.ShapeDtypeStruct((t, d), F32),
            jax.ShapeDtypeStruct((t, d), F32),
            jax.ShapeDtypeStruct((ne, t), F32),
        ],
        compiler_params=_params(("arbitrary",)),
        name="out_proj",
    )(x2, ya, yb, wo_bf, g2, rwt, rb)


def _route_kernel(lg_ref, idx_ref, gate_ref, rank_ref, cnt_ref, carry_scr):
    @pl.when(pl.program_id(0) == 0)
    def _():
        carry_scr[...] = jnp.zeros_like(carry_scr)

    l = lg_ref[...]
    ne, tr = l.shape
    e_iota = lax.broadcasted_iota(I32, (ne, tr), 0)
    chosen = jnp.zeros((ne, tr), F32)
    vals, sels = [], []
    for j in range(TOP_K):
        m = jnp.max(l, axis=0, keepdims=True)
        idx = jnp.min(jnp.where(l == m, e_iota, ne), axis=0, keepdims=True)
        sel = e_iota == idx
        idx_ref[j:j + 1, :] = idx
        vals.append(m)
        sels.append(sel)
        chosen = jnp.where(sel, 1.0, chosen)
        l = jnp.where(sel, -jnp.inf, l)
    ex = [jnp.exp(vj - vals[0]) for vj in vals]
    den = ex[0] + ex[1] + ex[2] + ex[3]
    for j in range(TOP_K):
        gate_ref[j:j + 1, :] = ex[j] / den

    ti = lax.broadcasted_iota(I32, (tr, tr), 0)
    tj = lax.broadcasted_iota(I32, (tr, tr), 1)
    upper = jnp.where(ti <= tj, 1.0, 0.0).astype(BF16)
    inc = _dot(chosen.astype(BF16), upper)
    carry = carry_scr[...]
    excl = inc - chosen + carry[:, 0:1]
    for j in range(TOP_K):
        rk = jnp.sum(jnp.where(sels[j], excl, 0.0), axis=0, keepdims=True)
        rank_ref[j:j + 1, :] = rk.astype(I32)
    new_carry = carry + inc[:, tr - 1:tr]
    carry_scr[...] = new_carry
    cnt_ref[...] = new_carry


def _route(logits_t):
    ne, t = logits_t.shape
    tr = TR_ROUTE
    tok = pl.BlockSpec((TOP_K, tr), lambda i: (0, i))
    return pl.pallas_call(
        _route_kernel,
        grid=(t // tr,),
        in_specs=[pl.BlockSpec((ne, tr), lambda i: (0, i))],
        out_specs=[tok, tok, tok, pl.BlockSpec((ne, 128), lambda i: (0, 0))],
        out_shape=[
            jax.ShapeDtypeStruct((TOP_K, t), I32),
            jax.ShapeDtypeStruct((TOP_K, t), F32),
            jax.ShapeDtypeStruct((TOP_K, t), I32),
            jax.ShapeDtypeStruct((ne, 128), F32),
        ],
        scratch_shapes=[pltpu.VMEM((ne, 128), F32)],
        compiler_params=_params(("arbitrary",)),
        name="route",
    )(logits_t)


def _dispatch_kernel(pend_ref, nused_ref, pos_ref, hn_ref, xin_ref, zero_scr, sem):
    bm = zero_scr.shape[0]
    td = pos_ref.shape[1]
    nb = xin_ref.shape[0] // bm
    i = pl.program_id(0)

    def zero_block(row0):
        cp = pltpu.make_async_copy(zero_scr, xin_ref.at[pl.ds(pl.multiple_of(row0, bm), bm)], sem)
        cp.start()
        cp.wait()

    @pl.when(i == 0)
    def _():
        zero_scr[...] = jnp.zeros_like(zero_scr)

        def per_expert(e, carry):
            end = pend_ref[e]

            @pl.when(end >= bm)
            def _():
                zero_block(end - bm)
            return carry

        lax.fori_loop(0, pend_ref.shape[0], per_expert, 0)

        def per_tail(blk, carry):
            zero_block(blk * bm)
            return carry

        lax.fori_loop(nused_ref[0], nb, per_tail, 0)

    def row_copy(src_row, dst_row):
        return pltpu.make_async_copy(hn_ref.at[pl.ds(src_row, 1)], xin_ref.at[pl.ds(dst_row, 1)], sem)

    def issue(tk, carry):
        for j in range(TOP_K):
            row_copy(i * td + tk, pos_ref[j, tk]).start()
        return carry

    lax.fori_loop(0, td, issue, 0)

    def drain(n, carry):
        row_copy(0, 0).wait()
        return carry

    lax.fori_loop(0, td * TOP_K, drain, 0)


def _dispatch(pad_end, n_used, pos, hn, n_rows):
    t, d = hn.shape
    td = TD_DISPATCH
    grid_spec = pltpu.PrefetchScalarGridSpec(
        num_scalar_prefetch=2,
        grid=(t // td,),
        in_specs=[
            pl.BlockSpec((TOP_K, td), lambda i, pe, nu: (0, i), memory_space=pltpu.SMEM),
            pl.BlockSpec(memory_space=pl.ANY),
        ],
        out_specs=pl.BlockSpec(memory_space=pl.ANY),
        scratch_shapes=[pltpu.VMEM((BM_MOE, d), F32), pltpu.SemaphoreType.DMA(())],
    )
    return pl.pallas_call(
        _dispatch_kernel,
        grid_spec=grid_spec,
        out_shape=jax.ShapeDtypeStruct((n_rows, d), F32),
        compiler_params=_params(("arbitrary",)),
        name="dispatch",
    )(pad_end, n_used, pos, hn)


def _experts_kernel(be_ref, nused_ref, x_ref, w1_ref, b1_ref, w2_ref, b2_ref, o_ref):
    b = pl.program_id(0)

    @pl.when(b < nused_ref[0])
    def _():
        dh = w2_ref.shape[1]
        hid = _dot(x_ref[...].astype(BF16), w1_ref[0]) + b1_ref[0]
        glu = jnp.minimum(hid[:, :dh], SWIGLU_LIMIT)
        lin = jnp.clip(hid[:, dh:], -SWIGLU_LIMIT, SWIGLU_LIMIT)
        act = glu * jax.nn.sigmoid(SWIGLU_ALPHA * glu) * (lin + 1.0)
        o_ref[...] = _dot(act.astype(BF16), w2_ref[0]) + b2_ref[0]

    @pl.when(b >= nused_ref[0])
    def _():
        o_ref[...] = jnp.zeros_like(o_ref)


def _experts(block_expert, n_used, xin, w1p, b1p, w2b, b2):
    n_rows, d = xin.shape
    ne, _, dh2 = w1p.shape
    dh = dh2 // 2
    bm = BM_MOE
    grid_spec = pltpu.PrefetchScalarGridSpec(
        num_scalar_prefetch=2,
        grid=(n_rows // bm,),
        in_specs=[
            pl.BlockSpec((bm, d), lambda b, be, nu: (b, 0)),
            pl.BlockSpec((1, d, dh2), lambda b, be, nu: (be[b], 0, 0)),
            pl.BlockSpec((1, 1, dh2), lambda b, be, nu: (be[b], 0, 0)),
            pl.BlockSpec((1, dh, d), lambda b, be, nu: (be[b], 0, 0)),
            pl.BlockSpec((1, 1, d), lambda b, be, nu: (be[b], 0, 0)),
        ],
        out_specs=pl.BlockSpec((bm, d), lambda b, be, nu: (b, 0)),
    )
    return pl.pallas_call(
        _experts_kernel,
        grid_spec=grid_spec,
        out_shape=jax.ShapeDtypeStruct((n_rows, d), F32),
        compiler_params=_params(("arbitrary",)),
        name="experts",
    )(block_expert, n_used, xin, w1p, b1p, w2b, b2)


def _combine_kernel(pos_ref, gate_ref, h1_ref, fg_ref, yout_ref, o_ref, ybuf, sem):
    tc = h1_ref.shape[0]

    def row_copy(src_row, j, tk):
        return pltpu.make_async_copy(yout_ref.at[pl.ds(src_row, 1)], ybuf.at[j, pl.ds(tk, 1)], sem)

    def issue(tk, carry):
        for j in range(TOP_K):
            row_copy(pos_ref[j, tk], j, tk).start()
        return carry

    lax.fori_loop(0, tc, issue, 0)

    def drain(n, carry):
        row_copy(0, 0, 0).wait()
        return carry

    lax.fori_loop(0, tc * TOP_K, drain, 0)

    gates = gate_ref[...]
    h = h1_ref[...]
    for j in range(TOP_K):
        h = h + ybuf[j] * gates[:, j:j + 1]
    o_ref[...] = h * lax.rsqrt(jnp.mean(h * h, axis=-1, keepdims=True) + RMS_EPS) * fg_ref[...]


def _combine(pos, gates_t, h1, fg, yout):
    t, d = h1.shape
    tc = TC_COMBINE
    return pl.pallas_call(
        _combine_kernel,
        grid=(t // tc,),
        in_specs=[
            pl.BlockSpec((TOP_K, tc), lambda i: (0, i), memory_space=pltpu.SMEM),
            pl.BlockSpec((tc, TOP_K), lambda i: (i, 0)),
            pl.BlockSpec((tc, d), lambda i: (i, 0)),
            pl.BlockSpec((1, d), lambda i: (0, 0)),
            pl.BlockSpec(memory_space=pl.ANY),
        ],
        out_specs=pl.BlockSpec((tc, d), lambda i: (i, 0)),
        out_shape=jax.ShapeDtypeStruct((t, d), F32),
        scratch_shapes=[pltpu.VMEM((TOP_K, tc, d), F32), pltpu.SemaphoreType.DMA(())],
        compiler_params=_params(("arbitrary",)),
        name="combine",
    )(pos, gates_t, h1, fg, yout)


def kernel(x, norm1_g, w_in, sgu_ln_g, sgu_ln_b, sgu_w, sgu_b, mu_rkv, mu_wag, decay_w0, decay_w1,
           decay_w2, iclr_a0, iclr_a1, iclr_a2, gate_g1, gate_g2, k_k, k_a, r_k, lnx_g, lnx_b, w_out,
           norm2_g, router_w, router_b, moe_w1, moe_b1, moe_w2, moe_b2, final_g):
    batch, seq, d = x.shape
    t = batch * seq
    depth = w_in.shape[0]
    sgu_width = sgu_ln_g.shape[1]
    rw = mu_rkv.shape[2]
    n_dec, n_icl, n_gate = decay_w1.shape[2], iclr_a1.shape[2], gate_g1.shape[2]
    assert n_dec == HEAD and n_icl == HEAD and n_gate == PAIR and rw % PAIR == 0
    assert seq % TT_RWKV == 0 and t % TM_PROJ == 0 and sgu_w.shape[2] == SGU_BLOCK
    assert depth == 1, "the final RMSNorm is fused into the last layer's combine kernel"

    h = x.reshape(t, d)
    for l in range(depth):
        win_bf = w_in[l].astype(BF16)
        wl = jnp.concatenate([decay_w1[l], iclr_a1[l], gate_g1[l]], axis=1)
        mucat = jnp.concatenate([
            jnp.broadcast_to(mu_wag[l, 0][:, None], (d, n_dec)),
            jnp.broadcast_to(mu_wag[l, 1][:, None], (d, n_icl)),
            jnp.broadcast_to(mu_wag[l, 2][:, None], (d, n_gate))], axis=1)
        zeros = jnp.zeros((HEAD, rw), F32)
        w2pad = jnp.concatenate([decay_w2[l], zeros], axis=0)
        a2pad = jnp.concatenate([zeros, iclr_a2[l]], axis=0)
        pvec = jnp.stack([decay_w0[l], iclr_a0[l], k_k[l], k_a[l], r_k[l].reshape(-1),
                          lnx_g[l], lnx_b[l], jnp.zeros((rw,), F32)], axis=0)
        bias2d = jnp.repeat(sgu_b[l].T, sgu_width // sgu_b.shape[1], axis=1)
        wo_bf = w_out[l].astype(BF16)
        dh = moe_w2.shape[2]
        w1p = jnp.concatenate([moe_w1[l][:, :, 0::2], moe_w1[l][:, :, 1::2]], axis=-1).astype(BF16)
        b1p = jnp.concatenate([moe_b1[l][:, 0::2], moe_b1[l][:, 1::2]], axis=-1)[:, None, :]
        w2b = moe_w2[l].astype(BF16)
        b2 = moe_b2[l][:, None, :]

        proj, lora = _in_proj(h, norm1_g[l][None, :], win_bf, wl, mucat)
        ya = _sgu(proj, sgu_ln_g[l][None, :], sgu_ln_b[l][None, :], sgu_w[l], bias2d)
        rp, yq, bonus, gate, g_m, h_m = _rwkv_prep(proj, lora, mu_rkv[l], pvec, w2pad, a2pad,
                                                   gate_g2[l], seq, 2 * sgu_width)
        yb = _rwkv_scan(rp, yq, bonus, gate, g_m, h_m, pvec, batch)
        h1, hn2, logits_t = _out_proj(h, ya, yb, wo_bf, norm2_g[l][None, :], router_w[l].T,
                                      router_b[l][:, None])
        idx, gates, rank, cnt = _route(logits_t)

        counts = cnt[:, 0].astype(I32)
        padded = (counts + BM_MOE - 1) // BM_MOE * BM_MOE
        pad_end = jnp.cumsum(padded).astype(I32)
        pad_start = pad_end - padded
        pos = pad_start[idx] + rank
        n_blocks = (t * TOP_K) // BM_MOE + N_EXPERTS
        n_used = (pad_end[-1:] // BM_MOE).astype(I32)
        block_start = jnp.arange(n_blocks, dtype=I32) * BM_MOE
        block_expert = jnp.minimum(jnp.searchsorted(pad_end, block_start, side='right'),
                                   N_EXPERTS - 1).astype(I32)

        xin = _dispatch(pad_end, n_used, pos, hn2, n_blocks * BM_MOE)
        yout = _experts(block_expert, n_used, xin, w1p, b1p, w2b, b2)
        h = _combine(pos, gates.T, h1, final_g[None, :], yout)
        assert dh * 2 == w1p.shape[2]
    return h.reshape(batch, seq, d)
```

```python
import functools

import jax
import jax.numpy as jnp
from jax import lax
from jax.experimental import pallas as pl
from jax.experimental.pallas import tpu as pltpu

F32 = jnp.float32
BF16 = jnp.bfloat16
I32 = jnp.int32

RMS_EPS = 1e-5
LN_EPS = 1e-5
LNX_EPS = 64e-5
CHUNK = 64
SGU_BLOCK = 128
HEAD = 64
PAIR = 2 * HEAD
N_EXPERTS = 32
TOP_K = 4
SWIGLU_ALPHA = 1.702
SWIGLU_LIMIT = 7.0

V7X_VMEM_LIMIT = 56 * 1024 * 1024

TM_PROJ = 512
TM_SGU = 512
TT_RWKV = 256
TR_ROUTE = 512
BM_MOE = 256
TD_DISPATCH = 256
TC_COMBINE = 256


def _dot(a, b):
    return jnp.dot(a, b, preferred_element_type=F32)


def _dot_nt(a, b):
    return lax.dot_general(a, b, (((1,), (1,)), ((), ())), preferred_element_type=F32)


def _split(x):
    hi = x.astype(BF16)
    lo = (x - hi.astype(F32)).astype(BF16)
    return hi, lo


def _dot3(a, b):
    ah, al = _split(a)
    bh, bl = _split(b)
    return _dot(ah, bh) + _dot(al, bh) + _dot(ah, bl)


def _dot3_nt(a, b):
    ah, al = _split(a)
    bh, bl = _split(b)
    return _dot_nt(ah, bh) + _dot_nt(al, bh) + _dot_nt(ah, bl)


def _head_sum(x, bd):
    hi, lo = _split(x)
    return _dot(hi, bd) + _dot(lo, bd)


def _params(sem, vmem=V7X_VMEM_LIMIT):
    return pltpu.CompilerParams(dimension_semantics=sem, vmem_limit_bytes=vmem)


def _in_proj_kernel(x_ref, g_ref, win_ref, wl_ref, mucat_ref, proj_ref, lora_ref, wl_scr):
    @pl.when(pl.program_id(0) == 0)
    def _():
        wl = wl_ref[...]
        mu = mucat_ref[...]
        n = wl.shape[1]
        wl_scr[:, 0:n] = (wl * (1.0 - mu)).astype(BF16)
        wl_scr[:, n:2 * n] = (wl * mu).astype(BF16)

    x = x_ref[...]
    hn = x * lax.rsqrt(jnp.mean(x * x, axis=-1, keepdims=True) + RMS_EPS) * g_ref[...]
    hb = hn.astype(BF16)
    proj_ref[...] = _dot(hb, win_ref[...])
    lora_ref[...] = _dot(hb, wl_scr[...])


def _in_proj(x2, g, win_bf, wl, mucat):
    t, d = x2.shape
    n_in = win_bf.shape[1]
    n_l = wl.shape[1]
    tm = TM_PROJ
    return pl.pallas_call(
        _in_proj_kernel,
        grid=(t // tm,),
        in_specs=[
            pl.BlockSpec((tm, d), lambda i: (i, 0)),
            pl.BlockSpec((1, d), lambda i: (0, 0)),
            pl.BlockSpec((d, n_in), lambda i: (0, 0)),
            pl.BlockSpec((d, n_l), lambda i: (0, 0)),
            pl.BlockSpec((d, n_l), lambda i: (0, 0)),
        ],
        out_specs=[
            pl.BlockSpec((tm, n_in), lambda i: (i, 0)),
            pl.BlockSpec((tm, 2 * n_l), lambda i: (i, 0)),
        ],
        out_shape=[
            jax.ShapeDtypeStruct((t, n_in), F32),
            jax.ShapeDtypeStruct((t, 2 * n_l), F32),
        ],
        scratch_shapes=[pltpu.VMEM((d, 2 * n_l), BF16)],
        compiler_params=_params(("arbitrary",)),
        name="in_proj",
    )(x2, g, win_bf, wl, mucat)


def _sgu_kernel(z_ref, lng_ref, lnb_ref, w_ref, bias_ref, o_ref, wm_scr):
    n_heads = w_ref.shape[0]

    @pl.when(pl.program_id(0) == 0)
    def _():
        qi = lax.broadcasted_iota(I32, (SGU_BLOCK, SGU_BLOCK), 0) // CHUNK
        kj = lax.broadcasted_iota(I32, (SGU_BLOCK, SGU_BLOCK), 1) // CHUNK
        for h in range(n_heads):
            wm_scr[h] = jnp.where(kj <= qi, w_ref[h], 0.0).astype(BF16)

    z = z_ref[...]
    tm, two_w = z.shape
    width = two_w // 2
    gz = 0.5 * z * (1.0 + lax.erf(z * (2.0 ** -0.5)))
    u = gz[:, :width]
    v = gz[:, width:]
    mu = jnp.mean(v, axis=-1, keepdims=True)
    vc = v - mu
    var = jnp.mean(vc * vc, axis=-1, keepdims=True)
    vn = vc * lax.rsqrt(var + LN_EPS) * lng_ref[...] + lnb_ref[...]
    lane = lax.broadcasted_iota(I32, (SGU_BLOCK, PAIR), 1)
    is_lo = lane < HEAD
    bias = bias_ref[...]
    for blk in range(tm // SGU_BLOCK):
        rows = slice(blk * SGU_BLOCK, (blk + 1) * SGU_BLOCK)
        for p in range(width // PAIR):
            cols = slice(p * PAIR, (p + 1) * PAIR)
            vp = vn[rows, cols]
            lo = jnp.where(is_lo, vp, 0.0).astype(BF16)
            hi = jnp.where(is_lo, 0.0, vp).astype(BF16)
            sv = _dot(wm_scr[2 * p], lo) + _dot(wm_scr[2 * p + 1], hi)
            o_ref[rows, cols] = u[rows, cols] * (sv + bias[:, cols])


def _sgu(proj, ln_g, ln_b, w_s, bias2d):
    t = proj.shape[0]
    width = ln_g.shape[1]
    n_heads = w_s.shape[0]
    tm = TM_SGU
    return pl.pallas_call(
        _sgu_kernel,
        grid=(t // tm,),
        in_specs=[
            pl.BlockSpec((tm, 2 * width), lambda i: (i, 0)),
            pl.BlockSpec((1, width), lambda i: (0, 0)),
            pl.BlockSpec((1, width), lambda i: (0, 0)),
            pl.BlockSpec((n_heads, SGU_BLOCK, SGU_BLOCK), lambda i: (0, 0, 0)),
            pl.BlockSpec((SGU_BLOCK, width), lambda i: (0, 0)),
        ],
        out_specs=pl.BlockSpec((tm, width), lambda i: (i, 0)),
        out_shape=jax.ShapeDtypeStruct((t, width), F32),
        scratch_shapes=[pltpu.VMEM((n_heads, SGU_BLOCK, SGU_BLOCK), BF16)],
        compiler_params=_params(("arbitrary",)),
        name="sgu",
    )(proj, ln_g, ln_b, w_s, bias2d)


def _rwkv_prep_kernel(seq_len, pr_ref, pk_ref, pv_ref, ppr_ref, ppk_ref, ppv_ref, lo_ref, plo_ref,
                      mu_ref, pvec_ref, w2_ref, a2_ref, g2_ref,
                      rp_ref, yq_ref, bonus_ref, gate_ref, g_ref, h_ref):
    tt = pr_ref.shape[0]
    n_chunks = tt // CHUNK
    i = pl.program_id(0)
    keep = jnp.where((i * tt) % seq_len == 0, 0.0, 1.0)

    def shift(x, prev):
        rolled = pltpu.roll(x, 1, 0)
        first = prev[7:8, :] * keep
        rowc = lax.broadcasted_iota(I32, x.shape, 0)
        return jnp.where(rowc == 0, first, rolled)

    mu = mu_ref[...]

    def shift_mix(ref, pref, m):
        x = ref[...]
        return x + (shift(x, pref[...]) - x) * m

    r = shift_mix(pr_ref, ppr_ref, mu[0:1])
    k = shift_mix(pk_ref, ppk_ref, mu[1:2])
    v = shift_mix(pv_ref, ppv_ref, mu[2:3])

    lo = lo_ref[...]
    plo = plo_ref[...]
    half = lo.shape[1] // 2
    l_all = lo[:, :half] + shift(lo[:, half:], plo[:, half:])
    l_wa = l_all[:, :PAIR]
    l_g = l_all[:, PAIR:]

    pvec = pvec_ref[...]
    w0, a0, k_k, k_a, r_k = pvec[0:1], pvec[1:2], pvec[2:3], pvec[3:4], pvec[4:5]

    dw = _dot(jnp.tanh(l_wa).astype(BF16), w2_ref[...].astype(BF16))
    ia = _dot(l_wa.astype(BF16), a2_ref[...].astype(BF16))
    gate_ref[...] = _dot(jax.nn.sigmoid(l_g).astype(BF16), g2_ref[...].astype(BF16))

    zneg = -(w0 + dw)
    softplus = jnp.maximum(zneg, 0.0) + jnp.log(1.0 + jnp.exp(-jnp.abs(zneg)))
    logw = -jnp.exp(-softplus - 0.5)
    iclr = jax.nn.sigmoid(a0 + ia)

    lane_r = lax.broadcasted_iota(I32, (PAIR, PAIR), 0)
    lane_c = lax.broadcasted_iota(I32, (PAIR, PAIR), 1)
    bd = jnp.where((lane_r // HEAD) == (lane_c // HEAD), 1.0, 0.0).astype(BF16)

    kk = k * k_k
    kk = kk / jnp.maximum(jnp.sqrt(_head_sum(kk * kk, bd)), 1e-12)
    k2 = k * (1.0 + (iclr - 1.0) * k_a)
    a = -kk
    b = kk * iclr
    bonus_ref[...] = _head_sum(r * k2 * r_k, bd) * v

    rc = lax.broadcasted_iota(I32, (tt, PAIR), 0) % CHUNK
    cl = logw
    s = 1
    while s < CHUNK:
        cl = cl + jnp.where(rc >= s, pltpu.roll(cl, s, 0), 0.0)
        s *= 2
    at_all = a * jnp.exp(cl - logw)
    rt_all = r * jnp.exp(cl)
    w_inv = jnp.exp(-cl)
    bt_all = b * w_inv
    kt_all = k2 * w_inv

    lane = lax.broadcasted_iota(I32, (CHUNK, PAIR), 1)
    is_lo = lane < HEAD

    def stack(x):
        return jnp.concatenate([jnp.where(is_lo, x, 0.0), jnp.where(is_lo, 0.0, x)], axis=0)

    n2 = 2 * PAIR
    ri = lax.broadcasted_iota(I32, (n2, n2), 0)
    ci = lax.broadcasted_iota(I32, (n2, n2), 1)
    same_head = ((ri // CHUNK) % 2) == ((ci // CHUNK) % 2)
    t_i = ri % CHUNK
    s_i = ci % CHUNK
    tri = same_head & (s_i < t_i + ri // PAIR)
    eye = lane_r == lane_c

    for c in range(n_chunks):
        rows = slice(c * CHUNK, (c + 1) * CHUNK)
        cl_c = cl[rows]
        last = cl_c[CHUNK - 1:CHUNK, :]
        w_rem = jnp.exp(last - cl_c)
        a_s = stack(at_all[rows])
        r_s = stack(rt_all[rows])
        b_s = stack(bt_all[rows])
        k_s = stack(kt_all[rows])
        v_s = stack(v[rows])
        bh_s = stack(b[rows] * w_rem)
        kh_s = stack(k2[rows] * w_rem)

        lhs = jnp.concatenate([a_s, r_s], axis=0).astype(BF16)
        rhs = jnp.concatenate([b_s, k_s], axis=0).astype(BF16)
        a_all = jnp.where(tri, _dot_nt(lhs, rhs), 0.0)
        n_k = a_all[:PAIR, :PAIR]
        a_ak = a_all[:PAIR, PAIR:].astype(BF16)
        a_rb = a_all[PAIR:, :PAIR].astype(BF16)
        a_rk = a_all[PAIR:, PAIR:].astype(BF16)
        v_sb = v_s.astype(BF16)

        x = jnp.concatenate([a_s, _dot(a_ak, v_sb)], axis=1)
        steps = CHUNK.bit_length() - 1
        for it in range(steps):
            nb = n_k.astype(BF16)
            x = x + _dot(nb, x.astype(BF16))
            if it + 1 < steps:
                n_k = _dot(nb, nb)
        xb = x.astype(BF16)
        ry = jnp.concatenate([r_s, _dot(a_rk, v_sb)], axis=1) + _dot(a_rb, xb)
        rp_ref[rows, :] = ry[:CHUNK, :PAIR] + ry[CHUNK:, :PAIR]
        yq_ref[rows, :] = ry[:CHUNK, PAIR:] + ry[CHUNK:, PAIR:]

        p_t = x[:, :PAIR].T.astype(BF16)
        g_ref[c, 0] = jnp.where(eye, jnp.exp(last), 0.0) + _dot(p_t, bh_s.astype(BF16))
        qv_t = jnp.concatenate([x[:, PAIR:], v_s], axis=0).T.astype(BF16)
        hbd = _dot(qv_t, jnp.concatenate([bh_s, kh_s], axis=0).astype(BF16))
        h_ref[c, 0] = hbd[:HEAD] + hbd[HEAD:]


def _rwkv_prep(proj, lora, mu_rkv, pvec, w2pad, a2pad, g2, seq_len, col0):
    t = proj.shape[0]
    width = mu_rkv.shape[1]
    n_pairs = width // PAIR
    tt = TT_RWKV
    nl = lora.shape[1]
    cb = col0 // PAIR
    wb = width // PAIR

    def prev_rows(i):
        return jnp.maximum(i * (tt // 8) - 1, 0)

    def tok(c):
        return pl.BlockSpec((tt, PAIR), lambda i, p: (i, c + p))

    def prev(c):
        return pl.BlockSpec((8, PAIR), lambda i, p: (prev_rows(i), c + p))

    out_tok = pl.BlockSpec((tt, PAIR), lambda i, p: (i, p))
    tok_shape = jax.ShapeDtypeStruct((t, width), F32)
    return pl.pallas_call(
        functools.partial(_rwkv_prep_kernel, seq_len),
        grid=(t // tt, n_pairs),
        in_specs=[
            tok(cb), tok(cb + wb), tok(cb + 2 * wb),
            prev(cb), prev(cb + wb), prev(cb + 2 * wb),
            pl.BlockSpec((tt, nl), lambda i, p: (i, 0)),
            pl.BlockSpec((8, nl), lambda i, p: (prev_rows(i), 0)),
            pl.BlockSpec((3, PAIR), lambda i, p: (0, p)),
            pl.BlockSpec((8, PAIR), lambda i, p: (0, p)),
            pl.BlockSpec((PAIR, PAIR), lambda i, p: (0, p)),
            pl.BlockSpec((PAIR, PAIR), lambda i, p: (0, p)),
            pl.BlockSpec((PAIR, PAIR), lambda i, p: (0, p)),
        ],
        out_specs=[
            out_tok, out_tok, out_tok, out_tok,
            pl.BlockSpec((tt // CHUNK, 1, PAIR, PAIR), lambda i, p: (i, p, 0, 0)),
            pl.BlockSpec((tt // CHUNK, 1, HEAD, PAIR), lambda i, p: (i, p, 0, 0)),
        ],
        out_shape=[
            tok_shape, tok_shape, tok_shape, tok_shape,
            jax.ShapeDtypeStruct((t // CHUNK, n_pairs, PAIR, PAIR), F32),
            jax.ShapeDtypeStruct((t // CHUNK, n_pairs, HEAD, PAIR), F32),
        ],
        compiler_params=_params(("arbitrary", "arbitrary")),
        name="rwkv_prep",
    )(proj, proj, proj, proj, proj, proj, lora, lora, mu_rkv, pvec, w2pad, a2pad, g2)


def _rwkv_scan_kernel(rp_ref, yq_ref, bonus_ref, gate_ref, g_ref, h_ref, pvec_ref, o_ref, s_scr):
    tt = rp_ref.shape[0]

    @pl.when(pl.program_id(2) == 0)
    def _():
        s_scr[...] = jnp.zeros_like(s_scr)

    pvec = pvec_ref[...]
    ln_g, ln_b = pvec[5:6], pvec[6:7]
    lane = lax.broadcasted_iota(I32, (HEAD, PAIR), 1)
    is_lo = lane < HEAD
    lane_r = lax.broadcasted_iota(I32, (PAIR, PAIR), 0)
    lane_c = lax.broadcasted_iota(I32, (PAIR, PAIR), 1)
    bd = jnp.where((lane_r // HEAD) == (lane_c // HEAD), 1.0, 0.0).astype(BF16)

    for c in range(tt // CHUNK):
        rows = slice(c * CHUNK, (c + 1) * CHUNK)
        s0 = s_scr[...]
        s_st = jnp.concatenate([jnp.where(is_lo, s0, 0.0), jnp.where(is_lo, 0.0, s0)], axis=0)
        y = _dot3_nt(rp_ref[rows, :], s_st) + yq_ref[rows, :]
        s_scr[...] = _dot3(s0, g_ref[c, 0]) + h_ref[c, 0]
        mu = _head_sum(y, bd) * (1.0 / HEAD)
        yc = y - mu
        var = _head_sum(yc * yc, bd) * (1.0 / HEAD)
        yn = yc * lax.rsqrt(var + LNX_EPS) * ln_g + ln_b
        o_ref[rows, :] = (yn + bonus_ref[rows, :]) * gate_ref[rows, :]


def _rwkv_scan(rp, yq, bonus, gate, g, h, pvec, batch):
    t, width = rp.shape
    n_pairs = width // PAIR
    tt = TT_RWKV
    nt = t // batch // tt
    tok = pl.BlockSpec((tt, PAIR), lambda b, p, i: (b * nt + i, p))
    return pl.pallas_call(
        _rwkv_scan_kernel,
        grid=(batch, n_pairs, nt),
        in_specs=[
            tok, tok, tok, tok,
            pl.BlockSpec((tt // CHUNK, 1, PAIR, PAIR), lambda b, p, i: (b * nt + i, p, 0, 0)),
            pl.BlockSpec((tt // CHUNK, 1, HEAD, PAIR), lambda b, p, i: (b * nt + i, p, 0, 0)),
            pl.BlockSpec((8, PAIR), lambda b, p, i: (0, p)),
        ],
        out_specs=tok,
        out_shape=jax.ShapeDtypeStruct((t, width), F32),
        scratch_shapes=[pltpu.VMEM((HEAD, PAIR), F32)],
        compiler_params=_params(("arbitrary", "arbitrary", "arbitrary")),
        name="rwkv_scan",
    )(rp, yq, bonus, gate, g, h, pvec)


def _out_proj_kernel(x_ref, ya_ref, yb_ref, wo_ref, g_ref, rwt_ref, rb_ref, h1_ref, hn_ref, lg_ref):
    wa = ya_ref.shape[1]
    h1 = (x_ref[...] + _dot(ya_ref[...].astype(BF16), wo_ref[0:wa, :])
          + _dot(yb_ref[...].astype(BF16), wo_ref[wa:, :]))
    h1_ref[...] = h1
    hn = h1 * lax.rsqrt(jnp.mean(h1 * h1, axis=-1, keepdims=True) + RMS_EPS) * g_ref[...]
    hn_ref[...] = hn
    lg_ref[...] = _dot3_nt(rwt_ref[...], hn) + rb_ref[...]


def _out_proj(x2, ya, yb, wo_bf, g2, rwt, rb):
    t, d = x2.shape
    wa = ya.shape[1]
    wb = yb.shape[1]
    ne = rwt.shape[0]
    tm = TM_PROJ
    return pl.pallas_call(
        _out_proj_kernel,
        grid=(t // tm,),
        in_specs=[
            pl.BlockSpec((tm, d), lambda i: (i, 0)),
            pl.BlockSpec((tm, wa), lambda i: (i, 0)),
            pl.BlockSpec((tm, wb), lambda i: (i, 0)),
            pl.BlockSpec((wa + wb, d), lambda i: (0, 0)),
            pl.BlockSpec((1, d), lambda i: (0, 0)),
            pl.BlockSpec((ne, d), lambda i: (0, 0)),
            pl.BlockSpec((ne, 1), lambda i: (0, 0)),
        ],
        out_specs=[
            pl.BlockSpec((tm, d), lambda i: (i, 0)),
            pl.BlockSpec((tm, d), lambda i: (i, 0)),
            pl.BlockSpec((ne, tm), lambda i: (0, i)),
        ],
        out_shape=[
            jax.ShapeDtypeStruct((t, d), F32),
            jax.ShapeDtypeStruct((t, d), F32),
            jax.ShapeDtypeStruct((ne, t), F32),
        ],
        compiler_params=_params(("arbitrary",)),
        name="out_proj",
    )(x2, ya, yb, wo_bf, g2, rwt, rb)


def _route_kernel(lg_ref, pos_ref, gate_ref, pend_ref, carry_scr, pstart_scr):
    phase = pl.program_id(0)
    first = pl.program_id(1) == 0

    @pl.when(first & (phase == 0))
    def _():
        carry_scr[...] = jnp.zeros_like(carry_scr)
        pstart_scr[...] = jnp.zeros_like(pstart_scr)
        pend_ref[...] = jnp.zeros_like(pend_ref)

    @pl.when(first & (phase == 1))
    def _():
        counts = carry_scr[...]
        padded = jnp.ceil(counts * (1.0 / BM_MOE)) * BM_MOE
        row = lax.broadcasted_iota(I32, counts.shape, 0)
        end = padded
        s = 1
        while s < counts.shape[0]:
            end = end + jnp.where(row >= s, pltpu.roll(end, s, 0), 0.0)
            s *= 2
        pstart_scr[...] = end - padded
        pend_ref[...] = end.astype(I32)
        carry_scr[...] = jnp.zeros_like(carry_scr)

    l = lg_ref[...]
    ne, tr = l.shape
    e_iota = lax.broadcasted_iota(I32, (ne, tr), 0)
    chosen = jnp.zeros((ne, tr), F32)
    vals, sels = [], []
    for j in range(TOP_K):
        m = jnp.max(l, axis=0, keepdims=True)
        idx = jnp.min(jnp.where(l == m, e_iota, ne), axis=0, keepdims=True)
        sel = e_iota == idx
        vals.append(m)
        sels.append(sel)
        chosen = jnp.where(sel, 1.0, chosen)
        l = jnp.where(sel, -jnp.inf, l)
    ex = [jnp.exp(vj - vals[0]) for vj in vals]
    den = ex[0] + ex[1] + ex[2] + ex[3]
    for j in range(TOP_K):
        gate_ref[j:j + 1, :] = ex[j] / den

    ti = lax.broadcasted_iota(I32, (tr, tr), 0)
    tj = lax.broadcasted_iota(I32, (tr, tr), 1)
    upper = jnp.where(ti <= tj, 1.0, 0.0).astype(BF16)
    inc = _dot(chosen.astype(BF16), upper)
    carry = carry_scr[...]
    row_of = inc - chosen + carry[:, 0:1] + pstart_scr[:, 0:1]
    for j in range(TOP_K):
        pj = jnp.sum(jnp.where(sels[j], row_of, 0.0), axis=0, keepdims=True)
        pos_ref[j:j + 1, :] = pj.astype(I32)
    carry_scr[...] = carry + inc[:, tr - 1:tr]


def _route(logits_t):
    ne, t = logits_t.shape
    tr = TR_ROUTE
    tok = pl.BlockSpec((TOP_K, tr), lambda ph, i: (0, i * ph))
    return pl.pallas_call(
        _route_kernel,
        grid=(2, t // tr),
        in_specs=[pl.BlockSpec((ne, tr), lambda ph, i: (0, i))],
        out_specs=[tok, tok, pl.BlockSpec((ne, 128), lambda ph, i: (0, 0))],
        out_shape=[
            jax.ShapeDtypeStruct((TOP_K, t), I32),
            jax.ShapeDtypeStruct((TOP_K, t), F32),
            jax.ShapeDtypeStruct((ne, 128), I32),
        ],
        scratch_shapes=[pltpu.VMEM((ne, 128), F32), pltpu.VMEM((ne, 128), F32)],
        compiler_params=_params(("arbitrary", "arbitrary")),
        name="route",
    )(logits_t)


def _dispatch_kernel(pend_ref, nused_ref, pos_ref, hn_ref, xin_ref, zero_scr, sem):
    bm = zero_scr.shape[0]
    td = pos_ref.shape[1]
    nb = xin_ref.shape[0] // bm
    i = pl.program_id(0)

    def zero_block(row0):
        cp = pltpu.make_async_copy(zero_scr, xin_ref.at[pl.ds(pl.multiple_of(row0, bm), bm)], sem)
        cp.start()
        cp.wait()

    @pl.when(i == 0)
    def _():
        zero_scr[...] = jnp.zeros_like(zero_scr)

        def per_expert(e, carry):
            end = pend_ref[e]

            @pl.when(end >= bm)
            def _():
                zero_block(end - bm)
            return carry

        lax.fori_loop(0, pend_ref.shape[0], per_expert, 0)

        def per_tail(blk, carry):
            zero_block(blk * bm)
            return carry

        lax.fori_loop(nused_ref[0], nb, per_tail, 0)

    def issue(tk, carry):
        for j in range(TOP_K):
            pltpu.make_async_copy(hn_ref.at[pl.ds(tk, 1)], xin_ref.at[pl.ds(pos_ref[j, tk], 1)],
                                  sem).start()
        return carry

    lax.fori_loop(0, td, issue, 0, unroll=8)

    for j in range(TOP_K):
        pltpu.make_async_copy(hn_ref, xin_ref.at[pl.ds(0, td)], sem).wait()


def _dispatch(pad_end, n_used, pos, hn, n_rows):
    t, d = hn.shape
    td = TD_DISPATCH
    grid_spec = pltpu.PrefetchScalarGridSpec(
        num_scalar_prefetch=2,
        grid=(t // td,),
        in_specs=[
            pl.BlockSpec((TOP_K, td), lambda i, pe, nu: (0, i), memory_space=pltpu.SMEM),
            pl.BlockSpec((td, d), lambda i, pe, nu: (i, 0)),
        ],
        out_specs=pl.BlockSpec(memory_space=pl.ANY),
        scratch_shapes=[pltpu.VMEM((BM_MOE, d), F32), pltpu.SemaphoreType.DMA(())],
    )
    return pl.pallas_call(
        _dispatch_kernel,
        grid_spec=grid_spec,
        out_shape=jax.ShapeDtypeStruct((n_rows, d), F32),
        compiler_params=_params(("arbitrary",)),
        name="dispatch",
    )(pad_end, n_used, pos, hn)


def _experts_kernel(be_ref, nused_ref, x_ref, w1_ref, b1_ref, w2_ref, b2_ref, o_ref,
                    wt_scr, wg_scr, wl_scr, w2_scr):
    b = pl.program_id(0)
    live = b < nused_ref[0]
    dh = w2_ref.shape[1]
    fresh = (b == 0) | (be_ref[b] != be_ref[jnp.maximum(b - 1, 0)])

    @pl.when(live & fresh)
    def _():
        n_slab, n_t, lanes = wt_scr.shape
        for part in range(2 * dh // n_t):
            rows = slice(part * (n_t // 2), (part + 1) * (n_t // 2))
            for s in range(n_slab):
                cols = slice(s * lanes, (s + 1) * lanes)
                wt_scr[s] = w1_ref[0, cols, part * n_t:(part + 1) * n_t].T
                wg_scr[rows, cols] = wt_scr[s, pl.ds(0, n_t // 2, stride=2), :].astype(BF16)
                wl_scr[rows, cols] = wt_scr[s, pl.ds(1, n_t // 2, stride=2), :].astype(BF16)
        w2_scr[...] = w2_ref[0].astype(BF16)

    @pl.when(live)
    def _():
        xb = x_ref[...].astype(BF16)
        bias = b1_ref[0]
        glu = jnp.minimum(_dot_nt(xb, wg_scr[...]) + bias[:, :dh], SWIGLU_LIMIT)
        lin = jnp.clip(_dot_nt(xb, wl_scr[...]) + bias[:, dh:], -SWIGLU_LIMIT, SWIGLU_LIMIT)
        act = glu * jax.nn.sigmoid(SWIGLU_ALPHA * glu) * (lin + 1.0)
        o_ref[...] = _dot(act.astype(BF16), w2_scr[...]) + b2_ref[0]

    @pl.when(jnp.logical_not(live))
    def _():
        o_ref[...] = jnp.zeros_like(o_ref)


def _experts(block_expert, n_used, xin, w1, b1p, w2, b2):
    n_rows, d = xin.shape
    ne, _, dh2 = w1.shape
    dh = dh2 // 2
    bm = BM_MOE
    grid_spec = pltpu.PrefetchScalarGridSpec(
        num_scalar_prefetch=2,
        grid=(n_rows // bm,),
        in_specs=[
            pl.BlockSpec((bm, d), lambda b, be, nu: (b, 0)),
            pl.BlockSpec((1, d, dh2), lambda b, be, nu: (be[b], 0, 0)),
            pl.BlockSpec((1, 1, dh2), lambda b, be, nu: (be[b], 0, 0)),
            pl.BlockSpec((1, dh, d), lambda b, be, nu: (be[b], 0, 0)),
            pl.BlockSpec((1, 1, d), lambda b, be, nu: (be[b], 0, 0)),
        ],
        out_specs=pl.BlockSpec((bm, d), lambda b, be, nu: (b, 0)),
        scratch_shapes=[
            pltpu.VMEM((d // 128, dh, 128), F32),
            pltpu.VMEM((dh, d), BF16),
            pltpu.VMEM((dh, d), BF16),
            pltpu.VMEM((dh, d), BF16),
        ],
    )
    return pl.pallas_call(
        _experts_kernel,
        grid_spec=grid_spec,
        out_shape=jax.ShapeDtypeStruct((n_rows, d), F32),
        compiler_params=_params(("arbitrary",)),
        name="experts",
    )(block_expert, n_used, xin, w1, b1p, w2, b2)


def _combine_kernel(pos_ref, gate_ref, h1_ref, fg_ref, yout_ref, o_ref, ybuf, sem):
    tc = h1_ref.shape[0]

    def issue(tk, carry):
        for j in range(TOP_K):
            pltpu.make_async_copy(yout_ref.at[pl.ds(pos_ref[j, tk], 1)],
                                  ybuf.at[pl.ds(j * tc + tk, 1)], sem).start()
        return carry

    lax.fori_loop(0, tc, issue, 0, unroll=8)
    pltpu.make_async_copy(yout_ref.at[pl.ds(0, TOP_K * tc)], ybuf, sem).wait()

    gates = gate_ref[...]
    h = h1_ref[...]
    for j in range(TOP_K):
        h = h + ybuf[j * tc:(j + 1) * tc, :] * gates[:, j:j + 1]
    o_ref[...] = h * lax.rsqrt(jnp.mean(h * h, axis=-1, keepdims=True) + RMS_EPS) * fg_ref[...]


def _combine(pos, gates_t, h1, fg, yout):
    t, d = h1.shape
    tc = TC_COMBINE
    return pl.pallas_call(
        _combine_kernel,
        grid=(t // tc,),
        in_specs=[
            pl.BlockSpec((TOP_K, tc), lambda i: (0, i), memory_space=pltpu.SMEM),
            pl.BlockSpec((tc, TOP_K), lambda i: (i, 0)),
            pl.BlockSpec((tc, d), lambda i: (i, 0)),
            pl.BlockSpec((1, d), lambda i: (0, 0)),
            pl.BlockSpec(memory_space=pl.ANY),
        ],
        out_specs=pl.BlockSpec((tc, d), lambda i: (i, 0)),
        out_shape=jax.ShapeDtypeStruct((t, d), F32),
        scratch_shapes=[pltpu.VMEM((TOP_K * tc, d), F32), pltpu.SemaphoreType.DMA(())],
        compiler_params=_params(("arbitrary",)),
        name="combine",
    )(pos, gates_t, h1, fg, yout)


def kernel(x, norm1_g, w_in, sgu_ln_g, sgu_ln_b, sgu_w, sgu_b, mu_rkv, mu_wag, decay_w0, decay_w1,
           decay_w2, iclr_a0, iclr_a1, iclr_a2, gate_g1, gate_g2, k_k, k_a, r_k, lnx_g, lnx_b, w_out,
           norm2_g, router_w, router_b, moe_w1, moe_b1, moe_w2, moe_b2, final_g):
    batch, seq, d = x.shape
    t = batch * seq
    depth = w_in.shape[0]
    sgu_width = sgu_ln_g.shape[1]
    rw = mu_rkv.shape[2]
    n_dec, n_icl, n_gate = decay_w1.shape[2], iclr_a1.shape[2], gate_g1.shape[2]
    assert n_dec == HEAD and n_icl == HEAD and n_gate == PAIR and rw % PAIR == 0
    assert seq % TT_RWKV == 0 and t % TM_PROJ == 0 and sgu_w.shape[2] == SGU_BLOCK
    assert depth == 1, "the final RMSNorm is fused into the last layer's combine kernel"

    h = x.reshape(t, d)
    for l in range(depth):
        win_bf = w_in[l].astype(BF16)
        wl = jnp.concatenate([decay_w1[l], iclr_a1[l], gate_g1[l]], axis=1)
        mucat = jnp.concatenate([
            jnp.broadcast_to(mu_wag[l, 0][:, None], (d, n_dec)),
            jnp.broadcast_to(mu_wag[l, 1][:, None], (d, n_icl)),
            jnp.broadcast_to(mu_wag[l, 2][:, None], (d, n_gate))], axis=1)
        zeros = jnp.zeros((HEAD, rw), F32)
        w2pad = jnp.concatenate([decay_w2[l], zeros], axis=0)
        a2pad = jnp.concatenate([zeros, iclr_a2[l]], axis=0)
        pvec = jnp.stack([decay_w0[l], iclr_a0[l], k_k[l], k_a[l], r_k[l].reshape(-1),
                          lnx_g[l], lnx_b[l], jnp.zeros((rw,), F32)], axis=0)
        bias2d = jnp.repeat(sgu_b[l].T, sgu_width // sgu_b.shape[1], axis=1)
        wo_bf = w_out[l].astype(BF16)
        b1p = jnp.concatenate([moe_b1[l][:, 0::2], moe_b1[l][:, 1::2]], axis=-1)[:, None, :]
        b2 = moe_b2[l][:, None, :]

        proj, lora = _in_proj(h, norm1_g[l][None, :], win_bf, wl, mucat)
        ya = _sgu(proj, sgu_ln_g[l][None, :], sgu_ln_b[l][None, :], sgu_w[l], bias2d)
        rp, yq, bonus, gate, g_m, h_m = _rwkv_prep(proj, lora, mu_rkv[l], pvec, w2pad, a2pad,
                                                   gate_g2[l], seq, 2 * sgu_width)
        yb = _rwkv_scan(rp, yq, bonus, gate, g_m, h_m, pvec, batch)
        h1, hn2, logits_t = _out_proj(h, ya, yb, wo_bf, norm2_g[l][None, :], router_w[l].T,
                                      router_b[l][:, None])
        pos, gates, pend = _route(logits_t)

        pad_end = pend[:, 0]
        n_blocks = (t * TOP_K) // BM_MOE + N_EXPERTS
        n_used = pad_end[-1:] // BM_MOE
        block_start = jnp.arange(n_blocks, dtype=I32) * BM_MOE
        block_expert = jnp.minimum(
            jnp.sum((pad_end[None, :] <= block_start[:, None]).astype(I32), axis=1), N_EXPERTS - 1)

        xin = _dispatch(pad_end, n_used, pos, hn2, n_blocks * BM_MOE)
        yout = _experts(block_expert, n_used, xin, moe_w1[l], b1p, moe_w2[l], b2)
        h = _combine(pos, gates.T, h1, final_g[None, :], yout)
    return h.reshape(batch, seq, d)
```

```python
import functools

import jax
import jax.numpy as jnp
from jax import lax
from jax.experimental import pallas as pl
from jax.experimental.pallas import tpu as pltpu

F32 = jnp.float32
BF16 = jnp.bfloat16
I32 = jnp.int32

RMS_EPS = 1e-5
LN_EPS = 1e-5
LNX_EPS = 64e-5
CHUNK = 64
SGU_BLOCK = 128
HEAD = 64
PAIR = 2 * HEAD
N_EXPERTS = 32
TOP_K = 4
SWIGLU_ALPHA = 1.702
SWIGLU_LIMIT = 7.0

V7X_VMEM_LIMIT = 56 * 1024 * 1024

TM_PROJ = 512
TM_SGU = 512
TT_RWKV = 256
TR_ROUTE = 512
BM_MOE = 256
TD_DISPATCH = 256
TC_COMBINE = 256


def _dot(a, b):
    return jnp.dot(a, b, preferred_element_type=F32)


def _dot_nt(a, b):
    return lax.dot_general(a, b, (((1,), (1,)), ((), ())), preferred_element_type=F32)


def _split(x):
    hi = x.astype(BF16)
    lo = (x - hi.astype(F32)).astype(BF16)
    return hi, lo


def _dot3(a, b):
    ah, al = _split(a)
    bh, bl = _split(b)
    return _dot(ah, bh) + _dot(al, bh) + _dot(ah, bl)


def _dot3_nt(a, b):
    ah, al = _split(a)
    bh, bl = _split(b)
    return _dot_nt(ah, bh) + _dot_nt(al, bh) + _dot_nt(ah, bl)


def _head_sum(x, bd):
    hi, lo = _split(x)
    return _dot(hi, bd) + _dot(lo, bd)


def _params(sem, vmem=V7X_VMEM_LIMIT):
    return pltpu.CompilerParams(dimension_semantics=sem, vmem_limit_bytes=vmem)


def _in_proj_kernel(x_ref, g_ref, win_ref, wl_ref, mucat_ref, proj_ref, lora_ref, wl_scr):
    @pl.when(pl.program_id(0) == 0)
    def _():
        wl = wl_ref[...]
        mu = mucat_ref[...]
        n = wl.shape[1]
        wl_scr[:, 0:n] = (wl * (1.0 - mu)).astype(BF16)
        wl_scr[:, n:2 * n] = (wl * mu).astype(BF16)

    x = x_ref[...]
    hn = x * lax.rsqrt(jnp.mean(x * x, axis=-1, keepdims=True) + RMS_EPS) * g_ref[...]
    hb = hn.astype(BF16)
    proj_ref[...] = _dot(hb, win_ref[...])
    lora_ref[...] = _dot(hb, wl_scr[...])


def _in_proj(x2, g, win_bf, wl, mucat):
    t, d = x2.shape
    n_in = win_bf.shape[1]
    n_l = wl.shape[1]
    tm = TM_PROJ
    return pl.pallas_call(
        _in_proj_kernel,
        grid=(t // tm,),
        in_specs=[
            pl.BlockSpec((tm, d), lambda i: (i, 0)),
            pl.BlockSpec((1, d), lambda i: (0, 0)),
            pl.BlockSpec((d, n_in), lambda i: (0, 0)),
            pl.BlockSpec((d, n_l), lambda i: (0, 0)),
            pl.BlockSpec((d, n_l), lambda i: (0, 0)),
        ],
        out_specs=[
            pl.BlockSpec((tm, n_in), lambda i: (i, 0)),
            pl.BlockSpec((tm, 2 * n_l), lambda i: (i, 0)),
        ],
        out_shape=[
            jax.ShapeDtypeStruct((t, n_in), F32),
            jax.ShapeDtypeStruct((t, 2 * n_l), F32),
        ],
        scratch_shapes=[pltpu.VMEM((d, 2 * n_l), BF16)],
        compiler_params=_params(("arbitrary",)),
        name="in_proj",
    )(x2, g, win_bf, wl, mucat)


def _sgu_kernel(z_ref, lng_ref, lnb_ref, w_ref, bias_ref, o_ref, wm_scr):
    n_heads = w_ref.shape[0]

    @pl.when(pl.program_id(0) == 0)
    def _():
        qi = lax.broadcasted_iota(I32, (SGU_BLOCK, SGU_BLOCK), 0) // CHUNK
        kj = lax.broadcasted_iota(I32, (SGU_BLOCK, SGU_BLOCK), 1) // CHUNK
        for h in range(n_heads):
            wm_scr[h] = jnp.where(kj <= qi, w_ref[h], 0.0).astype(BF16)

    z = z_ref[...]
    tm, two_w = z.shape
    width = two_w // 2
    gz = 0.5 * z * (1.0 + lax.erf(z * (2.0 ** -0.5)))
    u = gz[:, :width]
    v = gz[:, width:]
    mu = jnp.mean(v, axis=-1, keepdims=True)
    vc = v - mu
    var = jnp.mean(vc * vc, axis=-1, keepdims=True)
    vn = vc * lax.rsqrt(var + LN_EPS) * lng_ref[...] + lnb_ref[...]
    lane = lax.broadcasted_iota(I32, (SGU_BLOCK, PAIR), 1)
    is_lo = lane < HEAD
    bias = bias_ref[...]
    for blk in range(tm // SGU_BLOCK):
        rows = slice(blk * SGU_BLOCK, (blk + 1) * SGU_BLOCK)
        for p in range(width // PAIR):
            cols = slice(p * PAIR, (p + 1) * PAIR)
            vp = vn[rows, cols]
            lo = jnp.where(is_lo, vp, 0.0).astype(BF16)
            hi = jnp.where(is_lo, 0.0, vp).astype(BF16)
            sv = _dot(wm_scr[2 * p], lo) + _dot(wm_scr[2 * p + 1], hi)
            o_ref[rows, cols] = u[rows, cols] * (sv + bias[:, cols])


def _sgu(proj, ln_g, ln_b, w_s, bias2d):
    t = proj.shape[0]
    width = ln_g.shape[1]
    n_heads = w_s.shape[0]
    tm = TM_SGU
    return pl.pallas_call(
        _sgu_kernel,
        grid=(t // tm,),
        in_specs=[
            pl.BlockSpec((tm, 2 * width), lambda i: (i, 0)),
            pl.BlockSpec((1, width), lambda i: (0, 0)),
            pl.BlockSpec((1, width), lambda i: (0, 0)),
            pl.BlockSpec((n_heads, SGU_BLOCK, SGU_BLOCK), lambda i: (0, 0, 0)),
            pl.BlockSpec((SGU_BLOCK, width), lambda i: (0, 0)),
        ],
        out_specs=pl.BlockSpec((tm, width), lambda i: (i, 0)),
        out_shape=jax.ShapeDtypeStruct((t, width), F32),
        scratch_shapes=[pltpu.VMEM((n_heads, SGU_BLOCK, SGU_BLOCK), BF16)],
        compiler_params=_params(("arbitrary",)),
        name="sgu",
    )(proj, ln_g, ln_b, w_s, bias2d)


def _rwkv_prep_kernel(seq_len, pr_ref, pk_ref, pv_ref, ppr_ref, ppk_ref, ppv_ref, lo_ref, plo_ref,
                      mu_ref, pvec_ref, w2_ref, a2_ref, g2_ref,
                      rp_ref, yq_ref, bonus_ref, gate_ref, g_ref, h_ref):
    tt = pr_ref.shape[0]
    n_chunks = tt // CHUNK
    i = pl.program_id(0)
    keep = jnp.where((i * tt) % seq_len == 0, 0.0, 1.0)

    def shift(x, prev):
        rolled = pltpu.roll(x, 1, 0)
        first = prev[7:8, :] * keep
        rowc = lax.broadcasted_iota(I32, x.shape, 0)
        return jnp.where(rowc == 0, first, rolled)

    mu = mu_ref[...]

    def shift_mix(ref, pref, m):
        x = ref[...]
        return x + (shift(x, pref[...]) - x) * m

    r = shift_mix(pr_ref, ppr_ref, mu[0:1])
    k = shift_mix(pk_ref, ppk_ref, mu[1:2])
    v = shift_mix(pv_ref, ppv_ref, mu[2:3])

    lo = lo_ref[...]
    plo = plo_ref[...]
    half = lo.shape[1] // 2
    l_all = lo[:, :half] + shift(lo[:, half:], plo[:, half:])
    l_wa = l_all[:, :PAIR]
    l_g = l_all[:, PAIR:]

    pvec = pvec_ref[...]
    w0, a0, k_k, k_a, r_k = pvec[0:1], pvec[1:2], pvec[2:3], pvec[3:4], pvec[4:5]

    dw = _dot(jnp.tanh(l_wa).astype(BF16), w2_ref[...].astype(BF16))
    ia = _dot(l_wa.astype(BF16), a2_ref[...].astype(BF16))
    gate_ref[...] = _dot(jax.nn.sigmoid(l_g).astype(BF16), g2_ref[...].astype(BF16))

    zneg = -(w0 + dw)
    softplus = jnp.maximum(zneg, 0.0) + jnp.log(1.0 + jnp.exp(-jnp.abs(zneg)))
    logw = -jnp.exp(-softplus - 0.5)
    iclr = jax.nn.sigmoid(a0 + ia)

    lane_r = lax.broadcasted_iota(I32, (PAIR, PAIR), 0)
    lane_c = lax.broadcasted_iota(I32, (PAIR, PAIR), 1)
    bd = jnp.where((lane_r // HEAD) == (lane_c // HEAD), 1.0, 0.0).astype(BF16)

    kk = k * k_k
    kk = kk / jnp.maximum(jnp.sqrt(_head_sum(kk * kk, bd)), 1e-12)
    k2 = k * (1.0 + (iclr - 1.0) * k_a)
    a = -kk
    b = kk * iclr
    bonus_ref[...] = _head_sum(r * k2 * r_k, bd) * v

    rc = lax.broadcasted_iota(I32, (tt, PAIR), 0) % CHUNK
    cl = logw
    s = 1
    while s < CHUNK:
        cl = cl + jnp.where(rc >= s, pltpu.roll(cl, s, 0), 0.0)
        s *= 2
    at_all = a * jnp.exp(cl - logw)
    rt_all = r * jnp.exp(cl)
    w_inv = jnp.exp(-cl)
    bt_all = b * w_inv
    kt_all = k2 * w_inv

    lane = lax.broadcasted_iota(I32, (CHUNK, PAIR), 1)
    is_lo = lane < HEAD

    def stack(x):
        return jnp.concatenate([jnp.where(is_lo, x, 0.0), jnp.where(is_lo, 0.0, x)], axis=0)

    n2 = 2 * PAIR
    ri = lax.broadcasted_iota(I32, (n2, n2), 0)
    ci = lax.broadcasted_iota(I32, (n2, n2), 1)
    same_head = ((ri // CHUNK) % 2) == ((ci // CHUNK) % 2)
    t_i = ri % CHUNK
    s_i = ci % CHUNK
    tri = same_head & (s_i < t_i + ri // PAIR)
    eye = lane_r == lane_c

    chunks = range(n_chunks)
    rows = [slice(c * CHUNK, (c + 1) * CHUNK) for c in chunks]
    last = [cl[rows[c]][CHUNK - 1:CHUNK, :] for c in chunks]
    w_rem = [jnp.exp(last[c] - cl[rows[c]]) for c in chunks]
    a_s = [stack(at_all[rows[c]]) for c in chunks]
    r_s = [stack(rt_all[rows[c]]) for c in chunks]
    v_s = [stack(v[rows[c]]) for c in chunks]
    v_sb = [v_s[c].astype(BF16) for c in chunks]
    bk_h = [jnp.concatenate([stack(b[rows[c]] * w_rem[c]), stack(k2[rows[c]] * w_rem[c])],
                            axis=0).astype(BF16) for c in chunks]

    a_all = []
    for c in chunks:
        lhs = jnp.concatenate([a_s[c], r_s[c]], axis=0).astype(BF16)
        rhs = jnp.concatenate([stack(bt_all[rows[c]]), stack(kt_all[rows[c]])], axis=0).astype(BF16)
        a_all.append(jnp.where(tri, _dot_nt(lhs, rhs), 0.0))
    n_k = [a_all[c][:PAIR, :PAIR].astype(BF16) for c in chunks]
    a_rb = [a_all[c][PAIR:, :PAIR].astype(BF16) for c in chunks]
    x = [jnp.concatenate([a_s[c], _dot(a_all[c][:PAIR, PAIR:].astype(BF16), v_sb[c])], axis=1)
         for c in chunks]
    ry0 = [jnp.concatenate([r_s[c], _dot(a_all[c][PAIR:, PAIR:].astype(BF16), v_sb[c])], axis=1)
           for c in chunks]
    steps = CHUNK.bit_length() - 1
    for it in range(steps):
        x = [x[c] + _dot(n_k[c], x[c].astype(BF16)) for c in chunks]
        if it + 1 < steps:
            n_k = [_dot(n_k[c], n_k[c]).astype(BF16) for c in chunks]
    ry = [ry0[c] + _dot(a_rb[c], x[c].astype(BF16)) for c in chunks]
    for c in chunks:
        rp_ref[rows[c], :] = ry[c][:CHUNK, :PAIR] + ry[c][CHUNK:, :PAIR]
        yq_ref[rows[c], :] = ry[c][:CHUNK, PAIR:] + ry[c][CHUNK:, PAIR:]
    p_t = [x[c][:, :PAIR].T.astype(BF16) for c in chunks]
    qv_t = [jnp.concatenate([x[c][:, PAIR:], v_s[c]], axis=0).T.astype(BF16) for c in chunks]
    for c in chunks:
        g_ref[c, 0] = jnp.where(eye, jnp.exp(last[c]), 0.0) + _dot(p_t[c], bk_h[c][:PAIR])
    for c in chunks:
        hbd = _dot(qv_t[c], bk_h[c])
        h_ref[c, 0] = hbd[:HEAD] + hbd[HEAD:]


def _rwkv_prep(proj, lora, mu_rkv, pvec, w2pad, a2pad, g2, seq_len, col0):
    t = proj.shape[0]
    width = mu_rkv.shape[1]
    n_pairs = width // PAIR
    tt = TT_RWKV
    nl = lora.shape[1]
    cb = col0 // PAIR
    wb = width // PAIR

    def prev_rows(i):
        return jnp.maximum(i * (tt // 8) - 1, 0)

    def tok(c):
        return pl.BlockSpec((tt, PAIR), lambda i, p: (i, c + p))

    def prev(c):
        return pl.BlockSpec((8, PAIR), lambda i, p: (prev_rows(i), c + p))

    out_tok = pl.BlockSpec((tt, PAIR), lambda i, p: (i, p))
    tok_shape = jax.ShapeDtypeStruct((t, width), F32)
    return pl.pallas_call(
        functools.partial(_rwkv_prep_kernel, seq_len),
        grid=(t // tt, n_pairs),
        in_specs=[
            tok(cb), tok(cb + wb), tok(cb + 2 * wb),
            prev(cb), prev(cb + wb), prev(cb + 2 * wb),
            pl.BlockSpec((tt, nl), lambda i, p: (i, 0)),
            pl.BlockSpec((8, nl), lambda i, p: (prev_rows(i), 0)),
            pl.BlockSpec((3, PAIR), lambda i, p: (0, p)),
            pl.BlockSpec((8, PAIR), lambda i, p: (0, p)),
            pl.BlockSpec((PAIR, PAIR), lambda i, p: (0, p)),
            pl.BlockSpec((PAIR, PAIR), lambda i, p: (0, p)),
            pl.BlockSpec((PAIR, PAIR), lambda i, p: (0, p)),
        ],
        out_specs=[
            out_tok, out_tok, out_tok, out_tok,
            pl.BlockSpec((tt // CHUNK, 1, PAIR, PAIR), lambda i, p: (i, p, 0, 0)),
            pl.BlockSpec((tt // CHUNK, 1, HEAD, PAIR), lambda i, p: (i, p, 0, 0)),
        ],
        out_shape=[
            tok_shape, tok_shape, tok_shape, tok_shape,
            jax.ShapeDtypeStruct((t // CHUNK, n_pairs, PAIR, PAIR), F32),
            jax.ShapeDtypeStruct((t // CHUNK, n_pairs, HEAD, PAIR), F32),
        ],
        compiler_params=_params(("arbitrary", "arbitrary")),
        name="rwkv_prep",
    )(proj, proj, proj, proj, proj, proj, lora, lora, mu_rkv, pvec, w2pad, a2pad, g2)


def _rwkv_scan_kernel(rp_ref, yq_ref, bonus_ref, gate_ref, g_ref, h_ref, pvec_ref, o_ref, s_scr, y_scr):
    nb, tt, width = rp_ref.shape
    n_pairs = width // PAIR

    @pl.when(pl.program_id(0) == 0)
    def _():
        s_scr[...] = jnp.zeros_like(s_scr)

    lane = lax.broadcasted_iota(I32, (HEAD, PAIR), 1)
    is_lo = lane < HEAD
    lane_r = lax.broadcasted_iota(I32, (PAIR, PAIR), 0)
    lane_c = lax.broadcasted_iota(I32, (PAIR, PAIR), 1)
    bd = jnp.where((lane_r // HEAD) == (lane_c // HEAD), 1.0, 0.0).astype(BF16)

    chains = [(b, p) for b in range(nb) for p in range(n_pairs)]
    state = {ch: s_scr[ch[0], ch[1]] for ch in chains}
    for c in range(tt // CHUNK):
        rows = slice(c * CHUNK, (c + 1) * CHUNK)
        for (b, p) in chains:
            cols = slice(p * PAIR, (p + 1) * PAIR)
            s0 = state[(b, p)]
            s_st = jnp.concatenate([jnp.where(is_lo, s0, 0.0), jnp.where(is_lo, 0.0, s0)], axis=0)
            y_scr[b, rows, cols] = (_dot_nt(rp_ref[b, rows, cols].astype(BF16), s_st.astype(BF16))
                                    + yq_ref[b, rows, cols])
        state = {(b, p): _dot3(state[(b, p)], g_ref[b, c, p]) + h_ref[b, c, p] for (b, p) in chains}
    for (b, p) in chains:
        s_scr[b, p] = state[(b, p)]

    pvec = pvec_ref[...]
    for (b, p) in chains:
        cols = slice(p * PAIR, (p + 1) * PAIR)
        y = y_scr[b, :, cols]
        mu = _head_sum(y, bd) * (1.0 / HEAD)
        yc = y - mu
        var = _head_sum(yc * yc, bd) * (1.0 / HEAD)
        yn = yc * lax.rsqrt(var + LNX_EPS) * pvec[5:6, cols] + pvec[6:7, cols]
        o_ref[b, :, cols] = (yn + bonus_ref[b, :, cols]) * gate_ref[b, :, cols]


def _rwkv_scan(rp, yq, bonus, gate, g, h, pvec, batch):
    t, width = rp.shape
    n_pairs = width // PAIR
    seq = t // batch
    tt = TT_RWKV
    nc = tt // CHUNK
    tok = pl.BlockSpec((batch, tt, width), lambda i: (0, i, 0))
    as_seq = lambda z: z.reshape(batch, seq, width)
    out = pl.pallas_call(
        _rwkv_scan_kernel,
        grid=(seq // tt,),
        in_specs=[
            tok, tok, tok, tok,
            pl.BlockSpec((batch, nc, n_pairs, PAIR, PAIR), lambda i: (0, i, 0, 0, 0)),
            pl.BlockSpec((batch, nc, n_pairs, HEAD, PAIR), lambda i: (0, i, 0, 0, 0)),
            pl.BlockSpec((8, width), lambda i: (0, 0)),
        ],
        out_specs=tok,
        out_shape=jax.ShapeDtypeStruct((batch, seq, width), F32),
        scratch_shapes=[pltpu.VMEM((batch, n_pairs, HEAD, PAIR), F32),
                        pltpu.VMEM((batch, tt, width), F32)],
        compiler_params=_params(("arbitrary",)),
        name="rwkv_scan",
    )(as_seq(rp), as_seq(yq), as_seq(bonus), as_seq(gate),
      g.reshape(batch, seq // CHUNK, n_pairs, PAIR, PAIR),
      h.reshape(batch, seq // CHUNK, n_pairs, HEAD, PAIR), pvec)
    return out.reshape(t, width)


def _out_proj_kernel(x_ref, ya_ref, yb_ref, wo_ref, g_ref, rwt_ref, rb_ref, h1_ref, hn_ref, lg_ref):
    wa = ya_ref.shape[1]
    h1 = (x_ref[...] + _dot(ya_ref[...].astype(BF16), wo_ref[0:wa, :])
          + _dot(yb_ref[...].astype(BF16), wo_ref[wa:, :]))
    h1_ref[...] = h1
    hn = h1 * lax.rsqrt(jnp.mean(h1 * h1, axis=-1, keepdims=True) + RMS_EPS) * g_ref[...]
    hn_ref[...] = hn
    lg_ref[...] = _dot3_nt(rwt_ref[...], hn) + rb_ref[...]


def _out_proj(x2, ya, yb, wo_bf, g2, rwt, rb):
    t, d = x2.shape
    wa = ya.shape[1]
    wb = yb.shape[1]
    ne = rwt.shape[0]
    tm = TM_PROJ
    return pl.pallas_call(
        _out_proj_kernel,
        grid=(t // tm,),
        in_specs=[
            pl.BlockSpec((tm, d), lambda i: (i, 0)),
            pl.BlockSpec((tm, wa), lambda i: (i, 0)),
            pl.BlockSpec((tm, wb), lambda i: (i, 0)),
            pl.BlockSpec((wa + wb, d), lambda i: (0, 0)),
            pl.BlockSpec((1, d), lambda i: (0, 0)),
            pl.BlockSpec((ne, d), lambda i: (0, 0)),
            pl.BlockSpec((ne, 1), lambda i: (0, 0)),
        ],
        out_specs=[
            pl.BlockSpec((tm, d), lambda i: (i, 0)),
            pl.BlockSpec((tm, d), lambda i: (i, 0)),
            pl.BlockSpec((ne, tm), lambda i: (0, i)),
        ],
        out_shape=[
            jax.ShapeDtypeStruct((t, d), F32),
            jax.ShapeDtypeStruct((t, d), F32),
            jax.ShapeDtypeStruct((ne, t), F32),
        ],
        compiler_params=_params(("arbitrary",)),
        name="out_proj",
    )(x2, ya, yb, wo_bf, g2, rwt, rb)


def _route_kernel(lg_ref, pos_ref, gate_ref, pend_ref, carry_scr, pstart_scr):
    phase = pl.program_id(0)
    first = pl.program_id(1) == 0

    @pl.when(first & (phase == 0))
    def _():
        carry_scr[...] = jnp.zeros_like(carry_scr)
        pstart_scr[...] = jnp.zeros_like(pstart_scr)
        pend_ref[...] = jnp.zeros_like(pend_ref)

    @pl.when(first & (phase == 1))
    def _():
        counts = carry_scr[...]
        padded = jnp.ceil(counts * (1.0 / BM_MOE)) * BM_MOE
        row = lax.broadcasted_iota(I32, counts.shape, 0)
        end = padded
        s = 1
        while s < counts.shape[0]:
            end = end + jnp.where(row >= s, pltpu.roll(end, s, 0), 0.0)
            s *= 2
        pstart_scr[...] = end - padded
        pend_ref[...] = end.astype(I32)
        carry_scr[...] = jnp.zeros_like(carry_scr)

    l = lg_ref[...]
    ne, tr = l.shape
    e_iota = lax.broadcasted_iota(I32, (ne, tr), 0)
    chosen = jnp.zeros((ne, tr), F32)
    vals, sels = [], []
    for j in range(TOP_K):
        m = jnp.max(l, axis=0, keepdims=True)
        idx = jnp.min(jnp.where(l == m, e_iota, ne), axis=0, keepdims=True)
        sel = e_iota == idx
        vals.append(m)
        sels.append(sel)
        chosen = jnp.where(sel, 1.0, chosen)
        l = jnp.where(sel, -jnp.inf, l)
    ex = [jnp.exp(vj - vals[0]) for vj in vals]
    den = ex[0] + ex[1] + ex[2] + ex[3]
    for j in range(TOP_K):
        gate_ref[j:j + 1, :] = ex[j] / den

    ti = lax.broadcasted_iota(I32, (tr, tr), 0)
    tj = lax.broadcasted_iota(I32, (tr, tr), 1)
    upper = jnp.where(ti <= tj, 1.0, 0.0).astype(BF16)
    inc = _dot(chosen.astype(BF16), upper)
    carry = carry_scr[...]
    row_of = inc - chosen + carry[:, 0:1] + pstart_scr[:, 0:1]
    for j in range(TOP_K):
        pj = jnp.sum(jnp.where(sels[j], row_of, 0.0), axis=0, keepdims=True)
        pos_ref[j:j + 1, :] = pj.astype(I32)
    carry_scr[...] = carry + inc[:, tr - 1:tr]


def _route(logits_t):
    ne, t = logits_t.shape
    tr = TR_ROUTE
    tok = pl.BlockSpec((TOP_K, tr), lambda ph, i: (0, i * ph))
    return pl.pallas_call(
        _route_kernel,
        grid=(2, t // tr),
        in_specs=[pl.BlockSpec((ne, tr), lambda ph, i: (0, i))],
        out_specs=[tok, tok, pl.BlockSpec((ne, 128), lambda ph, i: (0, 0))],
        out_shape=[
            jax.ShapeDtypeStruct((TOP_K, t), I32),
            jax.ShapeDtypeStruct((TOP_K, t), F32),
            jax.ShapeDtypeStruct((ne, 128), I32),
        ],
        scratch_shapes=[pltpu.VMEM((ne, 128), F32), pltpu.VMEM((ne, 128), F32)],
        compiler_params=_params(("arbitrary", "arbitrary")),
        name="route",
    )(logits_t)


def _dispatch_kernel(pend_ref, nused_ref, pos_ref, hn_ref, xin_ref, zero_scr, sem):
    bm = zero_scr.shape[0]
    td = pos_ref.shape[1]
    nb = xin_ref.shape[0] // bm
    i = pl.program_id(0)

    def zero_block(row0):
        cp = pltpu.make_async_copy(zero_scr, xin_ref.at[pl.ds(pl.multiple_of(row0, bm), bm)], sem)
        cp.start()
        cp.wait()

    @pl.when(i == 0)
    def _():
        zero_scr[...] = jnp.zeros_like(zero_scr)

        def per_expert(e, carry):
            end = pend_ref[e]

            @pl.when(end >= bm)
            def _():
                zero_block(end - bm)
            return carry

        lax.fori_loop(0, pend_ref.shape[0], per_expert, 0)

        def per_tail(blk, carry):
            zero_block(blk * bm)
            return carry

        lax.fori_loop(nused_ref[0], nb, per_tail, 0)

    def issue(tk, carry):
        for j in range(TOP_K):
            pltpu.make_async_copy(hn_ref.at[pl.ds(tk, 1)], xin_ref.at[pl.ds(pos_ref[j, tk], 1)],
                                  sem).start()
        return carry

    lax.fori_loop(0, td, issue, 0, unroll=8)

    for j in range(TOP_K):
        pltpu.make_async_copy(hn_ref, xin_ref.at[pl.ds(0, td)], sem).wait()


def _dispatch(pad_end, n_used, pos, hn, n_rows):
    t, d = hn.shape
    td = TD_DISPATCH
    grid_spec = pltpu.PrefetchScalarGridSpec(
        num_scalar_prefetch=2,
        grid=(t // td,),
        in_specs=[
            pl.BlockSpec((TOP_K, td), lambda i, pe, nu: (0, i), memory_space=pltpu.SMEM),
            pl.BlockSpec((td, d), lambda i, pe, nu: (i, 0)),
        ],
        out_specs=pl.BlockSpec(memory_space=pl.ANY),
        scratch_shapes=[pltpu.VMEM((BM_MOE, d), F32), pltpu.SemaphoreType.DMA(())],
    )
    return pl.pallas_call(
        _dispatch_kernel,
        grid_spec=grid_spec,
        out_shape=jax.ShapeDtypeStruct((n_rows, d), F32),
        compiler_params=_params(("arbitrary",)),
        name="dispatch",
    )(pad_end, n_used, pos, hn)


def _experts_kernel(be_ref, nused_ref, x_ref, w1_ref, b1_ref, w2_ref, b2_ref, o_ref,
                    wt_scr, wg_scr, wl_scr, w2_scr):
    b = pl.program_id(0)
    live = b < nused_ref[0]
    dh = w2_ref.shape[1]
    fresh = (b == 0) | (be_ref[b] != be_ref[jnp.maximum(b - 1, 0)])

    @pl.when(live & fresh)
    def _():
        n_slab, n_t, lanes = wt_scr.shape
        for part in range(2 * dh // n_t):
            rows = slice(part * (n_t // 2), (part + 1) * (n_t // 2))
            for s in range(n_slab):
                cols = slice(s * lanes, (s + 1) * lanes)
                wt_scr[s] = w1_ref[0, cols, part * n_t:(part + 1) * n_t].T
                wg_scr[rows, cols] = wt_scr[s, pl.ds(0, n_t // 2, stride=2), :].astype(BF16)
                wl_scr[rows, cols] = wt_scr[s, pl.ds(1, n_t // 2, stride=2), :].astype(BF16)
        w2_scr[...] = w2_ref[0].astype(BF16)

    @pl.when(live)
    def _():
        xb = x_ref[...].astype(BF16)
        bias = b1_ref[0]
        glu = jnp.minimum(_dot_nt(xb, wg_scr[...]) + bias[:, :dh], SWIGLU_LIMIT)
        lin = jnp.clip(_dot_nt(xb, wl_scr[...]) + bias[:, dh:], -SWIGLU_LIMIT, SWIGLU_LIMIT)
        act = glu * jax.nn.sigmoid(SWIGLU_ALPHA * glu) * (lin + 1.0)
        o_ref[...] = _dot(act.astype(BF16), w2_scr[...]) + b2_ref[0]

    @pl.when(jnp.logical_not(live))
    def _():
        o_ref[...] = jnp.zeros_like(o_ref)


def _experts(block_expert, n_used, xin, w1, b1p, w2, b2):
    n_rows, d = xin.shape
    ne, _, dh2 = w1.shape
    dh = dh2 // 2
    bm = BM_MOE
    grid_spec = pltpu.PrefetchScalarGridSpec(
        num_scalar_prefetch=2,
        grid=(n_rows // bm,),
        in_specs=[
            pl.BlockSpec((bm, d), lambda b, be, nu: (b, 0)),
            pl.BlockSpec((1, d, dh2), lambda b, be, nu: (be[b], 0, 0)),
            pl.BlockSpec((1, 1, dh2), lambda b, be, nu: (be[b], 0, 0)),
            pl.BlockSpec((1, dh, d), lambda b, be, nu: (be[b], 0, 0)),
            pl.BlockSpec((1, 1, d), lambda b, be, nu: (be[b], 0, 0)),
        ],
        out_specs=pl.BlockSpec((bm, d), lambda b, be, nu: (b, 0)),
        scratch_shapes=[
            pltpu.VMEM((d // 128, dh, 128), F32),
            pltpu.VMEM((dh, d), BF16),
            pltpu.VMEM((dh, d), BF16),
            pltpu.VMEM((dh, d), BF16),
        ],
    )
    return pl.pallas_call(
        _experts_kernel,
        grid_spec=grid_spec,
        out_shape=jax.ShapeDtypeStruct((n_rows, d), F32),
        compiler_params=_params(("arbitrary",)),
        name="experts",
    )(block_expert, n_used, xin, w1, b1p, w2, b2)


def _combine_kernel(pos_ref, gate_ref, h1_ref, fg_ref, yout_ref, o_ref, ybuf, sem):
    tc = h1_ref.shape[0]

    def issue(tk, carry):
        for j in range(TOP_K):
            pltpu.make_async_copy(yout_ref.at[pl.ds(pos_ref[j, tk], 1)],
                                  ybuf.at[pl.ds(j * tc + tk, 1)], sem).start()
        return carry

    lax.fori_loop(0, tc, issue, 0, unroll=8)
    pltpu.make_async_copy(yout_ref.at[pl.ds(0, TOP_K * tc)], ybuf, sem).wait()

    gates = gate_ref[...]
    h = h1_ref[...]
    for j in range(TOP_K):
        h = h + ybuf[j * tc:(j + 1) * tc, :] * gates[:, j:j + 1]
    o_ref[...] = h * lax.rsqrt(jnp.mean(h * h, axis=-1, keepdims=True) + RMS_EPS) * fg_ref[...]


def _combine(pos, gates_t, h1, fg, yout):
    t, d = h1.shape
    tc = TC_COMBINE
    return pl.pallas_call(
        _combine_kernel,
        grid=(t // tc,),
        in_specs=[
            pl.BlockSpec((TOP_K, tc), lambda i: (0, i), memory_space=pltpu.SMEM),
            pl.BlockSpec((tc, TOP_K), lambda i: (i, 0)),
            pl.BlockSpec((tc, d), lambda i: (i, 0)),
            pl.BlockSpec((1, d), lambda i: (0, 0)),
            pl.BlockSpec(memory_space=pl.ANY),
        ],
        out_specs=pl.BlockSpec((tc, d), lambda i: (i, 0)),
        out_shape=jax.ShapeDtypeStruct((t, d), F32),
        scratch_shapes=[pltpu.VMEM((TOP_K * tc, d), F32), pltpu.SemaphoreType.DMA(())],
        compiler_params=_params(("arbitrary",)),
        name="combine",
    )(pos, gates_t, h1, fg, yout)


def kernel(x, norm1_g, w_in, sgu_ln_g, sgu_ln_b, sgu_w, sgu_b, mu_rkv, mu_wag, decay_w0, decay_w1,
           decay_w2, iclr_a0, iclr_a1, iclr_a2, gate_g1, gate_g2, k_k, k_a, r_k, lnx_g, lnx_b, w_out,
           norm2_g, router_w, router_b, moe_w1, moe_b1, moe_w2, moe_b2, final_g):
    batch, seq, d = x.shape
    t = batch * seq
    depth = w_in.shape[0]
    sgu_width = sgu_ln_g.shape[1]
    rw = mu_rkv.shape[2]
    n_dec, n_icl, n_gate = decay_w1.shape[2], iclr_a1.shape[2], gate_g1.shape[2]
    assert n_dec == HEAD and n_icl == HEAD and n_gate == PAIR and rw % PAIR == 0
    assert seq % TT_RWKV == 0 and t % TM_PROJ == 0 and sgu_w.shape[2] == SGU_BLOCK
    assert depth == 1, "the final RMSNorm is fused into the last layer's combine kernel"

    h = x.reshape(t, d)
    for l in range(depth):
        win_bf = w_in[l].astype(BF16)
        wl = jnp.concatenate([decay_w1[l], iclr_a1[l], gate_g1[l]], axis=1)
        mucat = jnp.concatenate([
            jnp.broadcast_to(mu_wag[l, 0][:, None], (d, n_dec)),
            jnp.broadcast_to(mu_wag[l, 1][:, None], (d, n_icl)),
            jnp.broadcast_to(mu_wag[l, 2][:, None], (d, n_gate))], axis=1)
        zeros = jnp.zeros((HEAD, rw), F32)
        w2pad = jnp.concatenate([decay_w2[l], zeros], axis=0)
        a2pad = jnp.concatenate([zeros, iclr_a2[l]], axis=0)
        pvec = jnp.stack([decay_w0[l], iclr_a0[l], k_k[l], k_a[l], r_k[l].reshape(-1),
                          lnx_g[l], lnx_b[l], jnp.zeros((rw,), F32)], axis=0)
        bias2d = jnp.repeat(sgu_b[l].T, sgu_width // sgu_b.shape[1], axis=1)
        wo_bf = w_out[l].astype(BF16)
        b1p = jnp.concatenate([moe_b1[l][:, 0::2], moe_b1[l][:, 1::2]], axis=-1)[:, None, :]
        b2 = moe_b2[l][:, None, :]

        proj, lora = _in_proj(h, norm1_g[l][None, :], win_bf, wl, mucat)
        ya = _sgu(proj, sgu_ln_g[l][None, :], sgu_ln_b[l][None, :], sgu_w[l], bias2d)
        rp, yq, bonus, gate, g_m, h_m = _rwkv_prep(proj, lora, mu_rkv[l], pvec, w2pad, a2pad,
                                                   gate_g2[l], seq, 2 * sgu_width)
        yb = _rwkv_scan(rp, yq, bonus, gate, g_m, h_m, pvec, batch)
        h1, hn2, logits_t = _out_proj(h, ya, yb, wo_bf, norm2_g[l][None, :], router_w[l].T,
                                      router_b[l][:, None])
        pos, gates, pend = _route(logits_t)

        pad_end = pend[:, 0]
        n_blocks = (t * TOP_K) // BM_MOE + N_EXPERTS
        n_used = pad_end[-1:] // BM_MOE
        block_start = jnp.arange(n_blocks, dtype=I32) * BM_MOE
        block_expert = jnp.minimum(
            jnp.sum((pad_end[None, :] <= block_start[:, None]).astype(I32), axis=1), N_EXPERTS - 1)

        xin = _dispatch(pad_end, n_used, pos, hn2, n_blocks * BM_MOE)
        yout = _experts(block_expert, n_used, xin, moe_w1[l], b1p, moe_w2[l], b2)
        h = _combine(pos, gates.T, h1, final_g[None, :], yout)
    return h.reshape(batch, seq, d)
```

```python
import functools

import jax
import jax.numpy as jnp
from jax import lax
from jax.experimental import pallas as pl
from jax.experimental.pallas import tpu as pltpu

F32 = jnp.float32
BF16 = jnp.bfloat16
I32 = jnp.int32

RMS_EPS = 1e-5
LN_EPS = 1e-5
LNX_EPS = 64e-5
CHUNK = 64
SGU_BLOCK = 128
HEAD = 64
PAIR = 2 * HEAD
N_EXPERTS = 32
TOP_K = 4
SWIGLU_ALPHA = 1.702
SWIGLU_LIMIT = 7.0

V7X_VMEM_LIMIT = 56 * 1024 * 1024

TM_PROJ = 512
TM_SGU = 512
TT_RWKV = 256
TR_ROUTE = 512
BM_MOE = 256
TD_DISPATCH = 256
TC_COMBINE = 256


def _dot(a, b):
    return jnp.dot(a, b, preferred_element_type=F32)


def _dot_nt(a, b):
    return lax.dot_general(a, b, (((1,), (1,)), ((), ())), preferred_element_type=F32)


def _split(x):
    hi = x.astype(BF16)
    lo = (x - hi.astype(F32)).astype(BF16)
    return hi, lo


def _dot3(a, b):
    ah, al = _split(a)
    bh, bl = _split(b)
    return _dot(ah, bh) + _dot(al, bh) + _dot(ah, bl)


def _dot3_nt(a, b):
    ah, al = _split(a)
    bh, bl = _split(b)
    return _dot_nt(ah, bh) + _dot_nt(al, bh) + _dot_nt(ah, bl)


def _head_sum(x, bd):
    hi, lo = _split(x)
    return _dot(hi, bd) + _dot(lo, bd)


SUBLANES = 8
LANES = 128


def _store_token_tiles(ref, x):
    n = x.shape[0]
    for s in range(SUBLANES):
        ref[pl.ds(s, n, stride=SUBLANES), :] = x[:, s * LANES:(s + 1) * LANES]


def _load_token_tiles(ref, row0, n):
    return jnp.concatenate(
        [ref[pl.ds(row0 + s, n, stride=SUBLANES), :] for s in range(SUBLANES)], axis=1)


def _params(sem, vmem=V7X_VMEM_LIMIT):
    return pltpu.CompilerParams(dimension_semantics=sem, vmem_limit_bytes=vmem)


def _in_proj_kernel(x_ref, g_ref, win_ref, wl_ref, mucat_ref, proj_ref, lora_ref, wl_scr):
    @pl.when(pl.program_id(0) == 0)
    def _():
        wl = wl_ref[...]
        mu = mucat_ref[...]
        n = wl.shape[1]
        wl_scr[:, 0:n] = (wl * (1.0 - mu)).astype(BF16)
        wl_scr[:, n:2 * n] = (wl * mu).astype(BF16)

    x = x_ref[...]
    hn = x * lax.rsqrt(jnp.mean(x * x, axis=-1, keepdims=True) + RMS_EPS) * g_ref[...]
    hb = hn.astype(BF16)
    proj_ref[...] = _dot(hb, win_ref[...])
    lora_ref[...] = _dot(hb, wl_scr[...])


def _in_proj(x2, g, win_bf, wl, mucat):
    t, d = x2.shape
    n_in = win_bf.shape[1]
    n_l = wl.shape[1]
    tm = TM_PROJ
    return pl.pallas_call(
        _in_proj_kernel,
        grid=(t // tm,),
        in_specs=[
            pl.BlockSpec((tm, d), lambda i: (i, 0)),
            pl.BlockSpec((1, d), lambda i: (0, 0)),
            pl.BlockSpec((d, n_in), lambda i: (0, 0)),
            pl.BlockSpec((d, n_l), lambda i: (0, 0)),
            pl.BlockSpec((d, n_l), lambda i: (0, 0)),
        ],
        out_specs=[
            pl.BlockSpec((tm, n_in), lambda i: (i, 0)),
            pl.BlockSpec((tm, 2 * n_l), lambda i: (i, 0)),
        ],
        out_shape=[
            jax.ShapeDtypeStruct((t, n_in), F32),
            jax.ShapeDtypeStruct((t, 2 * n_l), F32),
        ],
        scratch_shapes=[pltpu.VMEM((d, 2 * n_l), BF16)],
        compiler_params=_params(("arbitrary",)),
        name="in_proj",
    )(x2, g, win_bf, wl, mucat)


def _sgu_kernel(z_ref, lng_ref, lnb_ref, w_ref, bias_ref, o_ref, wm_scr):
    n_heads = w_ref.shape[0]

    @pl.when(pl.program_id(0) == 0)
    def _():
        qi = lax.broadcasted_iota(I32, (SGU_BLOCK, SGU_BLOCK), 0) // CHUNK
        kj = lax.broadcasted_iota(I32, (SGU_BLOCK, SGU_BLOCK), 1) // CHUNK
        for h in range(n_heads):
            wm_scr[h] = jnp.where(kj <= qi, w_ref[h], 0.0).astype(BF16)

    z = z_ref[...]
    tm, two_w = z.shape
    width = two_w // 2
    gz = 0.5 * z * (1.0 + lax.erf(z * (2.0 ** -0.5)))
    u = gz[:, :width]
    v = gz[:, width:]
    mu = jnp.mean(v, axis=-1, keepdims=True)
    vc = v - mu
    var = jnp.mean(vc * vc, axis=-1, keepdims=True)
    vn = vc * lax.rsqrt(var + LN_EPS) * lng_ref[...] + lnb_ref[...]
    lane = lax.broadcasted_iota(I32, (SGU_BLOCK, PAIR), 1)
    is_lo = lane < HEAD
    bias = bias_ref[...]
    for blk in range(tm // SGU_BLOCK):
        rows = slice(blk * SGU_BLOCK, (blk + 1) * SGU_BLOCK)
        for p in range(width // PAIR):
            cols = slice(p * PAIR, (p + 1) * PAIR)
            vp = vn[rows, cols]
            lo = jnp.where(is_lo, vp, 0.0).astype(BF16)
            hi = jnp.where(is_lo, 0.0, vp).astype(BF16)
            sv = _dot(wm_scr[2 * p], lo) + _dot(wm_scr[2 * p + 1], hi)
            o_ref[rows, cols] = u[rows, cols] * (sv + bias[:, cols])


def _sgu(proj, ln_g, ln_b, w_s, bias2d):
    t = proj.shape[0]
    width = ln_g.shape[1]
    n_heads = w_s.shape[0]
    tm = TM_SGU
    return pl.pallas_call(
        _sgu_kernel,
        grid=(t // tm,),
        in_specs=[
            pl.BlockSpec((tm, 2 * width), lambda i: (i, 0)),
            pl.BlockSpec((1, width), lambda i: (0, 0)),
            pl.BlockSpec((1, width), lambda i: (0, 0)),
            pl.BlockSpec((n_heads, SGU_BLOCK, SGU_BLOCK), lambda i: (0, 0, 0)),
            pl.BlockSpec((SGU_BLOCK, width), lambda i: (0, 0)),
        ],
        out_specs=pl.BlockSpec((tm, width), lambda i: (i, 0)),
        out_shape=jax.ShapeDtypeStruct((t, width), F32),
        scratch_shapes=[pltpu.VMEM((n_heads, SGU_BLOCK, SGU_BLOCK), BF16)],
        compiler_params=_params(("arbitrary",)),
        name="sgu",
    )(proj, ln_g, ln_b, w_s, bias2d)


def _rwkv_prep_kernel(seq_len, pr_ref, pk_ref, pv_ref, ppr_ref, ppk_ref, ppv_ref, lo_ref, plo_ref,
                      mu_ref, pvec_ref, w2_ref, a2_ref, g2_ref,
                      rp_ref, yq_ref, bonus_ref, gate_ref, g_ref, h_ref):
    tt = pr_ref.shape[0]
    n_chunks = tt // CHUNK
    i = pl.program_id(0)
    keep = jnp.where((i * tt) % seq_len == 0, 0.0, 1.0)

    def shift(x, prev):
        rolled = pltpu.roll(x, 1, 0)
        first = prev[7:8, :] * keep
        rowc = lax.broadcasted_iota(I32, x.shape, 0)
        return jnp.where(rowc == 0, first, rolled)

    mu = mu_ref[...]

    def shift_mix(ref, pref, m):
        x = ref[...]
        return x + (shift(x, pref[...]) - x) * m

    r = shift_mix(pr_ref, ppr_ref, mu[0:1])
    k = shift_mix(pk_ref, ppk_ref, mu[1:2])
    v = shift_mix(pv_ref, ppv_ref, mu[2:3])

    lo = lo_ref[...]
    plo = plo_ref[...]
    half = lo.shape[1] // 2
    l_all = lo[:, :half] + shift(lo[:, half:], plo[:, half:])
    l_wa = l_all[:, :PAIR]
    l_g = l_all[:, PAIR:]

    pvec = pvec_ref[...]
    w0, a0, k_k, k_a, r_k = pvec[0:1], pvec[1:2], pvec[2:3], pvec[3:4], pvec[4:5]

    dw = _dot(jnp.tanh(l_wa).astype(BF16), w2_ref[...].astype(BF16))
    ia = _dot(l_wa.astype(BF16), a2_ref[...].astype(BF16))
    gate_ref[...] = _dot(jax.nn.sigmoid(l_g).astype(BF16), g2_ref[...].astype(BF16))

    zneg = -(w0 + dw)
    softplus = jnp.maximum(zneg, 0.0) + jnp.log(1.0 + jnp.exp(-jnp.abs(zneg)))
    logw = -jnp.exp(-softplus - 0.5)
    iclr = jax.nn.sigmoid(a0 + ia)

    lane_r = lax.broadcasted_iota(I32, (PAIR, PAIR), 0)
    lane_c = lax.broadcasted_iota(I32, (PAIR, PAIR), 1)
    bd = jnp.where((lane_r // HEAD) == (lane_c // HEAD), 1.0, 0.0).astype(BF16)

    kk = k * k_k
    kk = kk / jnp.maximum(jnp.sqrt(_head_sum(kk * kk, bd)), 1e-12)
    k2 = k * (1.0 + (iclr - 1.0) * k_a)
    a = -kk
    b = kk * iclr
    bonus_ref[...] = _head_sum(r * k2 * r_k, bd) * v

    rc = lax.broadcasted_iota(I32, (tt, PAIR), 0) % CHUNK
    cl = logw
    s = 1
    while s < CHUNK:
        cl = cl + jnp.where(rc >= s, pltpu.roll(cl, s, 0), 0.0)
        s *= 2
    at_all = a * jnp.exp(cl - logw)
    rt_all = r * jnp.exp(cl)
    w_inv = jnp.exp(-cl)
    bt_all = b * w_inv
    kt_all = k2 * w_inv

    lane = lax.broadcasted_iota(I32, (CHUNK, PAIR), 1)
    is_lo = lane < HEAD

    def stack(x):
        return jnp.concatenate([jnp.where(is_lo, x, 0.0), jnp.where(is_lo, 0.0, x)], axis=0)

    n2 = 2 * PAIR
    ri = lax.broadcasted_iota(I32, (n2, n2), 0)
    ci = lax.broadcasted_iota(I32, (n2, n2), 1)
    same_head = ((ri // CHUNK) % 2) == ((ci // CHUNK) % 2)
    t_i = ri % CHUNK
    s_i = ci % CHUNK
    tri = same_head & (s_i < t_i + ri // PAIR)
    eye = lane_r == lane_c

    chunks = range(n_chunks)
    rows = [slice(c * CHUNK, (c + 1) * CHUNK) for c in chunks]
    last = [cl[rows[c]][CHUNK - 1:CHUNK, :] for c in chunks]
    w_rem = [jnp.exp(last[c] - cl[rows[c]]) for c in chunks]
    a_s = [stack(at_all[rows[c]]) for c in chunks]
    r_s = [stack(rt_all[rows[c]]) for c in chunks]
    v_s = [stack(v[rows[c]]) for c in chunks]
    v_sb = [v_s[c].astype(BF16) for c in chunks]
    bk_h = [jnp.concatenate([stack(b[rows[c]] * w_rem[c]), stack(k2[rows[c]] * w_rem[c])],
                            axis=0).astype(BF16) for c in chunks]

    a_all = []
    for c in chunks:
        lhs = jnp.concatenate([a_s[c], r_s[c]], axis=0).astype(BF16)
        rhs = jnp.concatenate([stack(bt_all[rows[c]]), stack(kt_all[rows[c]])], axis=0).astype(BF16)
        a_all.append(jnp.where(tri, _dot_nt(lhs, rhs), 0.0))
    n_k = [a_all[c][:PAIR, :PAIR].astype(BF16) for c in chunks]
    a_rb = [a_all[c][PAIR:, :PAIR].astype(BF16) for c in chunks]
    x = [jnp.concatenate([a_s[c], _dot(a_all[c][:PAIR, PAIR:].astype(BF16), v_sb[c])], axis=1)
         for c in chunks]
    ry0 = [jnp.concatenate([r_s[c], _dot(a_all[c][PAIR:, PAIR:].astype(BF16), v_sb[c])], axis=1)
           for c in chunks]
    steps = CHUNK.bit_length() - 1
    for it in range(steps):
        x = [x[c] + _dot(n_k[c], x[c].astype(BF16)) for c in chunks]
        if it + 1 < steps:
            n_k = [_dot(n_k[c], n_k[c]).astype(BF16) for c in chunks]
    ry = [ry0[c] + _dot(a_rb[c], x[c].astype(BF16)) for c in chunks]
    for c in chunks:
        rp_ref[rows[c], :] = ry[c][:CHUNK, :PAIR] + ry[c][CHUNK:, :PAIR]
        yq_ref[rows[c], :] = ry[c][:CHUNK, PAIR:] + ry[c][CHUNK:, PAIR:]
    p_t = [x[c][:, :PAIR].T.astype(BF16) for c in chunks]
    qv_t = [jnp.concatenate([x[c][:, PAIR:], v_s[c]], axis=0).T.astype(BF16) for c in chunks]
    for c in chunks:
        g_ref[c, 0] = jnp.where(eye, jnp.exp(last[c]), 0.0) + _dot(p_t[c], bk_h[c][:PAIR])
    for c in chunks:
        hbd = _dot(qv_t[c], bk_h[c])
        h_ref[c, 0] = hbd[:HEAD] + hbd[HEAD:]


def _rwkv_prep(proj, lora, mu_rkv, pvec, w2pad, a2pad, g2, seq_len, col0):
    t = proj.shape[0]
    width = mu_rkv.shape[1]
    n_pairs = width // PAIR
    tt = TT_RWKV
    nl = lora.shape[1]
    cb = col0 // PAIR
    wb = width // PAIR

    def prev_rows(i):
        return jnp.maximum(i * (tt // 8) - 1, 0)

    def tok(c):
        return pl.BlockSpec((tt, PAIR), lambda i, p: (i, c + p))

    def prev(c):
        return pl.BlockSpec((8, PAIR), lambda i, p: (prev_rows(i), c + p))

    out_tok = pl.BlockSpec((tt, PAIR), lambda i, p: (i, p))
    tok_shape = jax.ShapeDtypeStruct((t, width), F32)
    return pl.pallas_call(
        functools.partial(_rwkv_prep_kernel, seq_len),
        grid=(t // tt, n_pairs),
        in_specs=[
            tok(cb), tok(cb + wb), tok(cb + 2 * wb),
            prev(cb), prev(cb + wb), prev(cb + 2 * wb),
            pl.BlockSpec((tt, nl), lambda i, p: (i, 0)),
            pl.BlockSpec((8, nl), lambda i, p: (prev_rows(i), 0)),
            pl.BlockSpec((3, PAIR), lambda i, p: (0, p)),
            pl.BlockSpec((8, PAIR), lambda i, p: (0, p)),
            pl.BlockSpec((PAIR, PAIR), lambda i, p: (0, p)),
            pl.BlockSpec((PAIR, PAIR), lambda i, p: (0, p)),
            pl.BlockSpec((PAIR, PAIR), lambda i, p: (0, p)),
        ],
        out_specs=[
            out_tok, out_tok, out_tok, out_tok,
            pl.BlockSpec((tt // CHUNK, 1, PAIR, PAIR), lambda i, p: (i, p, 0, 0)),
            pl.BlockSpec((tt // CHUNK, 1, HEAD, PAIR), lambda i, p: (i, p, 0, 0)),
        ],
        out_shape=[
            tok_shape, tok_shape, tok_shape, tok_shape,
            jax.ShapeDtypeStruct((t // CHUNK, n_pairs, PAIR, PAIR), F32),
            jax.ShapeDtypeStruct((t // CHUNK, n_pairs, HEAD, PAIR), F32),
        ],
        compiler_params=_params(("arbitrary", "arbitrary")),
        name="rwkv_prep",
    )(proj, proj, proj, proj, proj, proj, lora, lora, mu_rkv, pvec, w2pad, a2pad, g2)


def _rwkv_scan_kernel(rp_ref, yq_ref, bonus_ref, gate_ref, g_ref, h_ref, pvec_ref, o_ref, s_scr, y_scr):
    nb, tt, width = rp_ref.shape
    n_pairs = width // PAIR

    @pl.when(pl.program_id(0) == 0)
    def _():
        s_scr[...] = jnp.zeros_like(s_scr)

    lane = lax.broadcasted_iota(I32, (HEAD, PAIR), 1)
    is_lo = lane < HEAD
    lane_r = lax.broadcasted_iota(I32, (PAIR, PAIR), 0)
    lane_c = lax.broadcasted_iota(I32, (PAIR, PAIR), 1)
    bd = jnp.where((lane_r // HEAD) == (lane_c // HEAD), 1.0, 0.0).astype(BF16)

    chains = [(b, p) for b in range(nb) for p in range(n_pairs)]
    state = {ch: s_scr[ch[0], ch[1]] for ch in chains}
    for c in range(tt // CHUNK):
        rows = slice(c * CHUNK, (c + 1) * CHUNK)
        for (b, p) in chains:
            cols = slice(p * PAIR, (p + 1) * PAIR)
            s0 = state[(b, p)]
            s_st = jnp.concatenate([jnp.where(is_lo, s0, 0.0), jnp.where(is_lo, 0.0, s0)], axis=0)
            y_scr[b, rows, cols] = (_dot_nt(rp_ref[b, rows, cols].astype(BF16), s_st.astype(BF16))
                                    + yq_ref[b, rows, cols])
        state = {(b, p): _dot3(state[(b, p)], g_ref[b, c, p]) + h_ref[b, c, p] for (b, p) in chains}
    for (b, p) in chains:
        s_scr[b, p] = state[(b, p)]

    pvec = pvec_ref[...]
    for (b, p) in chains:
        cols = slice(p * PAIR, (p + 1) * PAIR)
        y = y_scr[b, :, cols]
        mu = _head_sum(y, bd) * (1.0 / HEAD)
        yc = y - mu
        var = _head_sum(yc * yc, bd) * (1.0 / HEAD)
        yn = yc * lax.rsqrt(var + LNX_EPS) * pvec[5:6, cols] + pvec[6:7, cols]
        o_ref[b, :, cols] = (yn + bonus_ref[b, :, cols]) * gate_ref[b, :, cols]


def _rwkv_scan(rp, yq, bonus, gate, g, h, pvec, batch):
    t, width = rp.shape
    n_pairs = width // PAIR
    seq = t // batch
    tt = TT_RWKV
    nc = tt // CHUNK
    tok = pl.BlockSpec((batch, tt, width), lambda i: (0, i, 0))
    as_seq = lambda z: z.reshape(batch, seq, width)
    out = pl.pallas_call(
        _rwkv_scan_kernel,
        grid=(seq // tt,),
        in_specs=[
            tok, tok, tok, tok,
            pl.BlockSpec((batch, nc, n_pairs, PAIR, PAIR), lambda i: (0, i, 0, 0, 0)),
            pl.BlockSpec((batch, nc, n_pairs, HEAD, PAIR), lambda i: (0, i, 0, 0, 0)),
            pl.BlockSpec((8, width), lambda i: (0, 0)),
        ],
        out_specs=tok,
        out_shape=jax.ShapeDtypeStruct((batch, seq, width), F32),
        scratch_shapes=[pltpu.VMEM((batch, n_pairs, HEAD, PAIR), F32),
                        pltpu.VMEM((batch, tt, width), F32)],
        compiler_params=_params(("arbitrary",)),
        name="rwkv_scan",
    )(as_seq(rp), as_seq(yq), as_seq(bonus), as_seq(gate),
      g.reshape(batch, seq // CHUNK, n_pairs, PAIR, PAIR),
      h.reshape(batch, seq // CHUNK, n_pairs, HEAD, PAIR), pvec)
    return out.reshape(t, width)


def _out_proj_kernel(x_ref, ya_ref, yb_ref, wo_ref, g_ref, rwt_ref, rb_ref, h1_ref, hn_ref, lg_ref):
    wa = ya_ref.shape[1]
    h1 = (x_ref[...] + _dot(ya_ref[...].astype(BF16), wo_ref[0:wa, :])
          + _dot(yb_ref[...].astype(BF16), wo_ref[wa:, :]))
    h1_ref[...] = h1
    hn = h1 * lax.rsqrt(jnp.mean(h1 * h1, axis=-1, keepdims=True) + RMS_EPS) * g_ref[...]
    _store_token_tiles(hn_ref, hn)
    lg_ref[...] = _dot3_nt(rwt_ref[...], hn) + rb_ref[...]


def _out_proj(x2, ya, yb, wo_bf, g2, rwt, rb):
    t, d = x2.shape
    wa = ya.shape[1]
    wb = yb.shape[1]
    ne = rwt.shape[0]
    tm = TM_PROJ
    return pl.pallas_call(
        _out_proj_kernel,
        grid=(t // tm,),
        in_specs=[
            pl.BlockSpec((tm, d), lambda i: (i, 0)),
            pl.BlockSpec((tm, wa), lambda i: (i, 0)),
            pl.BlockSpec((tm, wb), lambda i: (i, 0)),
            pl.BlockSpec((wa + wb, d), lambda i: (0, 0)),
            pl.BlockSpec((1, d), lambda i: (0, 0)),
            pl.BlockSpec((ne, d), lambda i: (0, 0)),
            pl.BlockSpec((ne, 1), lambda i: (0, 0)),
        ],
        out_specs=[
            pl.BlockSpec((tm, d), lambda i: (i, 0)),
            pl.BlockSpec((tm * SUBLANES, LANES), lambda i: (i, 0)),
            pl.BlockSpec((ne, tm), lambda i: (0, i)),
        ],
        out_shape=[
            jax.ShapeDtypeStruct((t, d), F32),
            jax.ShapeDtypeStruct((t * SUBLANES, LANES), F32),
            jax.ShapeDtypeStruct((ne, t), F32),
        ],
        compiler_params=_params(("arbitrary",)),
        name="out_proj",
    )(x2, ya, yb, wo_bf, g2, rwt, rb)


def _route_kernel(lg_ref, pos_ref, gate_ref, pend_ref, carry_scr, pstart_scr):
    phase = pl.program_id(0)
    first = pl.program_id(1) == 0

    @pl.when(first & (phase == 0))
    def _():
        carry_scr[...] = jnp.zeros_like(carry_scr)
        pstart_scr[...] = jnp.zeros_like(pstart_scr)
        pend_ref[...] = jnp.zeros_like(pend_ref)

    @pl.when(first & (phase == 1))
    def _():
        counts = carry_scr[...]
        padded = jnp.ceil(counts * (1.0 / BM_MOE)) * BM_MOE
        row = lax.broadcasted_iota(I32, counts.shape, 0)
        end = padded
        s = 1
        while s < counts.shape[0]:
            end = end + jnp.where(row >= s, pltpu.roll(end, s, 0), 0.0)
            s *= 2
        pstart_scr[...] = end - padded
        pend_ref[...] = end.astype(I32)
        carry_scr[...] = jnp.zeros_like(carry_scr)

    l = lg_ref[...]
    ne, tr = l.shape
    e_iota = lax.broadcasted_iota(I32, (ne, tr), 0)
    chosen = jnp.zeros((ne, tr), F32)
    vals, sels = [], []
    for j in range(TOP_K):
        m = jnp.max(l, axis=0, keepdims=True)
        idx = jnp.min(jnp.where(l == m, e_iota, ne), axis=0, keepdims=True)
        sel = e_iota == idx
        vals.append(m)
        sels.append(sel)
        chosen = jnp.where(sel, 1.0, chosen)
        l = jnp.where(sel, -jnp.inf, l)
    ex = [jnp.exp(vj - vals[0]) for vj in vals]
    den = ex[0] + ex[1] + ex[2] + ex[3]
    for j in range(TOP_K):
        gate_ref[j:j + 1, :] = ex[j] / den

    ti = lax.broadcasted_iota(I32, (tr, tr), 0)
    tj = lax.broadcasted_iota(I32, (tr, tr), 1)
    upper = jnp.where(ti <= tj, 1.0, 0.0).astype(BF16)
    inc = _dot(chosen.astype(BF16), upper)
    carry = carry_scr[...]
    row_of = inc - chosen + carry[:, 0:1] + pstart_scr[:, 0:1]
    for j in range(TOP_K):
        pj = jnp.sum(jnp.where(sels[j], row_of, 0.0), axis=0, keepdims=True)
        pos_ref[j:j + 1, :] = pj.astype(I32)
    carry_scr[...] = carry + inc[:, tr - 1:tr]


def _route(logits_t):
    ne, t = logits_t.shape
    tr = TR_ROUTE
    tok = pl.BlockSpec((TOP_K, tr), lambda ph, i: (0, i * ph))
    return pl.pallas_call(
        _route_kernel,
        grid=(2, t // tr),
        in_specs=[pl.BlockSpec((ne, tr), lambda ph, i: (0, i))],
        out_specs=[tok, tok, pl.BlockSpec((ne, 128), lambda ph, i: (0, 0))],
        out_shape=[
            jax.ShapeDtypeStruct((TOP_K, t), I32),
            jax.ShapeDtypeStruct((TOP_K, t), F32),
            jax.ShapeDtypeStruct((ne, 128), I32),
        ],
        scratch_shapes=[pltpu.VMEM((ne, 128), F32), pltpu.VMEM((ne, 128), F32)],
        compiler_params=_params(("arbitrary", "arbitrary")),
        name="route",
    )(logits_t)


def _dispatch_kernel(pend_ref, nused_ref, pos_ref, hn_ref, xin_ref, zero_scr, sem):
    bm = zero_scr.shape[0] // SUBLANES
    td = pos_ref.shape[1]
    nb = xin_ref.shape[0] // (bm * SUBLANES)
    i = pl.program_id(0)

    def tile(ref, token):
        return ref.at[pl.ds(pl.multiple_of(token * SUBLANES, SUBLANES), SUBLANES)]

    def zero_block(row0):
        start = pl.multiple_of(row0 * SUBLANES, bm * SUBLANES)
        cp = pltpu.make_async_copy(zero_scr, xin_ref.at[pl.ds(start, bm * SUBLANES)], sem)
        cp.start()
        cp.wait()

    @pl.when(i == 0)
    def _():
        zero_scr[...] = jnp.zeros_like(zero_scr)

        def per_expert(e, carry):
            end = pend_ref[e]

            @pl.when(end >= bm)
            def _():
                zero_block(end - bm)
            return carry

        lax.fori_loop(0, pend_ref.shape[0], per_expert, 0)

        def per_tail(blk, carry):
            zero_block(blk * bm)
            return carry

        lax.fori_loop(nused_ref[0], nb, per_tail, 0)

    def issue(tk, carry):
        for j in range(TOP_K):
            pltpu.make_async_copy(tile(hn_ref, tk), tile(xin_ref, pos_ref[j, tk]), sem).start()
        return carry

    lax.fori_loop(0, td, issue, 0, unroll=8)

    for j in range(TOP_K):
        pltpu.make_async_copy(hn_ref, xin_ref.at[pl.ds(0, td * SUBLANES)], sem).wait()


def _dispatch(pad_end, n_used, pos, hn_tiles, n_rows):
    t = hn_tiles.shape[0] // SUBLANES
    td = TD_DISPATCH
    grid_spec = pltpu.PrefetchScalarGridSpec(
        num_scalar_prefetch=2,
        grid=(t // td,),
        in_specs=[
            pl.BlockSpec((TOP_K, td), lambda i, pe, nu: (0, i), memory_space=pltpu.SMEM),
            pl.BlockSpec((td * SUBLANES, LANES), lambda i, pe, nu: (i, 0)),
        ],
        out_specs=pl.BlockSpec(memory_space=pl.ANY),
        scratch_shapes=[pltpu.VMEM((BM_MOE * SUBLANES, LANES), F32), pltpu.SemaphoreType.DMA(())],
    )
    return pl.pallas_call(
        _dispatch_kernel,
        grid_spec=grid_spec,
        out_shape=jax.ShapeDtypeStruct((n_rows * SUBLANES, LANES), F32),
        compiler_params=_params(("arbitrary",)),
        name="dispatch",
    )(pad_end, n_used, pos, hn_tiles)


def _experts_kernel(be_ref, nused_ref, x_ref, w1_ref, b1_ref, w2_ref, b2_ref, o_ref,
                    wt_scr, wg_scr, wl_scr, w2_scr):
    b = pl.program_id(0)
    live = b < nused_ref[0]
    dh = w2_ref.shape[1]
    fresh = (b == 0) | (be_ref[b] != be_ref[jnp.maximum(b - 1, 0)])

    @pl.when(live & fresh)
    def _():
        n_slab, n_t, lanes = wt_scr.shape
        for part in range(2 * dh // n_t):
            rows = slice(part * (n_t // 2), (part + 1) * (n_t // 2))
            for s in range(n_slab):
                cols = slice(s * lanes, (s + 1) * lanes)
                wt_scr[s] = w1_ref[0, cols, part * n_t:(part + 1) * n_t].T
                wg_scr[rows, cols] = wt_scr[s, pl.ds(0, n_t // 2, stride=2), :].astype(BF16)
                wl_scr[rows, cols] = wt_scr[s, pl.ds(1, n_t // 2, stride=2), :].astype(BF16)
        w2_scr[...] = w2_ref[0].astype(BF16)

    @pl.when(live)
    def _():
        bm = x_ref.shape[0] // SUBLANES
        xb = _load_token_tiles(x_ref, 0, bm).astype(BF16)
        bias = b1_ref[0]
        glu = jnp.minimum(_dot_nt(xb, wg_scr[...]) + bias[:, :dh], SWIGLU_LIMIT)
        lin = jnp.clip(_dot_nt(xb, wl_scr[...]) + bias[:, dh:], -SWIGLU_LIMIT, SWIGLU_LIMIT)
        act = glu * jax.nn.sigmoid(SWIGLU_ALPHA * glu) * (lin + 1.0)
        _store_token_tiles(o_ref, _dot(act.astype(BF16), w2_scr[...]) + b2_ref[0])

    @pl.when(jnp.logical_not(live))
    def _():
        o_ref[...] = jnp.zeros_like(o_ref)


def _experts(block_expert, n_used, xin, w1, b1p, w2, b2):
    n_rows = xin.shape[0] // SUBLANES
    ne, d, dh2 = w1.shape
    dh = dh2 // 2
    bm = BM_MOE
    grid_spec = pltpu.PrefetchScalarGridSpec(
        num_scalar_prefetch=2,
        grid=(n_rows // bm,),
        in_specs=[
            pl.BlockSpec((bm * SUBLANES, LANES), lambda b, be, nu: (b, 0)),
            pl.BlockSpec((1, d, dh2), lambda b, be, nu: (be[b], 0, 0)),
            pl.BlockSpec((1, 1, dh2), lambda b, be, nu: (be[b], 0, 0)),
            pl.BlockSpec((1, dh, d), lambda b, be, nu: (be[b], 0, 0)),
            pl.BlockSpec((1, 1, d), lambda b, be, nu: (be[b], 0, 0)),
        ],
        out_specs=pl.BlockSpec((bm * SUBLANES, LANES), lambda b, be, nu: (b, 0)),
        scratch_shapes=[
            pltpu.VMEM((d // 128, dh, 128), F32),
            pltpu.VMEM((dh, d), BF16),
            pltpu.VMEM((dh, d), BF16),
            pltpu.VMEM((dh, d), BF16),
        ],
    )
    return pl.pallas_call(
        _experts_kernel,
        grid_spec=grid_spec,
        out_shape=jax.ShapeDtypeStruct((n_rows * SUBLANES, LANES), F32),
        compiler_params=_params(("arbitrary",)),
        name="experts",
    )(block_expert, n_used, xin, w1, b1p, w2, b2)


def _combine_kernel(pos_ref, gate_ref, h1_ref, fg_ref, yout_ref, o_ref, ybuf, sem):
    tc = h1_ref.shape[0]

    def tile(ref, token):
        return ref.at[pl.ds(pl.multiple_of(token * SUBLANES, SUBLANES), SUBLANES)]

    def issue(tk, carry):
        for j in range(TOP_K):
            pltpu.make_async_copy(tile(yout_ref, pos_ref[j, tk]), tile(ybuf, j * tc + tk), sem).start()
        return carry

    lax.fori_loop(0, tc, issue, 0, unroll=8)
    pltpu.make_async_copy(yout_ref.at[pl.ds(0, ybuf.shape[0])], ybuf, sem).wait()

    gates = gate_ref[...]
    h = h1_ref[...]
    for j in range(TOP_K):
        h = h + _load_token_tiles(ybuf, j * tc * SUBLANES, tc) * gates[:, j:j + 1]
    o_ref[...] = h * lax.rsqrt(jnp.mean(h * h, axis=-1, keepdims=True) + RMS_EPS) * fg_ref[...]


def _combine(pos, gates_t, h1, fg, yout):
    t, d = h1.shape
    tc = TC_COMBINE
    return pl.pallas_call(
        _combine_kernel,
        grid=(t // tc,),
        in_specs=[
            pl.BlockSpec((TOP_K, tc), lambda i: (0, i), memory_space=pltpu.SMEM),
            pl.BlockSpec((tc, TOP_K), lambda i: (i, 0)),
            pl.BlockSpec((tc, d), lambda i: (i, 0)),
            pl.BlockSpec((1, d), lambda i: (0, 0)),
            pl.BlockSpec(memory_space=pl.ANY),
        ],
        out_specs=pl.BlockSpec((tc, d), lambda i: (i, 0)),
        out_shape=jax.ShapeDtypeStruct((t, d), F32),
        scratch_shapes=[pltpu.VMEM((TOP_K * tc * SUBLANES, LANES), F32), pltpu.SemaphoreType.DMA(())],
        compiler_params=_params(("arbitrary",)),
        name="combine",
    )(pos, gates_t, h1, fg, yout)


def kernel(x, norm1_g, w_in, sgu_ln_g, sgu_ln_b, sgu_w, sgu_b, mu_rkv, mu_wag, decay_w0, decay_w1,
           decay_w2, iclr_a0, iclr_a1, iclr_a2, gate_g1, gate_g2, k_k, k_a, r_k, lnx_g, lnx_b, w_out,
           norm2_g, router_w, router_b, moe_w1, moe_b1, moe_w2, moe_b2, final_g):
    batch, seq, d = x.shape
    t = batch * seq
    depth = w_in.shape[0]
    sgu_width = sgu_ln_g.shape[1]
    rw = mu_rkv.shape[2]
    n_dec, n_icl, n_gate = decay_w1.shape[2], iclr_a1.shape[2], gate_g1.shape[2]
    assert n_dec == HEAD and n_icl == HEAD and n_gate == PAIR and rw % PAIR == 0
    assert seq % TT_RWKV == 0 and t % TM_PROJ == 0 and sgu_w.shape[2] == SGU_BLOCK
    assert depth == 1, "the final RMSNorm is fused into the last layer's combine kernel"
    assert d == SUBLANES * LANES, "the MoE row movers copy one (8, 128) f32 tile per token"

    h = x.reshape(t, d)
    for l in range(depth):
        win_bf = w_in[l].astype(BF16)
        wl = jnp.concatenate([decay_w1[l], iclr_a1[l], gate_g1[l]], axis=1)
        mucat = jnp.concatenate([
            jnp.broadcast_to(mu_wag[l, 0][:, None], (d, n_dec)),
            jnp.broadcast_to(mu_wag[l, 1][:, None], (d, n_icl)),
            jnp.broadcast_to(mu_wag[l, 2][:, None], (d, n_gate))], axis=1)
        zeros = jnp.zeros((HEAD, rw), F32)
        w2pad = jnp.concatenate([decay_w2[l], zeros], axis=0)
        a2pad = jnp.concatenate([zeros, iclr_a2[l]], axis=0)
        pvec = jnp.stack([decay_w0[l], iclr_a0[l], k_k[l], k_a[l], r_k[l].reshape(-1),
                          lnx_g[l], lnx_b[l], jnp.zeros((rw,), F32)], axis=0)
        bias2d = jnp.repeat(sgu_b[l].T, sgu_width // sgu_b.shape[1], axis=1)
        wo_bf = w_out[l].astype(BF16)
        b1p = jnp.concatenate([moe_b1[l][:, 0::2], moe_b1[l][:, 1::2]], axis=-1)[:, None, :]
        b2 = moe_b2[l][:, None, :]

        proj, lora = _in_proj(h, norm1_g[l][None, :], win_bf, wl, mucat)
        ya = _sgu(proj, sgu_ln_g[l][None, :], sgu_ln_b[l][None, :], sgu_w[l], bias2d)
        rp, yq, bonus, gate, g_m, h_m = _rwkv_prep(proj, lora, mu_rkv[l], pvec, w2pad, a2pad,
                                                   gate_g2[l], seq, 2 * sgu_width)
        yb = _rwkv_scan(rp, yq, bonus, gate, g_m, h_m, pvec, batch)
        h1, hn2, logits_t = _out_proj(h, ya, yb, wo_bf, norm2_g[l][None, :], router_w[l].T,
                                      router_b[l][:, None])
        pos, gates, pend = _route(logits_t)

        pad_end = pend[:, 0]
        n_blocks = (t * TOP_K) // BM_MOE + N_EXPERTS
        n_used = pad_end[-1:] // BM_MOE
        block_start = jnp.arange(n_blocks, dtype=I32) * BM_MOE
        block_expert = jnp.minimum(
            jnp.sum((pad_end[None, :] <= block_start[:, None]).astype(I32), axis=1), N_EXPERTS - 1)

        xin = _dispatch(pad_end, n_used, pos, hn2, n_blocks * BM_MOE)
        yout = _experts(block_expert, n_used, xin, moe_w1[l], b1p, moe_w2[l], b2)
        h = _combine(pos, gates.T, h1, final_g[None, :], yout)
    return h.reshape(batch, seq, d)
```

```python
import functools

import jax
import jax.numpy as jnp
from jax import lax
from jax.experimental import pallas as pl
from jax.experimental.pallas import tpu as pltpu

F32 = jnp.float32
BF16 = jnp.bfloat16
I32 = jnp.int32

RMS_EPS = 1e-5
LN_EPS = 1e-5
LNX_EPS = 64e-5
CHUNK = 64
SGU_BLOCK = 128
HEAD = 64
PAIR = 2 * HEAD
N_EXPERTS = 32
TOP_K = 4
SWIGLU_ALPHA = 1.702
SWIGLU_LIMIT = 7.0

V7X_VMEM_LIMIT = 56 * 1024 * 1024

TM_PROJ = 512
TM_SGU = 512
TT_PREP = 512
TT_RWKV = 256
TR_ROUTE = 512
BM_MOE = 256
TD_DISPATCH = 512
TC_COMBINE = 256


def _dot(a, b):
    return jnp.dot(a, b, preferred_element_type=F32)


def _dot_nt(a, b):
    return lax.dot_general(a, b, (((1,), (1,)), ((), ())), preferred_element_type=F32)


def _split(x):
    hi = x.astype(BF16)
    lo = (x - hi.astype(F32)).astype(BF16)
    return hi, lo


def _dot3(a, b):
    ah, al = _split(a)
    bh, bl = _split(b)
    return _dot(ah, bh) + _dot(al, bh) + _dot(ah, bl)


def _dot3_nt(a, b):
    ah, al = _split(a)
    bh, bl = _split(b)
    return _dot_nt(ah, bh) + _dot_nt(al, bh) + _dot_nt(ah, bl)


def _head_sum(x, bd):
    hi, lo = _split(x)
    return _dot(hi, bd) + _dot(lo, bd)


SUBLANES = 8
LANES = 128


def _store_token_tiles(ref, x):
    n = x.shape[0]
    for s in range(SUBLANES):
        ref[pl.ds(s, n, stride=SUBLANES), :] = x[:, s * LANES:(s + 1) * LANES]


def _load_token_tiles(ref, row0, n):
    return jnp.concatenate(
        [ref[pl.ds(row0 + s, n, stride=SUBLANES), :] for s in range(SUBLANES)], axis=1)


def _params(sem, vmem=V7X_VMEM_LIMIT):
    return pltpu.CompilerParams(dimension_semantics=sem, vmem_limit_bytes=vmem)


def _in_proj_kernel(x_ref, g_ref, win_ref, wl_ref, mucat_ref, proj_ref, lora_ref, wl_scr):
    @pl.when(pl.program_id(0) == 0)
    def _():
        wl = wl_ref[...]
        mu = mucat_ref[...]
        n = wl.shape[1]
        wl_scr[:, 0:n] = (wl * (1.0 - mu)).astype(BF16)
        wl_scr[:, n:2 * n] = (wl * mu).astype(BF16)

    x = x_ref[...]
    hn = x * lax.rsqrt(jnp.mean(x * x, axis=-1, keepdims=True) + RMS_EPS) * g_ref[...]
    hb = hn.astype(BF16)
    proj_ref[...] = _dot(hb, win_ref[...])
    lora_ref[...] = _dot(hb, wl_scr[...])


def _in_proj(x2, g, win_bf, wl, mucat):
    t, d = x2.shape
    n_in = win_bf.shape[1]
    n_l = wl.shape[1]
    tm = TM_PROJ
    return pl.pallas_call(
        _in_proj_kernel,
        grid=(t // tm,),
        in_specs=[
            pl.BlockSpec((tm, d), lambda i: (i, 0)),
            pl.BlockSpec((1, d), lambda i: (0, 0)),
            pl.BlockSpec((d, n_in), lambda i: (0, 0)),
            pl.BlockSpec((d, n_l), lambda i: (0, 0)),
            pl.BlockSpec((d, n_l), lambda i: (0, 0)),
        ],
        out_specs=[
            pl.BlockSpec((tm, n_in), lambda i: (i, 0)),
            pl.BlockSpec((tm, 2 * n_l), lambda i: (i, 0)),
        ],
        out_shape=[
            jax.ShapeDtypeStruct((t, n_in), F32),
            jax.ShapeDtypeStruct((t, 2 * n_l), F32),
        ],
        scratch_shapes=[pltpu.VMEM((d, 2 * n_l), BF16)],
        compiler_params=_params(("arbitrary",)),
        name="in_proj",
    )(x2, g, win_bf, wl, mucat)


def _sgu_kernel(z_ref, lng_ref, lnb_ref, w_ref, bias_ref, o_ref, wm_scr):
    n_heads = w_ref.shape[0]

    @pl.when(pl.program_id(0) == 0)
    def _():
        qi = lax.broadcasted_iota(I32, (SGU_BLOCK, SGU_BLOCK), 0) // CHUNK
        kj = lax.broadcasted_iota(I32, (SGU_BLOCK, SGU_BLOCK), 1) // CHUNK
        for h in range(n_heads):
            wm_scr[h] = jnp.where(kj <= qi, w_ref[h], 0.0).astype(BF16)

    z = z_ref[...]
    tm, two_w = z.shape
    width = two_w // 2
    gz = 0.5 * z * (1.0 + lax.erf(z * (2.0 ** -0.5)))
    u = gz[:, :width]
    v = gz[:, width:]
    mu = jnp.mean(v, axis=-1, keepdims=True)
    vc = v - mu
    var = jnp.mean(vc * vc, axis=-1, keepdims=True)
    vn = vc * lax.rsqrt(var + LN_EPS) * lng_ref[...] + lnb_ref[...]
    lane = lax.broadcasted_iota(I32, (SGU_BLOCK, PAIR), 1)
    is_lo = lane < HEAD
    bias = bias_ref[...]
    for blk in range(tm // SGU_BLOCK):
        rows = slice(blk * SGU_BLOCK, (blk + 1) * SGU_BLOCK)
        for p in range(width // PAIR):
            cols = slice(p * PAIR, (p + 1) * PAIR)
            vp = vn[rows, cols]
            lo = jnp.where(is_lo, vp, 0.0).astype(BF16)
            hi = jnp.where(is_lo, 0.0, vp).astype(BF16)
            sv = _dot(wm_scr[2 * p], lo) + _dot(wm_scr[2 * p + 1], hi)
            o_ref[rows, cols] = u[rows, cols] * (sv + bias[:, cols])


def _sgu(proj, ln_g, ln_b, w_s, bias2d):
    t = proj.shape[0]
    width = ln_g.shape[1]
    n_heads = w_s.shape[0]
    tm = TM_SGU
    return pl.pallas_call(
        _sgu_kernel,
        grid=(t // tm,),
        in_specs=[
            pl.BlockSpec((tm, 2 * width), lambda i: (i, 0)),
            pl.BlockSpec((1, width), lambda i: (0, 0)),
            pl.BlockSpec((1, width), lambda i: (0, 0)),
            pl.BlockSpec((n_heads, SGU_BLOCK, SGU_BLOCK), lambda i: (0, 0, 0)),
            pl.BlockSpec((SGU_BLOCK, width), lambda i: (0, 0)),
        ],
        out_specs=pl.BlockSpec((tm, width), lambda i: (i, 0)),
        out_shape=jax.ShapeDtypeStruct((t, width), F32),
        scratch_shapes=[pltpu.VMEM((n_heads, SGU_BLOCK, SGU_BLOCK), BF16)],
        compiler_params=_params(("arbitrary",)),
        name="sgu",
    )(proj, ln_g, ln_b, w_s, bias2d)


def _rwkv_prep_kernel(seq_len, pr_ref, pk_ref, pv_ref, ppr_ref, ppk_ref, ppv_ref, lo_ref, plo_ref,
                      mu_ref, pvec_ref, w2_ref, a2_ref, g2_ref,
                      rp_ref, yq_ref, bonus_ref, gate_ref, g_ref, h_ref):
    tt = pr_ref.shape[0]
    n_chunks = tt // CHUNK
    i = pl.program_id(0)
    keep = jnp.where((i * tt) % seq_len == 0, 0.0, 1.0)

    def shift(x, prev):
        rolled = pltpu.roll(x, 1, 0)
        first = prev[7:8, :] * keep
        rowc = lax.broadcasted_iota(I32, x.shape, 0)
        return jnp.where(rowc == 0, first, rolled)

    mu = mu_ref[...]

    def shift_mix(ref, pref, m):
        x = ref[...]
        return x + (shift(x, pref[...]) - x) * m

    r = shift_mix(pr_ref, ppr_ref, mu[0:1])
    k = shift_mix(pk_ref, ppk_ref, mu[1:2])
    v = shift_mix(pv_ref, ppv_ref, mu[2:3])

    lo = lo_ref[...]
    plo = plo_ref[...]
    half = lo.shape[1] // 2
    l_all = lo[:, :half] + shift(lo[:, half:], plo[:, half:])
    l_wa = l_all[:, :PAIR]
    l_g = l_all[:, PAIR:]

    pvec = pvec_ref[...]
    w0, a0, k_k, k_a, r_k = pvec[0:1], pvec[1:2], pvec[2:3], pvec[3:4], pvec[4:5]

    dw = _dot(jnp.tanh(l_wa).astype(BF16), w2_ref[...].astype(BF16))
    ia = _dot(l_wa.astype(BF16), a2_ref[...].astype(BF16))
    gate_ref[...] = _dot(jax.nn.sigmoid(l_g).astype(BF16), g2_ref[...].astype(BF16))

    zneg = -(w0 + dw)
    softplus = jnp.maximum(zneg, 0.0) + jnp.log(1.0 + jnp.exp(-jnp.abs(zneg)))
    logw = -jnp.exp(-softplus - 0.5)
    iclr = jax.nn.sigmoid(a0 + ia)

    lane_r = lax.broadcasted_iota(I32, (PAIR, PAIR), 0)
    lane_c = lax.broadcasted_iota(I32, (PAIR, PAIR), 1)
    bd = jnp.where((lane_r // HEAD) == (lane_c // HEAD), 1.0, 0.0).astype(BF16)

    kk = k * k_k
    kk = kk / jnp.maximum(jnp.sqrt(_head_sum(kk * kk, bd)), 1e-12)
    k2 = k * (1.0 + (iclr - 1.0) * k_a)
    a = -kk
    b = kk * iclr
    bonus_ref[...] = _head_sum(r * k2 * r_k, bd) * v

    rc = lax.broadcasted_iota(I32, (tt, PAIR), 0) % CHUNK
    cl = logw
    s = 1
    while s < CHUNK:
        cl = cl + jnp.where(rc >= s, pltpu.roll(cl, s, 0), 0.0)
        s *= 2
    at_all = a * jnp.exp(cl - logw)
    rt_all = r * jnp.exp(cl)
    w_inv = jnp.exp(-cl)
    bt_all = b * w_inv
    kt_all = k2 * w_inv

    lane = lax.broadcasted_iota(I32, (CHUNK, PAIR), 1)
    is_lo = lane < HEAD

    def stack(x):
        return jnp.concatenate([jnp.where(is_lo, x, 0.0), jnp.where(is_lo, 0.0, x)], axis=0)

    n2 = 2 * PAIR
    ri = lax.broadcasted_iota(I32, (n2, n2), 0)
    ci = lax.broadcasted_iota(I32, (n2, n2), 1)
    same_head = ((ri // CHUNK) % 2) == ((ci // CHUNK) % 2)
    t_i = ri % CHUNK
    s_i = ci % CHUNK
    tri = same_head & (s_i < t_i + ri // PAIR)
    eye = lane_r == lane_c

    chunks = range(n_chunks)
    rows = [slice(c * CHUNK, (c + 1) * CHUNK) for c in chunks]
    last = [cl[rows[c]][CHUNK - 1:CHUNK, :] for c in chunks]
    w_rem = [jnp.exp(last[c] - cl[rows[c]]) for c in chunks]
    a_s = [stack(at_all[rows[c]]) for c in chunks]
    r_s = [stack(rt_all[rows[c]]) for c in chunks]
    v_s = [stack(v[rows[c]]) for c in chunks]
    v_sb = [v_s[c].astype(BF16) for c in chunks]
    bk_h = [jnp.concatenate([stack(b[rows[c]] * w_rem[c]), stack(k2[rows[c]] * w_rem[c])],
                            axis=0).astype(BF16) for c in chunks]

    a_all = []
    for c in chunks:
        lhs = jnp.concatenate([a_s[c], r_s[c]], axis=0).astype(BF16)
        rhs = jnp.concatenate([stack(bt_all[rows[c]]), stack(kt_all[rows[c]])], axis=0).astype(BF16)
        a_all.append(jnp.where(tri, _dot_nt(lhs, rhs), 0.0))
    n_k = [a_all[c][:PAIR, :PAIR].astype(BF16) for c in chunks]
    a_rb = [a_all[c][PAIR:, :PAIR].astype(BF16) for c in chunks]
    x = [jnp.concatenate([a_s[c], _dot(a_all[c][:PAIR, PAIR:].astype(BF16), v_sb[c])], axis=1)
         for c in chunks]
    ry0 = [jnp.concatenate([r_s[c], _dot(a_all[c][PAIR:, PAIR:].astype(BF16), v_sb[c])], axis=1)
           for c in chunks]
    steps = CHUNK.bit_length() - 1
    for it in range(steps):
        x = [x[c] + _dot(n_k[c], x[c].astype(BF16)) for c in chunks]
        if it + 1 < steps:
            n_k = [_dot(n_k[c], n_k[c]).astype(BF16) for c in chunks]
    ry = [ry0[c] + _dot(a_rb[c], x[c].astype(BF16)) for c in chunks]
    for c in chunks:
        rp_ref[rows[c], :] = ry[c][:CHUNK, :PAIR] + ry[c][CHUNK:, :PAIR]
        yq_ref[rows[c], :] = ry[c][:CHUNK, PAIR:] + ry[c][CHUNK:, PAIR:]
    p_t = [x[c][:, :PAIR].T.astype(BF16) for c in chunks]
    qv_t = [jnp.concatenate([x[c][:, PAIR:], v_s[c]], axis=0).T.astype(BF16) for c in chunks]
    for c in chunks:
        g_ref[c, 0] = jnp.where(eye, jnp.exp(last[c]), 0.0) + _dot(p_t[c], bk_h[c][:PAIR])
    for c in chunks:
        hbd = _dot(qv_t[c], bk_h[c])
        h_ref[c, 0] = hbd[:HEAD] + hbd[HEAD:]


def _rwkv_prep(proj, lora, mu_rkv, pvec, w2pad, a2pad, g2, seq_len, col0):
    t = proj.shape[0]
    width = mu_rkv.shape[1]
    n_pairs = width // PAIR
    tt = TT_PREP
    nl = lora.shape[1]
    cb = col0 // PAIR
    wb = width // PAIR

    def prev_rows(i):
        return jnp.maximum(i * (tt // 8) - 1, 0)

    def tok(c):
        return pl.BlockSpec((tt, PAIR), lambda i, p: (i, c + p))

    def prev(c):
        return pl.BlockSpec((8, PAIR), lambda i, p: (prev_rows(i), c + p))

    out_tok = pl.BlockSpec((tt, PAIR), lambda i, p: (i, p))
    tok_shape = jax.ShapeDtypeStruct((t, width), F32)
    return pl.pallas_call(
        functools.partial(_rwkv_prep_kernel, seq_len),
        grid=(t // tt, n_pairs),
        in_specs=[
            tok(cb), tok(cb + wb), tok(cb + 2 * wb),
            prev(cb), prev(cb + wb), prev(cb + 2 * wb),
            pl.BlockSpec((tt, nl), lambda i, p: (i, 0)),
            pl.BlockSpec((8, nl), lambda i, p: (prev_rows(i), 0)),
            pl.BlockSpec((3, PAIR), lambda i, p: (0, p)),
            pl.BlockSpec((8, PAIR), lambda i, p: (0, p)),
            pl.BlockSpec((PAIR, PAIR), lambda i, p: (0, p)),
            pl.BlockSpec((PAIR, PAIR), lambda i, p: (0, p)),
            pl.BlockSpec((PAIR, PAIR), lambda i, p: (0, p)),
        ],
        out_specs=[
            out_tok, out_tok, out_tok, out_tok,
            pl.BlockSpec((tt // CHUNK, 1, PAIR, PAIR), lambda i, p: (i, p, 0, 0)),
            pl.BlockSpec((tt // CHUNK, 1, HEAD, PAIR), lambda i, p: (i, p, 0, 0)),
        ],
        out_shape=[
            tok_shape, tok_shape, tok_shape, tok_shape,
            jax.ShapeDtypeStruct((t // CHUNK, n_pairs, PAIR, PAIR), F32),
            jax.ShapeDtypeStruct((t // CHUNK, n_pairs, HEAD, PAIR), F32),
        ],
        compiler_params=_params(("arbitrary", "arbitrary")),
        name="rwkv_prep",
    )(proj, proj, proj, proj, proj, proj, lora, lora, mu_rkv, pvec, w2pad, a2pad, g2)


def _rwkv_scan_kernel(rp_ref, yq_ref, bonus_ref, gate_ref, g_ref, h_ref, pvec_ref, o_ref, s_scr, y_scr):
    nb, tt, width = rp_ref.shape
    n_pairs = width // PAIR

    @pl.when(pl.program_id(0) == 0)
    def _():
        s_scr[...] = jnp.zeros_like(s_scr)

    lane = lax.broadcasted_iota(I32, (HEAD, PAIR), 1)
    is_lo = lane < HEAD
    lane_r = lax.broadcasted_iota(I32, (PAIR, PAIR), 0)
    lane_c = lax.broadcasted_iota(I32, (PAIR, PAIR), 1)
    bd = jnp.where((lane_r // HEAD) == (lane_c // HEAD), 1.0, 0.0).astype(BF16)

    chains = [(b, p) for b in range(nb) for p in range(n_pairs)]
    state = {ch: s_scr[ch[0], ch[1]] for ch in chains}
    for c in range(tt // CHUNK):
        rows = slice(c * CHUNK, (c + 1) * CHUNK)
        for (b, p) in chains:
            cols = slice(p * PAIR, (p + 1) * PAIR)
            s0 = state[(b, p)]
            s_st = jnp.concatenate([jnp.where(is_lo, s0, 0.0), jnp.where(is_lo, 0.0, s0)], axis=0)
            y_scr[b, rows, cols] = (_dot_nt(rp_ref[b, rows, cols].astype(BF16), s_st.astype(BF16))
                                    + yq_ref[b, rows, cols])
        state = {(b, p): _dot3(state[(b, p)], g_ref[b, c, p]) + h_ref[b, c, p] for (b, p) in chains}
    for (b, p) in chains:
        s_scr[b, p] = state[(b, p)]

    pvec = pvec_ref[...]
    for (b, p) in chains:
        cols = slice(p * PAIR, (p + 1) * PAIR)
        y = y_scr[b, :, cols]
        mu = _head_sum(y, bd) * (1.0 / HEAD)
        yc = y - mu
        var = _head_sum(yc * yc, bd) * (1.0 / HEAD)
        yn = yc * lax.rsqrt(var + LNX_EPS) * pvec[5:6, cols] + pvec[6:7, cols]
        o_ref[b, :, cols] = (yn + bonus_ref[b, :, cols]) * gate_ref[b, :, cols]


def _rwkv_scan(rp, yq, bonus, gate, g, h, pvec, batch):
    t, width = rp.shape
    n_pairs = width // PAIR
    seq = t // batch
    tt = TT_RWKV
    nc = tt // CHUNK
    tok = pl.BlockSpec((batch, tt, width), lambda i: (0, i, 0))
    as_seq = lambda z: z.reshape(batch, seq, width)
    out = pl.pallas_call(
        _rwkv_scan_kernel,
        grid=(seq // tt,),
        in_specs=[
            tok, tok, tok, tok,
            pl.BlockSpec((batch, nc, n_pairs, PAIR, PAIR), lambda i: (0, i, 0, 0, 0)),
            pl.BlockSpec((batch, nc, n_pairs, HEAD, PAIR), lambda i: (0, i, 0, 0, 0)),
            pl.BlockSpec((8, width), lambda i: (0, 0)),
        ],
        out_specs=tok,
        out_shape=jax.ShapeDtypeStruct((batch, seq, width), F32),
        scratch_shapes=[pltpu.VMEM((batch, n_pairs, HEAD, PAIR), F32),
                        pltpu.VMEM((batch, tt, width), F32)],
        compiler_params=_params(("arbitrary",)),
        name="rwkv_scan",
    )(as_seq(rp), as_seq(yq), as_seq(bonus), as_seq(gate),
      g.reshape(batch, seq // CHUNK, n_pairs, PAIR, PAIR),
      h.reshape(batch, seq // CHUNK, n_pairs, HEAD, PAIR), pvec)
    return out.reshape(t, width)


def _out_proj_kernel(x_ref, ya_ref, yb_ref, wo_ref, g_ref, rwt_ref, rb_ref, h1_ref, hn_ref, lg_ref):
    wa = ya_ref.shape[1]
    h1 = (x_ref[...] + _dot(ya_ref[...].astype(BF16), wo_ref[0:wa, :])
          + _dot(yb_ref[...].astype(BF16), wo_ref[wa:, :]))
    h1_ref[...] = h1
    hn = h1 * lax.rsqrt(jnp.mean(h1 * h1, axis=-1, keepdims=True) + RMS_EPS) * g_ref[...]
    _store_token_tiles(hn_ref, hn)
    lg_ref[...] = _dot3_nt(rwt_ref[...], hn) + rb_ref[...]


def _out_proj(x2, ya, yb, wo_bf, g2, rwt, rb):
    t, d = x2.shape
    wa = ya.shape[1]
    wb = yb.shape[1]
    ne = rwt.shape[0]
    tm = TM_PROJ
    return pl.pallas_call(
        _out_proj_kernel,
        grid=(t // tm,),
        in_specs=[
            pl.BlockSpec((tm, d), lambda i: (i, 0)),
            pl.BlockSpec((tm, wa), lambda i: (i, 0)),
            pl.BlockSpec((tm, wb), lambda i: (i, 0)),
            pl.BlockSpec((wa + wb, d), lambda i: (0, 0)),
            pl.BlockSpec((1, d), lambda i: (0, 0)),
            pl.BlockSpec((ne, d), lambda i: (0, 0)),
            pl.BlockSpec((ne, 1), lambda i: (0, 0)),
        ],
        out_specs=[
            pl.BlockSpec((tm, d), lambda i: (i, 0)),
            pl.BlockSpec((tm * SUBLANES, LANES), lambda i: (i, 0)),
            pl.BlockSpec((ne, tm), lambda i: (0, i)),
        ],
        out_shape=[
            jax.ShapeDtypeStruct((t, d), F32),
            jax.ShapeDtypeStruct((t * SUBLANES, LANES), F32),
            jax.ShapeDtypeStruct((ne, t), F32),
        ],
        compiler_params=_params(("arbitrary",)),
        name="out_proj",
    )(x2, ya, yb, wo_bf, g2, rwt, rb)


def _route_kernel(lg_ref, pos_ref, gate_ref, pend_ref, carry_scr, pstart_scr):
    phase = pl.program_id(0)
    first = pl.program_id(1) == 0

    @pl.when(first & (phase == 0))
    def _():
        carry_scr[...] = jnp.zeros_like(carry_scr)
        pstart_scr[...] = jnp.zeros_like(pstart_scr)
        pend_ref[...] = jnp.zeros_like(pend_ref)

    @pl.when(first & (phase == 1))
    def _():
        counts = carry_scr[...]
        padded = jnp.ceil(counts * (1.0 / BM_MOE)) * BM_MOE
        row = lax.broadcasted_iota(I32, counts.shape, 0)
        end = padded
        s = 1
        while s < counts.shape[0]:
            end = end + jnp.where(row >= s, pltpu.roll(end, s, 0), 0.0)
            s *= 2
        pstart_scr[...] = end - padded
        pend_ref[...] = end.astype(I32)
        carry_scr[...] = jnp.zeros_like(carry_scr)

    l = lg_ref[...]
    ne, tr = l.shape
    e_iota = lax.broadcasted_iota(I32, (ne, tr), 0)
    chosen = jnp.zeros((ne, tr), F32)
    vals, sels = [], []
    for j in range(TOP_K):
        m = jnp.max(l, axis=0, keepdims=True)
        idx = jnp.min(jnp.where(l == m, e_iota, ne), axis=0, keepdims=True)
        sel = e_iota == idx
        vals.append(m)
        sels.append(sel)
        chosen = jnp.where(sel, 1.0, chosen)
        l = jnp.where(sel, -jnp.inf, l)
    ex = [jnp.exp(vj - vals[0]) for vj in vals]
    den = ex[0] + ex[1] + ex[2] + ex[3]
    for j in range(TOP_K):
        gate_ref[j:j + 1, :] = ex[j] / den

    ti = lax.broadcasted_iota(I32, (tr, tr), 0)
    tj = lax.broadcasted_iota(I32, (tr, tr), 1)
    upper = jnp.where(ti <= tj, 1.0, 0.0).astype(BF16)
    inc = _dot(chosen.astype(BF16), upper)
    carry = carry_scr[...]
    row_of = inc - chosen + carry[:, 0:1] + pstart_scr[:, 0:1]
    for j in range(TOP_K):
        pj = jnp.sum(jnp.where(sels[j], row_of, 0.0), axis=0, keepdims=True)
        pos_ref[j:j + 1, :] = pj.astype(I32)
    carry_scr[...] = carry + inc[:, tr - 1:tr]


def _route(logits_t):
    ne, t = logits_t.shape
    tr = TR_ROUTE
    tok = pl.BlockSpec((TOP_K, tr), lambda ph, i: (0, i * ph))
    return pl.pallas_call(
        _route_kernel,
        grid=(2, t // tr),
        in_specs=[pl.BlockSpec((ne, tr), lambda ph, i: (0, i))],
        out_specs=[tok, tok, pl.BlockSpec((ne, 128), lambda ph, i: (0, 0))],
        out_shape=[
            jax.ShapeDtypeStruct((TOP_K, t), I32),
            jax.ShapeDtypeStruct((TOP_K, t), F32),
            jax.ShapeDtypeStruct((ne, 128), I32),
        ],
        scratch_shapes=[pltpu.VMEM((ne, 128), F32), pltpu.VMEM((ne, 128), F32)],
        compiler_params=_params(("arbitrary", "arbitrary")),
        name="route",
    )(logits_t)


def _dispatch_kernel(pend_ref, nused_ref, pos_ref, hn_ref, xin_ref, zero_scr, sem):
    bm = zero_scr.shape[0] // SUBLANES
    td = pos_ref.shape[1]
    i = pl.program_id(0)

    def tile(ref, token):
        return ref.at[pl.ds(pl.multiple_of(token * SUBLANES, SUBLANES), SUBLANES)]

    def zero_block(row0):
        start = pl.multiple_of(row0 * SUBLANES, bm * SUBLANES)
        return pltpu.make_async_copy(zero_scr, xin_ref.at[pl.ds(start, bm * SUBLANES)], sem)

    @pl.when(i == 0)
    def _():
        zero_scr[...] = jnp.zeros_like(zero_scr)

        def has_rows(e):
            end = pend_ref[e]
            return jnp.where(e == 0, end >= bm, end > pend_ref[jnp.maximum(e - 1, 0)])

        def start_one(e, carry):
            @pl.when(has_rows(e))
            def _():
                zero_block(pend_ref[e] - bm).start()
            return carry

        def wait_one(e, carry):
            @pl.when(has_rows(e))
            def _():
                zero_block(0).wait()
            return carry

        lax.fori_loop(0, pend_ref.shape[0], start_one, 0)
        lax.fori_loop(0, pend_ref.shape[0], wait_one, 0)

    def issue(tk, carry):
        for j in range(TOP_K):
            pltpu.make_async_copy(tile(hn_ref, tk), tile(xin_ref, pos_ref[j, tk]),
                                  sem).start(priority=j % 2)
        return carry

    lax.fori_loop(0, td, issue, 0, unroll=8)

    for j in range(TOP_K):
        pltpu.make_async_copy(hn_ref, xin_ref.at[pl.ds(0, td * SUBLANES)], sem).wait()


def _dispatch(pad_end, n_used, pos, hn_tiles, n_rows):
    t = hn_tiles.shape[0] // SUBLANES
    td = TD_DISPATCH
    grid_spec = pltpu.PrefetchScalarGridSpec(
        num_scalar_prefetch=2,
        grid=(t // td,),
        in_specs=[
            pl.BlockSpec((TOP_K, td), lambda i, pe, nu: (0, i), memory_space=pltpu.SMEM),
            pl.BlockSpec((td * SUBLANES, LANES), lambda i, pe, nu: (i, 0)),
        ],
        out_specs=pl.BlockSpec(memory_space=pl.ANY),
        scratch_shapes=[pltpu.VMEM((BM_MOE * SUBLANES, LANES), F32), pltpu.SemaphoreType.DMA(())],
    )
    return pl.pallas_call(
        _dispatch_kernel,
        grid_spec=grid_spec,
        out_shape=jax.ShapeDtypeStruct((n_rows * SUBLANES, LANES), F32),
        compiler_params=_params(("arbitrary",)),
        name="dispatch",
    )(pad_end, n_used, pos, hn_tiles)


def _experts_kernel(be_ref, nused_ref, x_ref, w1_ref, b1_ref, w2_ref, b2_ref, o_ref,
                    wt_scr, wg_scr, wl_scr, w2_scr):
    b = pl.program_id(0)
    live = b < nused_ref[0]
    dh = w2_ref.shape[1]
    fresh = (b == 0) | (be_ref[b] != be_ref[jnp.maximum(b - 1, 0)])

    @pl.when(live & fresh)
    def _():
        n_slab, n_t, lanes = wt_scr.shape
        for part in range(2 * dh // n_t):
            rows = slice(part * (n_t // 2), (part + 1) * (n_t // 2))
            for s in range(n_slab):
                cols = slice(s * lanes, (s + 1) * lanes)
                wt_scr[s] = w1_ref[0, cols, part * n_t:(part + 1) * n_t].T
                wg_scr[rows, cols] = wt_scr[s, pl.ds(0, n_t // 2, stride=2), :].astype(BF16)
                wl_scr[rows, cols] = wt_scr[s, pl.ds(1, n_t // 2, stride=2), :].astype(BF16)
        w2_scr[...] = w2_ref[0].astype(BF16)

    @pl.when(live)
    def _():
        bm = x_ref.shape[0] // SUBLANES
        xb = _load_token_tiles(x_ref, 0, bm).astype(BF16)
        bias = b1_ref[0]
        glu = jnp.minimum(_dot_nt(xb, wg_scr[...]) + bias[:, :dh], SWIGLU_LIMIT)
        lin = jnp.clip(_dot_nt(xb, wl_scr[...]) + bias[:, dh:], -SWIGLU_LIMIT, SWIGLU_LIMIT)
        act = glu * jax.nn.sigmoid(SWIGLU_ALPHA * glu) * (lin + 1.0)
        _store_token_tiles(o_ref, _dot(act.astype(BF16), w2_scr[...]) + b2_ref[0])

    @pl.when(jnp.logical_not(live))
    def _():
        o_ref[...] = jnp.zeros_like(o_ref)


def _experts(block_expert, n_used, xin, w1, b1p, w2, b2):
    n_rows = xin.shape[0] // SUBLANES
    ne, d, dh2 = w1.shape
    dh = dh2 // 2
    bm = BM_MOE
    grid_spec = pltpu.PrefetchScalarGridSpec(
        num_scalar_prefetch=2,
        grid=(n_rows // bm,),
        in_specs=[
            pl.BlockSpec((bm * SUBLANES, LANES), lambda b, be, nu: (jnp.minimum(b, nu[0] - 1), 0)),
            pl.BlockSpec((1, d, dh2), lambda b, be, nu: (be[b], 0, 0)),
            pl.BlockSpec((1, 1, dh2), lambda b, be, nu: (be[b], 0, 0)),
            pl.BlockSpec((1, dh, d), lambda b, be, nu: (be[b], 0, 0)),
            pl.BlockSpec((1, 1, d), lambda b, be, nu: (be[b], 0, 0)),
        ],
        out_specs=pl.BlockSpec((bm * SUBLANES, LANES), lambda b, be, nu: (b, 0)),
        scratch_shapes=[
            pltpu.VMEM((d // 128, dh, 128), F32),
            pltpu.VMEM((dh, d), BF16),
            pltpu.VMEM((dh, d), BF16),
            pltpu.VMEM((dh, d), BF16),
        ],
    )
    return pl.pallas_call(
        _experts_kernel,
        grid_spec=grid_spec,
        out_shape=jax.ShapeDtypeStruct((n_rows * SUBLANES, LANES), F32),
        compiler_params=_params(("arbitrary",)),
        name="experts",
    )(block_expert, n_used, xin, w1, b1p, w2, b2)


def _combine_kernel(pos_ref, pos_next_ref, gate_ref, h1_ref, fg_ref, yout_ref, o_ref, ybuf, sem):
    tc = h1_ref.shape[0]
    i = pl.program_id(0)
    slot = i % 2

    def tile(ref, token):
        return ref.at[pl.ds(pl.multiple_of(token * SUBLANES, SUBLANES), SUBLANES)]

    def gather(p_ref, s):
        def issue(tk, carry):
            for j in range(TOP_K):
                pltpu.make_async_copy(tile(yout_ref, p_ref[j, tk]), tile(ybuf.at[s], j * tc + tk),
                                      sem.at[s]).start(priority=j % 2)
            return carry
        lax.fori_loop(0, tc, issue, 0, unroll=8)

    @pl.when(i == 0)
    def _():
        gather(pos_ref, 0)

    @pl.when(i + 1 < pl.num_programs(0))
    def _():
        gather(pos_next_ref, 1 - slot)

    pltpu.make_async_copy(yout_ref.at[pl.ds(0, ybuf.shape[1])], ybuf.at[slot], sem.at[slot]).wait()

    gates = gate_ref[...]
    h = h1_ref[...]
    for j in range(TOP_K):
        h = h + _load_token_tiles(ybuf.at[slot], j * tc * SUBLANES, tc) * gates[:, j:j + 1]
    o_ref[...] = h * lax.rsqrt(jnp.mean(h * h, axis=-1, keepdims=True) + RMS_EPS) * fg_ref[...]


def _combine(pos, gates_t, h1, fg, yout):
    t, d = h1.shape
    tc = TC_COMBINE
    n_steps = t // tc
    return pl.pallas_call(
        _combine_kernel,
        grid=(n_steps,),
        in_specs=[
            pl.BlockSpec((TOP_K, tc), lambda i: (0, i), memory_space=pltpu.SMEM),
            pl.BlockSpec((TOP_K, tc), lambda i: (0, jnp.minimum(i + 1, n_steps - 1)),
                         memory_space=pltpu.SMEM),
            pl.BlockSpec((tc, TOP_K), lambda i: (i, 0)),
            pl.BlockSpec((tc, d), lambda i: (i, 0)),
            pl.BlockSpec((1, d), lambda i: (0, 0)),
            pl.BlockSpec(memory_space=pl.ANY),
        ],
        out_specs=pl.BlockSpec((tc, d), lambda i: (i, 0)),
        out_shape=jax.ShapeDtypeStruct((t, d), F32),
        scratch_shapes=[pltpu.VMEM((2, TOP_K * tc * SUBLANES, LANES), F32),
                        pltpu.SemaphoreType.DMA((2,))],
        compiler_params=_params(("arbitrary",)),
        name="combine",
    )(pos, pos, gates_t, h1, fg, yout)


def kernel(x, norm1_g, w_in, sgu_ln_g, sgu_ln_b, sgu_w, sgu_b, mu_rkv, mu_wag, decay_w0, decay_w1,
           decay_w2, iclr_a0, iclr_a1, iclr_a2, gate_g1, gate_g2, k_k, k_a, r_k, lnx_g, lnx_b, w_out,
           norm2_g, router_w, router_b, moe_w1, moe_b1, moe_w2, moe_b2, final_g):
    batch, seq, d = x.shape
    t = batch * seq
    depth = w_in.shape[0]
    sgu_width = sgu_ln_g.shape[1]
    rw = mu_rkv.shape[2]
    n_dec, n_icl, n_gate = decay_w1.shape[2], iclr_a1.shape[2], gate_g1.shape[2]
    assert n_dec == HEAD and n_icl == HEAD and n_gate == PAIR and rw % PAIR == 0
    assert seq % TT_RWKV == 0 and seq % TT_PREP == 0 and t % TM_PROJ == 0
    assert sgu_w.shape[2] == SGU_BLOCK
    assert depth == 1, "the final RMSNorm is fused into the last layer's combine kernel"
    assert d == SUBLANES * LANES, "the MoE row movers copy one (8, 128) f32 tile per token"

    h = x.reshape(t, d)
    for l in range(depth):
        win_bf = w_in[l].astype(BF16)
        wl = jnp.concatenate([decay_w1[l], iclr_a1[l], gate_g1[l]], axis=1)
        mucat = jnp.concatenate([
            jnp.broadcast_to(mu_wag[l, 0][:, None], (d, n_dec)),
            jnp.broadcast_to(mu_wag[l, 1][:, None], (d, n_icl)),
            jnp.broadcast_to(mu_wag[l, 2][:, None], (d, n_gate))], axis=1)
        zeros = jnp.zeros((HEAD, rw), F32)
        w2pad = jnp.concatenate([decay_w2[l], zeros], axis=0)
        a2pad = jnp.concatenate([zeros, iclr_a2[l]], axis=0)
        pvec = jnp.stack([decay_w0[l], iclr_a0[l], k_k[l], k_a[l], r_k[l].reshape(-1),
                          lnx_g[l], lnx_b[l], jnp.zeros((rw,), F32)], axis=0)
        bias2d = jnp.repeat(sgu_b[l].T, sgu_width // sgu_b.shape[1], axis=1)
        wo_bf = w_out[l].astype(BF16)
        b1p = jnp.concatenate([moe_b1[l][:, 0::2], moe_b1[l][:, 1::2]], axis=-1)[:, None, :]
        b2 = moe_b2[l][:, None, :]

        proj, lora = _in_proj(h, norm1_g[l][None, :], win_bf, wl, mucat)
        ya = _sgu(proj, sgu_ln_g[l][None, :], sgu_ln_b[l][None, :], sgu_w[l], bias2d)
        rp, yq, bonus, gate, g_m, h_m = _rwkv_prep(proj, lora, mu_rkv[l], pvec, w2pad, a2pad,
                                                   gate_g2[l], seq, 2 * sgu_width)
        yb = _rwkv_scan(rp, yq, bonus, gate, g_m, h_m, pvec, batch)
        h1, hn2, logits_t = _out_proj(h, ya, yb, wo_bf, norm2_g[l][None, :], router_w[l].T,
                                      router_b[l][:, None])
        pos, gates, pend = _route(logits_t)

        pad_end = pend[:, 0]
        n_blocks = (t * TOP_K) // BM_MOE + N_EXPERTS
        n_used = pad_end[-1:] // BM_MOE
        block_start = jnp.arange(n_blocks, dtype=I32) * BM_MOE
        block_expert = jnp.minimum(
            jnp.sum((pad_end[None, :] <= block_start[:, None]).astype(I32), axis=1), N_EXPERTS - 1)

        xin = _dispatch(pad_end, n_used, pos, hn2, n_blocks * BM_MOE)
        yout = _experts(block_expert, n_used, xin, moe_w1[l], b1p, moe_w2[l], b2)
        h = _combine(pos, gates.T, h1, final_g[None, :], yout)
    return h.reshape(batch, seq, d)
```

```python
import functools

import jax
import jax.numpy as jnp
from jax import lax
from jax.experimental import pallas as pl
from jax.experimental.pallas import tpu as pltpu

F32 = jnp.float32
BF16 = jnp.bfloat16
I32 = jnp.int32

RMS_EPS = 1e-5
LN_EPS = 1e-5
LNX_EPS = 64e-5
CHUNK = 64
SGU_BLOCK = 128
HEAD = 64
PAIR = 2 * HEAD
N_EXPERTS = 32
TOP_K = 4
SWIGLU_ALPHA = 1.702
SWIGLU_LIMIT = 7.0

V7X_VMEM_LIMIT = 56 * 1024 * 1024

TM_PROJ = 512
TM_SGU = 512
TT_PREP = 512
TT_RWKV = 256
TR_ROUTE = 512
BM_MOE = 256
TD_DISPATCH = 512
TC_COMBINE = 256


def _dot(a, b):
    return jnp.dot(a, b, preferred_element_type=F32)


def _dot_nt(a, b):
    return lax.dot_general(a, b, (((1,), (1,)), ((), ())), preferred_element_type=F32)


def _split(x):
    hi = x.astype(BF16)
    lo = (x - hi.astype(F32)).astype(BF16)
    return hi, lo


def _dot3(a, b):
    ah, al = _split(a)
    bh, bl = _split(b)
    return _dot(ah, bh) + _dot(al, bh) + _dot(ah, bl)


def _dot3_nt(a, b):
    ah, al = _split(a)
    bh, bl = _split(b)
    return _dot_nt(ah, bh) + _dot_nt(al, bh) + _dot_nt(ah, bl)


def _head_sum(x, bd):
    hi, lo = _split(x)
    return _dot(hi, bd) + _dot(lo, bd)


SUBLANES = 8
LANES = 128


def _store_token_tiles(ref, x):
    n = x.shape[0]
    for s in range(SUBLANES):
        ref[pl.ds(s, n, stride=SUBLANES), :] = x[:, s * LANES:(s + 1) * LANES]


def _load_token_tiles(ref, row0, n):
    return jnp.concatenate(
        [ref[pl.ds(row0 + s, n, stride=SUBLANES), :] for s in range(SUBLANES)], axis=1)


def _params(sem, vmem=V7X_VMEM_LIMIT):
    return pltpu.CompilerParams(dimension_semantics=sem, vmem_limit_bytes=vmem)


def _in_proj_kernel(x_ref, g_ref, win_ref, wl_ref, mucat_ref, proj_ref, lora_ref, wl_scr):
    @pl.when(pl.program_id(0) == 0)
    def _():
        wl = wl_ref[...]
        mu = mucat_ref[...]
        n = wl.shape[1]
        wl_scr[:, 0:n] = (wl * (1.0 - mu)).astype(BF16)
        wl_scr[:, n:2 * n] = (wl * mu).astype(BF16)

    x = x_ref[...]
    hn = x * lax.rsqrt(jnp.mean(x * x, axis=-1, keepdims=True) + RMS_EPS) * g_ref[...]
    hb = hn.astype(BF16)
    proj_ref[...] = _dot(hb, win_ref[...])
    lora_ref[...] = _dot(hb, wl_scr[...])


def _in_proj(x2, g, win_bf, wl, mucat):
    t, d = x2.shape
    n_in = win_bf.shape[1]
    n_l = wl.shape[1]
    tm = TM_PROJ
    return pl.pallas_call(
        _in_proj_kernel,
        grid=(t // tm,),
        in_specs=[
            pl.BlockSpec((tm, d), lambda i: (i, 0)),
            pl.BlockSpec((1, d), lambda i: (0, 0)),
            pl.BlockSpec((d, n_in), lambda i: (0, 0)),
            pl.BlockSpec((d, n_l), lambda i: (0, 0)),
            pl.BlockSpec((d, n_l), lambda i: (0, 0)),
        ],
        out_specs=[
            pl.BlockSpec((tm, n_in), lambda i: (i, 0)),
            pl.BlockSpec((tm, 2 * n_l), lambda i: (i, 0)),
        ],
        out_shape=[
            jax.ShapeDtypeStruct((t, n_in), F32),
            jax.ShapeDtypeStruct((t, 2 * n_l), F32),
        ],
        scratch_shapes=[pltpu.VMEM((d, 2 * n_l), BF16)],
        compiler_params=_params(("arbitrary",)),
        name="in_proj",
    )(x2, g, win_bf, wl, mucat)


def _sgu_kernel(z_ref, lng_ref, lnb_ref, w_ref, bias_ref, o_ref, wm_scr):
    n_heads = w_ref.shape[0]

    @pl.when(pl.program_id(0) == 0)
    def _():
        qi = lax.broadcasted_iota(I32, (SGU_BLOCK, SGU_BLOCK), 0) // CHUNK
        kj = lax.broadcasted_iota(I32, (SGU_BLOCK, SGU_BLOCK), 1) // CHUNK
        for h in range(n_heads):
            wm_scr[h] = jnp.where(kj <= qi, w_ref[h], 0.0).astype(BF16)

    z = z_ref[...]
    tm, two_w = z.shape
    width = two_w // 2
    gz = 0.5 * z * (1.0 + lax.erf(z * (2.0 ** -0.5)))
    u = gz[:, :width]
    v = gz[:, width:]
    mu = jnp.mean(v, axis=-1, keepdims=True)
    vc = v - mu
    var = jnp.mean(vc * vc, axis=-1, keepdims=True)
    vn = vc * lax.rsqrt(var + LN_EPS) * lng_ref[...] + lnb_ref[...]
    lane = lax.broadcasted_iota(I32, (SGU_BLOCK, PAIR), 1)
    is_lo = lane < HEAD
    bias = bias_ref[...]
    for blk in range(tm // SGU_BLOCK):
        rows = slice(blk * SGU_BLOCK, (blk + 1) * SGU_BLOCK)
        for p in range(width // PAIR):
            cols = slice(p * PAIR, (p + 1) * PAIR)
            vp = vn[rows, cols]
            lo = jnp.where(is_lo, vp, 0.0).astype(BF16)
            hi = jnp.where(is_lo, 0.0, vp).astype(BF16)
            sv = _dot(wm_scr[2 * p], lo) + _dot(wm_scr[2 * p + 1], hi)
            o_ref[rows, cols] = u[rows, cols] * (sv + bias[:, cols])


def _sgu(proj, ln_g, ln_b, w_s, bias2d):
    t = proj.shape[0]
    width = ln_g.shape[1]
    n_heads = w_s.shape[0]
    tm = TM_SGU
    return pl.pallas_call(
        _sgu_kernel,
        grid=(t // tm,),
        in_specs=[
            pl.BlockSpec((tm, 2 * width), lambda i: (i, 0)),
            pl.BlockSpec((1, width), lambda i: (0, 0)),
            pl.BlockSpec((1, width), lambda i: (0, 0)),
            pl.BlockSpec((n_heads, SGU_BLOCK, SGU_BLOCK), lambda i: (0, 0, 0)),
            pl.BlockSpec((SGU_BLOCK, width), lambda i: (0, 0)),
        ],
        out_specs=pl.BlockSpec((tm, width), lambda i: (i, 0)),
        out_shape=jax.ShapeDtypeStruct((t, width), F32),
        scratch_shapes=[pltpu.VMEM((n_heads, SGU_BLOCK, SGU_BLOCK), BF16)],
        compiler_params=_params(("arbitrary",)),
        name="sgu",
    )(proj, ln_g, ln_b, w_s, bias2d)


def _rwkv_prep_kernel(seq_len, pr_ref, pk_ref, pv_ref, ppr_ref, ppk_ref, ppv_ref, lo_ref, plo_ref,
                      mu_ref, pvec_ref, w2_ref, a2_ref, g2_ref,
                      rp_ref, yq_ref, bonus_ref, gate_ref, g_ref, h_ref):
    tt = pr_ref.shape[0]
    n_chunks = tt // CHUNK
    i = pl.program_id(0)
    keep = jnp.where((i * tt) % seq_len == 0, 0.0, 1.0)

    def shift(x, prev):
        rolled = pltpu.roll(x, 1, 0)
        first = prev[7:8, :] * keep
        rowc = lax.broadcasted_iota(I32, x.shape, 0)
        return jnp.where(rowc == 0, first, rolled)

    mu = mu_ref[...]

    def shift_mix(ref, pref, m):
        x = ref[...]
        return x + (shift(x, pref[...]) - x) * m

    r = shift_mix(pr_ref, ppr_ref, mu[0:1])
    k = shift_mix(pk_ref, ppk_ref, mu[1:2])
    v = shift_mix(pv_ref, ppv_ref, mu[2:3])

    lo = lo_ref[...]
    plo = plo_ref[...]
    half = lo.shape[1] // 2
    l_all = lo[:, :half] + shift(lo[:, half:], plo[:, half:])
    l_wa = l_all[:, :PAIR]
    l_g = l_all[:, PAIR:]

    pvec = pvec_ref[...]
    w0, a0, k_k, k_a, r_k = pvec[0:1], pvec[1:2], pvec[2:3], pvec[3:4], pvec[4:5]

    dw = _dot(jnp.tanh(l_wa).astype(BF16), w2_ref[...].astype(BF16))
    ia = _dot(l_wa.astype(BF16), a2_ref[...].astype(BF16))
    gate_ref[...] = _dot(jax.nn.sigmoid(l_g).astype(BF16), g2_ref[...].astype(BF16))

    zneg = -(w0 + dw)
    softplus = jnp.maximum(zneg, 0.0) + jnp.log(1.0 + jnp.exp(-jnp.abs(zneg)))
    logw = -jnp.exp(-softplus - 0.5)
    iclr = jax.nn.sigmoid(a0 + ia)

    lane_r = lax.broadcasted_iota(I32, (PAIR, PAIR), 0)
    lane_c = lax.broadcasted_iota(I32, (PAIR, PAIR), 1)
    bd = jnp.where((lane_r // HEAD) == (lane_c // HEAD), 1.0, 0.0).astype(BF16)

    kk = k * k_k
    kk = kk / jnp.maximum(jnp.sqrt(_head_sum(kk * kk, bd)), 1e-12)
    k2 = k * (1.0 + (iclr - 1.0) * k_a)
    a = -kk
    b = kk * iclr
    bonus_ref[...] = _head_sum(r * k2 * r_k, bd) * v

    rc = lax.broadcasted_iota(I32, (tt, PAIR), 0) % CHUNK
    cl = logw
    s = 1
    while s < CHUNK:
        cl = cl + jnp.where(rc >= s, pltpu.roll(cl, s, 0), 0.0)
        s *= 2
    at_all = a * jnp.exp(cl - logw)
    rt_all = r * jnp.exp(cl)
    w_inv = jnp.exp(-cl)
    bt_all = b * w_inv
    kt_all = k2 * w_inv

    lane = lax.broadcasted_iota(I32, (CHUNK, PAIR), 1)
    is_lo = lane < HEAD

    def stack(x):
        return jnp.concatenate([jnp.where(is_lo, x, 0.0), jnp.where(is_lo, 0.0, x)], axis=0)

    n2 = 2 * PAIR
    ri = lax.broadcasted_iota(I32, (n2, n2), 0)
    ci = lax.broadcasted_iota(I32, (n2, n2), 1)
    same_head = ((ri // CHUNK) % 2) == ((ci // CHUNK) % 2)
    t_i = ri % CHUNK
    s_i = ci % CHUNK
    tri = same_head & (s_i < t_i + ri // PAIR)
    eye = lane_r == lane_c

    chunks = range(n_chunks)
    rows = [slice(c * CHUNK, (c + 1) * CHUNK) for c in chunks]
    last = [cl[rows[c]][CHUNK - 1:CHUNK, :] for c in chunks]
    w_rem = [jnp.exp(last[c] - cl[rows[c]]) for c in chunks]
    a_s = [stack(at_all[rows[c]]) for c in chunks]
    r_s = [stack(rt_all[rows[c]]) for c in chunks]
    v_s = [stack(v[rows[c]]) for c in chunks]
    v_sb = [v_s[c].astype(BF16) for c in chunks]
    bk_h = [jnp.concatenate([stack(b[rows[c]] * w_rem[c]), stack(k2[rows[c]] * w_rem[c])],
                            axis=0).astype(BF16) for c in chunks]

    a_all = []
    for c in chunks:
        lhs = jnp.concatenate([a_s[c], r_s[c]], axis=0).astype(BF16)
        rhs = jnp.concatenate([stack(bt_all[rows[c]]), stack(kt_all[rows[c]])], axis=0).astype(BF16)
        a_all.append(jnp.where(tri, _dot_nt(lhs, rhs), 0.0))
    n_k = [a_all[c][:PAIR, :PAIR].astype(BF16) for c in chunks]
    a_rb = [a_all[c][PAIR:, :PAIR].astype(BF16) for c in chunks]
    x = [jnp.concatenate([a_s[c], _dot(a_all[c][:PAIR, PAIR:].astype(BF16), v_sb[c])], axis=1)
         for c in chunks]
    ry0 = [jnp.concatenate([r_s[c], _dot(a_all[c][PAIR:, PAIR:].astype(BF16), v_sb[c])], axis=1)
           for c in chunks]
    steps = CHUNK.bit_length() - 1
    for it in range(steps):
        x = [x[c] + _dot(n_k[c], x[c].astype(BF16)) for c in chunks]
        if it + 1 < steps:
            n_k = [_dot(n_k[c], n_k[c]).astype(BF16) for c in chunks]
    ry = [ry0[c] + _dot(a_rb[c], x[c].astype(BF16)) for c in chunks]
    for c in chunks:
        rp_ref[rows[c], :] = ry[c][:CHUNK, :PAIR] + ry[c][CHUNK:, :PAIR]
        yq_ref[rows[c], :] = ry[c][:CHUNK, PAIR:] + ry[c][CHUNK:, PAIR:]
    p_t = [x[c][:, :PAIR].T.astype(BF16) for c in chunks]
    qv_t = [jnp.concatenate([x[c][:, PAIR:], v_s[c]], axis=0).T.astype(BF16) for c in chunks]
    for c in chunks:
        g_ref[c, 0] = jnp.where(eye, jnp.exp(last[c]), 0.0) + _dot(p_t[c], bk_h[c][:PAIR])
    for c in chunks:
        hbd = _dot(qv_t[c], bk_h[c])
        h_ref[c, 0] = hbd[:HEAD] + hbd[HEAD:]


def _rwkv_prep(proj, lora, mu_rkv, pvec, w2pad, a2pad, g2, seq_len, col0):
    t = proj.shape[0]
    width = mu_rkv.shape[1]
    n_pairs = width // PAIR
    tt = TT_PREP
    nl = lora.shape[1]
    cb = col0 // PAIR
    wb = width // PAIR

    def prev_rows(i):
        return jnp.maximum(i * (tt // 8) - 1, 0)

    def tok(c):
        return pl.BlockSpec((tt, PAIR), lambda i, p: (i, c + p))

    def prev(c):
        return pl.BlockSpec((8, PAIR), lambda i, p: (prev_rows(i), c + p))

    out_tok = pl.BlockSpec((tt, PAIR), lambda i, p: (i, p))
    tok_shape = jax.ShapeDtypeStruct((t, width), F32)
    return pl.pallas_call(
        functools.partial(_rwkv_prep_kernel, seq_len),
        grid=(t // tt, n_pairs),
        in_specs=[
            tok(cb), tok(cb + wb), tok(cb + 2 * wb),
            prev(cb), prev(cb + wb), prev(cb + 2 * wb),
            pl.BlockSpec((tt, nl), lambda i, p: (i, 0)),
            pl.BlockSpec((8, nl), lambda i, p: (prev_rows(i), 0)),
            pl.BlockSpec((3, PAIR), lambda i, p: (0, p)),
            pl.BlockSpec((8, PAIR), lambda i, p: (0, p)),
            pl.BlockSpec((PAIR, PAIR), lambda i, p: (0, p)),
            pl.BlockSpec((PAIR, PAIR), lambda i, p: (0, p)),
            pl.BlockSpec((PAIR, PAIR), lambda i, p: (0, p)),
        ],
        out_specs=[
            out_tok, out_tok, out_tok, out_tok,
            pl.BlockSpec((tt // CHUNK, 1, PAIR, PAIR), lambda i, p: (i, p, 0, 0)),
            pl.BlockSpec((tt // CHUNK, 1, HEAD, PAIR), lambda i, p: (i, p, 0, 0)),
        ],
        out_shape=[
            tok_shape, tok_shape, tok_shape, tok_shape,
            jax.ShapeDtypeStruct((t // CHUNK, n_pairs, PAIR, PAIR), F32),
            jax.ShapeDtypeStruct((t // CHUNK, n_pairs, HEAD, PAIR), F32),
        ],
        compiler_params=_params(("arbitrary", "arbitrary")),
        name="rwkv_prep",
    )(proj, proj, proj, proj, proj, proj, lora, lora, mu_rkv, pvec, w2pad, a2pad, g2)


def _rwkv_scan_kernel(rp_ref, yq_ref, bonus_ref, gate_ref, g_ref, h_ref, pvec_ref, o_ref, s_scr, y_scr):
    nb, tt, width = rp_ref.shape
    n_pairs = width // PAIR

    @pl.when(pl.program_id(0) == 0)
    def _():
        s_scr[...] = jnp.zeros_like(s_scr)

    lane = lax.broadcasted_iota(I32, (HEAD, PAIR), 1)
    is_lo = lane < HEAD
    lane_r = lax.broadcasted_iota(I32, (PAIR, PAIR), 0)
    lane_c = lax.broadcasted_iota(I32, (PAIR, PAIR), 1)
    bd = jnp.where((lane_r // HEAD) == (lane_c // HEAD), 1.0, 0.0).astype(BF16)

    chains = [(b, p) for b in range(nb) for p in range(n_pairs)]
    state = {ch: s_scr[ch[0], ch[1]] for ch in chains}
    for c in range(tt // CHUNK):
        rows = slice(c * CHUNK, (c + 1) * CHUNK)
        for (b, p) in chains:
            cols = slice(p * PAIR, (p + 1) * PAIR)
            s0 = state[(b, p)]
            s_st = jnp.concatenate([jnp.where(is_lo, s0, 0.0), jnp.where(is_lo, 0.0, s0)], axis=0)
            y_scr[b, rows, cols] = (_dot_nt(rp_ref[b, rows, cols].astype(BF16), s_st.astype(BF16))
                                    + yq_ref[b, rows, cols])
        state = {(b, p): _dot3(state[(b, p)], g_ref[b, c, p]) + h_ref[b, c, p] for (b, p) in chains}
    for (b, p) in chains:
        s_scr[b, p] = state[(b, p)]

    pvec = pvec_ref[...]
    for (b, p) in chains:
        cols = slice(p * PAIR, (p + 1) * PAIR)
        y = y_scr[b, :, cols]
        mu = _head_sum(y, bd) * (1.0 / HEAD)
        yc = y - mu
        var = _head_sum(yc * yc, bd) * (1.0 / HEAD)
        yn = yc * lax.rsqrt(var + LNX_EPS) * pvec[5:6, cols] + pvec[6:7, cols]
        o_ref[b, :, cols] = (yn + bonus_ref[b, :, cols]) * gate_ref[b, :, cols]


def _rwkv_scan(rp, yq, bonus, gate, g, h, pvec, batch):
    t, width = rp.shape
    n_pairs = width // PAIR
    seq = t // batch
    tt = TT_RWKV
    nc = tt // CHUNK
    tok = pl.BlockSpec((batch, tt, width), lambda i: (0, i, 0))
    as_seq = lambda z: z.reshape(batch, seq, width)
    out = pl.pallas_call(
        _rwkv_scan_kernel,
        grid=(seq // tt,),
        in_specs=[
            tok, tok, tok, tok,
            pl.BlockSpec((batch, nc, n_pairs, PAIR, PAIR), lambda i: (0, i, 0, 0, 0)),
            pl.BlockSpec((batch, nc, n_pairs, HEAD, PAIR), lambda i: (0, i, 0, 0, 0)),
            pl.BlockSpec((8, width), lambda i: (0, 0)),
        ],
        out_specs=tok,
        out_shape=jax.ShapeDtypeStruct((batch, seq, width), F32),
        scratch_shapes=[pltpu.VMEM((batch, n_pairs, HEAD, PAIR), F32),
                        pltpu.VMEM((batch, tt, width), F32)],
        compiler_params=_params(("arbitrary",)),
        name="rwkv_scan",
    )(as_seq(rp), as_seq(yq), as_seq(bonus), as_seq(gate),
      g.reshape(batch, seq // CHUNK, n_pairs, PAIR, PAIR),
      h.reshape(batch, seq // CHUNK, n_pairs, HEAD, PAIR), pvec)
    return out.reshape(t, width)


def _out_proj_kernel(x_ref, ya_ref, yb_ref, wo_ref, g_ref, rwt_ref, rb_ref, h1_ref, hn_ref, lg_ref):
    wa = ya_ref.shape[1]
    h1 = (x_ref[...] + _dot(ya_ref[...].astype(BF16), wo_ref[0:wa, :])
          + _dot(yb_ref[...].astype(BF16), wo_ref[wa:, :]))
    h1_ref[...] = h1
    hn = h1 * lax.rsqrt(jnp.mean(h1 * h1, axis=-1, keepdims=True) + RMS_EPS) * g_ref[...]
    _store_token_tiles(hn_ref, hn)
    lg_ref[...] = _dot3_nt(rwt_ref[...], hn) + rb_ref[...]


def _out_proj(x2, ya, yb, wo_bf, g2, rwt, rb):
    t, d = x2.shape
    wa = ya.shape[1]
    wb = yb.shape[1]
    ne = rwt.shape[0]
    tm = TM_PROJ
    return pl.pallas_call(
        _out_proj_kernel,
        grid=(t // tm,),
        in_specs=[
            pl.BlockSpec((tm, d), lambda i: (i, 0)),
            pl.BlockSpec((tm, wa), lambda i: (i, 0)),
            pl.BlockSpec((tm, wb), lambda i: (i, 0)),
            pl.BlockSpec((wa + wb, d), lambda i: (0, 0)),
            pl.BlockSpec((1, d), lambda i: (0, 0)),
            pl.BlockSpec((ne, d), lambda i: (0, 0)),
            pl.BlockSpec((ne, 1), lambda i: (0, 0)),
        ],
        out_specs=[
            pl.BlockSpec((tm, d), lambda i: (i, 0)),
            pl.BlockSpec((tm * SUBLANES, LANES), lambda i: (i, 0)),
            pl.BlockSpec((ne, tm), lambda i: (0, i)),
        ],
        out_shape=[
            jax.ShapeDtypeStruct((t, d), F32),
            jax.ShapeDtypeStruct((t * SUBLANES, LANES), F32),
            jax.ShapeDtypeStruct((ne, t), F32),
        ],
        compiler_params=_params(("arbitrary",)),
        name="out_proj",
    )(x2, ya, yb, wo_bf, g2, rwt, rb)


def _route_kernel(lg_ref, pos_ref, gate_ref, pend_ref, carry_scr, pstart_scr):
    phase = pl.program_id(0)
    first = pl.program_id(1) == 0

    @pl.when(first & (phase == 0))
    def _():
        carry_scr[...] = jnp.zeros_like(carry_scr)
        pstart_scr[...] = jnp.zeros_like(pstart_scr)
        pend_ref[...] = jnp.zeros_like(pend_ref)

    @pl.when(first & (phase == 1))
    def _():
        counts = carry_scr[...]
        padded = jnp.ceil(counts * (1.0 / BM_MOE)) * BM_MOE
        row = lax.broadcasted_iota(I32, counts.shape, 0)
        end = padded
        s = 1
        while s < counts.shape[0]:
            end = end + jnp.where(row >= s, pltpu.roll(end, s, 0), 0.0)
            s *= 2
        pstart_scr[...] = end - padded
        pend_ref[...] = end.astype(I32)
        carry_scr[...] = jnp.zeros_like(carry_scr)

    l = lg_ref[...]
    ne, tr = l.shape
    e_iota = lax.broadcasted_iota(I32, (ne, tr), 0)
    chosen = jnp.zeros((ne, tr), F32)
    vals, sels = [], []
    for j in range(TOP_K):
        m = jnp.max(l, axis=0, keepdims=True)
        idx = jnp.min(jnp.where(l == m, e_iota, ne), axis=0, keepdims=True)
        sel = e_iota == idx
        vals.append(m)
        sels.append(sel)
        chosen = jnp.where(sel, 1.0, chosen)
        l = jnp.where(sel, -jnp.inf, l)
    ex = [jnp.exp(vj - vals[0]) for vj in vals]
    den = ex[0] + ex[1] + ex[2] + ex[3]
    for j in range(TOP_K):
        gate_ref[j:j + 1, :] = ex[j] / den

    ti = lax.broadcasted_iota(I32, (tr, tr), 0)
    tj = lax.broadcasted_iota(I32, (tr, tr), 1)
    upper = jnp.where(ti <= tj, 1.0, 0.0).astype(BF16)
    inc = _dot(chosen.astype(BF16), upper)
    carry = carry_scr[...]
    row_of = inc - chosen + carry[:, 0:1] + pstart_scr[:, 0:1]
    for j in range(TOP_K):
        pj = jnp.sum(jnp.where(sels[j], row_of, 0.0), axis=0, keepdims=True)
        pos_ref[j:j + 1, :] = pj.astype(I32)
    carry_scr[...] = carry + inc[:, tr - 1:tr]


def _route(logits_t):
    ne, t = logits_t.shape
    tr = TR_ROUTE
    tok = pl.BlockSpec((TOP_K, tr), lambda ph, i: (0, i * ph))
    return pl.pallas_call(
        _route_kernel,
        grid=(2, t // tr),
        in_specs=[pl.BlockSpec((ne, tr), lambda ph, i: (0, i))],
        out_specs=[tok, tok, pl.BlockSpec((ne, 128), lambda ph, i: (0, 0))],
        out_shape=[
            jax.ShapeDtypeStruct((TOP_K, t), I32),
            jax.ShapeDtypeStruct((TOP_K, t), F32),
            jax.ShapeDtypeStruct((ne, 128), I32),
        ],
        scratch_shapes=[pltpu.VMEM((ne, 128), F32), pltpu.VMEM((ne, 128), F32)],
        compiler_params=_params(("arbitrary", "arbitrary")),
        name="route",
    )(logits_t)


def _dispatch_kernel(pend_ref, nused_ref, pos_ref, hn_ref, xin_ref, zero_scr, sem):
    bm = zero_scr.shape[0] // SUBLANES
    td = pos_ref.shape[1]
    i = pl.program_id(0)

    def tile(ref, token):
        return ref.at[pl.ds(pl.multiple_of(token * SUBLANES, SUBLANES), SUBLANES)]

    def zero_block(row0):
        start = pl.multiple_of(row0 * SUBLANES, bm * SUBLANES)
        return pltpu.make_async_copy(zero_scr, xin_ref.at[pl.ds(start, bm * SUBLANES)], sem)

    @pl.when(i == 0)
    def _():
        zero_scr[...] = jnp.zeros_like(zero_scr)

        def has_rows(e):
            end = pend_ref[e]
            return jnp.where(e == 0, end >= bm, end > pend_ref[jnp.maximum(e - 1, 0)])

        def start_one(e, carry):
            @pl.when(has_rows(e))
            def _():
                zero_block(pend_ref[e] - bm).start()
            return carry

        def wait_one(e, carry):
            @pl.when(has_rows(e))
            def _():
                zero_block(0).wait()
            return carry

        def start_tail(blk, carry):
            zero_block(blk * bm).start()
            return carry

        def wait_tail(blk, carry):
            zero_block(0).wait()
            return carry

        n_blocks = xin_ref.shape[0] // (bm * SUBLANES)
        lax.fori_loop(0, pend_ref.shape[0], start_one, 0)
        lax.fori_loop(nused_ref[0], n_blocks, start_tail, 0)
        lax.fori_loop(0, pend_ref.shape[0], wait_one, 0)
        lax.fori_loop(nused_ref[0], n_blocks, wait_tail, 0)

    def issue(tk, carry):
        for j in range(TOP_K):
            pltpu.make_async_copy(tile(hn_ref, tk), tile(xin_ref, pos_ref[j, tk]),
                                  sem).start(priority=j % 2)
        return carry

    lax.fori_loop(0, td, issue, 0, unroll=8)

    for j in range(TOP_K):
        pltpu.make_async_copy(hn_ref, xin_ref.at[pl.ds(0, td * SUBLANES)], sem).wait()


def _dispatch(pad_end, n_used, pos, hn_tiles, n_rows):
    t = hn_tiles.shape[0] // SUBLANES
    td = TD_DISPATCH
    grid_spec = pltpu.PrefetchScalarGridSpec(
        num_scalar_prefetch=2,
        grid=(t // td,),
        in_specs=[
            pl.BlockSpec((TOP_K, td), lambda i, pe, nu: (0, i), memory_space=pltpu.SMEM),
            pl.BlockSpec((td * SUBLANES, LANES), lambda i, pe, nu: (i, 0)),
        ],
        out_specs=pl.BlockSpec(memory_space=pl.ANY),
        scratch_shapes=[pltpu.VMEM((BM_MOE * SUBLANES, LANES), F32), pltpu.SemaphoreType.DMA(())],
    )
    return pl.pallas_call(
        _dispatch_kernel,
        grid_spec=grid_spec,
        out_shape=jax.ShapeDtypeStruct((n_rows * SUBLANES, LANES), F32),
        compiler_params=_params(("arbitrary",)),
        name="dispatch",
    )(pad_end, n_used, pos, hn_tiles)


def _experts_kernel(pend_ref, nused_ref, w1_ref, b1_ref, w2_ref, b2_ref, xin_ref, yout_ref,
                    xbuf, obuf, wt_scr, wg_scr, wl_scr, w2_scr, sem_in, sem_out):
    e = pl.program_id(0)
    rows_blk = xbuf.shape[1]
    bm = rows_blk // SUBLANES
    dh = w2_ref.shape[1]
    n_used = nused_ref[0]
    blk_lo = jnp.where(e == 0, 0, pend_ref[jnp.maximum(e - 1, 0)]) // bm
    blk_hi = pend_ref[e] // bm

    def block(ref, blk):
        return ref.at[pl.ds(pl.multiple_of(blk * rows_blk, rows_blk), rows_blk)]

    def x_copy(blk, slot):
        return pltpu.make_async_copy(block(xin_ref, blk), xbuf.at[slot], sem_in.at[slot])

    def o_copy(blk, slot):
        return pltpu.make_async_copy(obuf.at[slot], block(yout_ref, blk), sem_out.at[slot])

    @pl.when((e == 0) & (n_used > 0))
    def _():
        x_copy(0, 0).start()

    @pl.when(blk_hi > blk_lo)
    def _():
        n_slab, n_t, lanes = wt_scr.shape
        for part in range(2 * dh // n_t):
            rows = slice(part * (n_t // 2), (part + 1) * (n_t // 2))
            for s in range(n_slab):
                cols = slice(s * lanes, (s + 1) * lanes)
                wt_scr[s] = w1_ref[0, cols, part * n_t:(part + 1) * n_t].T
                wg_scr[rows, cols] = wt_scr[s, pl.ds(0, n_t // 2, stride=2), :].astype(BF16)
                wl_scr[rows, cols] = wt_scr[s, pl.ds(1, n_t // 2, stride=2), :].astype(BF16)
        w2_scr[...] = w2_ref[0].astype(BF16)

        def one_block(blk, carry):
            slot = blk % 2

            @pl.when(blk + 1 < n_used)
            def _():
                x_copy(blk + 1, 1 - slot).start()

            x_copy(blk, slot).wait()

            @pl.when(blk >= 2)
            def _():
                o_copy(blk - 2, slot).wait()

            xb = _load_token_tiles(xbuf.at[slot], 0, bm).astype(BF16)
            bias = b1_ref[0]
            glu = jnp.minimum(_dot_nt(xb, wg_scr[...]) + bias[:, :dh], SWIGLU_LIMIT)
            lin = jnp.clip(_dot_nt(xb, wl_scr[...]) + bias[:, dh:], -SWIGLU_LIMIT, SWIGLU_LIMIT)
            act = glu * jax.nn.sigmoid(SWIGLU_ALPHA * glu) * (lin + 1.0)
            _store_token_tiles(obuf.at[slot], _dot(act.astype(BF16), w2_scr[...]) + b2_ref[0])
            o_copy(blk, slot).start()
            return carry

        lax.fori_loop(blk_lo, blk_hi, one_block, 0)

    @pl.when(e == pl.num_programs(0) - 1)
    def _():
        @pl.when(n_used >= 1)
        def _():
            o_copy(0, (n_used - 1) % 2).wait()

        @pl.when(n_used >= 2)
        def _():
            o_copy(0, n_used % 2).wait()

        obuf[0] = jnp.zeros(obuf.shape[1:], obuf.dtype)
        n_blocks = yout_ref.shape[0] // rows_blk

        def start_tail(blk, carry):
            o_copy(blk, 0).start()
            return carry

        def wait_tail(blk, carry):
            o_copy(0, 0).wait()
            return carry

        lax.fori_loop(n_used, n_blocks, start_tail, 0)
        lax.fori_loop(n_used, n_blocks, wait_tail, 0)


def _experts(pad_end, n_used, xin, w1, b1p, w2, b2):
    ne, d, dh2 = w1.shape
    dh = dh2 // 2
    rows_blk = BM_MOE * SUBLANES
    grid_spec = pltpu.PrefetchScalarGridSpec(
        num_scalar_prefetch=2,
        grid=(ne,),
        in_specs=[
            pl.BlockSpec((1, d, dh2), lambda e, pe, nu: (e, 0, 0)),
            pl.BlockSpec((1, 1, dh2), lambda e, pe, nu: (e, 0, 0)),
            pl.BlockSpec((1, dh, d), lambda e, pe, nu: (e, 0, 0)),
            pl.BlockSpec((1, 1, d), lambda e, pe, nu: (e, 0, 0)),
            pl.BlockSpec(memory_space=pl.ANY),
        ],
        out_specs=pl.BlockSpec(memory_space=pl.ANY),
        scratch_shapes=[
            pltpu.VMEM((2, rows_blk, LANES), F32),
            pltpu.VMEM((2, rows_blk, LANES), F32),
            pltpu.VMEM((d // 128, dh, 128), F32),
            pltpu.VMEM((dh, d), BF16),
            pltpu.VMEM((dh, d), BF16),
            pltpu.VMEM((dh, d), BF16),
            pltpu.SemaphoreType.DMA((2,)),
            pltpu.SemaphoreType.DMA((2,)),
        ],
    )
    return pl.pallas_call(
        _experts_kernel,
        grid_spec=grid_spec,
        out_shape=jax.ShapeDtypeStruct(xin.shape, F32),
        compiler_params=_params(("arbitrary",)),
        name="experts",
    )(pad_end, n_used, w1, b1p, w2, b2, xin)


def _combine_kernel(pos_ref, pos_next_ref, gate_ref, h1_ref, fg_ref, yout_ref, o_ref, ybuf, sem):
    tc = h1_ref.shape[0]
    i = pl.program_id(0)
    slot = i % 2

    def tile(ref, token):
        return ref.at[pl.ds(pl.multiple_of(token * SUBLANES, SUBLANES), SUBLANES)]

    def gather(p_ref, s):
        def issue(tk, carry):
            for j in range(TOP_K):
                pltpu.make_async_copy(tile(yout_ref, p_ref[j, tk]), tile(ybuf.at[s], j * tc + tk),
                                      sem.at[s]).start(priority=j % 2)
            return carry
        lax.fori_loop(0, tc, issue, 0, unroll=8)

    @pl.when(i == 0)
    def _():
        gather(pos_ref, 0)

    @pl.when(i + 1 < pl.num_programs(0))
    def _():
        gather(pos_next_ref, 1 - slot)

    pltpu.make_async_copy(yout_ref.at[pl.ds(0, ybuf.shape[1])], ybuf.at[slot], sem.at[slot]).wait()

    gates = gate_ref[...]
    h = h1_ref[...]
    for j in range(TOP_K):
        h = h + _load_token_tiles(ybuf.at[slot], j * tc * SUBLANES, tc) * gates[:, j:j + 1]
    o_ref[...] = h * lax.rsqrt(jnp.mean(h * h, axis=-1, keepdims=True) + RMS_EPS) * fg_ref[...]


def _combine(pos, gates_t, h1, fg, yout):
    t, d = h1.shape
    tc = TC_COMBINE
    n_steps = t // tc
    return pl.pallas_call(
        _combine_kernel,
        grid=(n_steps,),
        in_specs=[
            pl.BlockSpec((TOP_K, tc), lambda i: (0, i), memory_space=pltpu.SMEM),
            pl.BlockSpec((TOP_K, tc), lambda i: (0, jnp.minimum(i + 1, n_steps - 1)),
                         memory_space=pltpu.SMEM),
            pl.BlockSpec((tc, TOP_K), lambda i: (i, 0)),
            pl.BlockSpec((tc, d), lambda i: (i, 0)),
            pl.BlockSpec((1, d), lambda i: (0, 0)),
            pl.BlockSpec(memory_space=pl.ANY),
        ],
        out_specs=pl.BlockSpec((tc, d), lambda i: (i, 0)),
        out_shape=jax.ShapeDtypeStruct((t, d), F32),
        scratch_shapes=[pltpu.VMEM((2, TOP_K * tc * SUBLANES, LANES), F32),
                        pltpu.SemaphoreType.DMA((2,))],
        compiler_params=_params(("arbitrary",)),
        name="combine",
    )(pos, pos, gates_t, h1, fg, yout)


def kernel(x, norm1_g, w_in, sgu_ln_g, sgu_ln_b, sgu_w, sgu_b, mu_rkv, mu_wag, decay_w0, decay_w1,
           decay_w2, iclr_a0, iclr_a1, iclr_a2, gate_g1, gate_g2, k_k, k_a, r_k, lnx_g, lnx_b, w_out,
           norm2_g, router_w, router_b, moe_w1, moe_b1, moe_w2, moe_b2, final_g):
    batch, seq, d = x.shape
    t = batch * seq
    depth = w_in.shape[0]
    sgu_width = sgu_ln_g.shape[1]
    rw = mu_rkv.shape[2]
    n_dec, n_icl, n_gate = decay_w1.shape[2], iclr_a1.shape[2], gate_g1.shape[2]
    assert n_dec == HEAD and n_icl == HEAD and n_gate == PAIR and rw % PAIR == 0
    assert seq % TT_RWKV == 0 and seq % TT_PREP == 0 and t % TM_PROJ == 0
    assert sgu_w.shape[2] == SGU_BLOCK
    assert depth == 1, "the final RMSNorm is fused into the last layer's combine kernel"
    assert d == SUBLANES * LANES, "the MoE row movers copy one (8, 128) f32 tile per token"

    h = x.reshape(t, d)
    for l in range(depth):
        win_bf = w_in[l].astype(BF16)
        wl = jnp.concatenate([decay_w1[l], iclr_a1[l], gate_g1[l]], axis=1)
        mucat = jnp.concatenate([
            jnp.broadcast_to(mu_wag[l, 0][:, None], (d, n_dec)),
            jnp.broadcast_to(mu_wag[l, 1][:, None], (d, n_icl)),
            jnp.broadcast_to(mu_wag[l, 2][:, None], (d, n_gate))], axis=1)
        zeros = jnp.zeros((HEAD, rw), F32)
        w2pad = jnp.concatenate([decay_w2[l], zeros], axis=0)
        a2pad = jnp.concatenate([zeros, iclr_a2[l]], axis=0)
        pvec = jnp.stack([decay_w0[l], iclr_a0[l], k_k[l], k_a[l], r_k[l].reshape(-1),
                          lnx_g[l], lnx_b[l], jnp.zeros((rw,), F32)], axis=0)
        bias2d = jnp.repeat(sgu_b[l].T, sgu_width // sgu_b.shape[1], axis=1)
        wo_bf = w_out[l].astype(BF16)
        b1p = jnp.concatenate([moe_b1[l][:, 0::2], moe_b1[l][:, 1::2]], axis=-1)[:, None, :]
        b2 = moe_b2[l][:, None, :]

        proj, lora = _in_proj(h, norm1_g[l][None, :], win_bf, wl, mucat)
        ya = _sgu(proj, sgu_ln_g[l][None, :], sgu_ln_b[l][None, :], sgu_w[l], bias2d)
        rp, yq, bonus, gate, g_m, h_m = _rwkv_prep(proj, lora, mu_rkv[l], pvec, w2pad, a2pad,
                                                   gate_g2[l], seq, 2 * sgu_width)
        yb = _rwkv_scan(rp, yq, bonus, gate, g_m, h_m, pvec, batch)
        h1, hn2, logits_t = _out_proj(h, ya, yb, wo_bf, norm2_g[l][None, :], router_w[l].T,
                                      router_b[l][:, None])
        pos, gates, pend = _route(logits_t)

        pad_end = pend[:, 0]
        n_blocks = (t * TOP_K) // BM_MOE + N_EXPERTS
        n_used = pad_end[-1:] // BM_MOE

        xin = _dispatch(pad_end, n_used, pos, hn2, n_blocks * BM_MOE)
        yout = _experts(pad_end, n_used, xin, moe_w1[l], b1p, moe_w2[l], b2)
        h = _combine(pos, gates.T, h1, final_g[None, :], yout)
    return h.reshape(batch, seq, d)
```

```python
import functools

import jax
import jax.numpy as jnp
from jax import lax
from jax.experimental import pallas as pl
from jax.experimental.pallas import tpu as pltpu

F32 = jnp.float32
BF16 = jnp.bfloat16
I32 = jnp.int32

RMS_EPS = 1e-5
LN_EPS = 1e-5
LNX_EPS = 64e-5
CHUNK = 64
SGU_BLOCK = 128
HEAD = 64
PAIR = 2 * HEAD
N_EXPERTS = 32
TOP_K = 4
SWIGLU_ALPHA = 1.702
SWIGLU_LIMIT = 7.0

V7X_VMEM_LIMIT = 56 * 1024 * 1024

TM_PROJ = 512
TT_PREP = 1024
TT_RWKV = 256
TR_ROUTE = 512
BM_MOE = 256
TD_DISPATCH = 512
TC_COMBINE = 256


def _dot(a, b):
    return jnp.dot(a, b, preferred_element_type=F32)


def _dot_nt(a, b):
    return lax.dot_general(a, b, (((1,), (1,)), ((), ())), preferred_element_type=F32)


def _split(x):
    hi = x.astype(BF16)
    lo = (x - hi.astype(F32)).astype(BF16)
    return hi, lo


def _dot3(a, b):
    ah, al = _split(a)
    bh, bl = _split(b)
    return _dot(ah, bh) + _dot(al, bh) + _dot(ah, bl)


def _dot3_nt(a, b):
    ah, al = _split(a)
    bh, bl = _split(b)
    return _dot_nt(ah, bh) + _dot_nt(al, bh) + _dot_nt(ah, bl)


def _head_sum(x, bd):
    hi, lo = _split(x)
    return _dot(hi, bd) + _dot(lo, bd)


SUBLANES = 8
LANES = 128


def _store_token_tiles(ref, x):
    n = x.shape[0]
    for s in range(SUBLANES):
        ref[pl.ds(s, n, stride=SUBLANES), :] = x[:, s * LANES:(s + 1) * LANES]


def _load_token_tiles(ref, row0, n):
    return jnp.concatenate(
        [ref[pl.ds(row0 + s, n, stride=SUBLANES), :] for s in range(SUBLANES)], axis=1)


def _params(sem, vmem=V7X_VMEM_LIMIT):
    return pltpu.CompilerParams(dimension_semantics=sem, vmem_limit_bytes=vmem)


def _in_proj_kernel(x_ref, g_ref, win_ref, wl_ref, mucat_ref, lng_ref, lnb_ref, w_ref, bias_ref,
                    proj_ref, lora_ref, o_ref, wl_scr, wm_scr):
    n_heads = w_ref.shape[0]

    @pl.when(pl.program_id(0) == 0)
    def _():
        wl = wl_ref[...]
        mu = mucat_ref[...]
        n = wl.shape[1]
        wl_scr[:, 0:n] = (wl * (1.0 - mu)).astype(BF16)
        wl_scr[:, n:2 * n] = (wl * mu).astype(BF16)
        qi = lax.broadcasted_iota(I32, (SGU_BLOCK, SGU_BLOCK), 0) // CHUNK
        kj = lax.broadcasted_iota(I32, (SGU_BLOCK, SGU_BLOCK), 1) // CHUNK
        for h in range(n_heads):
            wm_scr[h] = jnp.where(kj <= qi, w_ref[h], 0.0).astype(BF16)

    x = x_ref[...]
    hn = x * lax.rsqrt(jnp.mean(x * x, axis=-1, keepdims=True) + RMS_EPS) * g_ref[...]
    hb = hn.astype(BF16)
    width = lng_ref.shape[1]
    two_w = 2 * width
    proj_ref[...] = _dot(hb, win_ref[:, two_w:])
    lora_ref[...] = _dot(hb, wl_scr[...])

    z = _dot(hb, win_ref[:, :two_w])
    tm = z.shape[0]
    gz = 0.5 * z * (1.0 + lax.erf(z * (2.0 ** -0.5)))
    u = gz[:, :width]
    v = gz[:, width:]
    mu = jnp.mean(v, axis=-1, keepdims=True)
    vc = v - mu
    var = jnp.mean(vc * vc, axis=-1, keepdims=True)
    vn = vc * lax.rsqrt(var + LN_EPS) * lng_ref[...] + lnb_ref[...]
    lane = lax.broadcasted_iota(I32, (SGU_BLOCK, PAIR), 1)
    is_lo = lane < HEAD
    bias = bias_ref[...]
    for blk in range(tm // SGU_BLOCK):
        rows = slice(blk * SGU_BLOCK, (blk + 1) * SGU_BLOCK)
        for p in range(width // PAIR):
            cols = slice(p * PAIR, (p + 1) * PAIR)
            vp = vn[rows, cols]
            lo = jnp.where(is_lo, vp, 0.0).astype(BF16)
            hi = jnp.where(is_lo, 0.0, vp).astype(BF16)
            sv = _dot(wm_scr[2 * p], lo) + _dot(wm_scr[2 * p + 1], hi)
            o_ref[rows, cols] = u[rows, cols] * (sv + bias[:, cols])


def _in_proj(x2, g, win_bf, wl, mucat, ln_g, ln_b, w_s, bias2d):
    t, d = x2.shape
    width = ln_g.shape[1]
    n_rkv = win_bf.shape[1] - 2 * width
    n_l = wl.shape[1]
    n_heads = w_s.shape[0]
    tm = TM_PROJ
    const = lambda *shape: pl.BlockSpec(shape, lambda i: (0,) * len(shape))
    row = lambda n: pl.BlockSpec((tm, n), lambda i: (i, 0))
    return pl.pallas_call(
        _in_proj_kernel,
        grid=(t // tm,),
        in_specs=[
            row(d), const(1, d), const(d, win_bf.shape[1]), const(d, n_l), const(d, n_l),
            const(1, width), const(1, width), const(n_heads, SGU_BLOCK, SGU_BLOCK),
            const(SGU_BLOCK, width),
        ],
        out_specs=[row(n_rkv), row(2 * n_l), row(width)],
        out_shape=[
            jax.ShapeDtypeStruct((t, n_rkv), F32),
            jax.ShapeDtypeStruct((t, 2 * n_l), F32),
            jax.ShapeDtypeStruct((t, width), F32),
        ],
        scratch_shapes=[pltpu.VMEM((d, 2 * n_l), BF16),
                        pltpu.VMEM((n_heads, SGU_BLOCK, SGU_BLOCK), BF16)],
        compiler_params=_params(("arbitrary",)),
        name="in_proj_sgu",
    )(x2, g, win_bf, wl, mucat, ln_g, ln_b, w_s, bias2d)


def _rwkv_prep_kernel(seq_len, pr_ref, pk_ref, pv_ref, ppr_ref, ppk_ref, ppv_ref, lo_ref, plo_ref,
                      mu_ref, pvec_ref, w2_ref, a2_ref, g2_ref,
                      rp_ref, yq_ref, bonus_ref, gate_ref, g_ref, h_ref):
    tt = pr_ref.shape[0]
    n_chunks = tt // CHUNK
    i = pl.program_id(0)
    keep = jnp.where((i * tt) % seq_len == 0, 0.0, 1.0)

    def shift(x, prev):
        rolled = pltpu.roll(x, 1, 0)
        first = prev[7:8, :] * keep
        rowc = lax.broadcasted_iota(I32, x.shape, 0)
        return jnp.where(rowc == 0, first, rolled)

    mu = mu_ref[...]

    def shift_mix(ref, pref, m):
        x = ref[...]
        return x + (shift(x, pref[...]) - x) * m

    r = shift_mix(pr_ref, ppr_ref, mu[0:1])
    k = shift_mix(pk_ref, ppk_ref, mu[1:2])
    v = shift_mix(pv_ref, ppv_ref, mu[2:3])

    lo = lo_ref[...]
    plo = plo_ref[...]
    half = lo.shape[1] // 2
    l_all = lo[:, :half] + shift(lo[:, half:], plo[:, half:])
    l_wa = l_all[:, :PAIR]
    l_g = l_all[:, PAIR:]

    pvec = pvec_ref[...]
    w0, a0, k_k, k_a, r_k = pvec[0:1], pvec[1:2], pvec[2:3], pvec[3:4], pvec[4:5]

    dw = _dot(jnp.tanh(l_wa).astype(BF16), w2_ref[...].astype(BF16))
    ia = _dot(l_wa.astype(BF16), a2_ref[...].astype(BF16))
    gate_ref[...] = _dot(jax.nn.sigmoid(l_g).astype(BF16), g2_ref[...].astype(BF16)).astype(BF16)

    zneg = -(w0 + dw)
    softplus = jnp.maximum(zneg, 0.0) + jnp.log(1.0 + jnp.exp(-jnp.abs(zneg)))
    logw = -jnp.exp(-softplus - 0.5)
    iclr = jax.nn.sigmoid(a0 + ia)

    lane_r = lax.broadcasted_iota(I32, (PAIR, PAIR), 0)
    lane_c = lax.broadcasted_iota(I32, (PAIR, PAIR), 1)
    bd = jnp.where((lane_r // HEAD) == (lane_c // HEAD), 1.0, 0.0).astype(BF16)

    kk = k * k_k
    kk = kk / jnp.maximum(jnp.sqrt(_head_sum(kk * kk, bd)), 1e-12)
    k2 = k * (1.0 + (iclr - 1.0) * k_a)
    a = -kk
    b = kk * iclr
    bonus_ref[...] = (_head_sum(r * k2 * r_k, bd) * v).astype(BF16)

    rc = lax.broadcasted_iota(I32, (tt, PAIR), 0) % CHUNK
    cl = logw
    s = 1
    while s < CHUNK:
        cl = cl + jnp.where(rc >= s, pltpu.roll(cl, s, 0), 0.0)
        s *= 2
    at_all = a * jnp.exp(cl - logw)
    rt_all = r * jnp.exp(cl)
    w_inv = jnp.exp(-cl)
    bt_all = b * w_inv
    kt_all = k2 * w_inv

    lane = lax.broadcasted_iota(I32, (CHUNK, PAIR), 1)
    is_lo = lane < HEAD

    def stack(x):
        return jnp.concatenate([jnp.where(is_lo, x, 0.0), jnp.where(is_lo, 0.0, x)], axis=0)

    n2 = 2 * PAIR
    ri = lax.broadcasted_iota(I32, (n2, n2), 0)
    ci = lax.broadcasted_iota(I32, (n2, n2), 1)
    same_head = ((ri // CHUNK) % 2) == ((ci // CHUNK) % 2)
    t_i = ri % CHUNK
    s_i = ci % CHUNK
    tri = same_head & (s_i < t_i + ri // PAIR)
    eye = lane_r == lane_c

    chunks = range(n_chunks)
    rows = [slice(c * CHUNK, (c + 1) * CHUNK) for c in chunks]
    last = [cl[rows[c]][CHUNK - 1:CHUNK, :] for c in chunks]
    w_rem = [jnp.exp(last[c] - cl[rows[c]]) for c in chunks]
    a_s = [stack(at_all[rows[c]]) for c in chunks]
    r_s = [stack(rt_all[rows[c]]) for c in chunks]
    v_s = [stack(v[rows[c]]) for c in chunks]
    v_sb = [v_s[c].astype(BF16) for c in chunks]
    bk_h = [jnp.concatenate([stack(b[rows[c]] * w_rem[c]), stack(k2[rows[c]] * w_rem[c])],
                            axis=0).astype(BF16) for c in chunks]

    a_all = []
    for c in chunks:
        lhs = jnp.concatenate([a_s[c], r_s[c]], axis=0).astype(BF16)
        rhs = jnp.concatenate([stack(bt_all[rows[c]]), stack(kt_all[rows[c]])], axis=0).astype(BF16)
        a_all.append(jnp.where(tri, _dot_nt(lhs, rhs), 0.0))
    n_k = [a_all[c][:PAIR, :PAIR].astype(BF16) for c in chunks]
    a_rb = [a_all[c][PAIR:, :PAIR].astype(BF16) for c in chunks]
    x = [jnp.concatenate([a_s[c], _dot(a_all[c][:PAIR, PAIR:].astype(BF16), v_sb[c])], axis=1)
         for c in chunks]
    ry0 = [jnp.concatenate([r_s[c], _dot(a_all[c][PAIR:, PAIR:].astype(BF16), v_sb[c])], axis=1)
           for c in chunks]
    steps = CHUNK.bit_length() - 1
    for it in range(steps):
        x = [x[c] + _dot(n_k[c], x[c].astype(BF16)) for c in chunks]
        if it + 1 < steps:
            n_k = [_dot(n_k[c], n_k[c]).astype(BF16) for c in chunks]
    ry = [ry0[c] + _dot(a_rb[c], x[c].astype(BF16)) for c in chunks]
    for c in chunks:
        rp_ref[rows[c], :] = (ry[c][:CHUNK, :PAIR] + ry[c][CHUNK:, :PAIR]).astype(BF16)
        yq_ref[rows[c], :] = (ry[c][:CHUNK, PAIR:] + ry[c][CHUNK:, PAIR:]).astype(BF16)
    p_t = [x[c][:, :PAIR].T.astype(BF16) for c in chunks]
    qv_t = [jnp.concatenate([x[c][:, PAIR:], v_s[c]], axis=0).T.astype(BF16) for c in chunks]
    for c in chunks:
        g_ref[c, 0] = jnp.where(eye, jnp.exp(last[c]), 0.0) + _dot(p_t[c], bk_h[c][:PAIR])
    for c in chunks:
        hbd = _dot(qv_t[c], bk_h[c])
        h_ref[c, 0] = hbd[:HEAD] + hbd[HEAD:]


def _rwkv_prep(proj, lora, mu_rkv, pvec, w2pad, a2pad, g2, seq_len, col0):
    t = proj.shape[0]
    width = mu_rkv.shape[1]
    n_pairs = width // PAIR
    tt = TT_PREP
    nl = lora.shape[1]
    cb = col0 // PAIR
    wb = width // PAIR

    def prev_rows(i):
        return jnp.maximum(i * (tt // 8) - 1, 0)

    def tok(c):
        return pl.BlockSpec((tt, PAIR), lambda i, p: (i, c + p))

    def prev(c):
        return pl.BlockSpec((8, PAIR), lambda i, p: (prev_rows(i), c + p))

    out_tok = pl.BlockSpec((tt, PAIR), lambda i, p: (i, p))
    tok_shape = jax.ShapeDtypeStruct((t, width), BF16)
    return pl.pallas_call(
        functools.partial(_rwkv_prep_kernel, seq_len),
        grid=(t // tt, n_pairs),
        in_specs=[
            tok(cb), tok(cb + wb), tok(cb + 2 * wb),
            prev(cb), prev(cb + wb), prev(cb + 2 * wb),
            pl.BlockSpec((tt, nl), lambda i, p: (i, 0)),
            pl.BlockSpec((8, nl), lambda i, p: (prev_rows(i), 0)),
            pl.BlockSpec((3, PAIR), lambda i, p: (0, p)),
            pl.BlockSpec((8, PAIR), lambda i, p: (0, p)),
            pl.BlockSpec((PAIR, PAIR), lambda i, p: (0, p)),
            pl.BlockSpec((PAIR, PAIR), lambda i, p: (0, p)),
            pl.BlockSpec((PAIR, PAIR), lambda i, p: (0, p)),
        ],
        out_specs=[
            out_tok, out_tok, out_tok, out_tok,
            pl.BlockSpec((tt // CHUNK, 1, PAIR, PAIR), lambda i, p: (i, p, 0, 0)),
            pl.BlockSpec((tt // CHUNK, 1, HEAD, PAIR), lambda i, p: (i, p, 0, 0)),
        ],
        out_shape=[
            tok_shape, tok_shape, tok_shape, tok_shape,
            jax.ShapeDtypeStruct((t // CHUNK, n_pairs, PAIR, PAIR), F32),
            jax.ShapeDtypeStruct((t // CHUNK, n_pairs, HEAD, PAIR), F32),
        ],
        compiler_params=_params(("arbitrary", "arbitrary")),
        name="rwkv_prep",
    )(proj, proj, proj, proj, proj, proj, lora, lora, mu_rkv, pvec, w2pad, a2pad, g2)


def _rwkv_scan_kernel(rp_ref, yq_ref, bonus_ref, gate_ref, g_ref, h_ref, pvec_ref, o_ref, s_scr, y_scr):
    nb, tt, width = rp_ref.shape
    n_pairs = width // PAIR

    @pl.when(pl.program_id(0) == 0)
    def _():
        s_scr[...] = jnp.zeros_like(s_scr)

    lane = lax.broadcasted_iota(I32, (HEAD, PAIR), 1)
    is_lo = lane < HEAD
    lane_r = lax.broadcasted_iota(I32, (PAIR, PAIR), 0)
    lane_c = lax.broadcasted_iota(I32, (PAIR, PAIR), 1)
    bd = jnp.where((lane_r // HEAD) == (lane_c // HEAD), 1.0, 0.0).astype(BF16)

    chains = [(b, p) for b in range(nb) for p in range(n_pairs)]
    state = {ch: s_scr[ch[0], ch[1]] for ch in chains}
    for c in range(tt // CHUNK):
        rows = slice(c * CHUNK, (c + 1) * CHUNK)
        for (b, p) in chains:
            cols = slice(p * PAIR, (p + 1) * PAIR)
            s0 = state[(b, p)]
            s_st = jnp.concatenate([jnp.where(is_lo, s0, 0.0), jnp.where(is_lo, 0.0, s0)], axis=0)
            y_scr[b, rows, cols] = (_dot_nt(rp_ref[b, rows, cols], s_st.astype(BF16))
                                    + yq_ref[b, rows, cols].astype(F32))
        state = {(b, p): _dot3(state[(b, p)], g_ref[b, c, p]) + h_ref[b, c, p] for (b, p) in chains}
    for (b, p) in chains:
        s_scr[b, p] = state[(b, p)]

    pvec = pvec_ref[...]
    for (b, p) in chains:
        cols = slice(p * PAIR, (p + 1) * PAIR)
        y = y_scr[b, :, cols]
        mu = _head_sum(y, bd) * (1.0 / HEAD)
        yc = y - mu
        var = _head_sum(yc * yc, bd) * (1.0 / HEAD)
        yn = yc * lax.rsqrt(var + LNX_EPS) * pvec[5:6, cols] + pvec[6:7, cols]
        o_ref[b, :, cols] = ((yn + bonus_ref[b, :, cols].astype(F32))
                             * gate_ref[b, :, cols].astype(F32))


def _rwkv_scan(rp, yq, bonus, gate, g, h, pvec, batch):
    t, width = rp.shape
    n_pairs = width // PAIR
    seq = t // batch
    tt = TT_RWKV
    nc = tt // CHUNK
    tok = pl.BlockSpec((batch, tt, width), lambda i: (0, i, 0))
    as_seq = lambda z: z.reshape(batch, seq, width)
    out = pl.pallas_call(
        _rwkv_scan_kernel,
        grid=(seq // tt,),
        in_specs=[
            tok, tok, tok, tok,
            pl.BlockSpec((batch, nc, n_pairs, PAIR, PAIR), lambda i: (0, i, 0, 0, 0)),
            pl.BlockSpec((batch, nc, n_pairs, HEAD, PAIR), lambda i: (0, i, 0, 0, 0)),
            pl.BlockSpec((8, width), lambda i: (0, 0)),
        ],
        out_specs=tok,
        out_shape=jax.ShapeDtypeStruct((batch, seq, width), F32),
        scratch_shapes=[pltpu.VMEM((batch, n_pairs, HEAD, PAIR), F32),
                        pltpu.VMEM((batch, tt, width), F32)],
        compiler_params=_params(("arbitrary",)),
        name="rwkv_scan",
    )(as_seq(rp), as_seq(yq), as_seq(bonus), as_seq(gate),
      g.reshape(batch, seq // CHUNK, n_pairs, PAIR, PAIR),
      h.reshape(batch, seq // CHUNK, n_pairs, HEAD, PAIR), pvec)
    return out.reshape(t, width)


def _out_proj_kernel(x_ref, ya_ref, yb_ref, wo_ref, g_ref, rwt_ref, rb_ref, h1_ref, hn_ref, lg_ref):
    wa = ya_ref.shape[1]
    h1 = (x_ref[...] + _dot(ya_ref[...].astype(BF16), wo_ref[0:wa, :])
          + _dot(yb_ref[...].astype(BF16), wo_ref[wa:, :]))
    h1_ref[...] = h1
    hn = h1 * lax.rsqrt(jnp.mean(h1 * h1, axis=-1, keepdims=True) + RMS_EPS) * g_ref[...]
    _store_token_tiles(hn_ref, hn)
    lg_ref[...] = _dot3_nt(rwt_ref[...], hn) + rb_ref[...]


def _out_proj(x2, ya, yb, wo_bf, g2, rwt, rb):
    t, d = x2.shape
    wa = ya.shape[1]
    wb = yb.shape[1]
    ne = rwt.shape[0]
    tm = TM_PROJ
    return pl.pallas_call(
        _out_proj_kernel,
        grid=(t // tm,),
        in_specs=[
            pl.BlockSpec((tm, d), lambda i: (i, 0)),
            pl.BlockSpec((tm, wa), lambda i: (i, 0)),
            pl.BlockSpec((tm, wb), lambda i: (i, 0)),
            pl.BlockSpec((wa + wb, d), lambda i: (0, 0)),
            pl.BlockSpec((1, d), lambda i: (0, 0)),
            pl.BlockSpec((ne, d), lambda i: (0, 0)),
            pl.BlockSpec((ne, 1), lambda i: (0, 0)),
        ],
        out_specs=[
            pl.BlockSpec((tm, d), lambda i: (i, 0)),
            pl.BlockSpec((tm * SUBLANES, LANES), lambda i: (i, 0)),
            pl.BlockSpec((ne, tm), lambda i: (0, i)),
        ],
        out_shape=[
            jax.ShapeDtypeStruct((t, d), F32),
            jax.ShapeDtypeStruct((t * SUBLANES, LANES), F32),
            jax.ShapeDtypeStruct((ne, t), F32),
        ],
        compiler_params=_params(("arbitrary",)),
        name="out_proj",
    )(x2, ya, yb, wo_bf, g2, rwt, rb)


def _route_kernel(lg_ref, pos_ref, gate_ref, pend_ref, carry_scr, pstart_scr):
    phase = pl.program_id(0)
    first = pl.program_id(1) == 0

    @pl.when(first & (phase == 0))
    def _():
        carry_scr[...] = jnp.zeros_like(carry_scr)
        pstart_scr[...] = jnp.zeros_like(pstart_scr)
        pend_ref[...] = jnp.zeros_like(pend_ref)

    @pl.when(first & (phase == 1))
    def _():
        counts = carry_scr[...]
        padded = jnp.ceil(counts * (1.0 / BM_MOE)) * BM_MOE
        row = lax.broadcasted_iota(I32, counts.shape, 0)
        end = padded
        s = 1
        while s < counts.shape[0]:
            end = end + jnp.where(row >= s, pltpu.roll(end, s, 0), 0.0)
            s *= 2
        pstart_scr[...] = end - padded
        pend_ref[...] = end.astype(I32)
        carry_scr[...] = jnp.zeros_like(carry_scr)

    l = lg_ref[...]
    ne, tr = l.shape
    e_iota = lax.broadcasted_iota(I32, (ne, tr), 0)
    chosen = jnp.zeros((ne, tr), F32)
    vals, sels = [], []
    for j in range(TOP_K):
        m = jnp.max(l, axis=0, keepdims=True)
        idx = jnp.min(jnp.where(l == m, e_iota, ne), axis=0, keepdims=True)
        sel = e_iota == idx
        vals.append(m)
        sels.append(sel)
        chosen = jnp.where(sel, 1.0, chosen)
        l = jnp.where(sel, -jnp.inf, l)
    ex = [jnp.exp(vj - vals[0]) for vj in vals]
    den = ex[0] + ex[1] + ex[2] + ex[3]
    for j in range(TOP_K):
        gate_ref[j:j + 1, :] = ex[j] / den

    ti = lax.broadcasted_iota(I32, (tr, tr), 0)
    tj = lax.broadcasted_iota(I32, (tr, tr), 1)
    upper = jnp.where(ti <= tj, 1.0, 0.0).astype(BF16)
    inc = _dot(chosen.astype(BF16), upper)
    carry = carry_scr[...]
    row_of = inc - chosen + carry[:, 0:1] + pstart_scr[:, 0:1]
    for j in range(TOP_K):
        pj = jnp.sum(jnp.where(sels[j], row_of, 0.0), axis=0, keepdims=True)
        pos_ref[j:j + 1, :] = pj.astype(I32)
    carry_scr[...] = carry + inc[:, tr - 1:tr]


def _route(logits_t):
    ne, t = logits_t.shape
    tr = TR_ROUTE
    tok = pl.BlockSpec((TOP_K, tr), lambda ph, i: (0, i * ph))
    return pl.pallas_call(
        _route_kernel,
        grid=(2, t // tr),
        in_specs=[pl.BlockSpec((ne, tr), lambda ph, i: (0, i))],
        out_specs=[tok, tok, pl.BlockSpec((ne, 128), lambda ph, i: (0, 0))],
        out_shape=[
            jax.ShapeDtypeStruct((TOP_K, t), I32),
            jax.ShapeDtypeStruct((TOP_K, t), F32),
            jax.ShapeDtypeStruct((ne, 128), I32),
        ],
        scratch_shapes=[pltpu.VMEM((ne, 128), F32), pltpu.VMEM((ne, 128), F32)],
        compiler_params=_params(("arbitrary", "arbitrary")),
        name="route",
    )(logits_t)


def _dispatch_kernel(pend_ref, nused_ref, pos_ref, hn_ref, xin_ref, zero_scr, sem):
    bm = zero_scr.shape[0] // SUBLANES
    td = pos_ref.shape[0] // TOP_K
    i = pl.program_id(0)

    def tile(ref, token):
        return ref.at[pl.ds(pl.multiple_of(token * SUBLANES, SUBLANES), SUBLANES)]

    def zero_block(row0):
        start = pl.multiple_of(row0 * SUBLANES, bm * SUBLANES)
        return pltpu.make_async_copy(zero_scr, xin_ref.at[pl.ds(start, bm * SUBLANES)], sem)

    @pl.when(i == 0)
    def _():
        zero_scr[...] = jnp.zeros_like(zero_scr)

        def has_rows(e):
            end = pend_ref[e]
            return jnp.where(e == 0, end >= bm, end > pend_ref[jnp.maximum(e - 1, 0)])

        def start_one(e, carry):
            @pl.when(has_rows(e))
            def _():
                zero_block(pend_ref[e] - bm).start()
            return carry

        def wait_one(e, carry):
            @pl.when(has_rows(e))
            def _():
                zero_block(0).wait()
            return carry

        def start_tail(blk, carry):
            zero_block(blk * bm).start()
            return carry

        def wait_tail(blk, carry):
            zero_block(0).wait()
            return carry

        n_blocks = xin_ref.shape[0] // (bm * SUBLANES)
        lax.fori_loop(0, pend_ref.shape[0], start_one, 0)
        lax.fori_loop(nused_ref[0], n_blocks, start_tail, 0)
        lax.fori_loop(0, pend_ref.shape[0], wait_one, 0)
        lax.fori_loop(nused_ref[0], n_blocks, wait_tail, 0)

    def issue(tk, carry):
        for j in range(TOP_K):
            pltpu.make_async_copy(tile(hn_ref, tk), tile(xin_ref, pos_ref[TOP_K * tk + j]),
                                  sem).start(priority=j % 2)
        return carry

    lax.fori_loop(0, td, issue, 0, unroll=8)

    for j in range(TOP_K):
        pltpu.make_async_copy(hn_ref, xin_ref.at[pl.ds(0, td * SUBLANES)], sem).wait()


def _dispatch(pad_end, n_used, pos, hn_tiles, n_rows):
    t = hn_tiles.shape[0] // SUBLANES
    td = TD_DISPATCH
    grid_spec = pltpu.PrefetchScalarGridSpec(
        num_scalar_prefetch=2,
        grid=(t // td,),
        in_specs=[
            pl.BlockSpec((TOP_K * td,), lambda i, pe, nu: (i,), memory_space=pltpu.SMEM),
            pl.BlockSpec((td * SUBLANES, LANES), lambda i, pe, nu: (i, 0)),
        ],
        out_specs=pl.BlockSpec(memory_space=pl.ANY),
        scratch_shapes=[pltpu.VMEM((BM_MOE * SUBLANES, LANES), F32), pltpu.SemaphoreType.DMA(())],
    )
    return pl.pallas_call(
        _dispatch_kernel,
        grid_spec=grid_spec,
        out_shape=jax.ShapeDtypeStruct((n_rows * SUBLANES, LANES), F32),
        compiler_params=_params(("arbitrary",)),
        name="dispatch",
    )(pad_end, n_used, pos, hn_tiles)


def _experts_kernel(pend_ref, nused_ref, w1_ref, b1_ref, w2_ref, b2_ref, xin_ref, yout_ref,
                    xbuf, obuf, wt_scr, wg_scr, wl_scr, w2_scr, sem_in, sem_out):
    e = pl.program_id(0)
    rows_blk = xbuf.shape[1]
    bm = rows_blk // SUBLANES
    dh = w2_ref.shape[1]
    n_used = nused_ref[0]
    blk_lo = jnp.where(e == 0, 0, pend_ref[jnp.maximum(e - 1, 0)]) // bm
    blk_hi = pend_ref[e] // bm

    def block(ref, blk):
        return ref.at[pl.ds(pl.multiple_of(blk * rows_blk, rows_blk), rows_blk)]

    def x_copy(blk, slot):
        return pltpu.make_async_copy(block(xin_ref, blk), xbuf.at[slot], sem_in.at[slot])

    def o_copy(blk, slot):
        return pltpu.make_async_copy(obuf.at[slot], block(yout_ref, blk), sem_out.at[slot])

    @pl.when((e == 0) & (n_used > 0))
    def _():
        x_copy(0, 0).start()

    @pl.when(blk_hi > blk_lo)
    def _():
        n_slab, n_t, lanes = wt_scr.shape
        for part in range(2 * dh // n_t):
            rows = slice(part * (n_t // 2), (part + 1) * (n_t // 2))
            for s in range(n_slab):
                cols = slice(s * lanes, (s + 1) * lanes)
                wt_scr[s] = w1_ref[0, cols, part * n_t:(part + 1) * n_t].T
                wg_scr[rows, cols] = wt_scr[s, pl.ds(0, n_t // 2, stride=2), :].astype(BF16)
                wl_scr[rows, cols] = wt_scr[s, pl.ds(1, n_t // 2, stride=2), :].astype(BF16)
        w2_scr[...] = w2_ref[0].astype(BF16)

        def one_block(blk, carry):
            slot = blk % 2

            @pl.when(blk + 1 < n_used)
            def _():
                x_copy(blk + 1, 1 - slot).start()

            x_copy(blk, slot).wait()

            @pl.when(blk >= 2)
            def _():
                o_copy(blk - 2, slot).wait()

            xb = _load_token_tiles(xbuf.at[slot], 0, bm).astype(BF16)
            bias = b1_ref[0]
            glu = jnp.minimum(_dot_nt(xb, wg_scr[...]) + bias[:, :dh], SWIGLU_LIMIT)
            lin = jnp.clip(_dot_nt(xb, wl_scr[...]) + bias[:, dh:], -SWIGLU_LIMIT, SWIGLU_LIMIT)
            act = glu * jax.nn.sigmoid(SWIGLU_ALPHA * glu) * (lin + 1.0)
            _store_token_tiles(obuf.at[slot], _dot(act.astype(BF16), w2_scr[...]) + b2_ref[0])
            o_copy(blk, slot).start()
            return carry

        lax.fori_loop(blk_lo, blk_hi, one_block, 0)

    @pl.when(e == pl.num_programs(0) - 1)
    def _():
        @pl.when(n_used >= 1)
        def _():
            o_copy(0, (n_used - 1) % 2).wait()

        @pl.when(n_used >= 2)
        def _():
            o_copy(0, n_used % 2).wait()

        obuf[0] = jnp.zeros(obuf.shape[1:], obuf.dtype)
        n_blocks = yout_ref.shape[0] // rows_blk

        def start_tail(blk, carry):
            o_copy(blk, 0).start()
            return carry

        def wait_tail(blk, carry):
            o_copy(0, 0).wait()
            return carry

        lax.fori_loop(n_used, n_blocks, start_tail, 0)
        lax.fori_loop(n_used, n_blocks, wait_tail, 0)


def _experts(pad_end, n_used, xin, w1, b1p, w2, b2):
    ne, d, dh2 = w1.shape
    dh = dh2 // 2
    rows_blk = BM_MOE * SUBLANES
    grid_spec = pltpu.PrefetchScalarGridSpec(
        num_scalar_prefetch=2,
        grid=(ne,),
        in_specs=[
            pl.BlockSpec((1, d, dh2), lambda e, pe, nu: (e, 0, 0)),
            pl.BlockSpec((1, 1, dh2), lambda e, pe, nu: (e, 0, 0)),
            pl.BlockSpec((1, dh, d), lambda e, pe, nu: (e, 0, 0)),
            pl.BlockSpec((1, 1, d), lambda e, pe, nu: (e, 0, 0)),
            pl.BlockSpec(memory_space=pl.ANY),
        ],
        out_specs=pl.BlockSpec(memory_space=pl.ANY),
        scratch_shapes=[
            pltpu.VMEM((2, rows_blk, LANES), F32),
            pltpu.VMEM((2, rows_blk, LANES), F32),
            pltpu.VMEM((d // 128, dh, 128), F32),
            pltpu.VMEM((dh, d), BF16),
            pltpu.VMEM((dh, d), BF16),
            pltpu.VMEM((dh, d), BF16),
            pltpu.SemaphoreType.DMA((2,)),
            pltpu.SemaphoreType.DMA((2,)),
        ],
    )
    return pl.pallas_call(
        _experts_kernel,
        grid_spec=grid_spec,
        out_shape=jax.ShapeDtypeStruct(xin.shape, F32),
        compiler_params=_params(("arbitrary",)),
        name="experts",
    )(pad_end, n_used, w1, b1p, w2, b2, xin)


def _combine_kernel(pos_ref, pos_next_ref, gate_ref, h1_ref, fg_ref, yout_ref, o_ref, ybuf, sem):
    tc = h1_ref.shape[0]
    i = pl.program_id(0)
    slot = i % 2

    def tile(ref, token):
        return ref.at[pl.ds(pl.multiple_of(token * SUBLANES, SUBLANES), SUBLANES)]

    def gather(p_ref, s):
        def issue(tk, carry):
            for j in range(TOP_K):
                pltpu.make_async_copy(tile(yout_ref, p_ref[TOP_K * tk + j]), tile(ybuf.at[s], j * tc + tk),
                                      sem.at[s]).start(priority=j % 2)
            return carry
        lax.fori_loop(0, tc, issue, 0, unroll=8)

    @pl.when(i == 0)
    def _():
        gather(pos_ref, 0)

    @pl.when(i + 1 < pl.num_programs(0))
    def _():
        gather(pos_next_ref, 1 - slot)

    pltpu.make_async_copy(yout_ref.at[pl.ds(0, ybuf.shape[1])], ybuf.at[slot], sem.at[slot]).wait()

    gates = gate_ref[...]
    h = h1_ref[...]
    for j in range(TOP_K):
        h = h + _load_token_tiles(ybuf.at[slot], j * tc * SUBLANES, tc) * gates[:, j:j + 1]
    o_ref[...] = h * lax.rsqrt(jnp.mean(h * h, axis=-1, keepdims=True) + RMS_EPS) * fg_ref[...]


def _combine(pos, gates_t, h1, fg, yout):
    t, d = h1.shape
    tc = TC_COMBINE
    n_steps = t // tc
    return pl.pallas_call(
        _combine_kernel,
        grid=(n_steps,),
        in_specs=[
            pl.BlockSpec((TOP_K * tc,), lambda i: (i,), memory_space=pltpu.SMEM),
            pl.BlockSpec((TOP_K * tc,), lambda i: (jnp.minimum(i + 1, n_steps - 1),),
                         memory_space=pltpu.SMEM),
            pl.BlockSpec((tc, TOP_K), lambda i: (i, 0)),
            pl.BlockSpec((tc, d), lambda i: (i, 0)),
            pl.BlockSpec((1, d), lambda i: (0, 0)),
            pl.BlockSpec(memory_space=pl.ANY),
        ],
        out_specs=pl.BlockSpec((tc, d), lambda i: (i, 0)),
        out_shape=jax.ShapeDtypeStruct((t, d), F32),
        scratch_shapes=[pltpu.VMEM((2, TOP_K * tc * SUBLANES, LANES), F32),
                        pltpu.SemaphoreType.DMA((2,))],
        compiler_params=_params(("arbitrary",)),
        name="combine",
    )(pos, pos, gates_t, h1, fg, yout)


def kernel(x, norm1_g, w_in, sgu_ln_g, sgu_ln_b, sgu_w, sgu_b, mu_rkv, mu_wag, decay_w0, decay_w1,
           decay_w2, iclr_a0, iclr_a1, iclr_a2, gate_g1, gate_g2, k_k, k_a, r_k, lnx_g, lnx_b, w_out,
           norm2_g, router_w, router_b, moe_w1, moe_b1, moe_w2, moe_b2, final_g):
    batch, seq, d = x.shape
    t = batch * seq
    depth = w_in.shape[0]
    sgu_width = sgu_ln_g.shape[1]
    rw = mu_rkv.shape[2]
    n_dec, n_icl, n_gate = decay_w1.shape[2], iclr_a1.shape[2], gate_g1.shape[2]
    assert n_dec == HEAD and n_icl == HEAD and n_gate == PAIR and rw % PAIR == 0
    assert seq % TT_RWKV == 0 and seq % TT_PREP == 0 and t % TM_PROJ == 0
    assert sgu_w.shape[2] == SGU_BLOCK
    assert depth == 1, "the final RMSNorm is fused into the last layer's combine kernel"
    assert d == SUBLANES * LANES, "the MoE row movers copy one (8, 128) f32 tile per token"

    h = x.reshape(t, d)
    for l in range(depth):
        win_bf = w_in[l].astype(BF16)
        wl = jnp.concatenate([decay_w1[l], iclr_a1[l], gate_g1[l]], axis=1)
        mucat = jnp.concatenate([
            jnp.broadcast_to(mu_wag[l, 0][:, None], (d, n_dec)),
            jnp.broadcast_to(mu_wag[l, 1][:, None], (d, n_icl)),
            jnp.broadcast_to(mu_wag[l, 2][:, None], (d, n_gate))], axis=1)
        zeros = jnp.zeros((HEAD, rw), F32)
        w2pad = jnp.concatenate([decay_w2[l], zeros], axis=0)
        a2pad = jnp.concatenate([zeros, iclr_a2[l]], axis=0)
        pvec = jnp.stack([decay_w0[l], iclr_a0[l], k_k[l], k_a[l], r_k[l].reshape(-1),
                          lnx_g[l], lnx_b[l], jnp.zeros((rw,), F32)], axis=0)
        bias2d = jnp.repeat(sgu_b[l].T, sgu_width // sgu_b.shape[1], axis=1)
        wo_bf = w_out[l].astype(BF16)
        b1p = jnp.concatenate([moe_b1[l][:, 0::2], moe_b1[l][:, 1::2]], axis=-1)[:, None, :]
        b2 = moe_b2[l][:, None, :]

        assert w_in.shape[2] == 2 * sgu_width + 3 * rw
        proj, lora, ya = _in_proj(h, norm1_g[l][None, :], win_bf, wl, mucat, sgu_ln_g[l][None, :],
                                  sgu_ln_b[l][None, :], sgu_w[l], bias2d)
        rp, yq, bonus, gate, g_m, h_m = _rwkv_prep(proj, lora, mu_rkv[l], pvec, w2pad, a2pad,
                                                   gate_g2[l], seq, 0)
        yb = _rwkv_scan(rp, yq, bonus, gate, g_m, h_m, pvec, batch)
        h1, hn2, logits_t = _out_proj(h, ya, yb, wo_bf, norm2_g[l][None, :], router_w[l].T,
                                      router_b[l][:, None])
        pos, gates, pend = _route(logits_t)

        pad_end = pend[:, 0]
        n_blocks = (t * TOP_K) // BM_MOE + N_EXPERTS
        n_used = pad_end[-1:] // BM_MOE

        pos = pos.T.reshape(-1)
        xin = _dispatch(pad_end, n_used, pos, hn2, n_blocks * BM_MOE)
        yout = _experts(pad_end, n_used, xin, moe_w1[l], b1p, moe_w2[l], b2)
        h = _combine(pos, gates.T, h1, final_g[None, :], yout)
    return h.reshape(batch, seq, d)
```

```python
import functools

import jax
import jax.numpy as jnp
from jax import lax
from jax.experimental import pallas as pl
from jax.experimental.pallas import tpu as pltpu

F32 = jnp.float32
BF16 = jnp.bfloat16
I32 = jnp.int32

RMS_EPS = 1e-5
LN_EPS = 1e-5
LNX_EPS = 64e-5
CHUNK = 64
SGU_BLOCK = 128
HEAD = 64
PAIR = 2 * HEAD
N_EXPERTS = 32
TOP_K = 4
SWIGLU_ALPHA = 1.702
SWIGLU_LIMIT = 7.0

V7X_VMEM_LIMIT = 56 * 1024 * 1024

TM_PROJ = 512
TT_PREP = 1024
N_SUB_PREP = 1
TT_RWKV = 256
TR_ROUTE = 1024
BM_MOE = 256
TD_DISPATCH = 1024
TC_COMBINE = 256


def _dot(a, b):
    return jnp.dot(a, b, preferred_element_type=F32)


def _dot_nt(a, b):
    return lax.dot_general(a, b, (((1,), (1,)), ((), ())), preferred_element_type=F32)


def _split(x):
    hi = x.astype(BF16)
    lo = (x - hi.astype(F32)).astype(BF16)
    return hi, lo


def _dot3(a, b):
    ah, al = _split(a)
    bh, bl = _split(b)
    return _dot(ah, bh) + _dot(al, bh) + _dot(ah, bl)


def _dot3_nt(a, b):
    ah, al = _split(a)
    bh, bl = _split(b)
    return _dot_nt(ah, bh) + _dot_nt(al, bh) + _dot_nt(ah, bl)


def _head_sum(x, bd):
    hi, lo = _split(x)
    return _dot(hi, bd) + _dot(lo, bd)


SUBLANES = 8
LANES = 128


def _store_token_tiles(ref, x):
    n = x.shape[0]
    for s in range(SUBLANES):
        ref[pl.ds(s, n, stride=SUBLANES), :] = x[:, s * LANES:(s + 1) * LANES]


def _load_token_tiles(ref, row0, n):
    return jnp.concatenate(
        [ref[pl.ds(row0 + s, n, stride=SUBLANES), :] for s in range(SUBLANES)], axis=1)


def _params(sem, vmem=V7X_VMEM_LIMIT):
    return pltpu.CompilerParams(dimension_semantics=sem, vmem_limit_bytes=vmem)


def _in_proj_kernel(x_ref, g_ref, win_ref, wl_ref, mucat_ref, lng_ref, lnb_ref, w_ref, bias_ref,
                    proj_ref, lora_ref, o_ref, wl_scr, wm_scr):
    n_heads = w_ref.shape[0]

    @pl.when(pl.program_id(0) == 0)
    def _():
        wl = wl_ref[...]
        mu = mucat_ref[...]
        n = wl.shape[1]
        wl_scr[:, 0:n] = (wl * (1.0 - mu)).astype(BF16)
        wl_scr[:, n:2 * n] = (wl * mu).astype(BF16)
        qi = lax.broadcasted_iota(I32, (SGU_BLOCK, SGU_BLOCK), 0) // CHUNK
        kj = lax.broadcasted_iota(I32, (SGU_BLOCK, SGU_BLOCK), 1) // CHUNK
        for h in range(n_heads):
            wm_scr[h] = jnp.where(kj <= qi, w_ref[h], 0.0).astype(BF16)

    x = x_ref[...]
    hn = x * lax.rsqrt(jnp.mean(x * x, axis=-1, keepdims=True) + RMS_EPS) * g_ref[...]
    hb = hn.astype(BF16)
    width = lng_ref.shape[1]
    two_w = 2 * width
    proj_ref[...] = _dot(hb, win_ref[:, two_w:])
    lora_ref[...] = _dot(hb, wl_scr[...])

    z = _dot(hb, win_ref[:, :two_w])
    tm = z.shape[0]
    gz = 0.5 * z * (1.0 + lax.erf(z * (2.0 ** -0.5)))
    u = gz[:, :width]
    v = gz[:, width:]
    mu = jnp.mean(v, axis=-1, keepdims=True)
    vc = v - mu
    var = jnp.mean(vc * vc, axis=-1, keepdims=True)
    vn = vc * lax.rsqrt(var + LN_EPS) * lng_ref[...] + lnb_ref[...]
    lane = lax.broadcasted_iota(I32, (SGU_BLOCK, PAIR), 1)
    is_lo = lane < HEAD
    bias = bias_ref[...]
    for blk in range(tm // SGU_BLOCK):
        rows = slice(blk * SGU_BLOCK, (blk + 1) * SGU_BLOCK)
        for p in range(width // PAIR):
            cols = slice(p * PAIR, (p + 1) * PAIR)
            vp = vn[rows, cols]
            lo = jnp.where(is_lo, vp, 0.0).astype(BF16)
            hi = jnp.where(is_lo, 0.0, vp).astype(BF16)
            sv = _dot(wm_scr[2 * p], lo) + _dot(wm_scr[2 * p + 1], hi)
            o_ref[rows, cols] = u[rows, cols] * (sv + bias[:, cols])


def _in_proj(x2, g, win_bf, wl, mucat, ln_g, ln_b, w_s, bias2d):
    t, d = x2.shape
    width = ln_g.shape[1]
    n_rkv = win_bf.shape[1] - 2 * width
    n_l = wl.shape[1]
    n_heads = w_s.shape[0]
    tm = TM_PROJ
    const = lambda *shape: pl.BlockSpec(shape, lambda i: (0,) * len(shape))
    row = lambda n: pl.BlockSpec((tm, n), lambda i: (i, 0))
    return pl.pallas_call(
        _in_proj_kernel,
        grid=(t // tm,),
        in_specs=[
            row(d), const(1, d), const(d, win_bf.shape[1]), const(d, n_l), const(d, n_l),
            const(1, width), const(1, width), const(n_heads, SGU_BLOCK, SGU_BLOCK),
            const(SGU_BLOCK, width),
        ],
        out_specs=[row(n_rkv), row(2 * n_l), row(width)],
        out_shape=[
            jax.ShapeDtypeStruct((t, n_rkv), F32),
            jax.ShapeDtypeStruct((t, 2 * n_l), F32),
            jax.ShapeDtypeStruct((t, width), F32),
        ],
        scratch_shapes=[pltpu.VMEM((d, 2 * n_l), BF16),
                        pltpu.VMEM((n_heads, SGU_BLOCK, SGU_BLOCK), BF16)],
        compiler_params=_params(("arbitrary",)),
        name="in_proj_sgu",
    )(x2, g, win_bf, wl, mucat, ln_g, ln_b, w_s, bias2d)


def _rwkv_prep_kernel(seq_len, pr_ref, pk_ref, pv_ref, ppr_ref, ppk_ref, ppv_ref, lo_ref, plo_ref,
                      mu_ref, pvec_ref, w2_ref, a2_ref, g2_ref,
                      rp_ref, yq_ref, bonus_ref, gate_ref, g_ref, h_ref):
    tt = pr_ref.shape[0]
    ts = tt // N_SUB_PREP
    n_chunks = ts // CHUNK
    i = pl.program_id(0)
    keep = jnp.where((i * tt) % seq_len == 0, 0.0, 1.0)

    mu = mu_ref[...]
    pvec = pvec_ref[...]
    w0, a0, k_k, k_a, r_k = pvec[0:1], pvec[1:2], pvec[2:3], pvec[3:4], pvec[4:5]
    w2b = w2_ref[...].astype(BF16)
    a2b = a2_ref[...].astype(BF16)
    g2b = g2_ref[...].astype(BF16)
    half = lo_ref.shape[1] // 2

    lane_r = lax.broadcasted_iota(I32, (PAIR, PAIR), 0)
    lane_c = lax.broadcasted_iota(I32, (PAIR, PAIR), 1)
    bd = jnp.where((lane_r // HEAD) == (lane_c // HEAD), 1.0, 0.0).astype(BF16)
    tri_ones = jnp.where(((lane_r // CHUNK) == (lane_c // CHUNK)) & (lane_c <= lane_r),
                         1.0, 0.0).astype(BF16)
    lane = lax.broadcasted_iota(I32, (CHUNK, PAIR), 1)
    is_lo = lane < HEAD

    def stack(x):
        return jnp.concatenate([jnp.where(is_lo, x, 0.0), jnp.where(is_lo, 0.0, x)], axis=0)

    n2 = 2 * PAIR
    ri = lax.broadcasted_iota(I32, (n2, n2), 0)
    ci = lax.broadcasted_iota(I32, (n2, n2), 1)
    same_head = ((ri // CHUNK) % 2) == ((ci // CHUNK) % 2)
    t_i = ri % CHUNK
    s_i = ci % CHUNK
    tri = same_head & (s_i < t_i + ri // PAIR)
    eye = lane_r == lane_c

    for sub in range(N_SUB_PREP):
        r0 = sub * ts
        tile_rows = slice(r0, r0 + ts)

        def shift(ref, pref, cols=slice(None)):
            x = ref[tile_rows, cols]
            if sub == 0:
                first = pref[7:8, cols] * keep
            else:
                first = ref[r0 - 1:r0, cols]
            rowc = lax.broadcasted_iota(I32, x.shape, 0)
            return x, jnp.where(rowc == 0, first, pltpu.roll(x, 1, 0))

        def shift_mix(ref, pref, m):
            x, xs = shift(ref, pref)
            return x + (xs - x) * m

        r = shift_mix(pr_ref, ppr_ref, mu[0:1])
        k = shift_mix(pk_ref, ppk_ref, mu[1:2])
        v = shift_mix(pv_ref, ppv_ref, mu[2:3])

        _, lo_b = shift(lo_ref, plo_ref, slice(half, 2 * half))
        l_all = lo_ref[tile_rows, :half] + lo_b
        l_wa = l_all[:, :PAIR]
        l_g = l_all[:, PAIR:]

        dw = _dot(jnp.tanh(l_wa).astype(BF16), w2b)
        ia = _dot(l_wa.astype(BF16), a2b)
        gate_ref[tile_rows, :] = _dot(jax.nn.sigmoid(l_g).astype(BF16), g2b).astype(BF16)

        zneg = -(w0 + dw)
        softplus = jnp.maximum(zneg, 0.0) + jnp.log(1.0 + jnp.exp(-jnp.abs(zneg)))
        logw = -jnp.exp(-softplus - 0.5)
        iclr = jax.nn.sigmoid(a0 + ia)

        kk = k * k_k
        kk = kk / jnp.maximum(jnp.sqrt(_head_sum(kk * kk, bd)), 1e-12)
        k2 = k * (1.0 + (iclr - 1.0) * k_a)
        a = -kk
        b = kk * iclr
        bonus_ref[tile_rows, :] = (_head_sum(r * k2 * r_k, bd) * v).astype(BF16)

        lw_hi, lw_lo = _split(logw)
        cl = jnp.concatenate(
            [_dot(tri_ones, lw_hi[q * PAIR:(q + 1) * PAIR]) + _dot(tri_ones, lw_lo[q * PAIR:(q + 1) * PAIR])
             for q in range(ts // PAIR)], axis=0)
        at_all = a * jnp.exp(cl - logw)
        rt_all = r * jnp.exp(cl)
        w_inv = jnp.exp(-cl)
        bt_all = b * w_inv
        kt_all = k2 * w_inv

        chunks = range(n_chunks)
        rows = [slice(c * CHUNK, (c + 1) * CHUNK) for c in chunks]
        out_rows = [slice(r0 + c * CHUNK, r0 + (c + 1) * CHUNK) for c in chunks]
        last = [cl[rows[c]][CHUNK - 1:CHUNK, :] for c in chunks]
        w_rem = [jnp.exp(last[c] - cl[rows[c]]) for c in chunks]
        a_s = [stack(at_all[rows[c]]) for c in chunks]
        r_s = [stack(rt_all[rows[c]]) for c in chunks]
        v_s = [stack(v[rows[c]]) for c in chunks]
        v_sb = [v_s[c].astype(BF16) for c in chunks]
        bk_h = [jnp.concatenate([stack(b[rows[c]] * w_rem[c]), stack(k2[rows[c]] * w_rem[c])],
                                axis=0).astype(BF16) for c in chunks]

        a_all = []
        for c in chunks:
            lhs = jnp.concatenate([a_s[c], r_s[c]], axis=0).astype(BF16)
            rhs = jnp.concatenate([stack(bt_all[rows[c]]), stack(kt_all[rows[c]])],
                                  axis=0).astype(BF16)
            a_all.append(jnp.where(tri, _dot_nt(lhs, rhs), 0.0))
        n_k = [a_all[c][:PAIR, :PAIR].astype(BF16) for c in chunks]
        a_rb = [a_all[c][PAIR:, :PAIR].astype(BF16) for c in chunks]
        x = [jnp.concatenate([a_s[c], _dot(a_all[c][:PAIR, PAIR:].astype(BF16), v_sb[c])], axis=1)
             for c in chunks]
        ry0 = [jnp.concatenate([r_s[c], _dot(a_all[c][PAIR:, PAIR:].astype(BF16), v_sb[c])], axis=1)
               for c in chunks]
        steps = CHUNK.bit_length() - 1
        for it in range(steps):
            x = [x[c] + _dot(n_k[c], x[c].astype(BF16)) for c in chunks]
            if it + 1 < steps:
                n_k = [_dot(n_k[c], n_k[c]).astype(BF16) for c in chunks]
        ry = [ry0[c] + _dot(a_rb[c], x[c].astype(BF16)) for c in chunks]
        for c in chunks:
            rp_ref[out_rows[c], :] = (ry[c][:CHUNK, :PAIR] + ry[c][CHUNK:, :PAIR]).astype(BF16)
            yq_ref[out_rows[c], :] = (ry[c][:CHUNK, PAIR:] + ry[c][CHUNK:, PAIR:]).astype(BF16)
        p_t = [x[c][:, :PAIR].T.astype(BF16) for c in chunks]
        qv_t = [jnp.concatenate([x[c][:, PAIR:], v_s[c]], axis=0).T.astype(BF16) for c in chunks]
        c0 = sub * n_chunks
        for c in chunks:
            g_ref[c0 + c, 0] = (jnp.where(eye, jnp.exp(last[c]), 0.0)
                                + _dot(p_t[c], bk_h[c][:PAIR]))
        for c in chunks:
            hbd = _dot(qv_t[c], bk_h[c])
            h_ref[c0 + c, 0] = hbd[:HEAD] + hbd[HEAD:]


def _rwkv_prep(proj, lora, mu_rkv, pvec, w2pad, a2pad, g2, seq_len, col0):
    t = proj.shape[0]
    width = mu_rkv.shape[1]
    n_pairs = width // PAIR
    tt = TT_PREP
    nl = lora.shape[1]
    cb = col0 // PAIR
    wb = width // PAIR

    def prev_rows(i):
        return jnp.maximum(i * (tt // 8) - 1, 0)

    def tok(c):
        return pl.BlockSpec((tt, PAIR), lambda i, p: (i, c + p))

    def prev(c):
        return pl.BlockSpec((8, PAIR), lambda i, p: (prev_rows(i), c + p))

    out_tok = pl.BlockSpec((tt, PAIR), lambda i, p: (i, p))
    tok_shape = jax.ShapeDtypeStruct((t, width), BF16)
    return pl.pallas_call(
        functools.partial(_rwkv_prep_kernel, seq_len),
        grid=(t // tt, n_pairs),
        in_specs=[
            tok(cb), tok(cb + wb), tok(cb + 2 * wb),
            prev(cb), prev(cb + wb), prev(cb + 2 * wb),
            pl.BlockSpec((tt, nl), lambda i, p: (i, 0)),
            pl.BlockSpec((8, nl), lambda i, p: (prev_rows(i), 0)),
            pl.BlockSpec((3, PAIR), lambda i, p: (0, p)),
            pl.BlockSpec((8, PAIR), lambda i, p: (0, p)),
            pl.BlockSpec((PAIR, PAIR), lambda i, p: (0, p)),
            pl.BlockSpec((PAIR, PAIR), lambda i, p: (0, p)),
            pl.BlockSpec((PAIR, PAIR), lambda i, p: (0, p)),
        ],
        out_specs=[
            out_tok, out_tok, out_tok, out_tok,
            pl.BlockSpec((tt // CHUNK, 1, PAIR, PAIR), lambda i, p: (i, p, 0, 0)),
            pl.BlockSpec((tt // CHUNK, 1, HEAD, PAIR), lambda i, p: (i, p, 0, 0)),
        ],
        out_shape=[
            tok_shape, tok_shape, tok_shape, tok_shape,
            jax.ShapeDtypeStruct((t // CHUNK, n_pairs, PAIR, PAIR), F32),
            jax.ShapeDtypeStruct((t // CHUNK, n_pairs, HEAD, PAIR), F32),
        ],
        compiler_params=_params(("arbitrary", "arbitrary")),
        name="rwkv_prep",
    )(proj, proj, proj, proj, proj, proj, lora, lora, mu_rkv, pvec, w2pad, a2pad, g2)


def _rwkv_scan_kernel(rp_ref, yq_ref, bonus_ref, gate_ref, g_ref, h_ref, pvec_ref, o_ref, s_scr, y_scr):
    nb, tt, width = rp_ref.shape
    n_pairs = width // PAIR

    @pl.when(pl.program_id(0) == 0)
    def _():
        s_scr[...] = jnp.zeros_like(s_scr)

    lane = lax.broadcasted_iota(I32, (HEAD, PAIR), 1)
    is_lo = lane < HEAD
    lane_r = lax.broadcasted_iota(I32, (PAIR, PAIR), 0)
    lane_c = lax.broadcasted_iota(I32, (PAIR, PAIR), 1)
    bd = jnp.where((lane_r // HEAD) == (lane_c // HEAD), 1.0, 0.0).astype(BF16)

    chains = [(b, p) for b in range(nb) for p in range(n_pairs)]
    state = {ch: s_scr[ch[0], ch[1]] for ch in chains}
    for c in range(tt // CHUNK):
        rows = slice(c * CHUNK, (c + 1) * CHUNK)
        for (b, p) in chains:
            cols = slice(p * PAIR, (p + 1) * PAIR)
            s0 = state[(b, p)]
            s_st = jnp.concatenate([jnp.where(is_lo, s0, 0.0), jnp.where(is_lo, 0.0, s0)], axis=0)
            y_scr[b, rows, cols] = (_dot_nt(rp_ref[b, rows, cols], s_st.astype(BF16))
                                    + yq_ref[b, rows, cols].astype(F32))
        state = {(b, p): _dot3(state[(b, p)], g_ref[b, c, p]) + h_ref[b, c, p] for (b, p) in chains}
    for (b, p) in chains:
        s_scr[b, p] = state[(b, p)]

    pvec = pvec_ref[...]
    for (b, p) in chains:
        cols = slice(p * PAIR, (p + 1) * PAIR)
        y = y_scr[b, :, cols]
        mu = _head_sum(y, bd) * (1.0 / HEAD)
        yc = y - mu
        var = _head_sum(yc * yc, bd) * (1.0 / HEAD)
        yn = yc * lax.rsqrt(var + LNX_EPS) * pvec[5:6, cols] + pvec[6:7, cols]
        o_ref[b, :, cols] = ((yn + bonus_ref[b, :, cols].astype(F32))
                             * gate_ref[b, :, cols].astype(F32))


def _rwkv_scan(rp, yq, bonus, gate, g, h, pvec, batch):
    t, width = rp.shape
    n_pairs = width // PAIR
    seq = t // batch
    tt = TT_RWKV
    nc = tt // CHUNK
    tok = pl.BlockSpec((batch, tt, width), lambda i: (0, i, 0))
    as_seq = lambda z: z.reshape(batch, seq, width)
    out = pl.pallas_call(
        _rwkv_scan_kernel,
        grid=(seq // tt,),
        in_specs=[
            tok, tok, tok, tok,
            pl.BlockSpec((batch, nc, n_pairs, PAIR, PAIR), lambda i: (0, i, 0, 0, 0)),
            pl.BlockSpec((batch, nc, n_pairs, HEAD, PAIR), lambda i: (0, i, 0, 0, 0)),
            pl.BlockSpec((8, width), lambda i: (0, 0)),
        ],
        out_specs=tok,
        out_shape=jax.ShapeDtypeStruct((batch, seq, width), F32),
        scratch_shapes=[pltpu.VMEM((batch, n_pairs, HEAD, PAIR), F32),
                        pltpu.VMEM((batch, tt, width), F32)],
        compiler_params=_params(("arbitrary",)),
        name="rwkv_scan",
    )(as_seq(rp), as_seq(yq), as_seq(bonus), as_seq(gate),
      g.reshape(batch, seq // CHUNK, n_pairs, PAIR, PAIR),
      h.reshape(batch, seq // CHUNK, n_pairs, HEAD, PAIR), pvec)
    return out.reshape(t, width)


def _out_proj_kernel(x_ref, ya_ref, yb_ref, wo_ref, g_ref, rwt_ref, rb_ref, h1_ref, hn_ref, lg_ref):
    wa = ya_ref.shape[1]
    h1 = (x_ref[...] + _dot(ya_ref[...].astype(BF16), wo_ref[0:wa, :])
          + _dot(yb_ref[...].astype(BF16), wo_ref[wa:, :]))
    h1_ref[...] = h1
    hn = h1 * lax.rsqrt(jnp.mean(h1 * h1, axis=-1, keepdims=True) + RMS_EPS) * g_ref[...]
    _store_token_tiles(hn_ref, hn)
    lg_ref[...] = _dot3_nt(rwt_ref[...], hn) + rb_ref[...]


def _out_proj(x2, ya, yb, wo_bf, g2, rwt, rb):
    t, d = x2.shape
    wa = ya.shape[1]
    wb = yb.shape[1]
    ne = rwt.shape[0]
    tm = TM_PROJ
    return pl.pallas_call(
        _out_proj_kernel,
        grid=(t // tm,),
        in_specs=[
            pl.BlockSpec((tm, d), lambda i: (i, 0)),
            pl.BlockSpec((tm, wa), lambda i: (i, 0)),
            pl.BlockSpec((tm, wb), lambda i: (i, 0)),
            pl.BlockSpec((wa + wb, d), lambda i: (0, 0)),
            pl.BlockSpec((1, d), lambda i: (0, 0)),
            pl.BlockSpec((ne, d), lambda i: (0, 0)),
            pl.BlockSpec((ne, 1), lambda i: (0, 0)),
        ],
        out_specs=[
            pl.BlockSpec((tm, d), lambda i: (i, 0)),
            pl.BlockSpec((tm * SUBLANES, LANES), lambda i: (i, 0)),
            pl.BlockSpec((ne, tm), lambda i: (0, i)),
        ],
        out_shape=[
            jax.ShapeDtypeStruct((t, d), F32),
            jax.ShapeDtypeStruct((t * SUBLANES, LANES), F32),
            jax.ShapeDtypeStruct((ne, t), F32),
        ],
        compiler_params=_params(("arbitrary",)),
        name="out_proj",
    )(x2, ya, yb, wo_bf, g2, rwt, rb)


def _route_kernel(lg_ref, pos_ref, gate_ref, pend_ref, carry_scr, pstart_scr):
    phase = pl.program_id(0)
    first = pl.program_id(1) == 0

    @pl.when(first & (phase == 0))
    def _():
        carry_scr[...] = jnp.zeros_like(carry_scr)
        pstart_scr[...] = jnp.zeros_like(pstart_scr)
        pend_ref[...] = jnp.zeros_like(pend_ref)

    @pl.when(first & (phase == 1))
    def _():
        counts = carry_scr[...]
        padded = jnp.ceil(counts * (1.0 / BM_MOE)) * BM_MOE
        row = lax.broadcasted_iota(I32, counts.shape, 0)
        end = padded
        s = 1
        while s < counts.shape[0]:
            end = end + jnp.where(row >= s, pltpu.roll(end, s, 0), 0.0)
            s *= 2
        pstart_scr[...] = end - padded
        pend_ref[...] = end.astype(I32)
        carry_scr[...] = jnp.zeros_like(carry_scr)

    l = lg_ref[...]
    ne, tr = l.shape
    e_iota = lax.broadcasted_iota(I32, (ne, tr), 0)
    chosen = jnp.zeros((ne, tr), F32)
    vals, sels = [], []
    for j in range(TOP_K):
        m = jnp.max(l, axis=0, keepdims=True)
        idx = jnp.min(jnp.where(l == m, e_iota, ne), axis=0, keepdims=True)
        sel = e_iota == idx
        vals.append(m)
        sels.append(sel)
        chosen = jnp.where(sel, 1.0, chosen)
        l = jnp.where(sel, -jnp.inf, l)
    ex = [jnp.exp(vj - vals[0]) for vj in vals]
    den = ex[0] + ex[1] + ex[2] + ex[3]
    for j in range(TOP_K):
        gate_ref[j:j + 1, :] = ex[j] / den

    ti = lax.broadcasted_iota(I32, (tr, tr), 0)
    tj = lax.broadcasted_iota(I32, (tr, tr), 1)
    upper = jnp.where(ti <= tj, 1.0, 0.0).astype(BF16)
    inc = _dot(chosen.astype(BF16), upper)
    carry = carry_scr[...]
    row_of = inc - chosen + carry[:, 0:1] + pstart_scr[:, 0:1]
    for j in range(TOP_K):
        pj = jnp.sum(jnp.where(sels[j], row_of, 0.0), axis=0, keepdims=True)
        pos_ref[j:j + 1, :] = pj.astype(I32)
    carry_scr[...] = carry + inc[:, tr - 1:tr]


def _route(logits_t):
    ne, t = logits_t.shape
    tr = TR_ROUTE
    tok = pl.BlockSpec((TOP_K, tr), lambda ph, i: (0, i * ph))
    return pl.pallas_call(
        _route_kernel,
        grid=(2, t // tr),
        in_specs=[pl.BlockSpec((ne, tr), lambda ph, i: (0, i))],
        out_specs=[tok, tok, pl.BlockSpec((ne, 128), lambda ph, i: (0, 0))],
        out_shape=[
            jax.ShapeDtypeStruct((TOP_K, t), I32),
            jax.ShapeDtypeStruct((TOP_K, t), F32),
            jax.ShapeDtypeStruct((ne, 128), I32),
        ],
        scratch_shapes=[pltpu.VMEM((ne, 128), F32), pltpu.VMEM((ne, 128), F32)],
        compiler_params=_params(("arbitrary", "arbitrary")),
        name="route",
    )(logits_t)


def _dispatch_kernel(pend_ref, nused_ref, pos_ref, hn_ref, xin_ref, zero_scr, sem):
    bm = zero_scr.shape[0] // SUBLANES
    td = pos_ref.shape[0] // TOP_K
    i = pl.program_id(0)

    def tile(ref, token):
        return ref.at[pl.ds(pl.multiple_of(token * SUBLANES, SUBLANES), SUBLANES)]

    def zero_block(row0):
        start = pl.multiple_of(row0 * SUBLANES, bm * SUBLANES)
        return pltpu.make_async_copy(zero_scr, xin_ref.at[pl.ds(start, bm * SUBLANES)], sem)

    @pl.when(i == 0)
    def _():
        zero_scr[...] = jnp.zeros_like(zero_scr)

        def has_rows(e):
            end = pend_ref[e]
            return jnp.where(e == 0, end >= bm, end > pend_ref[jnp.maximum(e - 1, 0)])

        def start_one(e, carry):
            @pl.when(has_rows(e))
            def _():
                zero_block(pend_ref[e] - bm).start()
            return carry

        def wait_one(e, carry):
            @pl.when(has_rows(e))
            def _():
                zero_block(0).wait()
            return carry

        def start_tail(blk, carry):
            zero_block(blk * bm).start()
            return carry

        def wait_tail(blk, carry):
            zero_block(0).wait()
            return carry

        n_blocks = xin_ref.shape[0] // (bm * SUBLANES)
        lax.fori_loop(0, pend_ref.shape[0], start_one, 0)
        lax.fori_loop(nused_ref[0], n_blocks, start_tail, 0)
        lax.fori_loop(0, pend_ref.shape[0], wait_one, 0)
        lax.fori_loop(nused_ref[0], n_blocks, wait_tail, 0)

    def issue(tk, carry):
        for j in range(TOP_K):
            pltpu.make_async_copy(tile(hn_ref, tk), tile(xin_ref, pos_ref[TOP_K * tk + j]),
                                  sem).start(priority=j % 2)
        return carry

    lax.fori_loop(0, td, issue, 0, unroll=8)

    for j in range(TOP_K):
        pltpu.make_async_copy(hn_ref, xin_ref.at[pl.ds(0, td * SUBLANES)], sem).wait()


def _dispatch(pad_end, n_used, pos, hn_tiles, n_rows):
    t = hn_tiles.shape[0] // SUBLANES
    td = TD_DISPATCH
    grid_spec = pltpu.PrefetchScalarGridSpec(
        num_scalar_prefetch=2,
        grid=(t // td,),
        in_specs=[
            pl.BlockSpec((TOP_K * td,), lambda i, pe, nu: (i,), memory_space=pltpu.SMEM),
            pl.BlockSpec((td * SUBLANES, LANES), lambda i, pe, nu: (i, 0)),
        ],
        out_specs=pl.BlockSpec(memory_space=pl.ANY),
        scratch_shapes=[pltpu.VMEM((BM_MOE * SUBLANES, LANES), F32), pltpu.SemaphoreType.DMA(())],
    )
    return pl.pallas_call(
        _dispatch_kernel,
        grid_spec=grid_spec,
        out_shape=jax.ShapeDtypeStruct((n_rows * SUBLANES, LANES), F32),
        compiler_params=_params(("arbitrary",)),
        name="dispatch",
    )(pad_end, n_used, pos, hn_tiles)


def _experts_kernel(pend_ref, nused_ref, w1_ref, b1_ref, w2_ref, b2_ref, xin_ref, yout_ref,
                    xbuf, obuf, wt_scr, wg_scr, wl_scr, w2_scr, sem_in, sem_out):
    e = pl.program_id(0)
    rows_blk = xbuf.shape[1]
    bm = rows_blk // SUBLANES
    dh = w2_ref.shape[1]
    n_used = nused_ref[0]
    blk_lo = jnp.where(e == 0, 0, pend_ref[jnp.maximum(e - 1, 0)]) // bm
    blk_hi = pend_ref[e] // bm

    def block(ref, blk):
        return ref.at[pl.ds(pl.multiple_of(blk * rows_blk, rows_blk), rows_blk)]

    def x_copy(blk, slot):
        return pltpu.make_async_copy(block(xin_ref, blk), xbuf.at[slot], sem_in.at[slot])

    def o_copy(blk, slot):
        return pltpu.make_async_copy(obuf.at[slot], block(yout_ref, blk), sem_out.at[slot])

    @pl.when((e == 0) & (n_used > 0))
    def _():
        x_copy(0, 0).start()

    @pl.when(blk_hi > blk_lo)
    def _():
        n_slab, n_t, lanes = wt_scr.shape
        for part in range(2 * dh // n_t):
            rows = slice(part * (n_t // 2), (part + 1) * (n_t // 2))
            for s in range(n_slab):
                cols = slice(s * lanes, (s + 1) * lanes)
                wt_scr[s] = w1_ref[0, cols, part * n_t:(part + 1) * n_t].T
                wg_scr[rows, cols] = wt_scr[s, pl.ds(0, n_t // 2, stride=2), :].astype(BF16)
                wl_scr[rows, cols] = wt_scr[s, pl.ds(1, n_t // 2, stride=2), :].astype(BF16)
        w2_scr[...] = w2_ref[0].astype(BF16)

        def one_block(blk, carry):
            slot = blk % 2

            @pl.when(blk + 1 < n_used)
            def _():
                x_copy(blk + 1, 1 - slot).start()

            x_copy(blk, slot).wait()

            @pl.when(blk >= 2)
            def _():
                o_copy(blk - 2, slot).wait()

            xb = _load_token_tiles(xbuf.at[slot], 0, bm).astype(BF16)
            bias = b1_ref[0]
            glu = jnp.minimum(_dot_nt(xb, wg_scr[...]) + bias[:, :dh], SWIGLU_LIMIT)
            lin = jnp.clip(_dot_nt(xb, wl_scr[...]) + bias[:, dh:], -SWIGLU_LIMIT, SWIGLU_LIMIT)
            act = glu * jax.nn.sigmoid(SWIGLU_ALPHA * glu) * (lin + 1.0)
            _store_token_tiles(obuf.at[slot], _dot(act.astype(BF16), w2_scr[...]) + b2_ref[0])
            o_copy(blk, slot).start()
            return carry

        lax.fori_loop(blk_lo, blk_hi, one_block, 0)

    @pl.when(e == pl.num_programs(0) - 1)
    def _():
        @pl.when(n_used >= 1)
        def _():
            o_copy(0, (n_used - 1) % 2).wait()

        @pl.when(n_used >= 2)
        def _():
            o_copy(0, n_used % 2).wait()

        obuf[0] = jnp.zeros(obuf.shape[1:], obuf.dtype)
        n_blocks = yout_ref.shape[0] // rows_blk

        def start_tail(blk, carry):
            o_copy(blk, 0).start()
            return carry

        def wait_tail(blk, carry):
            o_copy(0, 0).wait()
            return carry

        lax.fori_loop(n_used, n_blocks, start_tail, 0)
        lax.fori_loop(n_used, n_blocks, wait_tail, 0)


def _experts(pad_end, n_used, xin, w1, b1p, w2, b2):
    ne, d, dh2 = w1.shape
    dh = dh2 // 2
    rows_blk = BM_MOE * SUBLANES
    grid_spec = pltpu.PrefetchScalarGridSpec(
        num_scalar_prefetch=2,
        grid=(ne,),
        in_specs=[
            pl.BlockSpec((1, d, dh2), lambda e, pe, nu: (e, 0, 0)),
            pl.BlockSpec((1, 1, dh2), lambda e, pe, nu: (e, 0, 0)),
            pl.BlockSpec((1, dh, d), lambda e, pe, nu: (e, 0, 0)),
            pl.BlockSpec((1, 1, d), lambda e, pe, nu: (e, 0, 0)),
            pl.BlockSpec(memory_space=pl.ANY),
        ],
        out_specs=pl.BlockSpec(memory_space=pl.ANY),
        scratch_shapes=[
            pltpu.VMEM((2, rows_blk, LANES), F32),
            pltpu.VMEM((2, rows_blk, LANES), F32),
            pltpu.VMEM((d // 128, dh, 128), F32),
            pltpu.VMEM((dh, d), BF16),
            pltpu.VMEM((dh, d), BF16),
            pltpu.VMEM((dh, d), BF16),
            pltpu.SemaphoreType.DMA((2,)),
            pltpu.SemaphoreType.DMA((2,)),
        ],
    )
    return pl.pallas_call(
        _experts_kernel,
        grid_spec=grid_spec,
        out_shape=jax.ShapeDtypeStruct(xin.shape, F32),
        compiler_params=_params(("arbitrary",)),
        name="experts",
    )(pad_end, n_used, w1, b1p, w2, b2, xin)


def _combine_kernel(pos_ref, pos_next_ref, gate_ref, h1_ref, fg_ref, yout_ref, o_ref, ybuf, sem):
    tc = h1_ref.shape[0]
    i = pl.program_id(0)
    slot = i % 2

    def tile(ref, token):
        return ref.at[pl.ds(pl.multiple_of(token * SUBLANES, SUBLANES), SUBLANES)]

    def gather(p_ref, s):
        def issue(tk, carry):
            for j in range(TOP_K):
                pltpu.make_async_copy(tile(yout_ref, p_ref[TOP_K * tk + j]), tile(ybuf.at[s], j * tc + tk),
                                      sem.at[s]).start(priority=j % 2)
            return carry
        lax.fori_loop(0, tc, issue, 0, unroll=8)

    @pl.when(i == 0)
    def _():
        gather(pos_ref, 0)

    @pl.when(i + 1 < pl.num_programs(0))
    def _():
        gather(pos_next_ref, 1 - slot)

    pltpu.make_async_copy(yout_ref.at[pl.ds(0, ybuf.shape[1])], ybuf.at[slot], sem.at[slot]).wait()

    gates = gate_ref[...]
    h = h1_ref[...]
    for j in range(TOP_K):
        h = h + _load_token_tiles(ybuf.at[slot], j * tc * SUBLANES, tc) * gates[:, j:j + 1]
    o_ref[...] = h * lax.rsqrt(jnp.mean(h * h, axis=-1, keepdims=True) + RMS_EPS) * fg_ref[...]


def _combine(pos, gates_t, h1, fg, yout):
    t, d = h1.shape
    tc = TC_COMBINE
    n_steps = t // tc
    return pl.pallas_call(
        _combine_kernel,
        grid=(n_steps,),
        in_specs=[
            pl.BlockSpec((TOP_K * tc,), lambda i: (i,), memory_space=pltpu.SMEM),
            pl.BlockSpec((TOP_K * tc,), lambda i: (jnp.minimum(i + 1, n_steps - 1),),
                         memory_space=pltpu.SMEM),
            pl.BlockSpec((tc, TOP_K), lambda i: (i, 0)),
            pl.BlockSpec((tc, d), lambda i: (i, 0)),
            pl.BlockSpec((1, d), lambda i: (0, 0)),
            pl.BlockSpec(memory_space=pl.ANY),
        ],
        out_specs=pl.BlockSpec((tc, d), lambda i: (i, 0)),
        out_shape=jax.ShapeDtypeStruct((t, d), F32),
        scratch_shapes=[pltpu.VMEM((2, TOP_K * tc * SUBLANES, LANES), F32),
                        pltpu.SemaphoreType.DMA((2,))],
        compiler_params=_params(("arbitrary",)),
        name="combine",
    )(pos, pos, gates_t, h1, fg, yout)


def kernel(x, norm1_g, w_in, sgu_ln_g, sgu_ln_b, sgu_w, sgu_b, mu_rkv, mu_wag, decay_w0, decay_w1,
           decay_w2, iclr_a0, iclr_a1, iclr_a2, gate_g1, gate_g2, k_k, k_a, r_k, lnx_g, lnx_b, w_out,
           norm2_g, router_w, router_b, moe_w1, moe_b1, moe_w2, moe_b2, final_g):
    batch, seq, d = x.shape
    t = batch * seq
    depth = w_in.shape[0]
    sgu_width = sgu_ln_g.shape[1]
    rw = mu_rkv.shape[2]
    n_dec, n_icl, n_gate = decay_w1.shape[2], iclr_a1.shape[2], gate_g1.shape[2]
    assert n_dec == HEAD and n_icl == HEAD and n_gate == PAIR and rw % PAIR == 0
    assert seq % TT_RWKV == 0 and seq % TT_PREP == 0 and t % TM_PROJ == 0
    assert sgu_w.shape[2] == SGU_BLOCK
    assert depth == 1, "the final RMSNorm is fused into the last layer's combine kernel"
    assert d == SUBLANES * LANES, "the MoE row movers copy one (8, 128) f32 tile per token"

    h = x.reshape(t, d)
    for l in range(depth):
        win_bf = w_in[l].astype(BF16)
        wl = jnp.concatenate([decay_w1[l], iclr_a1[l], gate_g1[l]], axis=1)
        mucat = jnp.concatenate([
            jnp.broadcast_to(mu_wag[l, 0][:, None], (d, n_dec)),
            jnp.broadcast_to(mu_wag[l, 1][:, None], (d, n_icl)),
            jnp.broadcast_to(mu_wag[l, 2][:, None], (d, n_gate))], axis=1)
        zeros = jnp.zeros((HEAD, rw), F32)
        w2pad = jnp.concatenate([decay_w2[l], zeros], axis=0)
        a2pad = jnp.concatenate([zeros, iclr_a2[l]], axis=0)
        pvec = jnp.stack([decay_w0[l], iclr_a0[l], k_k[l], k_a[l], r_k[l].reshape(-1),
                          lnx_g[l], lnx_b[l], jnp.zeros((rw,), F32)], axis=0)
        bias2d = jnp.repeat(sgu_b[l].T, sgu_width // sgu_b.shape[1], axis=1)
        wo_bf = w_out[l].astype(BF16)
        b1p = jnp.concatenate([moe_b1[l][:, 0::2], moe_b1[l][:, 1::2]], axis=-1)[:, None, :]
        b2 = moe_b2[l][:, None, :]

        assert w_in.shape[2] == 2 * sgu_width + 3 * rw
        proj, lora, ya = _in_proj(h, norm1_g[l][None, :], win_bf, wl, mucat, sgu_ln_g[l][None, :],
                                  sgu_ln_b[l][None, :], sgu_w[l], bias2d)
        rp, yq, bonus, gate, g_m, h_m = _rwkv_prep(proj, lora, mu_rkv[l], pvec, w2pad, a2pad,
                                                   gate_g2[l], seq, 0)
        yb = _rwkv_scan(rp, yq, bonus, gate, g_m, h_m, pvec, batch)
        h1, hn2, logits_t = _out_proj(h, ya, yb, wo_bf, norm2_g[l][None, :], router_w[l].T,
                                      router_b[l][:, None])
        pos, gates, pend = _route(logits_t)

        pad_end = pend[:, 0]
        n_blocks = (t * TOP_K) // BM_MOE + N_EXPERTS
        n_used = pad_end[-1:] // BM_MOE

        pos = pos.T.reshape(-1)
        xin = _dispatch(pad_end, n_used, pos, hn2, n_blocks * BM_MOE)
        yout = _experts(pad_end, n_used, xin, moe_w1[l], b1p, moe_w2[l], b2)
        h = _combine(pos, gates.T, h1, final_g[None, :], yout)
    return h.reshape(batch, seq, d)
```

```python
import functools

import jax
import jax.numpy as jnp
from jax import lax
from jax.experimental import pallas as pl
from jax.experimental.pallas import tpu as pltpu

F32 = jnp.float32
BF16 = jnp.bfloat16
I32 = jnp.int32

RMS_EPS = 1e-5
LN_EPS = 1e-5
LNX_EPS = 64e-5
CHUNK = 64
SGU_BLOCK = 128
HEAD = 64
PAIR = 2 * HEAD
N_EXPERTS = 32
TOP_K = 4
SWIGLU_ALPHA = 1.702
SWIGLU_LIMIT = 7.0

V7X_VMEM_LIMIT = 56 * 1024 * 1024

TM_PROJ = 512
TT_PREP = 1024
N_SUB_PREP = 1
TT_RWKV = 256
TR_ROUTE = 1024
BM_MOE = 256
TD_DISPATCH = 1024
TC_COMBINE = 512


def _dot(a, b):
    return jnp.dot(a, b, preferred_element_type=F32)


def _dot_nt(a, b):
    return lax.dot_general(a, b, (((1,), (1,)), ((), ())), preferred_element_type=F32)


def _split(x):
    hi = x.astype(BF16)
    lo = (x - hi.astype(F32)).astype(BF16)
    return hi, lo


def _dot3(a, b):
    ah, al = _split(a)
    bh, bl = _split(b)
    return _dot(ah, bh) + _dot(al, bh) + _dot(ah, bl)


def _dot3_nt(a, b):
    ah, al = _split(a)
    bh, bl = _split(b)
    return _dot_nt(ah, bh) + _dot_nt(al, bh) + _dot_nt(ah, bl)


def _head_sum(x, bd):
    hi, lo = _split(x)
    return _dot(hi, bd) + _dot(lo, bd)


SUBLANES = 8
LANES = 128


def _store_token_tiles(ref, x):
    n = x.shape[0]
    for s in range(SUBLANES):
        ref[pl.ds(s, n, stride=SUBLANES), :] = x[:, s * LANES:(s + 1) * LANES]


def _load_token_tiles(ref, row0, n):
    return jnp.concatenate(
        [ref[pl.ds(row0 + s, n, stride=SUBLANES), :] for s in range(SUBLANES)], axis=1)


def _params(sem, vmem=V7X_VMEM_LIMIT):
    return pltpu.CompilerParams(dimension_semantics=sem, vmem_limit_bytes=vmem)


def _in_proj_kernel(x_ref, g_ref, win_ref, wl_ref, mucat_ref, lng_ref, lnb_ref, w_ref, bias_ref,
                    proj_ref, lora_ref, o_ref, wl_scr, wm_scr):
    n_heads = w_ref.shape[0]

    @pl.when(pl.program_id(0) == 0)
    def _():
        wl = wl_ref[...]
        mu = mucat_ref[...]
        n = wl.shape[1]
        wl_scr[:, 0:n] = (wl * (1.0 - mu)).astype(BF16)
        wl_scr[:, n:2 * n] = (wl * mu).astype(BF16)
        qi = lax.broadcasted_iota(I32, (SGU_BLOCK, SGU_BLOCK), 0) // CHUNK
        kj = lax.broadcasted_iota(I32, (SGU_BLOCK, SGU_BLOCK), 1) // CHUNK
        for h in range(n_heads):
            wm_scr[h] = jnp.where(kj <= qi, w_ref[h], 0.0).astype(BF16)

    x = x_ref[...]
    hn = x * lax.rsqrt(jnp.mean(x * x, axis=-1, keepdims=True) + RMS_EPS) * g_ref[...]
    hb = hn.astype(BF16)
    width = lng_ref.shape[1]
    two_w = 2 * width
    proj_ref[...] = _dot(hb, win_ref[:, two_w:])
    lora_ref[...] = _dot(hb, wl_scr[...])

    z = _dot(hb, win_ref[:, :two_w])
    tm = z.shape[0]
    gz = 0.5 * z * (1.0 + lax.erf(z * (2.0 ** -0.5)))
    u = gz[:, :width]
    v = gz[:, width:]
    mu = jnp.mean(v, axis=-1, keepdims=True)
    vc = v - mu
    var = jnp.mean(vc * vc, axis=-1, keepdims=True)
    vn = vc * lax.rsqrt(var + LN_EPS) * lng_ref[...] + lnb_ref[...]
    lane = lax.broadcasted_iota(I32, (SGU_BLOCK, PAIR), 1)
    is_lo = lane < HEAD
    bias = bias_ref[...]
    for blk in range(tm // SGU_BLOCK):
        rows = slice(blk * SGU_BLOCK, (blk + 1) * SGU_BLOCK)
        for p in range(width // PAIR):
            cols = slice(p * PAIR, (p + 1) * PAIR)
            vp = vn[rows, cols]
            lo = jnp.where(is_lo, vp, 0.0).astype(BF16)
            hi = jnp.where(is_lo, 0.0, vp).astype(BF16)
            sv = _dot(wm_scr[2 * p], lo) + _dot(wm_scr[2 * p + 1], hi)
            o_ref[rows, cols] = u[rows, cols] * (sv + bias[:, cols])


def _in_proj(x2, g, win_bf, wl, mucat, ln_g, ln_b, w_s, bias2d):
    t, d = x2.shape
    width = ln_g.shape[1]
    n_rkv = win_bf.shape[1] - 2 * width
    n_l = wl.shape[1]
    n_heads = w_s.shape[0]
    tm = TM_PROJ
    const = lambda *shape: pl.BlockSpec(shape, lambda i: (0,) * len(shape))
    row = lambda n: pl.BlockSpec((tm, n), lambda i: (i, 0))
    return pl.pallas_call(
        _in_proj_kernel,
        grid=(t // tm,),
        in_specs=[
            row(d), const(1, d), const(d, win_bf.shape[1]), const(d, n_l), const(d, n_l),
            const(1, width), const(1, width), const(n_heads, SGU_BLOCK, SGU_BLOCK),
            const(SGU_BLOCK, width),
        ],
        out_specs=[row(n_rkv), row(2 * n_l), row(width)],
        out_shape=[
            jax.ShapeDtypeStruct((t, n_rkv), F32),
            jax.ShapeDtypeStruct((t, 2 * n_l), F32),
            jax.ShapeDtypeStruct((t, width), F32),
        ],
        scratch_shapes=[pltpu.VMEM((d, 2 * n_l), BF16),
                        pltpu.VMEM((n_heads, SGU_BLOCK, SGU_BLOCK), BF16)],
        compiler_params=_params(("arbitrary",)),
        name="in_proj_sgu",
    )(x2, g, win_bf, wl, mucat, ln_g, ln_b, w_s, bias2d)


def _rwkv_prep_kernel(seq_len, pr_ref, pk_ref, pv_ref, ppr_ref, ppk_ref, ppv_ref, lo_ref, plo_ref,
                      mu_ref, pvec_ref, w2_ref, a2_ref, g2_ref,
                      rp_ref, yq_ref, bonus_ref, gate_ref, g_ref, h_ref):
    tt = pr_ref.shape[0]
    ts = tt // N_SUB_PREP
    n_chunks = ts // CHUNK
    i = pl.program_id(0)
    keep = jnp.where((i * tt) % seq_len == 0, 0.0, 1.0)

    mu = mu_ref[...]
    pvec = pvec_ref[...]
    w0, a0, k_k, k_a, r_k = pvec[0:1], pvec[1:2], pvec[2:3], pvec[3:4], pvec[4:5]
    w2b = w2_ref[...].astype(BF16)
    a2b = a2_ref[...].astype(BF16)
    g2b = g2_ref[...].astype(BF16)
    half = lo_ref.shape[1] // 2

    lane_r = lax.broadcasted_iota(I32, (PAIR, PAIR), 0)
    lane_c = lax.broadcasted_iota(I32, (PAIR, PAIR), 1)
    bd = jnp.where((lane_r // HEAD) == (lane_c // HEAD), 1.0, 0.0).astype(BF16)
    tri_ones = jnp.where(((lane_r // CHUNK) == (lane_c // CHUNK)) & (lane_c <= lane_r),
                         1.0, 0.0).astype(BF16)
    lane = lax.broadcasted_iota(I32, (CHUNK, PAIR), 1)
    is_lo = lane < HEAD

    def stack(x):
        return jnp.concatenate([jnp.where(is_lo, x, 0.0), jnp.where(is_lo, 0.0, x)], axis=0)

    n2 = 2 * PAIR
    ri = lax.broadcasted_iota(I32, (n2, n2), 0)
    ci = lax.broadcasted_iota(I32, (n2, n2), 1)
    same_head = ((ri // CHUNK) % 2) == ((ci // CHUNK) % 2)
    t_i = ri % CHUNK
    s_i = ci % CHUNK
    tri = same_head & (s_i < t_i + ri // PAIR)
    eye = lane_r == lane_c

    for sub in range(N_SUB_PREP):
        r0 = sub * ts
        tile_rows = slice(r0, r0 + ts)

        def shift(ref, pref, cols=slice(None)):
            x = ref[tile_rows, cols]
            if sub == 0:
                first = pref[7:8, cols] * keep
            else:
                first = ref[r0 - 1:r0, cols]
            rowc = lax.broadcasted_iota(I32, x.shape, 0)
            return x, jnp.where(rowc == 0, first, pltpu.roll(x, 1, 0))

        def shift_mix(ref, pref, m):
            x, xs = shift(ref, pref)
            return x + (xs - x) * m

        r = shift_mix(pr_ref, ppr_ref, mu[0:1])
        k = shift_mix(pk_ref, ppk_ref, mu[1:2])
        v = shift_mix(pv_ref, ppv_ref, mu[2:3])

        _, lo_b = shift(lo_ref, plo_ref, slice(half, 2 * half))
        l_all = lo_ref[tile_rows, :half] + lo_b
        l_wa = l_all[:, :PAIR]
        l_g = l_all[:, PAIR:]

        dw = _dot(jnp.tanh(l_wa).astype(BF16), w2b)
        ia = _dot(l_wa.astype(BF16), a2b)
        gate_ref[tile_rows, :] = _dot(jax.nn.sigmoid(l_g).astype(BF16), g2b).astype(BF16)

        zneg = -(w0 + dw)
        softplus = jnp.maximum(zneg, 0.0) + jnp.log(1.0 + jnp.exp(-jnp.abs(zneg)))
        logw = -jnp.exp(-softplus - 0.5)
        iclr = jax.nn.sigmoid(a0 + ia)

        kk = k * k_k
        kk = kk / jnp.maximum(jnp.sqrt(_head_sum(kk * kk, bd)), 1e-12)
        k2 = k * (1.0 + (iclr - 1.0) * k_a)
        a = -kk
        b = kk * iclr
        bonus_ref[tile_rows, :] = (_head_sum(r * k2 * r_k, bd) * v).astype(BF16)

        lw_hi, lw_lo = _split(logw)
        cl = jnp.concatenate(
            [_dot(tri_ones, lw_hi[q * PAIR:(q + 1) * PAIR]) + _dot(tri_ones, lw_lo[q * PAIR:(q + 1) * PAIR])
             for q in range(ts // PAIR)], axis=0)
        at_all = a * jnp.exp(cl - logw)
        rt_all = r * jnp.exp(cl)
        w_inv = jnp.exp(-cl)
        bt_all = b * w_inv
        kt_all = k2 * w_inv

        chunks = range(n_chunks)
        rows = [slice(c * CHUNK, (c + 1) * CHUNK) for c in chunks]
        out_rows = [slice(r0 + c * CHUNK, r0 + (c + 1) * CHUNK) for c in chunks]
        last = [cl[rows[c]][CHUNK - 1:CHUNK, :] for c in chunks]
        w_rem = [jnp.exp(last[c] - cl[rows[c]]) for c in chunks]
        a_s = [stack(at_all[rows[c]]) for c in chunks]
        r_s = [stack(rt_all[rows[c]]) for c in chunks]
        v_s = [stack(v[rows[c]]) for c in chunks]
        v_sb = [v_s[c].astype(BF16) for c in chunks]
        bk_h = [jnp.concatenate([stack(b[rows[c]] * w_rem[c]), stack(k2[rows[c]] * w_rem[c])],
                                axis=0).astype(BF16) for c in chunks]

        a_all = []
        for c in chunks:
            lhs = jnp.concatenate([a_s[c], r_s[c]], axis=0).astype(BF16)
            rhs = jnp.concatenate([stack(bt_all[rows[c]]), stack(kt_all[rows[c]])],
                                  axis=0).astype(BF16)
            a_all.append(jnp.where(tri, _dot_nt(lhs, rhs), 0.0))
        n_k = [a_all[c][:PAIR, :PAIR].astype(BF16) for c in chunks]
        a_r = [a_all[c][PAIR:, :].astype(BF16) for c in chunks]
        x = [jnp.concatenate([a_s[c], _dot(a_all[c][:PAIR, PAIR:].astype(BF16), v_sb[c])], axis=1)
             for c in chunks]
        steps = CHUNK.bit_length() - 1
        for it in range(steps):
            x = [x[c] + _dot(n_k[c], x[c].astype(BF16)) for c in chunks]
            if it + 1 < steps:
                n_k = [_dot(n_k[c], n_k[c]).astype(BF16) for c in chunks]
        zero_b = jnp.zeros((PAIR, PAIR), BF16)
        ry = [_dot(a_r[c], jnp.concatenate(
            [x[c].astype(BF16), jnp.concatenate([zero_b, v_sb[c]], axis=1)], axis=0)) for c in chunks]
        for c in chunks:
            r_c = rt_all[rows[c]]
            rp_ref[out_rows[c], :] = (ry[c][:CHUNK, :PAIR] + ry[c][CHUNK:, :PAIR] + r_c).astype(BF16)
            yq_ref[out_rows[c], :] = (ry[c][:CHUNK, PAIR:] + ry[c][CHUNK:, PAIR:]).astype(BF16)
        p_t = [x[c][:, :PAIR].T.astype(BF16) for c in chunks]
        qv_t = [jnp.concatenate([x[c][:, PAIR:], v_s[c]], axis=0).T.astype(BF16) for c in chunks]
        c0 = sub * n_chunks
        for c in chunks:
            g_ref[c0 + c, 0] = (jnp.where(eye, jnp.exp(last[c]), 0.0)
                                + _dot(p_t[c], bk_h[c][:PAIR]))
        for c in chunks:
            hbd = _dot(qv_t[c], bk_h[c])
            h_ref[c0 + c, 0] = hbd[:HEAD] + hbd[HEAD:]


def _rwkv_prep(proj, lora, mu_rkv, pvec, w2pad, a2pad, g2, seq_len, col0):
    t = proj.shape[0]
    width = mu_rkv.shape[1]
    n_pairs = width // PAIR
    tt = TT_PREP
    nl = lora.shape[1]
    cb = col0 // PAIR
    wb = width // PAIR

    def prev_rows(i):
        return jnp.maximum(i * (tt // 8) - 1, 0)

    def tok(c):
        return pl.BlockSpec((tt, PAIR), lambda i, p: (i, c + p))

    def prev(c):
        return pl.BlockSpec((8, PAIR), lambda i, p: (prev_rows(i), c + p))

    out_tok = pl.BlockSpec((tt, PAIR), lambda i, p: (i, p))
    tok_shape = jax.ShapeDtypeStruct((t, width), BF16)
    return pl.pallas_call(
        functools.partial(_rwkv_prep_kernel, seq_len),
        grid=(t // tt, n_pairs),
        in_specs=[
            tok(cb), tok(cb + wb), tok(cb + 2 * wb),
            prev(cb), prev(cb + wb), prev(cb + 2 * wb),
            pl.BlockSpec((tt, nl), lambda i, p: (i, 0)),
            pl.BlockSpec((8, nl), lambda i, p: (prev_rows(i), 0)),
            pl.BlockSpec((3, PAIR), lambda i, p: (0, p)),
            pl.BlockSpec((8, PAIR), lambda i, p: (0, p)),
            pl.BlockSpec((PAIR, PAIR), lambda i, p: (0, p)),
            pl.BlockSpec((PAIR, PAIR), lambda i, p: (0, p)),
            pl.BlockSpec((PAIR, PAIR), lambda i, p: (0, p)),
        ],
        out_specs=[
            out_tok, out_tok, out_tok, out_tok,
            pl.BlockSpec((tt // CHUNK, 1, PAIR, PAIR), lambda i, p: (i, p, 0, 0)),
            pl.BlockSpec((tt // CHUNK, 1, HEAD, PAIR), lambda i, p: (i, p, 0, 0)),
        ],
        out_shape=[
            tok_shape, tok_shape, tok_shape, tok_shape,
            jax.ShapeDtypeStruct((t // CHUNK, n_pairs, PAIR, PAIR), F32),
            jax.ShapeDtypeStruct((t // CHUNK, n_pairs, HEAD, PAIR), F32),
        ],
        compiler_params=_params(("arbitrary", "arbitrary")),
        name="rwkv_prep",
    )(proj, proj, proj, proj, proj, proj, lora, lora, mu_rkv, pvec, w2pad, a2pad, g2)


def _rwkv_scan_kernel(rp_ref, yq_ref, bonus_ref, gate_ref, g_ref, h_ref, pvec_ref, o_ref, s_scr, y_scr):
    nb, tt, width = rp_ref.shape
    n_pairs = width // PAIR

    @pl.when(pl.program_id(0) == 0)
    def _():
        s_scr[...] = jnp.zeros_like(s_scr)

    lane = lax.broadcasted_iota(I32, (HEAD, PAIR), 1)
    is_lo = lane < HEAD
    lane_r = lax.broadcasted_iota(I32, (PAIR, PAIR), 0)
    lane_c = lax.broadcasted_iota(I32, (PAIR, PAIR), 1)
    bd = jnp.where((lane_r // HEAD) == (lane_c // HEAD), 1.0, 0.0).astype(BF16)

    chains = [(b, p) for b in range(nb) for p in range(n_pairs)]
    state = {ch: s_scr[ch[0], ch[1]] for ch in chains}
    for c in range(tt // CHUNK):
        rows = slice(c * CHUNK, (c + 1) * CHUNK)
        for (b, p) in chains:
            cols = slice(p * PAIR, (p + 1) * PAIR)
            s0 = state[(b, p)]
            s_st = jnp.concatenate([jnp.where(is_lo, s0, 0.0), jnp.where(is_lo, 0.0, s0)], axis=0)
            y_scr[b, rows, cols] = (_dot_nt(rp_ref[b, rows, cols], s_st.astype(BF16))
                                    + yq_ref[b, rows, cols].astype(F32))
        state = {(b, p): _dot3(state[(b, p)], g_ref[b, c, p]) + h_ref[b, c, p] for (b, p) in chains}
    for (b, p) in chains:
        s_scr[b, p] = state[(b, p)]

    pvec = pvec_ref[...]
    for (b, p) in chains:
        cols = slice(p * PAIR, (p + 1) * PAIR)
        y = y_scr[b, :, cols]
        mu = _head_sum(y, bd) * (1.0 / HEAD)
        yc = y - mu
        var = _head_sum(yc * yc, bd) * (1.0 / HEAD)
        yn = yc * lax.rsqrt(var + LNX_EPS) * pvec[5:6, cols] + pvec[6:7, cols]
        o_ref[b, :, cols] = ((yn + bonus_ref[b, :, cols].astype(F32))
                             * gate_ref[b, :, cols].astype(F32))


def _rwkv_scan(rp, yq, bonus, gate, g, h, pvec, batch):
    t, width = rp.shape
    n_pairs = width // PAIR
    seq = t // batch
    tt = TT_RWKV
    nc = tt // CHUNK
    tok = pl.BlockSpec((batch, tt, width), lambda i: (0, i, 0))
    as_seq = lambda z: z.reshape(batch, seq, width)
    out = pl.pallas_call(
        _rwkv_scan_kernel,
        grid=(seq // tt,),
        in_specs=[
            tok, tok, tok, tok,
            pl.BlockSpec((batch, nc, n_pairs, PAIR, PAIR), lambda i: (0, i, 0, 0, 0)),
            pl.BlockSpec((batch, nc, n_pairs, HEAD, PAIR), lambda i: (0, i, 0, 0, 0)),
            pl.BlockSpec((8, width), lambda i: (0, 0)),
        ],
        out_specs=tok,
        out_shape=jax.ShapeDtypeStruct((batch, seq, width), F32),
        scratch_shapes=[pltpu.VMEM((batch, n_pairs, HEAD, PAIR), F32),
                        pltpu.VMEM((batch, tt, width), F32)],
        compiler_params=_params(("arbitrary",)),
        name="rwkv_scan",
    )(as_seq(rp), as_seq(yq), as_seq(bonus), as_seq(gate),
      g.reshape(batch, seq // CHUNK, n_pairs, PAIR, PAIR),
      h.reshape(batch, seq // CHUNK, n_pairs, HEAD, PAIR), pvec)
    return out.reshape(t, width)


def _out_proj_kernel(x_ref, ya_ref, yb_ref, wo_ref, g_ref, rwt_ref, rb_ref, h1_ref, hn_ref, lg_ref):
    wa = ya_ref.shape[1]
    h1 = (x_ref[...] + _dot(ya_ref[...].astype(BF16), wo_ref[0:wa, :])
          + _dot(yb_ref[...].astype(BF16), wo_ref[wa:, :]))
    h1_ref[...] = h1
    hn = h1 * lax.rsqrt(jnp.mean(h1 * h1, axis=-1, keepdims=True) + RMS_EPS) * g_ref[...]
    _store_token_tiles(hn_ref, hn)
    lg_ref[...] = _dot3_nt(rwt_ref[...], hn) + rb_ref[...]


def _out_proj(x2, ya, yb, wo_bf, g2, rwt, rb):
    t, d = x2.shape
    wa = ya.shape[1]
    wb = yb.shape[1]
    ne = rwt.shape[0]
    tm = TM_PROJ
    return pl.pallas_call(
        _out_proj_kernel,
        grid=(t // tm,),
        in_specs=[
            pl.BlockSpec((tm, d), lambda i: (i, 0)),
            pl.BlockSpec((tm, wa), lambda i: (i, 0)),
            pl.BlockSpec((tm, wb), lambda i: (i, 0)),
            pl.BlockSpec((wa + wb, d), lambda i: (0, 0)),
            pl.BlockSpec((1, d), lambda i: (0, 0)),
            pl.BlockSpec((ne, d), lambda i: (0, 0)),
            pl.BlockSpec((ne, 1), lambda i: (0, 0)),
        ],
        out_specs=[
            pl.BlockSpec((tm, d), lambda i: (i, 0)),
            pl.BlockSpec((tm * SUBLANES, LANES), lambda i: (i, 0)),
            pl.BlockSpec((ne, tm), lambda i: (0, i)),
        ],
        out_shape=[
            jax.ShapeDtypeStruct((t, d), F32),
            jax.ShapeDtypeStruct((t * SUBLANES, LANES), F32),
            jax.ShapeDtypeStruct((ne, t), F32),
        ],
        compiler_params=_params(("arbitrary",)),
        name="out_proj",
    )(x2, ya, yb, wo_bf, g2, rwt, rb)


def _route_kernel(lg_ref, pos_ref, gate_ref, pend_ref, carry_scr, pstart_scr):
    phase = pl.program_id(0)
    first = pl.program_id(1) == 0

    @pl.when(first & (phase == 0))
    def _():
        carry_scr[...] = jnp.zeros_like(carry_scr)
        pstart_scr[...] = jnp.zeros_like(pstart_scr)
        pend_ref[...] = jnp.zeros_like(pend_ref)

    @pl.when(first & (phase == 1))
    def _():
        counts = carry_scr[...]
        padded = jnp.ceil(counts * (1.0 / BM_MOE)) * BM_MOE
        row = lax.broadcasted_iota(I32, counts.shape, 0)
        end = padded
        s = 1
        while s < counts.shape[0]:
            end = end + jnp.where(row >= s, pltpu.roll(end, s, 0), 0.0)
            s *= 2
        pstart_scr[...] = end - padded
        pend_ref[...] = end.astype(I32)
        carry_scr[...] = jnp.zeros_like(carry_scr)

    l = lg_ref[...]
    ne, tr = l.shape
    e_iota = lax.broadcasted_iota(I32, (ne, tr), 0)
    chosen = jnp.zeros((ne, tr), F32)
    vals, sels = [], []
    for j in range(TOP_K):
        m = jnp.max(l, axis=0, keepdims=True)
        idx = jnp.min(jnp.where(l == m, e_iota, ne), axis=0, keepdims=True)
        sel = e_iota == idx
        vals.append(m)
        sels.append(sel)
        chosen = jnp.where(sel, 1.0, chosen)
        l = jnp.where(sel, -jnp.inf, l)
    ex = [jnp.exp(vj - vals[0]) for vj in vals]
    den = ex[0] + ex[1] + ex[2] + ex[3]
    for j in range(TOP_K):
        gate_ref[j:j + 1, :] = ex[j] / den

    ti = lax.broadcasted_iota(I32, (tr, tr), 0)
    tj = lax.broadcasted_iota(I32, (tr, tr), 1)
    upper = jnp.where(ti <= tj, 1.0, 0.0).astype(BF16)
    inc = _dot(chosen.astype(BF16), upper)
    carry = carry_scr[...]
    row_of = inc - chosen + carry[:, 0:1] + pstart_scr[:, 0:1]
    for j in range(TOP_K):
        pj = jnp.sum(jnp.where(sels[j], row_of, 0.0), axis=0, keepdims=True)
        pos_ref[j:j + 1, :] = pj.astype(I32)
    carry_scr[...] = carry + inc[:, tr - 1:tr]


def _route(logits_t):
    ne, t = logits_t.shape
    tr = TR_ROUTE
    tok = pl.BlockSpec((TOP_K, tr), lambda ph, i: (0, i * ph))
    return pl.pallas_call(
        _route_kernel,
        grid=(2, t // tr),
        in_specs=[pl.BlockSpec((ne, tr), lambda ph, i: (0, i))],
        out_specs=[tok, tok, pl.BlockSpec((ne, 128), lambda ph, i: (0, 0))],
        out_shape=[
            jax.ShapeDtypeStruct((TOP_K, t), I32),
            jax.ShapeDtypeStruct((TOP_K, t), F32),
            jax.ShapeDtypeStruct((ne, 128), I32),
        ],
        scratch_shapes=[pltpu.VMEM((ne, 128), F32), pltpu.VMEM((ne, 128), F32)],
        compiler_params=_params(("arbitrary", "arbitrary")),
        name="route",
    )(logits_t)


def _dispatch_kernel(pend_ref, nused_ref, pos_ref, hn_ref, xin_ref, zero_scr, sem):
    bm = zero_scr.shape[0] // SUBLANES
    td = pos_ref.shape[0] // TOP_K
    i = pl.program_id(0)

    def tile(ref, token):
        return ref.at[pl.ds(pl.multiple_of(token * SUBLANES, SUBLANES), SUBLANES)]

    def zero_block(row0):
        start = pl.multiple_of(row0 * SUBLANES, bm * SUBLANES)
        return pltpu.make_async_copy(zero_scr, xin_ref.at[pl.ds(start, bm * SUBLANES)], sem)

    @pl.when(i == 0)
    def _():
        zero_scr[...] = jnp.zeros_like(zero_scr)

        def has_rows(e):
            end = pend_ref[e]
            return jnp.where(e == 0, end >= bm, end > pend_ref[jnp.maximum(e - 1, 0)])

        def start_one(e, carry):
            @pl.when(has_rows(e))
            def _():
                zero_block(pend_ref[e] - bm).start()
            return carry

        def wait_one(e, carry):
            @pl.when(has_rows(e))
            def _():
                zero_block(0).wait()
            return carry

        def start_tail(blk, carry):
            zero_block(blk * bm).start()
            return carry

        def wait_tail(blk, carry):
            zero_block(0).wait()
            return carry

        n_blocks = xin_ref.shape[0] // (bm * SUBLANES)
        lax.fori_loop(0, pend_ref.shape[0], start_one, 0)
        lax.fori_loop(nused_ref[0], n_blocks, start_tail, 0)
        lax.fori_loop(0, pend_ref.shape[0], wait_one, 0)
        lax.fori_loop(nused_ref[0], n_blocks, wait_tail, 0)

    def issue(tk, carry):
        for j in range(TOP_K):
            pltpu.make_async_copy(tile(hn_ref, tk), tile(xin_ref, pos_ref[TOP_K * tk + j]),
                                  sem).start(priority=j % 2)
        return carry

    lax.fori_loop(0, td, issue, 0, unroll=8)

    for j in range(TOP_K):
        pltpu.make_async_copy(hn_ref, xin_ref.at[pl.ds(0, td * SUBLANES)], sem).wait()


def _dispatch(pad_end, n_used, pos, hn_tiles, n_rows):
    t = hn_tiles.shape[0] // SUBLANES
    td = TD_DISPATCH
    grid_spec = pltpu.PrefetchScalarGridSpec(
        num_scalar_prefetch=2,
        grid=(t // td,),
        in_specs=[
            pl.BlockSpec((TOP_K * td,), lambda i, pe, nu: (i,), memory_space=pltpu.SMEM),
            pl.BlockSpec((td * SUBLANES, LANES), lambda i, pe, nu: (i, 0)),
        ],
        out_specs=pl.BlockSpec(memory_space=pl.ANY),
        scratch_shapes=[pltpu.VMEM((BM_MOE * SUBLANES, LANES), F32), pltpu.SemaphoreType.DMA(())],
    )
    return pl.pallas_call(
        _dispatch_kernel,
        grid_spec=grid_spec,
        out_shape=jax.ShapeDtypeStruct((n_rows * SUBLANES, LANES), F32),
        compiler_params=_params(("arbitrary",)),
        name="dispatch",
    )(pad_end, n_used, pos, hn_tiles)


def _experts_kernel(pend_ref, nused_ref, w1_ref, b1_ref, w2_ref, b2_ref, xin_ref, yout_ref,
                    xbuf, obuf, wt_scr, wg_scr, wl_scr, w2_scr, sem_in, sem_out):
    e = pl.program_id(0)
    rows_blk = xbuf.shape[1]
    bm = rows_blk // SUBLANES
    dh = w2_ref.shape[1]
    n_used = nused_ref[0]
    blk_lo = jnp.where(e == 0, 0, pend_ref[jnp.maximum(e - 1, 0)]) // bm
    blk_hi = pend_ref[e] // bm

    def block(ref, blk):
        return ref.at[pl.ds(pl.multiple_of(blk * rows_blk, rows_blk), rows_blk)]

    def x_copy(blk, slot):
        return pltpu.make_async_copy(block(xin_ref, blk), xbuf.at[slot], sem_in.at[slot])

    def o_copy(blk, slot):
        return pltpu.make_async_copy(obuf.at[slot], block(yout_ref, blk), sem_out.at[slot])

    @pl.when((e == 0) & (n_used > 0))
    def _():
        x_copy(0, 0).start()

    @pl.when(blk_hi > blk_lo)
    def _():
        n_slab, n_t, lanes = wt_scr.shape
        for part in range(2 * dh // n_t):
            rows = slice(part * (n_t // 2), (part + 1) * (n_t // 2))
            for s in range(n_slab):
                cols = slice(s * lanes, (s + 1) * lanes)
                wt_scr[s] = w1_ref[0, cols, part * n_t:(part + 1) * n_t].T
                wg_scr[rows, cols] = wt_scr[s, pl.ds(0, n_t // 2, stride=2), :].astype(BF16)
                wl_scr[rows, cols] = wt_scr[s, pl.ds(1, n_t // 2, stride=2), :].astype(BF16)
        w2_scr[...] = w2_ref[0].astype(BF16)

        def one_block(blk, carry):
            slot = blk % 2

            @pl.when(blk + 1 < n_used)
            def _():
                x_copy(blk + 1, 1 - slot).start()

            x_copy(blk, slot).wait()

            @pl.when(blk >= 2)
            def _():
                o_copy(blk - 2, slot).wait()

            xb = _load_token_tiles(xbuf.at[slot], 0, bm).astype(BF16)
            bias = b1_ref[0]
            glu = jnp.minimum(_dot_nt(xb, wg_scr[...]) + bias[:, :dh], SWIGLU_LIMIT)
            lin = jnp.clip(_dot_nt(xb, wl_scr[...]) + bias[:, dh:], -SWIGLU_LIMIT, SWIGLU_LIMIT)
            act = glu * jax.nn.sigmoid(SWIGLU_ALPHA * glu) * (lin + 1.0)
            _store_token_tiles(obuf.at[slot], _dot(act.astype(BF16), w2_scr[...]) + b2_ref[0])
            o_copy(blk, slot).start()
            return carry

        lax.fori_loop(blk_lo, blk_hi, one_block, 0)

    @pl.when(e == pl.num_programs(0) - 1)
    def _():
        @pl.when(n_used >= 1)
        def _():
            o_copy(0, (n_used - 1) % 2).wait()

        @pl.when(n_used >= 2)
        def _():
            o_copy(0, n_used % 2).wait()

        obuf[0] = jnp.zeros(obuf.shape[1:], obuf.dtype)
        n_blocks = yout_ref.shape[0] // rows_blk

        def start_tail(blk, carry):
            o_copy(blk, 0).start()
            return carry

        def wait_tail(blk, carry):
            o_copy(0, 0).wait()
            return carry

        lax.fori_loop(n_used, n_blocks, start_tail, 0)
        lax.fori_loop(n_used, n_blocks, wait_tail, 0)


def _experts(pad_end, n_used, xin, w1, b1p, w2, b2):
    ne, d, dh2 = w1.shape
    dh = dh2 // 2
    rows_blk = BM_MOE * SUBLANES
    grid_spec = pltpu.PrefetchScalarGridSpec(
        num_scalar_prefetch=2,
        grid=(ne,),
        in_specs=[
            pl.BlockSpec((1, d, dh2), lambda e, pe, nu: (e, 0, 0)),
            pl.BlockSpec((1, 1, dh2), lambda e, pe, nu: (e, 0, 0)),
            pl.BlockSpec((1, dh, d), lambda e, pe, nu: (e, 0, 0)),
            pl.BlockSpec((1, 1, d), lambda e, pe, nu: (e, 0, 0)),
            pl.BlockSpec(memory_space=pl.ANY),
        ],
        out_specs=pl.BlockSpec(memory_space=pl.ANY),
        scratch_shapes=[
            pltpu.VMEM((2, rows_blk, LANES), F32),
            pltpu.VMEM((2, rows_blk, LANES), F32),
            pltpu.VMEM((d // 128, dh, 128), F32),
            pltpu.VMEM((dh, d), BF16),
            pltpu.VMEM((dh, d), BF16),
            pltpu.VMEM((dh, d), BF16),
            pltpu.SemaphoreType.DMA((2,)),
            pltpu.SemaphoreType.DMA((2,)),
        ],
    )
    return pl.pallas_call(
        _experts_kernel,
        grid_spec=grid_spec,
        out_shape=jax.ShapeDtypeStruct(xin.shape, F32),
        compiler_params=_params(("arbitrary",)),
        name="experts",
    )(pad_end, n_used, w1, b1p, w2, b2, xin)


def _combine_kernel(pos_ref, pos_next_ref, gate_ref, h1_ref, fg_ref, yout_ref, o_ref, ybuf, sem):
    tc = h1_ref.shape[0]
    i = pl.program_id(0)
    slot = i % 2

    def tile(ref, token):
        return ref.at[pl.ds(pl.multiple_of(token * SUBLANES, SUBLANES), SUBLANES)]

    def gather(p_ref, s):
        def issue(tk, carry):
            for j in range(TOP_K):
                pltpu.make_async_copy(tile(yout_ref, p_ref[TOP_K * tk + j]), tile(ybuf.at[s], j * tc + tk),
                                      sem.at[s]).start(priority=j % 2)
            return carry
        lax.fori_loop(0, tc, issue, 0, unroll=8)

    @pl.when(i == 0)
    def _():
        gather(pos_ref, 0)

    @pl.when(i + 1 < pl.num_programs(0))
    def _():
        gather(pos_next_ref, 1 - slot)

    pltpu.make_async_copy(yout_ref.at[pl.ds(0, ybuf.shape[1])], ybuf.at[slot], sem.at[slot]).wait()

    gates = gate_ref[...]
    h = h1_ref[...]
    for j in range(TOP_K):
        h = h + _load_token_tiles(ybuf.at[slot], j * tc * SUBLANES, tc) * gates[:, j:j + 1]
    o_ref[...] = h * lax.rsqrt(jnp.mean(h * h, axis=-1, keepdims=True) + RMS_EPS) * fg_ref[...]


def _combine(pos, gates_t, h1, fg, yout):
    t, d = h1.shape
    tc = TC_COMBINE
    n_steps = t // tc
    return pl.pallas_call(
        _combine_kernel,
        grid=(n_steps,),
        in_specs=[
            pl.BlockSpec((TOP_K * tc,), lambda i: (i,), memory_space=pltpu.SMEM),
            pl.BlockSpec((TOP_K * tc,), lambda i: (jnp.minimum(i + 1, n_steps - 1),),
                         memory_space=pltpu.SMEM),
            pl.BlockSpec((tc, TOP_K), lambda i: (i, 0)),
            pl.BlockSpec((tc, d), lambda i: (i, 0)),
            pl.BlockSpec((1, d), lambda i: (0, 0)),
            pl.BlockSpec(memory_space=pl.ANY),
        ],
        out_specs=pl.BlockSpec((tc, d), lambda i: (i, 0)),
        out_shape=jax.ShapeDtypeStruct((t, d), F32),
        scratch_shapes=[pltpu.VMEM((2, TOP_K * tc * SUBLANES, LANES), F32),
                        pltpu.SemaphoreType.DMA((2,))],
        compiler_params=_params(("arbitrary",)),
        name="combine",
    )(pos, pos, gates_t, h1, fg, yout)


def kernel(x, norm1_g, w_in, sgu_ln_g, sgu_ln_b, sgu_w, sgu_b, mu_rkv, mu_wag, decay_w0, decay_w1,
           decay_w2, iclr_a0, iclr_a1, iclr_a2, gate_g1, gate_g2, k_k, k_a, r_k, lnx_g, lnx_b, w_out,
           norm2_g, router_w, router_b, moe_w1, moe_b1, moe_w2, moe_b2, final_g):
    batch, seq, d = x.shape
    t = batch * seq
    depth = w_in.shape[0]
    sgu_width = sgu_ln_g.shape[1]
    rw = mu_rkv.shape[2]
    n_dec, n_icl, n_gate = decay_w1.shape[2], iclr_a1.shape[2], gate_g1.shape[2]
    assert n_dec == HEAD and n_icl == HEAD and n_gate == PAIR and rw % PAIR == 0
    assert seq % TT_RWKV == 0 and seq % TT_PREP == 0 and t % TM_PROJ == 0
    assert sgu_w.shape[2] == SGU_BLOCK
    assert depth == 1, "the final RMSNorm is fused into the last layer's combine kernel"
    assert d == SUBLANES * LANES, "the MoE row movers copy one (8, 128) f32 tile per token"

    h = x.reshape(t, d)
    for l in range(depth):
        win_bf = w_in[l].astype(BF16)
        wl = jnp.concatenate([decay_w1[l], iclr_a1[l], gate_g1[l]], axis=1)
        mucat = jnp.concatenate([
            jnp.broadcast_to(mu_wag[l, 0][:, None], (d, n_dec)),
            jnp.broadcast_to(mu_wag[l, 1][:, None], (d, n_icl)),
            jnp.broadcast_to(mu_wag[l, 2][:, None], (d, n_gate))], axis=1)
        zeros = jnp.zeros((HEAD, rw), F32)
        w2pad = jnp.concatenate([decay_w2[l], zeros], axis=0)
        a2pad = jnp.concatenate([zeros, iclr_a2[l]], axis=0)
        pvec = jnp.stack([decay_w0[l], iclr_a0[l], k_k[l], k_a[l], r_k[l].reshape(-1),
                          lnx_g[l], lnx_b[l], jnp.zeros((rw,), F32)], axis=0)
        bias2d = jnp.repeat(sgu_b[l].T, sgu_width // sgu_b.shape[1], axis=1)
        wo_bf = w_out[l].astype(BF16)
        b1p = jnp.concatenate([moe_b1[l][:, 0::2], moe_b1[l][:, 1::2]], axis=-1)[:, None, :]
        b2 = moe_b2[l][:, None, :]

        assert w_in.shape[2] == 2 * sgu_width + 3 * rw
        proj, lora, ya = _in_proj(h, norm1_g[l][None, :], win_bf, wl, mucat, sgu_ln_g[l][None, :],
                                  sgu_ln_b[l][None, :], sgu_w[l], bias2d)
        rp, yq, bonus, gate, g_m, h_m = _rwkv_prep(proj, lora, mu_rkv[l], pvec, w2pad, a2pad,
                                                   gate_g2[l], seq, 0)
        yb = _rwkv_scan(rp, yq, bonus, gate, g_m, h_m, pvec, batch)
        h1, hn2, logits_t = _out_proj(h, ya, yb, wo_bf, norm2_g[l][None, :], router_w[l].T,
                                      router_b[l][:, None])
        pos, gates, pend = _route(logits_t)

        pad_end = pend[:, 0]
        n_blocks = (t * TOP_K) // BM_MOE + N_EXPERTS
        n_used = pad_end[-1:] // BM_MOE

        pos = pos.T.reshape(-1)
        xin = _dispatch(pad_end, n_used, pos, hn2, n_blocks * BM_MOE)
        yout = _experts(pad_end, n_used, xin, moe_w1[l], b1p, moe_w2[l], b2)
        h = _combine(pos, gates.T, h1, final_g[None, :], yout)
    return h.reshape(batch, seq, d)
```

```python
import functools

import jax
import jax.numpy as jnp
from jax import lax
from jax.experimental import pallas as pl
from jax.experimental.pallas import tpu as pltpu

F32 = jnp.float32
BF16 = jnp.bfloat16
I32 = jnp.int32

RMS_EPS = 1e-5
LN_EPS = 1e-5
LNX_EPS = 64e-5
CHUNK = 64
SGU_BLOCK = 128
HEAD = 64
PAIR = 2 * HEAD
N_EXPERTS = 32
TOP_K = 4
SWIGLU_ALPHA = 1.702
SWIGLU_LIMIT = 7.0

V7X_VMEM_LIMIT = 56 * 1024 * 1024

TM_PROJ = 1024
TT_PREP = 1024
N_SUB_PREP = 1
TT_RWKV = 256
TR_ROUTE = 1024
BM_MOE = 256
TD_DISPATCH = 1024
TC_COMBINE = 256


def _dot(a, b):
    return jnp.dot(a, b, preferred_element_type=F32)


def _dot_nt(a, b):
    return lax.dot_general(a, b, (((1,), (1,)), ((), ())), preferred_element_type=F32)


def _split(x):
    hi = x.astype(BF16)
    lo = (x - hi.astype(F32)).astype(BF16)
    return hi, lo


def _dot3(a, b):
    ah, al = _split(a)
    bh, bl = _split(b)
    return _dot(ah, bh) + _dot(al, bh) + _dot(ah, bl)


def _dot3_nt(a, b):
    ah, al = _split(a)
    bh, bl = _split(b)
    return _dot_nt(ah, bh) + _dot_nt(al, bh) + _dot_nt(ah, bl)


def _head_sum(x, bd):
    hi, lo = _split(x)
    return _dot(hi, bd) + _dot(lo, bd)


SUBLANES = 8
LANES = 128


def _store_token_tiles(ref, x):
    n = x.shape[0]
    for s in range(SUBLANES):
        ref[pl.ds(s, n, stride=SUBLANES), :] = x[:, s * LANES:(s + 1) * LANES]


def _load_token_tiles(ref, row0, n):
    return jnp.concatenate(
        [ref[pl.ds(row0 + s, n, stride=SUBLANES), :] for s in range(SUBLANES)], axis=1)


def _params(sem, vmem=V7X_VMEM_LIMIT):
    return pltpu.CompilerParams(dimension_semantics=sem, vmem_limit_bytes=vmem)


def _in_proj_kernel(x_ref, g_ref, win_ref, wl_ref, mucat_ref, lng_ref, lnb_ref, w_ref, bias_ref,
                    proj_ref, lora_ref, o_ref, wl_scr, wm_scr):
    n_heads = w_ref.shape[0]

    @pl.when(pl.program_id(0) == 0)
    def _():
        wl = wl_ref[...]
        mu = mucat_ref[...]
        n = wl.shape[1]
        wl_scr[:, 0:n] = (wl * (1.0 - mu)).astype(BF16)
        wl_scr[:, n:2 * n] = (wl * mu).astype(BF16)
        qi = lax.broadcasted_iota(I32, (SGU_BLOCK, SGU_BLOCK), 0) // CHUNK
        kj = lax.broadcasted_iota(I32, (SGU_BLOCK, SGU_BLOCK), 1) // CHUNK
        for h in range(n_heads):
            wm_scr[h] = jnp.where(kj <= qi, w_ref[h], 0.0).astype(BF16)

    x = x_ref[...]
    hn = x * lax.rsqrt(jnp.mean(x * x, axis=-1, keepdims=True) + RMS_EPS) * g_ref[...]
    hb = hn.astype(BF16)
    width = lng_ref.shape[1]
    two_w = 2 * width
    proj_ref[...] = _dot(hb, win_ref[:, two_w:])
    lora_ref[...] = _dot(hb, wl_scr[...])

    z = _dot(hb, win_ref[:, :two_w])
    tm = z.shape[0]
    gz = 0.5 * z * (1.0 + lax.erf(z * (2.0 ** -0.5)))
    u = gz[:, :width]
    v = gz[:, width:]
    mu = jnp.mean(v, axis=-1, keepdims=True)
    vc = v - mu
    var = jnp.mean(vc * vc, axis=-1, keepdims=True)
    vn = vc * lax.rsqrt(var + LN_EPS) * lng_ref[...] + lnb_ref[...]
    lane = lax.broadcasted_iota(I32, (SGU_BLOCK, PAIR), 1)
    is_lo = lane < HEAD
    bias = bias_ref[...]
    for blk in range(tm // SGU_BLOCK):
        rows = slice(blk * SGU_BLOCK, (blk + 1) * SGU_BLOCK)
        for p in range(width // PAIR):
            cols = slice(p * PAIR, (p + 1) * PAIR)
            vp = vn[rows, cols]
            lo = jnp.where(is_lo, vp, 0.0).astype(BF16)
            hi = jnp.where(is_lo, 0.0, vp).astype(BF16)
            sv = _dot(wm_scr[2 * p], lo) + _dot(wm_scr[2 * p + 1], hi)
            o_ref[rows, cols] = u[rows, cols] * (sv + bias[:, cols])


def _in_proj(x2, g, win_bf, wl, mucat, ln_g, ln_b, w_s, bias2d):
    t, d = x2.shape
    width = ln_g.shape[1]
    n_rkv = win_bf.shape[1] - 2 * width
    n_l = wl.shape[1]
    n_heads = w_s.shape[0]
    tm = TM_PROJ
    const = lambda *shape: pl.BlockSpec(shape, lambda i: (0,) * len(shape))
    row = lambda n: pl.BlockSpec((tm, n), lambda i: (i, 0))
    return pl.pallas_call(
        _in_proj_kernel,
        grid=(t // tm,),
        in_specs=[
            row(d), const(1, d), const(d, win_bf.shape[1]), const(d, n_l), const(d, n_l),
            const(1, width), const(1, width), const(n_heads, SGU_BLOCK, SGU_BLOCK),
            const(SGU_BLOCK, width),
        ],
        out_specs=[row(n_rkv), row(2 * n_l), row(width)],
        out_shape=[
            jax.ShapeDtypeStruct((t, n_rkv), F32),
            jax.ShapeDtypeStruct((t, 2 * n_l), F32),
            jax.ShapeDtypeStruct((t, width), F32),
        ],
        scratch_shapes=[pltpu.VMEM((d, 2 * n_l), BF16),
                        pltpu.VMEM((n_heads, SGU_BLOCK, SGU_BLOCK), BF16)],
        compiler_params=_params(("arbitrary",)),
        name="in_proj_sgu",
    )(x2, g, win_bf, wl, mucat, ln_g, ln_b, w_s, bias2d)


def _rwkv_prep_kernel(seq_len, pr_ref, pk_ref, pv_ref, ppr_ref, ppk_ref, ppv_ref, lo_ref, plo_ref,
                      mu_ref, pvec_ref, w2_ref, a2_ref, g2_ref,
                      rp_ref, yq_ref, bonus_ref, gate_ref, g_ref, h_ref):
    tt = pr_ref.shape[0]
    ts = tt // N_SUB_PREP
    n_chunks = ts // CHUNK
    i = pl.program_id(0)
    keep = jnp.where((i * tt) % seq_len == 0, 0.0, 1.0)

    mu = mu_ref[...]
    pvec = pvec_ref[...]
    w0, a0, k_k, k_a, r_k = pvec[0:1], pvec[1:2], pvec[2:3], pvec[3:4], pvec[4:5]
    w2b = w2_ref[...].astype(BF16)
    a2b = a2_ref[...].astype(BF16)
    g2b = g2_ref[...].astype(BF16)
    half = lo_ref.shape[1] // 2

    lane_r = lax.broadcasted_iota(I32, (PAIR, PAIR), 0)
    lane_c = lax.broadcasted_iota(I32, (PAIR, PAIR), 1)
    bd = jnp.where((lane_r // HEAD) == (lane_c // HEAD), 1.0, 0.0).astype(BF16)
    tri_ones = jnp.where(((lane_r // CHUNK) == (lane_c // CHUNK)) & (lane_c <= lane_r),
                         1.0, 0.0).astype(BF16)
    lane = lax.broadcasted_iota(I32, (CHUNK, PAIR), 1)
    is_lo = lane < HEAD

    def stack(x):
        return jnp.concatenate([jnp.where(is_lo, x, 0.0), jnp.where(is_lo, 0.0, x)], axis=0)

    n2 = 2 * PAIR
    ri = lax.broadcasted_iota(I32, (n2, n2), 0)
    ci = lax.broadcasted_iota(I32, (n2, n2), 1)
    same_head = ((ri // CHUNK) % 2) == ((ci // CHUNK) % 2)
    t_i = ri % CHUNK
    s_i = ci % CHUNK
    tri = same_head & (s_i < t_i + ri // PAIR)
    eye = lane_r == lane_c

    for sub in range(N_SUB_PREP):
        r0 = sub * ts
        tile_rows = slice(r0, r0 + ts)

        def shift(ref, pref, cols=slice(None)):
            x = ref[tile_rows, cols]
            if sub == 0:
                first = pref[7:8, cols] * keep
            else:
                first = ref[r0 - 1:r0, cols]
            rowc = lax.broadcasted_iota(I32, x.shape, 0)
            return x, jnp.where(rowc == 0, first, pltpu.roll(x, 1, 0))

        def shift_mix(ref, pref, m):
            x, xs = shift(ref, pref)
            return x + (xs - x) * m

        r = shift_mix(pr_ref, ppr_ref, mu[0:1])
        k = shift_mix(pk_ref, ppk_ref, mu[1:2])
        v = shift_mix(pv_ref, ppv_ref, mu[2:3])

        _, lo_b = shift(lo_ref, plo_ref, slice(half, 2 * half))
        l_all = lo_ref[tile_rows, :half] + lo_b
        l_wa = l_all[:, :PAIR]
        l_g = l_all[:, PAIR:]

        dw = _dot(jnp.tanh(l_wa).astype(BF16), w2b)
        ia = _dot(l_wa.astype(BF16), a2b)
        gate_ref[tile_rows, :] = _dot(jax.nn.sigmoid(l_g).astype(BF16), g2b).astype(BF16)

        zneg = -(w0 + dw)
        softplus = jnp.maximum(zneg, 0.0) + jnp.log(1.0 + jnp.exp(-jnp.abs(zneg)))
        logw = -jnp.exp(-softplus - 0.5)
        iclr = jax.nn.sigmoid(a0 + ia)

        kk = k * k_k
        kk = kk / jnp.maximum(jnp.sqrt(_head_sum(kk * kk, bd)), 1e-12)
        k2 = k * (1.0 + (iclr - 1.0) * k_a)
        a = -kk
        b = kk * iclr
        bonus_ref[tile_rows, :] = (_head_sum(r * k2 * r_k, bd) * v).astype(BF16)

        lw_hi, lw_lo = _split(logw)
        cl = jnp.concatenate(
            [_dot(tri_ones, lw_hi[q * PAIR:(q + 1) * PAIR]) + _dot(tri_ones, lw_lo[q * PAIR:(q + 1) * PAIR])
             for q in range(ts // PAIR)], axis=0)
        at_all = a * jnp.exp(cl - logw)
        rt_all = r * jnp.exp(cl)
        w_inv = jnp.exp(-cl)
        bt_all = b * w_inv
        kt_all = k2 * w_inv

        chunks = range(n_chunks)
        rows = [slice(c * CHUNK, (c + 1) * CHUNK) for c in chunks]
        out_rows = [slice(r0 + c * CHUNK, r0 + (c + 1) * CHUNK) for c in chunks]
        last = [cl[rows[c]][CHUNK - 1:CHUNK, :] for c in chunks]
        w_rem = [jnp.exp(last[c] - cl[rows[c]]) for c in chunks]
        a_s = [stack(at_all[rows[c]]) for c in chunks]
        r_s = [stack(rt_all[rows[c]]) for c in chunks]
        v_s = [stack(v[rows[c]]) for c in chunks]
        v_sb = [v_s[c].astype(BF16) for c in chunks]
        bk_h = [jnp.concatenate([stack(b[rows[c]] * w_rem[c]), stack(k2[rows[c]] * w_rem[c])],
                                axis=0).astype(BF16) for c in chunks]

        a_all = []
        for c in chunks:
            lhs = jnp.concatenate([a_s[c], r_s[c]], axis=0).astype(BF16)
            rhs = jnp.concatenate([stack(bt_all[rows[c]]), stack(kt_all[rows[c]])],
                                  axis=0).astype(BF16)
            a_all.append(jnp.where(tri, _dot_nt(lhs, rhs), 0.0))
        n_k = [a_all[c][:PAIR, :PAIR].astype(BF16) for c in chunks]
        a_r = [a_all[c][PAIR:, :].astype(BF16) for c in chunks]
        x = [jnp.concatenate([a_s[c], _dot(a_all[c][:PAIR, PAIR:].astype(BF16), v_sb[c])], axis=1)
             for c in chunks]
        steps = CHUNK.bit_length() - 1
        for it in range(steps):
            x = [x[c] + _dot(n_k[c], x[c].astype(BF16)) for c in chunks]
            if it + 1 < steps:
                n_k = [_dot(n_k[c], n_k[c]).astype(BF16) for c in chunks]
        zero_b = jnp.zeros((PAIR, PAIR), BF16)
        ry = [_dot(a_r[c], jnp.concatenate(
            [x[c].astype(BF16), jnp.concatenate([zero_b, v_sb[c]], axis=1)], axis=0)) for c in chunks]
        for c in chunks:
            r_c = rt_all[rows[c]]
            rp_ref[out_rows[c], :] = (ry[c][:CHUNK, :PAIR] + ry[c][CHUNK:, :PAIR] + r_c).astype(BF16)
            yq_ref[out_rows[c], :] = (ry[c][:CHUNK, PAIR:] + ry[c][CHUNK:, PAIR:]).astype(BF16)
        p_t = [x[c][:, :PAIR].T.astype(BF16) for c in chunks]
        qv_t = [jnp.concatenate([x[c][:, PAIR:], v_s[c]], axis=0).T.astype(BF16) for c in chunks]
        c0 = sub * n_chunks
        for c in chunks:
            g_ref[c0 + c, 0] = (jnp.where(eye, jnp.exp(last[c]), 0.0)
                                + _dot(p_t[c], bk_h[c][:PAIR]))
        for c in chunks:
            hbd = _dot(qv_t[c], bk_h[c])
            h_ref[c0 + c, 0] = hbd[:HEAD] + hbd[HEAD:]


def _rwkv_prep(proj, lora, mu_rkv, pvec, w2pad, a2pad, g2, seq_len, col0):
    t = proj.shape[0]
    width = mu_rkv.shape[1]
    n_pairs = width // PAIR
    tt = TT_PREP
    nl = lora.shape[1]
    cb = col0 // PAIR
    wb = width // PAIR

    def prev_rows(i):
        return jnp.maximum(i * (tt // 8) - 1, 0)

    def tok(c):
        return pl.BlockSpec((tt, PAIR), lambda i, p: (i, c + p))

    def prev(c):
        return pl.BlockSpec((8, PAIR), lambda i, p: (prev_rows(i), c + p))

    out_tok = pl.BlockSpec((tt, PAIR), lambda i, p: (i, p))
    tok_shape = jax.ShapeDtypeStruct((t, width), BF16)
    return pl.pallas_call(
        functools.partial(_rwkv_prep_kernel, seq_len),
        grid=(t // tt, n_pairs),
        in_specs=[
            tok(cb), tok(cb + wb), tok(cb + 2 * wb),
            prev(cb), prev(cb + wb), prev(cb + 2 * wb),
            pl.BlockSpec((tt, nl), lambda i, p: (i, 0)),
            pl.BlockSpec((8, nl), lambda i, p: (prev_rows(i), 0)),
            pl.BlockSpec((3, PAIR), lambda i, p: (0, p)),
            pl.BlockSpec((8, PAIR), lambda i, p: (0, p)),
            pl.BlockSpec((PAIR, PAIR), lambda i, p: (0, p)),
            pl.BlockSpec((PAIR, PAIR), lambda i, p: (0, p)),
            pl.BlockSpec((PAIR, PAIR), lambda i, p: (0, p)),
        ],
        out_specs=[
            out_tok, out_tok, out_tok, out_tok,
            pl.BlockSpec((tt // CHUNK, 1, PAIR, PAIR), lambda i, p: (i, p, 0, 0)),
            pl.BlockSpec((tt // CHUNK, 1, HEAD, PAIR), lambda i, p: (i, p, 0, 0)),
        ],
        out_shape=[
            tok_shape, tok_shape, tok_shape, tok_shape,
            jax.ShapeDtypeStruct((t // CHUNK, n_pairs, PAIR, PAIR), F32),
            jax.ShapeDtypeStruct((t // CHUNK, n_pairs, HEAD, PAIR), F32),
        ],
        compiler_params=_params(("arbitrary", "arbitrary")),
        name="rwkv_prep",
    )(proj, proj, proj, proj, proj, proj, lora, lora, mu_rkv, pvec, w2pad, a2pad, g2)


def _rwkv_scan_kernel(rp_ref, yq_ref, bonus_ref, gate_ref, g_ref, h_ref, pvec_ref, o_ref, s_scr, y_scr):
    nb, tt, width = rp_ref.shape
    n_pairs = width // PAIR

    @pl.when(pl.program_id(0) == 0)
    def _():
        s_scr[...] = jnp.zeros_like(s_scr)

    lane = lax.broadcasted_iota(I32, (HEAD, PAIR), 1)
    is_lo = lane < HEAD
    lane_r = lax.broadcasted_iota(I32, (PAIR, PAIR), 0)
    lane_c = lax.broadcasted_iota(I32, (PAIR, PAIR), 1)
    bd = jnp.where((lane_r // HEAD) == (lane_c // HEAD), 1.0, 0.0).astype(BF16)

    chains = [(b, p) for b in range(nb) for p in range(n_pairs)]
    state = {ch: s_scr[ch[0], ch[1]] for ch in chains}
    for c in range(tt // CHUNK):
        rows = slice(c * CHUNK, (c + 1) * CHUNK)
        for (b, p) in chains:
            cols = slice(p * PAIR, (p + 1) * PAIR)
            s0 = state[(b, p)]
            s_st = jnp.concatenate([jnp.where(is_lo, s0, 0.0), jnp.where(is_lo, 0.0, s0)], axis=0)
            y_scr[b, rows, cols] = (_dot_nt(rp_ref[b, rows, cols], s_st.astype(BF16))
                                    + yq_ref[b, rows, cols].astype(F32))
        state = {(b, p): _dot3(state[(b, p)], g_ref[b, c, p]) + h_ref[b, c, p] for (b, p) in chains}
    for (b, p) in chains:
        s_scr[b, p] = state[(b, p)]

    pvec = pvec_ref[...]
    for (b, p) in chains:
        cols = slice(p * PAIR, (p + 1) * PAIR)
        y = y_scr[b, :, cols]
        mu = _head_sum(y, bd) * (1.0 / HEAD)
        yc = y - mu
        var = _head_sum(yc * yc, bd) * (1.0 / HEAD)
        yn = yc * lax.rsqrt(var + LNX_EPS) * pvec[5:6, cols] + pvec[6:7, cols]
        o_ref[b, :, cols] = ((yn + bonus_ref[b, :, cols].astype(F32))
                             * gate_ref[b, :, cols].astype(F32))


def _rwkv_scan(rp, yq, bonus, gate, g, h, pvec, batch):
    t, width = rp.shape
    n_pairs = width // PAIR
    seq = t // batch
    tt = TT_RWKV
    nc = tt // CHUNK
    tok = pl.BlockSpec((batch, tt, width), lambda i: (0, i, 0))
    as_seq = lambda z: z.reshape(batch, seq, width)
    out = pl.pallas_call(
        _rwkv_scan_kernel,
        grid=(seq // tt,),
        in_specs=[
            tok, tok, tok, tok,
            pl.BlockSpec((batch, nc, n_pairs, PAIR, PAIR), lambda i: (0, i, 0, 0, 0)),
            pl.BlockSpec((batch, nc, n_pairs, HEAD, PAIR), lambda i: (0, i, 0, 0, 0)),
            pl.BlockSpec((8, width), lambda i: (0, 0)),
        ],
        out_specs=tok,
        out_shape=jax.ShapeDtypeStruct((batch, seq, width), F32),
        scratch_shapes=[pltpu.VMEM((batch, n_pairs, HEAD, PAIR), F32),
                        pltpu.VMEM((batch, tt, width), F32)],
        compiler_params=_params(("arbitrary",)),
        name="rwkv_scan",
    )(as_seq(rp), as_seq(yq), as_seq(bonus), as_seq(gate),
      g.reshape(batch, seq // CHUNK, n_pairs, PAIR, PAIR),
      h.reshape(batch, seq // CHUNK, n_pairs, HEAD, PAIR), pvec)
    return out.reshape(t, width)


def _out_proj_kernel(x_ref, ya_ref, yb_ref, wo_ref, g_ref, rwt_ref, rb_ref, h1_ref, hn_ref, lg_ref):
    wa = ya_ref.shape[1]
    h1 = (x_ref[...] + _dot(ya_ref[...].astype(BF16), wo_ref[0:wa, :])
          + _dot(yb_ref[...].astype(BF16), wo_ref[wa:, :]))
    h1_ref[...] = h1
    hn = h1 * lax.rsqrt(jnp.mean(h1 * h1, axis=-1, keepdims=True) + RMS_EPS) * g_ref[...]
    _store_token_tiles(hn_ref, hn)
    lg_ref[...] = _dot3_nt(rwt_ref[...], hn) + rb_ref[...]


def _out_proj(x2, ya, yb, wo_bf, g2, rwt, rb):
    t, d = x2.shape
    wa = ya.shape[1]
    wb = yb.shape[1]
    ne = rwt.shape[0]
    tm = TM_PROJ
    return pl.pallas_call(
        _out_proj_kernel,
        grid=(t // tm,),
        in_specs=[
            pl.BlockSpec((tm, d), lambda i: (i, 0)),
            pl.BlockSpec((tm, wa), lambda i: (i, 0)),
            pl.BlockSpec((tm, wb), lambda i: (i, 0)),
            pl.BlockSpec((wa + wb, d), lambda i: (0, 0)),
            pl.BlockSpec((1, d), lambda i: (0, 0)),
            pl.BlockSpec((ne, d), lambda i: (0, 0)),
            pl.BlockSpec((ne, 1), lambda i: (0, 0)),
        ],
        out_specs=[
            pl.BlockSpec((tm, d), lambda i: (i, 0)),
            pl.BlockSpec((tm * SUBLANES, LANES), lambda i: (i, 0)),
            pl.BlockSpec((ne, tm), lambda i: (0, i)),
        ],
        out_shape=[
            jax.ShapeDtypeStruct((t, d), F32),
            jax.ShapeDtypeStruct((t * SUBLANES, LANES), F32),
            jax.ShapeDtypeStruct((ne, t), F32),
        ],
        compiler_params=_params(("arbitrary",)),
        name="out_proj",
    )(x2, ya, yb, wo_bf, g2, rwt, rb)


def _route_kernel(lg_ref, pos_ref, gate_ref, pend_ref, carry_scr, pstart_scr):
    phase = pl.program_id(0)
    first = pl.program_id(1) == 0

    @pl.when(first & (phase == 0))
    def _():
        carry_scr[...] = jnp.zeros_like(carry_scr)
        pstart_scr[...] = jnp.zeros_like(pstart_scr)
        pend_ref[...] = jnp.zeros_like(pend_ref)

    @pl.when(first & (phase == 1))
    def _():
        counts = carry_scr[...]
        padded = jnp.ceil(counts * (1.0 / BM_MOE)) * BM_MOE
        row = lax.broadcasted_iota(I32, counts.shape, 0)
        end = padded
        s = 1
        while s < counts.shape[0]:
            end = end + jnp.where(row >= s, pltpu.roll(end, s, 0), 0.0)
            s *= 2
        pstart_scr[...] = end - padded
        pend_ref[...] = end.astype(I32)
        carry_scr[...] = jnp.zeros_like(carry_scr)

    l = lg_ref[...]
    ne, tr = l.shape
    e_iota = lax.broadcasted_iota(I32, (ne, tr), 0)
    chosen = jnp.zeros((ne, tr), F32)
    vals, sels = [], []
    for j in range(TOP_K):
        m = jnp.max(l, axis=0, keepdims=True)
        idx = jnp.min(jnp.where(l == m, e_iota, ne), axis=0, keepdims=True)
        sel = e_iota == idx
        vals.append(m)
        sels.append(sel)
        chosen = jnp.where(sel, 1.0, chosen)
        l = jnp.where(sel, -jnp.inf, l)
    ex = [jnp.exp(vj - vals[0]) for vj in vals]
    den = ex[0] + ex[1] + ex[2] + ex[3]
    for j in range(TOP_K):
        gate_ref[j:j + 1, :] = ex[j] / den

    ti = lax.broadcasted_iota(I32, (tr, tr), 0)
    tj = lax.broadcasted_iota(I32, (tr, tr), 1)
    upper = jnp.where(ti <= tj, 1.0, 0.0).astype(BF16)
    inc = _dot(chosen.astype(BF16), upper)
    carry = carry_scr[...]
    row_of = inc - chosen + carry[:, 0:1] + pstart_scr[:, 0:1]
    for j in range(TOP_K):
        pj = jnp.sum(jnp.where(sels[j], row_of, 0.0), axis=0, keepdims=True)
        pos_ref[j:j + 1, :] = pj.astype(I32)
    carry_scr[...] = carry + inc[:, tr - 1:tr]


def _route(logits_t):
    ne, t = logits_t.shape
    tr = TR_ROUTE
    tok = pl.BlockSpec((TOP_K, tr), lambda ph, i: (0, i * ph))
    return pl.pallas_call(
        _route_kernel,
        grid=(2, t // tr),
        in_specs=[pl.BlockSpec((ne, tr), lambda ph, i: (0, i))],
        out_specs=[tok, tok, pl.BlockSpec((ne, 128), lambda ph, i: (0, 0))],
        out_shape=[
            jax.ShapeDtypeStruct((TOP_K, t), I32),
            jax.ShapeDtypeStruct((TOP_K, t), F32),
            jax.ShapeDtypeStruct((ne, 128), I32),
        ],
        scratch_shapes=[pltpu.VMEM((ne, 128), F32), pltpu.VMEM((ne, 128), F32)],
        compiler_params=_params(("arbitrary", "arbitrary")),
        name="route",
    )(logits_t)


def _dispatch_kernel(pend_ref, nused_ref, pos_ref, hn_ref, xin_ref, zero_scr, sem):
    bm = zero_scr.shape[0] // SUBLANES
    td = pos_ref.shape[0] // TOP_K
    i = pl.program_id(0)

    def tile(ref, token):
        return ref.at[pl.ds(pl.multiple_of(token * SUBLANES, SUBLANES), SUBLANES)]

    def zero_block(row0):
        start = pl.multiple_of(row0 * SUBLANES, bm * SUBLANES)
        return pltpu.make_async_copy(zero_scr, xin_ref.at[pl.ds(start, bm * SUBLANES)], sem)

    @pl.when(i == 0)
    def _():
        zero_scr[...] = jnp.zeros_like(zero_scr)

        def has_rows(e):
            end = pend_ref[e]
            return jnp.where(e == 0, end >= bm, end > pend_ref[jnp.maximum(e - 1, 0)])

        def start_one(e, carry):
            @pl.when(has_rows(e))
            def _():
                zero_block(pend_ref[e] - bm).start()
            return carry

        def wait_one(e, carry):
            @pl.when(has_rows(e))
            def _():
                zero_block(0).wait()
            return carry

        def start_tail(blk, carry):
            zero_block(blk * bm).start()
            return carry

        def wait_tail(blk, carry):
            zero_block(0).wait()
            return carry

        n_blocks = xin_ref.shape[0] // (bm * SUBLANES)
        lax.fori_loop(0, pend_ref.shape[0], start_one, 0)
        lax.fori_loop(nused_ref[0], n_blocks, start_tail, 0)
        lax.fori_loop(0, pend_ref.shape[0], wait_one, 0)
        lax.fori_loop(nused_ref[0], n_blocks, wait_tail, 0)

    def issue(tk, carry):
        for j in range(TOP_K):
            pltpu.make_async_copy(tile(hn_ref, tk), tile(xin_ref, pos_ref[TOP_K * tk + j]),
                                  sem).start(priority=j % 2)
        return carry

    lax.fori_loop(0, td, issue, 0, unroll=8)

    for j in range(TOP_K):
        pltpu.make_async_copy(hn_ref, xin_ref.at[pl.ds(0, td * SUBLANES)], sem).wait()


def _dispatch(pad_end, n_used, pos, hn_tiles, n_rows):
    t = hn_tiles.shape[0] // SUBLANES
    td = TD_DISPATCH
    grid_spec = pltpu.PrefetchScalarGridSpec(
        num_scalar_prefetch=2,
        grid=(t // td,),
        in_specs=[
            pl.BlockSpec((TOP_K * td,), lambda i, pe, nu: (i,), memory_space=pltpu.SMEM),
            pl.BlockSpec((td * SUBLANES, LANES), lambda i, pe, nu: (i, 0)),
        ],
        out_specs=pl.BlockSpec(memory_space=pl.ANY),
        scratch_shapes=[pltpu.VMEM((BM_MOE * SUBLANES, LANES), F32), pltpu.SemaphoreType.DMA(())],
    )
    return pl.pallas_call(
        _dispatch_kernel,
        grid_spec=grid_spec,
        out_shape=jax.ShapeDtypeStruct((n_rows * SUBLANES, LANES), F32),
        compiler_params=_params(("arbitrary",)),
        name="dispatch",
    )(pad_end, n_used, pos, hn_tiles)


def _experts_kernel(pend_ref, nused_ref, w1_ref, b1_ref, w2_ref, b2_ref, xin_ref, yout_ref,
                    xbuf, xb16, ystage, obuf, wt_scr, wg_scr, wl_scr, w2_scr, sem_in, sem_out):
    e = pl.program_id(0)
    rows_blk = xbuf.shape[1]
    bm = rows_blk // SUBLANES
    dh = w2_ref.shape[1]
    n_used = nused_ref[0]
    blk_lo = jnp.where(e == 0, 0, pend_ref[jnp.maximum(e - 1, 0)]) // bm
    blk_hi = pend_ref[e] // bm

    def block(ref, blk):
        return ref.at[pl.ds(pl.multiple_of(blk * rows_blk, rows_blk), rows_blk)]

    def x_copy(blk):
        slot = blk % 3
        return pltpu.make_async_copy(block(xin_ref, blk), xbuf.at[slot], sem_in.at[slot])

    def o_copy(blk):
        slot = blk % 2
        return pltpu.make_async_copy(obuf.at[slot], block(yout_ref, blk), sem_out.at[slot])

    def stage_in(blk):
        xb16[...] = _load_token_tiles(xbuf.at[blk % 3], 0, bm).astype(BF16)

    def stage_out(blk):
        _store_token_tiles(obuf.at[blk % 2], ystage[...])

    @pl.when(e == 0)
    def _():
        xbuf[...] = jnp.zeros_like(xbuf)
        ystage[...] = jnp.zeros_like(ystage)

        @pl.when(n_used > 0)
        def _():
            x_copy(0).start()

        @pl.when(n_used > 1)
        def _():
            x_copy(1).start()

        @pl.when(n_used > 0)
        def _():
            x_copy(0).wait()
            stage_in(0)

    @pl.when(blk_hi > blk_lo)
    def _():
        n_slab, n_t, lanes = wt_scr.shape
        for part in range(2 * dh // n_t):
            rows = slice(part * (n_t // 2), (part + 1) * (n_t // 2))
            for s in range(n_slab):
                cols = slice(s * lanes, (s + 1) * lanes)
                wt_scr[s] = w1_ref[0, cols, part * n_t:(part + 1) * n_t].T
                wg_scr[rows, cols] = wt_scr[s, pl.ds(0, n_t // 2, stride=2), :].astype(BF16)
                wl_scr[rows, cols] = wt_scr[s, pl.ds(1, n_t // 2, stride=2), :].astype(BF16)
        w2_scr[...] = w2_ref[0].astype(BF16)

        def one_block(blk, carry):
            @pl.when(blk + 2 < n_used)
            def _():
                x_copy(blk + 2).start()

            @pl.when(blk + 1 < n_used)
            def _():
                x_copy(blk + 1).wait()

            @pl.when(blk >= 3)
            def _():
                o_copy(blk - 3).wait()

            stage_out(blk + 1)
            xb = xb16[...]
            bias = b1_ref[0]
            glu = jnp.minimum(_dot_nt(xb, wg_scr[...]) + bias[:, :dh], SWIGLU_LIMIT)
            lin = jnp.clip(_dot_nt(xb, wl_scr[...]) + bias[:, dh:], -SWIGLU_LIMIT, SWIGLU_LIMIT)
            act = glu * jax.nn.sigmoid(SWIGLU_ALPHA * glu) * (lin + 1.0)
            stage_in(blk + 1)
            ystage[...] = _dot(act.astype(BF16), w2_scr[...]) + b2_ref[0]

            @pl.when(blk >= 1)
            def _():
                o_copy(blk - 1).start()
            return carry

        lax.fori_loop(blk_lo, blk_hi, one_block, 0)

    @pl.when(e == pl.num_programs(0) - 1)
    def _():
        @pl.when(n_used >= 3)
        def _():
            o_copy(n_used - 3).wait()

        @pl.when(n_used >= 1)
        def _():
            stage_out(n_used - 1)
            o_copy(n_used - 1).start()

        @pl.when(n_used >= 2)
        def _():
            o_copy(n_used - 2).wait()

        @pl.when(n_used >= 1)
        def _():
            o_copy(n_used - 1).wait()

        obuf[0] = jnp.zeros(obuf.shape[1:], obuf.dtype)
        n_blocks = yout_ref.shape[0] // rows_blk

        def tail_copy(blk):
            return pltpu.make_async_copy(obuf.at[0], block(yout_ref, blk), sem_out.at[0])

        def start_tail(blk, carry):
            tail_copy(blk).start()
            return carry

        def wait_tail(blk, carry):
            tail_copy(blk).wait()
            return carry

        lax.fori_loop(n_used, n_blocks, start_tail, 0)
        lax.fori_loop(n_used, n_blocks, wait_tail, 0)


def _experts(pad_end, n_used, xin, w1, b1p, w2, b2):
    ne, d, dh2 = w1.shape
    dh = dh2 // 2
    rows_blk = BM_MOE * SUBLANES
    grid_spec = pltpu.PrefetchScalarGridSpec(
        num_scalar_prefetch=2,
        grid=(ne,),
        in_specs=[
            pl.BlockSpec((1, d, dh2), lambda e, pe, nu: (e, 0, 0)),
            pl.BlockSpec((1, 1, dh2), lambda e, pe, nu: (e, 0, 0)),
            pl.BlockSpec((1, dh, d), lambda e, pe, nu: (e, 0, 0)),
            pl.BlockSpec((1, 1, d), lambda e, pe, nu: (e, 0, 0)),
            pl.BlockSpec(memory_space=pl.ANY),
        ],
        out_specs=pl.BlockSpec(memory_space=pl.ANY),
        scratch_shapes=[
            pltpu.VMEM((3, rows_blk, LANES), F32),
            pltpu.VMEM((BM_MOE, d), BF16),
            pltpu.VMEM((BM_MOE, d), F32),
            pltpu.VMEM((2, rows_blk, LANES), F32),
            pltpu.VMEM((d // 128, dh, 128), F32),
            pltpu.VMEM((dh, d), BF16),
            pltpu.VMEM((dh, d), BF16),
            pltpu.VMEM((dh, d), BF16),
            pltpu.SemaphoreType.DMA((3,)),
            pltpu.SemaphoreType.DMA((2,)),
        ],
    )
    return pl.pallas_call(
        _experts_kernel,
        grid_spec=grid_spec,
        out_shape=jax.ShapeDtypeStruct(xin.shape, F32),
        compiler_params=_params(("arbitrary",)),
        name="experts",
    )(pad_end, n_used, w1, b1p, w2, b2, xin)


def _combine_kernel(pos_ref, pos_next_ref, gate_ref, h1_ref, fg_ref, yout_ref, o_ref, ybuf, sem):
    tc = h1_ref.shape[0]
    i = pl.program_id(0)
    slot = i % 2

    def tile(ref, token):
        return ref.at[pl.ds(pl.multiple_of(token * SUBLANES, SUBLANES), SUBLANES)]

    def gather(p_ref, s):
        def issue(tk, carry):
            for j in range(TOP_K):
                pltpu.make_async_copy(tile(yout_ref, p_ref[TOP_K * tk + j]), tile(ybuf.at[s], j * tc + tk),
                                      sem.at[s]).start(priority=j % 2)
            return carry
        lax.fori_loop(0, tc, issue, 0, unroll=8)

    @pl.when(i == 0)
    def _():
        gather(pos_ref, 0)

    @pl.when(i + 1 < pl.num_programs(0))
    def _():
        gather(pos_next_ref, 1 - slot)

    pltpu.make_async_copy(yout_ref.at[pl.ds(0, ybuf.shape[1])], ybuf.at[slot], sem.at[slot]).wait()

    gates = gate_ref[...]
    h = h1_ref[...]
    for j in range(TOP_K):
        h = h + _load_token_tiles(ybuf.at[slot], j * tc * SUBLANES, tc) * gates[:, j:j + 1]
    o_ref[...] = h * lax.rsqrt(jnp.mean(h * h, axis=-1, keepdims=True) + RMS_EPS) * fg_ref[...]


def _combine(pos, gates_t, h1, fg, yout):
    t, d = h1.shape
    tc = TC_COMBINE
    n_steps = t // tc
    return pl.pallas_call(
        _combine_kernel,
        grid=(n_steps,),
        in_specs=[
            pl.BlockSpec((TOP_K * tc,), lambda i: (i,), memory_space=pltpu.SMEM),
            pl.BlockSpec((TOP_K * tc,), lambda i: (jnp.minimum(i + 1, n_steps - 1),),
                         memory_space=pltpu.SMEM),
            pl.BlockSpec((tc, TOP_K), lambda i: (i, 0)),
            pl.BlockSpec((tc, d), lambda i: (i, 0)),
            pl.BlockSpec((1, d), lambda i: (0, 0)),
            pl.BlockSpec(memory_space=pl.ANY),
        ],
        out_specs=pl.BlockSpec((tc, d), lambda i: (i, 0)),
        out_shape=jax.ShapeDtypeStruct((t, d), F32),
        scratch_shapes=[pltpu.VMEM((2, TOP_K * tc * SUBLANES, LANES), F32),
                        pltpu.SemaphoreType.DMA((2,))],
        compiler_params=_params(("arbitrary",)),
        name="combine",
    )(pos, pos, gates_t, h1, fg, yout)


def kernel(x, norm1_g, w_in, sgu_ln_g, sgu_ln_b, sgu_w, sgu_b, mu_rkv, mu_wag, decay_w0, decay_w1,
           decay_w2, iclr_a0, iclr_a1, iclr_a2, gate_g1, gate_g2, k_k, k_a, r_k, lnx_g, lnx_b, w_out,
           norm2_g, router_w, router_b, moe_w1, moe_b1, moe_w2, moe_b2, final_g):
    batch, seq, d = x.shape
    t = batch * seq
    depth = w_in.shape[0]
    sgu_width = sgu_ln_g.shape[1]
    rw = mu_rkv.shape[2]
    n_dec, n_icl, n_gate = decay_w1.shape[2], iclr_a1.shape[2], gate_g1.shape[2]
    assert n_dec == HEAD and n_icl == HEAD and n_gate == PAIR and rw % PAIR == 0
    assert seq % TT_RWKV == 0 and seq % TT_PREP == 0 and t % TM_PROJ == 0
    assert sgu_w.shape[2] == SGU_BLOCK
    assert depth == 1, "the final RMSNorm is fused into the last layer's combine kernel"
    assert d == SUBLANES * LANES, "the MoE row movers copy one (8, 128) f32 tile per token"

    h = x.reshape(t, d)
    for l in range(depth):
        win_bf = w_in[l].astype(BF16)
        wl = jnp.concatenate([decay_w1[l], iclr_a1[l], gate_g1[l]], axis=1)
        mucat = jnp.concatenate([
            jnp.broadcast_to(mu_wag[l, 0][:, None], (d, n_dec)),
            jnp.broadcast_to(mu_wag[l, 1][:, None], (d, n_icl)),
            jnp.broadcast_to(mu_wag[l, 2][:, None], (d, n_gate))], axis=1)
        zeros = jnp.zeros((HEAD, rw), F32)
        w2pad = jnp.concatenate([decay_w2[l], zeros], axis=0)
        a2pad = jnp.concatenate([zeros, iclr_a2[l]], axis=0)
        pvec = jnp.stack([decay_w0[l], iclr_a0[l], k_k[l], k_a[l], r_k[l].reshape(-1),
                          lnx_g[l], lnx_b[l], jnp.zeros((rw,), F32)], axis=0)
        bias2d = jnp.repeat(sgu_b[l].T, sgu_width // sgu_b.shape[1], axis=1)
        wo_bf = w_out[l].astype(BF16)
        b1p = jnp.concatenate([moe_b1[l][:, 0::2], moe_b1[l][:, 1::2]], axis=-1)[:, None, :]
        b2 = moe_b2[l][:, None, :]

        assert w_in.shape[2] == 2 * sgu_width + 3 * rw
        proj, lora, ya = _in_proj(h, norm1_g[l][None, :], win_bf, wl, mucat, sgu_ln_g[l][None, :],
                                  sgu_ln_b[l][None, :], sgu_w[l], bias2d)
        rp, yq, bonus, gate, g_m, h_m = _rwkv_prep(proj, lora, mu_rkv[l], pvec, w2pad, a2pad,
                                                   gate_g2[l], seq, 0)
        yb = _rwkv_scan(rp, yq, bonus, gate, g_m, h_m, pvec, batch)
        h1, hn2, logits_t = _out_proj(h, ya, yb, wo_bf, norm2_g[l][None, :], router_w[l].T,
                                      router_b[l][:, None])
        pos, gates, pend = _route(logits_t)

        pad_end = pend[:, 0]
        n_blocks = (t * TOP_K) // BM_MOE + N_EXPERTS
        n_used = pad_end[-1:] // BM_MOE

        pos = pos.T.reshape(-1)
        xin = _dispatch(pad_end, n_used, pos, hn2, n_blocks * BM_MOE)
        yout = _experts(pad_end, n_used, xin, moe_w1[l], b1p, moe_w2[l], b2)
        h = _combine(pos, gates.T, h1, final_g[None, :], yout)
    return h.reshape(batch, seq, d)
```

```python
import functools

import jax
import jax.numpy as jnp
from jax import lax
from jax.experimental import pallas as pl
from jax.experimental.pallas import tpu as pltpu

F32 = jnp.float32
BF16 = jnp.bfloat16
I32 = jnp.int32

RMS_EPS = 1e-5
LN_EPS = 1e-5
LNX_EPS = 64e-5
CHUNK = 64
SGU_BLOCK = 128
HEAD = 64
PAIR = 2 * HEAD
N_EXPERTS = 32
TOP_K = 4
SWIGLU_ALPHA = 1.702
SWIGLU_LIMIT = 7.0

V7X_VMEM_LIMIT = 56 * 1024 * 1024

TM_PROJ = 1024
TT_PREP = 1024
N_SUB_PREP = 1
TT_RWKV = 256
TR_ROUTE = 1024
BM_MOE = 256
TD_DISPATCH = 1024
TC_COMBINE = 256


def _dot(a, b):
    return jnp.dot(a, b, preferred_element_type=F32)


def _dot_nt(a, b):
    return lax.dot_general(a, b, (((1,), (1,)), ((), ())), preferred_element_type=F32)


def _split(x):
    hi = x.astype(BF16)
    lo = (x - hi.astype(F32)).astype(BF16)
    return hi, lo


def _dot3(a, b):
    ah, al = _split(a)
    bh, bl = _split(b)
    return _dot(ah, bh) + _dot(al, bh) + _dot(ah, bl)


def _dot3_nt(a, b):
    ah, al = _split(a)
    bh, bl = _split(b)
    return _dot_nt(ah, bh) + _dot_nt(al, bh) + _dot_nt(ah, bl)


def _head_sum(x, bd):
    hi, lo = _split(x)
    return _dot(hi, bd) + _dot(lo, bd)


SUBLANES = 8
LANES = 128


def _store_token_tiles(ref, x):
    n = x.shape[0]
    for s in range(SUBLANES):
        ref[pl.ds(s, n, stride=SUBLANES), :] = x[:, s * LANES:(s + 1) * LANES]


def _load_token_tiles(ref, row0, n):
    return jnp.concatenate(
        [ref[pl.ds(row0 + s, n, stride=SUBLANES), :] for s in range(SUBLANES)], axis=1)


def _params(sem, vmem=V7X_VMEM_LIMIT):
    return pltpu.CompilerParams(dimension_semantics=sem, vmem_limit_bytes=vmem)


def _in_proj_kernel(x_ref, g_ref, win_ref, wl_ref, mucat_ref, lng_ref, lnb_ref, w_ref, bias_ref,
                    proj_ref, lora_ref, o_ref, wl_scr, wm_scr):
    n_heads = w_ref.shape[0]

    @pl.when(pl.program_id(0) == 0)
    def _():
        wl = wl_ref[...]
        mu = mucat_ref[...]
        n = wl.shape[1]
        wl_scr[:, 0:n] = (wl * (1.0 - mu)).astype(BF16)
        wl_scr[:, n:2 * n] = (wl * mu).astype(BF16)
        qi = lax.broadcasted_iota(I32, (SGU_BLOCK, SGU_BLOCK), 0) // CHUNK
        kj = lax.broadcasted_iota(I32, (SGU_BLOCK, SGU_BLOCK), 1) // CHUNK
        for h in range(n_heads):
            wm_scr[h] = jnp.where(kj <= qi, w_ref[h], 0.0).astype(BF16)

    x = x_ref[...]
    hn = x * lax.rsqrt(jnp.mean(x * x, axis=-1, keepdims=True) + RMS_EPS) * g_ref[...]
    hb = hn.astype(BF16)
    width = lng_ref.shape[1]
    two_w = 2 * width
    proj_ref[...] = _dot(hb, win_ref[:, two_w:])
    lora_ref[...] = _dot(hb, wl_scr[...])

    z = _dot(hb, win_ref[:, :two_w])
    tm = z.shape[0]
    gz = 0.5 * z * (1.0 + lax.erf(z * (2.0 ** -0.5)))
    u = gz[:, :width]
    v = gz[:, width:]
    mu = jnp.mean(v, axis=-1, keepdims=True)
    vc = v - mu
    var = jnp.mean(vc * vc, axis=-1, keepdims=True)
    vn = vc * lax.rsqrt(var + LN_EPS) * lng_ref[...] + lnb_ref[...]
    lane = lax.broadcasted_iota(I32, (SGU_BLOCK, PAIR), 1)
    is_lo = lane < HEAD
    bias = bias_ref[...]
    for blk in range(tm // SGU_BLOCK):
        rows = slice(blk * SGU_BLOCK, (blk + 1) * SGU_BLOCK)
        for p in range(width // PAIR):
            cols = slice(p * PAIR, (p + 1) * PAIR)
            vp = vn[rows, cols]
            lo = jnp.where(is_lo, vp, 0.0).astype(BF16)
            hi = jnp.where(is_lo, 0.0, vp).astype(BF16)
            sv = _dot(wm_scr[2 * p], lo) + _dot(wm_scr[2 * p + 1], hi)
            o_ref[rows, cols] = u[rows, cols] * (sv + bias[:, cols])


def _in_proj(x2, g, win_bf, wl, mucat, ln_g, ln_b, w_s, bias2d):
    t, d = x2.shape
    width = ln_g.shape[1]
    n_rkv = win_bf.shape[1] - 2 * width
    n_l = wl.shape[1]
    n_heads = w_s.shape[0]
    tm = TM_PROJ
    const = lambda *shape: pl.BlockSpec(shape, lambda i: (0,) * len(shape))
    row = lambda n: pl.BlockSpec((tm, n), lambda i: (i, 0))
    return pl.pallas_call(
        _in_proj_kernel,
        grid=(t // tm,),
        in_specs=[
            row(d), const(1, d), const(d, win_bf.shape[1]), const(d, n_l), const(d, n_l),
            const(1, width), const(1, width), const(n_heads, SGU_BLOCK, SGU_BLOCK),
            const(SGU_BLOCK, width),
        ],
        out_specs=[row(n_rkv), row(2 * n_l), row(width)],
        out_shape=[
            jax.ShapeDtypeStruct((t, n_rkv), F32),
            jax.ShapeDtypeStruct((t, 2 * n_l), F32),
            jax.ShapeDtypeStruct((t, width), F32),
        ],
        scratch_shapes=[pltpu.VMEM((d, 2 * n_l), BF16),
                        pltpu.VMEM((n_heads, SGU_BLOCK, SGU_BLOCK), BF16)],
        compiler_params=_params(("arbitrary",)),
        name="in_proj_sgu",
    )(x2, g, win_bf, wl, mucat, ln_g, ln_b, w_s, bias2d)


def _rwkv_prep_kernel(seq_len, pr_ref, pk_ref, pv_ref, ppr_ref, ppk_ref, ppv_ref, lo_ref, plo_ref,
                      mu_ref, pvec_ref, w2_ref, a2_ref, g2_ref,
                      rp_ref, yq_ref, bonus_ref, gate_ref, g_ref, h_ref):
    tt = pr_ref.shape[0]
    ts = tt // N_SUB_PREP
    n_chunks = ts // CHUNK
    i = pl.program_id(0)
    keep = jnp.where((i * tt) % seq_len == 0, 0.0, 1.0)

    mu = mu_ref[...]
    pvec = pvec_ref[...]
    w0, a0, k_k, k_a, r_k = pvec[0:1], pvec[1:2], pvec[2:3], pvec[3:4], pvec[4:5]
    w2b = w2_ref[...].astype(BF16)
    a2b = a2_ref[...].astype(BF16)
    g2b = g2_ref[...].astype(BF16)
    half = lo_ref.shape[1] // 2

    lane_r = lax.broadcasted_iota(I32, (PAIR, PAIR), 0)
    lane_c = lax.broadcasted_iota(I32, (PAIR, PAIR), 1)
    bd = jnp.where((lane_r // HEAD) == (lane_c // HEAD), 1.0, 0.0).astype(BF16)
    tri_ones = jnp.where(((lane_r // CHUNK) == (lane_c // CHUNK)) & (lane_c <= lane_r),
                         1.0, 0.0).astype(BF16)
    lane = lax.broadcasted_iota(I32, (CHUNK, PAIR), 1)
    is_lo = lane < HEAD

    def stack(x):
        return jnp.concatenate([jnp.where(is_lo, x, 0.0), jnp.where(is_lo, 0.0, x)], axis=0)

    n2 = 2 * PAIR
    ri = lax.broadcasted_iota(I32, (n2, n2), 0)
    ci = lax.broadcasted_iota(I32, (n2, n2), 1)
    same_head = ((ri // CHUNK) % 2) == ((ci // CHUNK) % 2)
    t_i = ri % CHUNK
    s_i = ci % CHUNK
    tri = same_head & (s_i < t_i + ri // PAIR)
    eye = lane_r == lane_c

    for sub in range(N_SUB_PREP):
        r0 = sub * ts
        tile_rows = slice(r0, r0 + ts)

        def shift(ref, pref, cols=slice(None)):
            x = ref[tile_rows, cols]
            if sub == 0:
                first = pref[7:8, cols] * keep
            else:
                first = ref[r0 - 1:r0, cols]
            rowc = lax.broadcasted_iota(I32, x.shape, 0)
            return x, jnp.where(rowc == 0, first, pltpu.roll(x, 1, 0))

        def shift_mix(ref, pref, m):
            x, xs = shift(ref, pref)
            return x + (xs - x) * m

        r = shift_mix(pr_ref, ppr_ref, mu[0:1])
        k = shift_mix(pk_ref, ppk_ref, mu[1:2])
        v = shift_mix(pv_ref, ppv_ref, mu[2:3])

        _, lo_b = shift(lo_ref, plo_ref, slice(half, 2 * half))
        l_all = lo_ref[tile_rows, :half] + lo_b
        l_wa = l_all[:, :PAIR]
        l_g = l_all[:, PAIR:]

        dw = _dot(jnp.tanh(l_wa).astype(BF16), w2b)
        ia = _dot(l_wa.astype(BF16), a2b)
        gate_ref[tile_rows, :] = _dot(jax.nn.sigmoid(l_g).astype(BF16), g2b).astype(BF16)

        zneg = -(w0 + dw)
        softplus = jnp.maximum(zneg, 0.0) + jnp.log(1.0 + jnp.exp(-jnp.abs(zneg)))
        logw = -jnp.exp(-softplus - 0.5)
        iclr = jax.nn.sigmoid(a0 + ia)

        kk = k * k_k
        kk = kk / jnp.maximum(jnp.sqrt(_head_sum(kk * kk, bd)), 1e-12)
        k2 = k * (1.0 + (iclr - 1.0) * k_a)
        a = -kk
        b = kk * iclr
        bonus_ref[tile_rows, :] = (_head_sum(r * k2 * r_k, bd) * v).astype(BF16)

        lw_hi, lw_lo = _split(logw)
        cl = jnp.concatenate(
            [_dot(tri_ones, lw_hi[q * PAIR:(q + 1) * PAIR]) + _dot(tri_ones, lw_lo[q * PAIR:(q + 1) * PAIR])
             for q in range(ts // PAIR)], axis=0)
        at_all = a * jnp.exp(cl - logw)
        rt_all = r * jnp.exp(cl)
        w_inv = jnp.exp(-cl)
        bt_all = b * w_inv
        kt_all = k2 * w_inv

        chunks = range(n_chunks)
        rows = [slice(c * CHUNK, (c + 1) * CHUNK) for c in chunks]
        out_rows = [slice(r0 + c * CHUNK, r0 + (c + 1) * CHUNK) for c in chunks]
        last = [cl[rows[c]][CHUNK - 1:CHUNK, :] for c in chunks]
        w_rem = [jnp.exp(last[c] - cl[rows[c]]) for c in chunks]
        a_s = [stack(at_all[rows[c]]) for c in chunks]
        r_s = [stack(rt_all[rows[c]]) for c in chunks]
        v_s = [stack(v[rows[c]]) for c in chunks]
        v_sb = [v_s[c].astype(BF16) for c in chunks]
        bk_h = [jnp.concatenate([stack(b[rows[c]] * w_rem[c]), stack(k2[rows[c]] * w_rem[c])],
                                axis=0).astype(BF16) for c in chunks]

        a_all = []
        for c in chunks:
            lhs = jnp.concatenate([a_s[c], r_s[c]], axis=0).astype(BF16)
            rhs = jnp.concatenate([stack(bt_all[rows[c]]), stack(kt_all[rows[c]])],
                                  axis=0).astype(BF16)
            a_all.append(jnp.where(tri, _dot_nt(lhs, rhs), 0.0))
        n_k = [a_all[c][:PAIR, :PAIR].astype(BF16) for c in chunks]
        a_r = [a_all[c][PAIR:, :].astype(BF16) for c in chunks]
        x = [jnp.concatenate([a_s[c], _dot(a_all[c][:PAIR, PAIR:].astype(BF16), v_sb[c])], axis=1)
             for c in chunks]
        steps = CHUNK.bit_length() - 1
        for it in range(steps):
            x = [x[c] + _dot(n_k[c], x[c].astype(BF16)) for c in chunks]
            if it + 1 < steps:
                n_k = [_dot(n_k[c], n_k[c]).astype(BF16) for c in chunks]
        zero_b = jnp.zeros((PAIR, PAIR), BF16)
        ry = [_dot(a_r[c], jnp.concatenate(
            [x[c].astype(BF16), jnp.concatenate([zero_b, v_sb[c]], axis=1)], axis=0)) for c in chunks]
        for c in chunks:
            r_c = rt_all[rows[c]]
            rp_ref[out_rows[c], :] = (ry[c][:CHUNK, :PAIR] + ry[c][CHUNK:, :PAIR] + r_c).astype(BF16)
            yq_ref[out_rows[c], :] = (ry[c][:CHUNK, PAIR:] + ry[c][CHUNK:, PAIR:]).astype(BF16)
        p_t = [x[c][:, :PAIR].T.astype(BF16) for c in chunks]
        qv_t = [jnp.concatenate([x[c][:, PAIR:], v_s[c]], axis=0).T.astype(BF16) for c in chunks]
        c0 = sub * n_chunks
        for c in chunks:
            g_ref[c0 + c, 0] = (jnp.where(eye, jnp.exp(last[c]), 0.0)
                                + _dot(p_t[c], bk_h[c][:PAIR]))
        for c in chunks:
            hbd = _dot(qv_t[c], bk_h[c])
            h_ref[c0 + c, 0] = hbd[:HEAD] + hbd[HEAD:]


def _rwkv_prep(proj, lora, mu_rkv, pvec, w2pad, a2pad, g2, seq_len, col0):
    t = proj.shape[0]
    width = mu_rkv.shape[1]
    n_pairs = width // PAIR
    tt = TT_PREP
    nl = lora.shape[1]
    cb = col0 // PAIR
    wb = width // PAIR

    def prev_rows(i):
        return jnp.maximum(i * (tt // 8) - 1, 0)

    def tok(c):
        return pl.BlockSpec((tt, PAIR), lambda i, p: (i, c + p))

    def prev(c):
        return pl.BlockSpec((8, PAIR), lambda i, p: (prev_rows(i), c + p))

    out_tok = pl.BlockSpec((tt, PAIR), lambda i, p: (i, p))
    tok_shape = jax.ShapeDtypeStruct((t, width), BF16)
    return pl.pallas_call(
        functools.partial(_rwkv_prep_kernel, seq_len),
        grid=(t // tt, n_pairs),
        in_specs=[
            tok(cb), tok(cb + wb), tok(cb + 2 * wb),
            prev(cb), prev(cb + wb), prev(cb + 2 * wb),
            pl.BlockSpec((tt, nl), lambda i, p: (i, 0)),
            pl.BlockSpec((8, nl), lambda i, p: (prev_rows(i), 0)),
            pl.BlockSpec((3, PAIR), lambda i, p: (0, p)),
            pl.BlockSpec((8, PAIR), lambda i, p: (0, p)),
            pl.BlockSpec((PAIR, PAIR), lambda i, p: (0, p)),
            pl.BlockSpec((PAIR, PAIR), lambda i, p: (0, p)),
            pl.BlockSpec((PAIR, PAIR), lambda i, p: (0, p)),
        ],
        out_specs=[
            out_tok, out_tok, out_tok, out_tok,
            pl.BlockSpec((tt // CHUNK, 1, PAIR, PAIR), lambda i, p: (i, p, 0, 0)),
            pl.BlockSpec((tt // CHUNK, 1, HEAD, PAIR), lambda i, p: (i, p, 0, 0)),
        ],
        out_shape=[
            tok_shape, tok_shape, tok_shape, tok_shape,
            jax.ShapeDtypeStruct((t // CHUNK, n_pairs, PAIR, PAIR), F32),
            jax.ShapeDtypeStruct((t // CHUNK, n_pairs, HEAD, PAIR), F32),
        ],
        compiler_params=_params(("arbitrary", "arbitrary")),
        name="rwkv_prep",
    )(proj, proj, proj, proj, proj, proj, lora, lora, mu_rkv, pvec, w2pad, a2pad, g2)


def _rwkv_scan_kernel(rp_ref, yq_ref, bonus_ref, gate_ref, g_ref, h_ref, pvec_ref, o_ref, s_scr, y_scr):
    nb, tt, width = rp_ref.shape
    n_pairs = width // PAIR

    @pl.when(pl.program_id(0) == 0)
    def _():
        s_scr[...] = jnp.zeros_like(s_scr)

    lane = lax.broadcasted_iota(I32, (HEAD, PAIR), 1)
    is_lo = lane < HEAD
    lane_r = lax.broadcasted_iota(I32, (PAIR, PAIR), 0)
    lane_c = lax.broadcasted_iota(I32, (PAIR, PAIR), 1)
    bd = jnp.where((lane_r // HEAD) == (lane_c // HEAD), 1.0, 0.0).astype(BF16)

    chains = [(b, p) for b in range(nb) for p in range(n_pairs)]
    state = {ch: s_scr[ch[0], ch[1]] for ch in chains}
    for c in range(tt // CHUNK):
        rows = slice(c * CHUNK, (c + 1) * CHUNK)
        for (b, p) in chains:
            cols = slice(p * PAIR, (p + 1) * PAIR)
            s0 = state[(b, p)]
            s_st = jnp.concatenate([jnp.where(is_lo, s0, 0.0), jnp.where(is_lo, 0.0, s0)], axis=0)
            y_scr[b, rows, cols] = (_dot_nt(rp_ref[b, rows, cols], s_st.astype(BF16))
                                    + yq_ref[b, rows, cols].astype(F32))
        state = {(b, p): _dot3(state[(b, p)], g_ref[b, c, p]) + h_ref[b, c, p] for (b, p) in chains}
    for (b, p) in chains:
        s_scr[b, p] = state[(b, p)]

    pvec = pvec_ref[...]
    for (b, p) in chains:
        cols = slice(p * PAIR, (p + 1) * PAIR)
        y = y_scr[b, :, cols]
        mu = _head_sum(y, bd) * (1.0 / HEAD)
        yc = y - mu
        var = _head_sum(yc * yc, bd) * (1.0 / HEAD)
        yn = yc * lax.rsqrt(var + LNX_EPS) * pvec[5:6, cols] + pvec[6:7, cols]
        o_ref[b, :, cols] = ((yn + bonus_ref[b, :, cols].astype(F32))
                             * gate_ref[b, :, cols].astype(F32))


def _rwkv_scan(rp, yq, bonus, gate, g, h, pvec, batch):
    t, width = rp.shape
    n_pairs = width // PAIR
    seq = t // batch
    tt = TT_RWKV
    nc = tt // CHUNK
    tok = pl.BlockSpec((batch, tt, width), lambda i: (0, i, 0))
    as_seq = lambda z: z.reshape(batch, seq, width)
    out = pl.pallas_call(
        _rwkv_scan_kernel,
        grid=(seq // tt,),
        in_specs=[
            tok, tok, tok, tok,
            pl.BlockSpec((batch, nc, n_pairs, PAIR, PAIR), lambda i: (0, i, 0, 0, 0)),
            pl.BlockSpec((batch, nc, n_pairs, HEAD, PAIR), lambda i: (0, i, 0, 0, 0)),
            pl.BlockSpec((8, width), lambda i: (0, 0)),
        ],
        out_specs=tok,
        out_shape=jax.ShapeDtypeStruct((batch, seq, width), F32),
        scratch_shapes=[pltpu.VMEM((batch, n_pairs, HEAD, PAIR), F32),
                        pltpu.VMEM((batch, tt, width), F32)],
        compiler_params=_params(("arbitrary",)),
        name="rwkv_scan",
    )(as_seq(rp), as_seq(yq), as_seq(bonus), as_seq(gate),
      g.reshape(batch, seq // CHUNK, n_pairs, PAIR, PAIR),
      h.reshape(batch, seq // CHUNK, n_pairs, HEAD, PAIR), pvec)
    return out.reshape(t, width)


def _out_proj_kernel(x_ref, ya_ref, yb_ref, wo_ref, g_ref, rwt_ref, rb_ref, h1_ref, hn_ref, lg_ref):
    wa = ya_ref.shape[1]
    h1 = (x_ref[...] + _dot(ya_ref[...].astype(BF16), wo_ref[0:wa, :])
          + _dot(yb_ref[...].astype(BF16), wo_ref[wa:, :]))
    h1_ref[...] = h1
    hn = h1 * lax.rsqrt(jnp.mean(h1 * h1, axis=-1, keepdims=True) + RMS_EPS) * g_ref[...]
    _store_token_tiles(hn_ref, hn)
    lg_ref[...] = _dot3_nt(rwt_ref[...], hn) + rb_ref[...]


def _out_proj(x2, ya, yb, wo_bf, g2, rwt, rb):
    t, d = x2.shape
    wa = ya.shape[1]
    wb = yb.shape[1]
    ne = rwt.shape[0]
    tm = TM_PROJ
    return pl.pallas_call(
        _out_proj_kernel,
        grid=(t // tm,),
        in_specs=[
            pl.BlockSpec((tm, d), lambda i: (i, 0)),
            pl.BlockSpec((tm, wa), lambda i: (i, 0)),
            pl.BlockSpec((tm, wb), lambda i: (i, 0)),
            pl.BlockSpec((wa + wb, d), lambda i: (0, 0)),
            pl.BlockSpec((1, d), lambda i: (0, 0)),
            pl.BlockSpec((ne, d), lambda i: (0, 0)),
            pl.BlockSpec((ne, 1), lambda i: (0, 0)),
        ],
        out_specs=[
            pl.BlockSpec((tm, d), lambda i: (i, 0)),
            pl.BlockSpec((tm * SUBLANES, LANES), lambda i: (i, 0)),
            pl.BlockSpec((ne, tm), lambda i: (0, i)),
        ],
        out_shape=[
            jax.ShapeDtypeStruct((t, d), F32),
            jax.ShapeDtypeStruct((t * SUBLANES, LANES), F32),
            jax.ShapeDtypeStruct((ne, t), F32),
        ],
        compiler_params=_params(("arbitrary",)),
        name="out_proj",
    )(x2, ya, yb, wo_bf, g2, rwt, rb)


def _route_kernel(lg_ref, pos_ref, gate_ref, pend_ref, pfill_ref, carry_scr, pstart_scr):
    phase = pl.program_id(0)
    first = pl.program_id(1) == 0

    @pl.when(first & (phase == 0))
    def _():
        carry_scr[...] = jnp.zeros_like(carry_scr)
        pstart_scr[...] = jnp.zeros_like(pstart_scr)
        pend_ref[...] = jnp.zeros_like(pend_ref)
        pfill_ref[...] = jnp.zeros_like(pfill_ref)

    @pl.when(first & (phase == 1))
    def _():
        counts = carry_scr[...]
        padded = jnp.ceil(counts * (1.0 / BM_MOE)) * BM_MOE
        row = lax.broadcasted_iota(I32, counts.shape, 0)
        end = padded
        s = 1
        while s < counts.shape[0]:
            end = end + jnp.where(row >= s, pltpu.roll(end, s, 0), 0.0)
            s *= 2
        pstart_scr[...] = end - padded
        pend_ref[...] = end.astype(I32)
        pfill_ref[...] = (end - padded + counts).astype(I32)
        carry_scr[...] = jnp.zeros_like(carry_scr)

    l = lg_ref[...]
    ne, tr = l.shape
    e_iota = lax.broadcasted_iota(I32, (ne, tr), 0)
    chosen = jnp.zeros((ne, tr), F32)
    vals, sels = [], []
    for j in range(TOP_K):
        m = jnp.max(l, axis=0, keepdims=True)
        idx = jnp.min(jnp.where(l == m, e_iota, ne), axis=0, keepdims=True)
        sel = e_iota == idx
        vals.append(m)
        sels.append(sel)
        chosen = jnp.where(sel, 1.0, chosen)
        l = jnp.where(sel, -jnp.inf, l)
    ex = [jnp.exp(vj - vals[0]) for vj in vals]
    den = ex[0] + ex[1] + ex[2] + ex[3]
    for j in range(TOP_K):
        gate_ref[j:j + 1, :] = ex[j] / den

    ti = lax.broadcasted_iota(I32, (tr, tr), 0)
    tj = lax.broadcasted_iota(I32, (tr, tr), 1)
    upper = jnp.where(ti <= tj, 1.0, 0.0).astype(BF16)
    inc = _dot(chosen.astype(BF16), upper)
    carry = carry_scr[...]
    row_of = inc - chosen + carry[:, 0:1] + pstart_scr[:, 0:1]
    for j in range(TOP_K):
        pj = jnp.sum(jnp.where(sels[j], row_of, 0.0), axis=0, keepdims=True)
        pos_ref[j:j + 1, :] = pj.astype(I32)
    carry_scr[...] = carry + inc[:, tr - 1:tr]


def _route(logits_t):
    ne, t = logits_t.shape
    tr = TR_ROUTE
    tok = pl.BlockSpec((TOP_K, tr), lambda ph, i: (0, i * ph))
    return pl.pallas_call(
        _route_kernel,
        grid=(2, t // tr),
        in_specs=[pl.BlockSpec((ne, tr), lambda ph, i: (0, i))],
        out_specs=[tok, tok, pl.BlockSpec((ne, 128), lambda ph, i: (0, 0)),
                   pl.BlockSpec((ne, 128), lambda ph, i: (0, 0))],
        out_shape=[
            jax.ShapeDtypeStruct((TOP_K, t), I32),
            jax.ShapeDtypeStruct((TOP_K, t), F32),
            jax.ShapeDtypeStruct((ne, 128), I32),
            jax.ShapeDtypeStruct((ne, 128), I32),
        ],
        scratch_shapes=[pltpu.VMEM((ne, 128), F32), pltpu.VMEM((ne, 128), F32)],
        compiler_params=_params(("arbitrary", "arbitrary")),
        name="route",
    )(logits_t)


def _dispatch_kernel(pend_ref, pfill_ref, nused_ref, pos_ref, hn_ref, xin_ref, zero_scr, sem, sem_z):
    bm = zero_scr.shape[0] // SUBLANES
    td = pos_ref.shape[0] // TOP_K
    i = pl.program_id(0)
    n_experts = pend_ref.shape[0]
    n_blocks = xin_ref.shape[0] // (bm * SUBLANES)

    def tile(ref, token):
        return ref.at[pl.ds(pl.multiple_of(token * SUBLANES, SUBLANES), SUBLANES)]

    def zero_block(blk):
        start = pl.multiple_of(blk * (bm * SUBLANES), bm * SUBLANES)
        return pltpu.make_async_copy(zero_scr, xin_ref.at[pl.ds(start, bm * SUBLANES)], sem_z)

    def zero_row(row):
        return pltpu.make_async_copy(zero_scr.at[pl.ds(0, SUBLANES)], tile(xin_ref, row), sem_z)

    def for_padding(fn_row, fn_block):
        def per_expert(e, carry):
            def per_row(row, c):
                fn_row(row)
                return c
            return lax.fori_loop(pfill_ref[e], pend_ref[e], per_row, carry)

        lax.fori_loop(0, n_experts, per_expert, 0)

        def per_block(blk, carry):
            fn_block(blk)
            return carry

        lax.fori_loop(nused_ref[0], n_blocks, per_block, 0)

    @pl.when(i == 0)
    def _():
        zero_scr[...] = jnp.zeros_like(zero_scr)
        for_padding(lambda row: zero_row(row).start(), lambda blk: zero_block(blk).start())

    def issue(tk, carry):
        for j in range(TOP_K):
            pltpu.make_async_copy(tile(hn_ref, tk), tile(xin_ref, pos_ref[TOP_K * tk + j]),
                                  sem).start(priority=j % 2)
        return carry

    lax.fori_loop(0, td, issue, 0, unroll=8)

    for j in range(TOP_K):
        pltpu.make_async_copy(hn_ref, xin_ref.at[pl.ds(0, td * SUBLANES)], sem).wait()

    @pl.when(i == 0)
    def _():
        for_padding(lambda row: zero_row(0).wait(), lambda blk: zero_block(0).wait())


def _dispatch(pad_end, pad_fill, n_used, pos, hn_tiles, n_rows):
    t = hn_tiles.shape[0] // SUBLANES
    td = TD_DISPATCH
    grid_spec = pltpu.PrefetchScalarGridSpec(
        num_scalar_prefetch=3,
        grid=(t // td,),
        in_specs=[
            pl.BlockSpec((TOP_K * td,), lambda i, pe, pf, nu: (i,), memory_space=pltpu.SMEM),
            pl.BlockSpec((td * SUBLANES, LANES), lambda i, pe, pf, nu: (i, 0)),
        ],
        out_specs=pl.BlockSpec(memory_space=pl.ANY),
        scratch_shapes=[pltpu.VMEM((BM_MOE * SUBLANES, LANES), F32), pltpu.SemaphoreType.DMA(()),
                        pltpu.SemaphoreType.DMA(())],
    )
    return pl.pallas_call(
        _dispatch_kernel,
        grid_spec=grid_spec,
        out_shape=jax.ShapeDtypeStruct((n_rows * SUBLANES, LANES), F32),
        compiler_params=_params(("arbitrary",)),
        name="dispatch",
    )(pad_end, pad_fill, n_used, pos, hn_tiles)


def _experts_kernel(pend_ref, nused_ref, w1_ref, b1_ref, w2_ref, b2_ref, xin_ref, yout_ref,
                    xbuf, obuf, wt_scr, wg_scr, wl_scr, w2_scr, sem_in, sem_out):
    e = pl.program_id(0)
    rows_blk = xbuf.shape[1]
    bm = rows_blk // SUBLANES
    dh = w2_ref.shape[1]
    n_used = nused_ref[0]
    blk_lo = jnp.where(e == 0, 0, pend_ref[jnp.maximum(e - 1, 0)]) // bm
    blk_hi = pend_ref[e] // bm

    def block(ref, blk):
        return ref.at[pl.ds(pl.multiple_of(blk * rows_blk, rows_blk), rows_blk)]

    def x_copy(blk, slot):
        return pltpu.make_async_copy(block(xin_ref, blk), xbuf.at[slot], sem_in.at[slot])

    def o_copy(blk, slot):
        return pltpu.make_async_copy(obuf.at[slot], block(yout_ref, blk), sem_out.at[slot])

    @pl.when((e == 0) & (n_used > 0))
    def _():
        x_copy(0, 0).start()

    @pl.when(blk_hi > blk_lo)
    def _():
        n_slab, n_t, lanes = wt_scr.shape
        for part in range(2 * dh // n_t):
            rows = slice(part * (n_t // 2), (part + 1) * (n_t // 2))
            for s in range(n_slab):
                cols = slice(s * lanes, (s + 1) * lanes)
                wt_scr[s] = w1_ref[0, cols, part * n_t:(part + 1) * n_t].T
                wg_scr[rows, cols] = wt_scr[s, pl.ds(0, n_t // 2, stride=2), :].astype(BF16)
                wl_scr[rows, cols] = wt_scr[s, pl.ds(1, n_t // 2, stride=2), :].astype(BF16)
        w2_scr[...] = w2_ref[0].astype(BF16)

        def one_block(blk, carry):
            slot = blk % 2

            @pl.when(blk + 1 < n_used)
            def _():
                x_copy(blk + 1, 1 - slot).start()

            x_copy(blk, slot).wait()

            @pl.when(blk >= 2)
            def _():
                o_copy(blk - 2, slot).wait()

            xb = _load_token_tiles(xbuf.at[slot], 0, bm).astype(BF16)
            bias = b1_ref[0]
            glu = jnp.minimum(_dot_nt(xb, wg_scr[...]) + bias[:, :dh], SWIGLU_LIMIT)
            lin = jnp.clip(_dot_nt(xb, wl_scr[...]) + bias[:, dh:], -SWIGLU_LIMIT, SWIGLU_LIMIT)
            act = glu * jax.nn.sigmoid(SWIGLU_ALPHA * glu) * (lin + 1.0)
            _store_token_tiles(obuf.at[slot], _dot(act.astype(BF16), w2_scr[...]) + b2_ref[0])
            o_copy(blk, slot).start()
            return carry

        lax.fori_loop(blk_lo, blk_hi, one_block, 0)

    @pl.when(e == pl.num_programs(0) - 1)
    def _():
        @pl.when(n_used >= 1)
        def _():
            o_copy(0, (n_used - 1) % 2).wait()

        @pl.when(n_used >= 2)
        def _():
            o_copy(0, n_used % 2).wait()

        obuf[0] = jnp.zeros(obuf.shape[1:], obuf.dtype)
        n_blocks = yout_ref.shape[0] // rows_blk

        def start_tail(blk, carry):
            o_copy(blk, 0).start()
            return carry

        def wait_tail(blk, carry):
            o_copy(0, 0).wait()
            return carry

        lax.fori_loop(n_used, n_blocks, start_tail, 0)
        lax.fori_loop(n_used, n_blocks, wait_tail, 0)


def _experts(pad_end, n_used, xin, w1, b1p, w2, b2):
    ne, d, dh2 = w1.shape
    dh = dh2 // 2
    rows_blk = BM_MOE * SUBLANES
    grid_spec = pltpu.PrefetchScalarGridSpec(
        num_scalar_prefetch=2,
        grid=(ne,),
        in_specs=[
            pl.BlockSpec((1, d, dh2), lambda e, pe, nu: (e, 0, 0)),
            pl.BlockSpec((1, 1, dh2), lambda e, pe, nu: (e, 0, 0)),
            pl.BlockSpec((1, dh, d), lambda e, pe, nu: (e, 0, 0)),
            pl.BlockSpec((1, 1, d), lambda e, pe, nu: (e, 0, 0)),
            pl.BlockSpec(memory_space=pl.ANY),
        ],
        out_specs=pl.BlockSpec(memory_space=pl.ANY),
        scratch_shapes=[
            pltpu.VMEM((2, rows_blk, LANES), F32),
            pltpu.VMEM((2, rows_blk, LANES), F32),
            pltpu.VMEM((d // 128, dh, 128), F32),
            pltpu.VMEM((dh, d), BF16),
            pltpu.VMEM((dh, d), BF16),
            pltpu.VMEM((dh, d), BF16),
            pltpu.SemaphoreType.DMA((2,)),
            pltpu.SemaphoreType.DMA((2,)),
        ],
    )
    return pl.pallas_call(
        _experts_kernel,
        grid_spec=grid_spec,
        out_shape=jax.ShapeDtypeStruct(xin.shape, F32),
        compiler_params=_params(("arbitrary",)),
        name="experts",
    )(pad_end, n_used, w1, b1p, w2, b2, xin)


def _combine_kernel(pos_ref, pos_next_ref, gate_ref, h1_ref, fg_ref, yout_ref, o_ref, ybuf, sem):
    tc = h1_ref.shape[0]
    i = pl.program_id(0)
    slot = i % 2

    def tile(ref, token):
        return ref.at[pl.ds(pl.multiple_of(token * SUBLANES, SUBLANES), SUBLANES)]

    def gather(p_ref, s):
        def issue(tk, carry):
            for j in range(TOP_K):
                pltpu.make_async_copy(tile(yout_ref, p_ref[TOP_K * tk + j]), tile(ybuf.at[s], j * tc + tk),
                                      sem.at[s]).start(priority=j % 2)
            return carry
        lax.fori_loop(0, tc, issue, 0, unroll=8)

    @pl.when(i == 0)
    def _():
        gather(pos_ref, 0)

    @pl.when(i + 1 < pl.num_programs(0))
    def _():
        gather(pos_next_ref, 1 - slot)

    pltpu.make_async_copy(yout_ref.at[pl.ds(0, ybuf.shape[1])], ybuf.at[slot], sem.at[slot]).wait()

    gates = gate_ref[...]
    h = h1_ref[...]
    for j in range(TOP_K):
        h = h + _load_token_tiles(ybuf.at[slot], j * tc * SUBLANES, tc) * gates[:, j:j + 1]
    o_ref[...] = h * lax.rsqrt(jnp.mean(h * h, axis=-1, keepdims=True) + RMS_EPS) * fg_ref[...]


def _combine(pos, gates_t, h1, fg, yout):
    t, d = h1.shape
    tc = TC_COMBINE
    n_steps = t // tc
    return pl.pallas_call(
        _combine_kernel,
        grid=(n_steps,),
        in_specs=[
            pl.BlockSpec((TOP_K * tc,), lambda i: (i,), memory_space=pltpu.SMEM),
            pl.BlockSpec((TOP_K * tc,), lambda i: (jnp.minimum(i + 1, n_steps - 1),),
                         memory_space=pltpu.SMEM),
            pl.BlockSpec((tc, TOP_K), lambda i: (i, 0)),
            pl.BlockSpec((tc, d), lambda i: (i, 0)),
            pl.BlockSpec((1, d), lambda i: (0, 0)),
            pl.BlockSpec(memory_space=pl.ANY),
        ],
        out_specs=pl.BlockSpec((tc, d), lambda i: (i, 0)),
        out_shape=jax.ShapeDtypeStruct((t, d), F32),
        scratch_shapes=[pltpu.VMEM((2, TOP_K * tc * SUBLANES, LANES), F32),
                        pltpu.SemaphoreType.DMA((2,))],
        compiler_params=_params(("arbitrary",)),
        name="combine",
    )(pos, pos, gates_t, h1, fg, yout)


def kernel(x, norm1_g, w_in, sgu_ln_g, sgu_ln_b, sgu_w, sgu_b, mu_rkv, mu_wag, decay_w0, decay_w1,
           decay_w2, iclr_a0, iclr_a1, iclr_a2, gate_g1, gate_g2, k_k, k_a, r_k, lnx_g, lnx_b, w_out,
           norm2_g, router_w, router_b, moe_w1, moe_b1, moe_w2, moe_b2, final_g):
    batch, seq, d = x.shape
    t = batch * seq
    depth = w_in.shape[0]
    sgu_width = sgu_ln_g.shape[1]
    rw = mu_rkv.shape[2]
    n_dec, n_icl, n_gate = decay_w1.shape[2], iclr_a1.shape[2], gate_g1.shape[2]
    assert n_dec == HEAD and n_icl == HEAD and n_gate == PAIR and rw % PAIR == 0
    assert seq % TT_RWKV == 0 and seq % TT_PREP == 0 and t % TM_PROJ == 0
    assert sgu_w.shape[2] == SGU_BLOCK
    assert depth == 1, "the final RMSNorm is fused into the last layer's combine kernel"
    assert d == SUBLANES * LANES, "the MoE row movers copy one (8, 128) f32 tile per token"

    h = x.reshape(t, d)
    for l in range(depth):
        win_bf = w_in[l].astype(BF16)
        wl = jnp.concatenate([decay_w1[l], iclr_a1[l], gate_g1[l]], axis=1)
        mucat = jnp.concatenate([
            jnp.broadcast_to(mu_wag[l, 0][:, None], (d, n_dec)),
            jnp.broadcast_to(mu_wag[l, 1][:, None], (d, n_icl)),
            jnp.broadcast_to(mu_wag[l, 2][:, None], (d, n_gate))], axis=1)
        zeros = jnp.zeros((HEAD, rw), F32)
        w2pad = jnp.concatenate([decay_w2[l], zeros], axis=0)
        a2pad = jnp.concatenate([zeros, iclr_a2[l]], axis=0)
        pvec = jnp.stack([decay_w0[l], iclr_a0[l], k_k[l], k_a[l], r_k[l].reshape(-1),
                          lnx_g[l], lnx_b[l], jnp.zeros((rw,), F32)], axis=0)
        bias2d = jnp.repeat(sgu_b[l].T, sgu_width // sgu_b.shape[1], axis=1)
        wo_bf = w_out[l].astype(BF16)
        b1p = jnp.concatenate([moe_b1[l][:, 0::2], moe_b1[l][:, 1::2]], axis=-1)[:, None, :]
        b2 = moe_b2[l][:, None, :]

        assert w_in.shape[2] == 2 * sgu_width + 3 * rw
        proj, lora, ya = _in_proj(h, norm1_g[l][None, :], win_bf, wl, mucat, sgu_ln_g[l][None, :],
                                  sgu_ln_b[l][None, :], sgu_w[l], bias2d)
        rp, yq, bonus, gate, g_m, h_m = _rwkv_prep(proj, lora, mu_rkv[l], pvec, w2pad, a2pad,
                                                   gate_g2[l], seq, 0)
        yb = _rwkv_scan(rp, yq, bonus, gate, g_m, h_m, pvec, batch)
        h1, hn2, logits_t = _out_proj(h, ya, yb, wo_bf, norm2_g[l][None, :], router_w[l].T,
                                      router_b[l][:, None])
        pos, gates, pend, pfill = _route(logits_t)

        pad_end = pend[:, 0]
        n_blocks = (t * TOP_K) // BM_MOE + N_EXPERTS
        n_used = pad_end[-1:] // BM_MOE

        pos = pos.T.reshape(-1)
        xin = _dispatch(pad_end, pfill[:, 0], n_used, pos, hn2, n_blocks * BM_MOE)
        yout = _experts(pad_end, n_used, xin, moe_w1[l], b1p, moe_w2[l], b2)
        h = _combine(pos, gates.T, h1, final_g[None, :], yout)
    return h.reshape(batch, seq, d)
```

```python
import functools

import jax
import jax.numpy as jnp
from jax import lax
from jax.experimental import pallas as pl
from jax.experimental.pallas import tpu as pltpu

F32 = jnp.float32
BF16 = jnp.bfloat16
I32 = jnp.int32

RMS_EPS = 1e-5
LN_EPS = 1e-5
LNX_EPS = 64e-5
CHUNK = 64
SGU_BLOCK = 128
HEAD = 64
PAIR = 2 * HEAD
N_EXPERTS = 32
TOP_K = 4
SWIGLU_ALPHA = 1.702
SWIGLU_LIMIT = 7.0

V7X_VMEM_LIMIT = 56 * 1024 * 1024

TM_PROJ = 1024
TT_PREP = 1024
N_SUB_PREP = 1
TT_RWKV = 512
TR_ROUTE = 1024
BM_MOE = 256
TD_DISPATCH = 1024
TC_COMBINE = 256


def _dot(a, b):
    return jnp.dot(a, b, preferred_element_type=F32)


def _dot_nt(a, b):
    return lax.dot_general(a, b, (((1,), (1,)), ((), ())), preferred_element_type=F32)


def _split(x):
    hi = x.astype(BF16)
    lo = (x - hi.astype(F32)).astype(BF16)
    return hi, lo


def _dot3(a, b):
    ah, al = _split(a)
    bh, bl = _split(b)
    return _dot(ah, bh) + _dot(al, bh) + _dot(ah, bl)


def _dot3_nt(a, b):
    ah, al = _split(a)
    bh, bl = _split(b)
    return _dot_nt(ah, bh) + _dot_nt(al, bh) + _dot_nt(ah, bl)


def _head_sum(x, bd):
    hi, lo = _split(x)
    return _dot(hi, bd) + _dot(lo, bd)


SUBLANES = 8
LANES = 128


def _store_token_tiles(ref, x):
    n = x.shape[0]
    for s in range(SUBLANES):
        ref[pl.ds(s, n, stride=SUBLANES), :] = x[:, s * LANES:(s + 1) * LANES]


def _load_token_tiles(ref, row0, n):
    return jnp.concatenate(
        [ref[pl.ds(row0 + s, n, stride=SUBLANES), :] for s in range(SUBLANES)], axis=1)


def _params(sem, vmem=V7X_VMEM_LIMIT):
    return pltpu.CompilerParams(dimension_semantics=sem, vmem_limit_bytes=vmem)


def _in_proj_kernel(x_ref, g_ref, win_ref, wl_ref, mucat_ref, lng_ref, lnb_ref, w_ref, bias_ref,
                    proj_ref, lora_ref, o_ref, wl_scr, wm_scr):
    n_heads = w_ref.shape[0]

    @pl.when(pl.program_id(0) == 0)
    def _():
        wl = wl_ref[...]
        mu = mucat_ref[...]
        n = wl.shape[1]
        wl_scr[:, 0:n] = (wl * (1.0 - mu)).astype(BF16)
        wl_scr[:, n:2 * n] = (wl * mu).astype(BF16)
        qi = lax.broadcasted_iota(I32, (SGU_BLOCK, SGU_BLOCK), 0) // CHUNK
        kj = lax.broadcasted_iota(I32, (SGU_BLOCK, SGU_BLOCK), 1) // CHUNK
        for h in range(n_heads):
            wm_scr[h] = jnp.where(kj <= qi, w_ref[h], 0.0).astype(BF16)

    x = x_ref[...]
    hn = x * lax.rsqrt(jnp.mean(x * x, axis=-1, keepdims=True) + RMS_EPS) * g_ref[...]
    hb = hn.astype(BF16)
    width = lng_ref.shape[1]
    two_w = 2 * width
    proj_ref[...] = _dot(hb, win_ref[:, two_w:])
    lora_ref[...] = _dot(hb, wl_scr[...])

    z = _dot(hb, win_ref[:, :two_w])
    tm = z.shape[0]
    gz = 0.5 * z * (1.0 + lax.erf(z * (2.0 ** -0.5)))
    u = gz[:, :width]
    v = gz[:, width:]
    mu = jnp.mean(v, axis=-1, keepdims=True)
    vc = v - mu
    var = jnp.mean(vc * vc, axis=-1, keepdims=True)
    vn = vc * lax.rsqrt(var + LN_EPS) * lng_ref[...] + lnb_ref[...]
    lane = lax.broadcasted_iota(I32, (SGU_BLOCK, PAIR), 1)
    is_lo = lane < HEAD
    bias = bias_ref[...]
    for blk in range(tm // SGU_BLOCK):
        rows = slice(blk * SGU_BLOCK, (blk + 1) * SGU_BLOCK)
        for p in range(width // PAIR):
            cols = slice(p * PAIR, (p + 1) * PAIR)
            vp = vn[rows, cols]
            lo = jnp.where(is_lo, vp, 0.0).astype(BF16)
            hi = jnp.where(is_lo, 0.0, vp).astype(BF16)
            sv = _dot(wm_scr[2 * p], lo) + _dot(wm_scr[2 * p + 1], hi)
            o_ref[rows, cols] = u[rows, cols] * (sv + bias[:, cols])


def _in_proj(x2, g, win_bf, wl, mucat, ln_g, ln_b, w_s, bias2d):
    t, d = x2.shape
    width = ln_g.shape[1]
    n_rkv = win_bf.shape[1] - 2 * width
    n_l = wl.shape[1]
    n_heads = w_s.shape[0]
    tm = TM_PROJ
    const = lambda *shape: pl.BlockSpec(shape, lambda i: (0,) * len(shape))
    row = lambda n: pl.BlockSpec((tm, n), lambda i: (i, 0))
    return pl.pallas_call(
        _in_proj_kernel,
        grid=(t // tm,),
        in_specs=[
            row(d), const(1, d), const(d, win_bf.shape[1]), const(d, n_l), const(d, n_l),
            const(1, width), const(1, width), const(n_heads, SGU_BLOCK, SGU_BLOCK),
            const(SGU_BLOCK, width),
        ],
        out_specs=[row(n_rkv), row(2 * n_l), row(width)],
        out_shape=[
            jax.ShapeDtypeStruct((t, n_rkv), F32),
            jax.ShapeDtypeStruct((t, 2 * n_l), F32),
            jax.ShapeDtypeStruct((t, width), F32),
        ],
        scratch_shapes=[pltpu.VMEM((d, 2 * n_l), BF16),
                        pltpu.VMEM((n_heads, SGU_BLOCK, SGU_BLOCK), BF16)],
        compiler_params=_params(("arbitrary",)),
        name="in_proj_sgu",
    )(x2, g, win_bf, wl, mucat, ln_g, ln_b, w_s, bias2d)


def _rwkv_prep_kernel(seq_len, pr_ref, pk_ref, pv_ref, ppr_ref, ppk_ref, ppv_ref, lo_ref, plo_ref,
                      mu_ref, pvec_ref, w2_ref, a2_ref, g2_ref,
                      rp_ref, yq_ref, bonus_ref, gate_ref, g_ref, h_ref):
    tt = pr_ref.shape[0]
    ts = tt // N_SUB_PREP
    n_chunks = ts // CHUNK
    i = pl.program_id(0)
    keep = jnp.where((i * tt) % seq_len == 0, 0.0, 1.0)

    mu = mu_ref[...]
    pvec = pvec_ref[...]
    w0, a0, k_k, k_a, r_k = pvec[0:1], pvec[1:2], pvec[2:3], pvec[3:4], pvec[4:5]
    w2b = w2_ref[...].astype(BF16)
    a2b = a2_ref[...].astype(BF16)
    g2b = g2_ref[...].astype(BF16)
    half = lo_ref.shape[1] // 2

    lane_r = lax.broadcasted_iota(I32, (PAIR, PAIR), 0)
    lane_c = lax.broadcasted_iota(I32, (PAIR, PAIR), 1)
    bd = jnp.where((lane_r // HEAD) == (lane_c // HEAD), 1.0, 0.0).astype(BF16)
    tri_ones = jnp.where(((lane_r // CHUNK) == (lane_c // CHUNK)) & (lane_c <= lane_r),
                         1.0, 0.0).astype(BF16)
    lane = lax.broadcasted_iota(I32, (CHUNK, PAIR), 1)
    is_lo = lane < HEAD

    def stack(x):
        return jnp.concatenate([jnp.where(is_lo, x, 0.0), jnp.where(is_lo, 0.0, x)], axis=0)

    n2 = 2 * PAIR
    ri = lax.broadcasted_iota(I32, (n2, n2), 0)
    ci = lax.broadcasted_iota(I32, (n2, n2), 1)
    same_head = ((ri // CHUNK) % 2) == ((ci // CHUNK) % 2)
    t_i = ri % CHUNK
    s_i = ci % CHUNK
    tri = same_head & (s_i < t_i + ri // PAIR)
    eye = lane_r == lane_c

    for sub in range(N_SUB_PREP):
        r0 = sub * ts
        tile_rows = slice(r0, r0 + ts)

        def shift(ref, pref, cols=slice(None)):
            x = ref[tile_rows, cols]
            if sub == 0:
                first = pref[7:8, cols] * keep
            else:
                first = ref[r0 - 1:r0, cols]
            rowc = lax.broadcasted_iota(I32, x.shape, 0)
            return x, jnp.where(rowc == 0, first, pltpu.roll(x, 1, 0))

        def shift_mix(ref, pref, m):
            x, xs = shift(ref, pref)
            return x + (xs - x) * m

        r = shift_mix(pr_ref, ppr_ref, mu[0:1])
        k = shift_mix(pk_ref, ppk_ref, mu[1:2])
        v = shift_mix(pv_ref, ppv_ref, mu[2:3])

        _, lo_b = shift(lo_ref, plo_ref, slice(half, 2 * half))
        l_all = lo_ref[tile_rows, :half] + lo_b
        l_wa = l_all[:, :PAIR]
        l_g = l_all[:, PAIR:]

        dw = _dot(jnp.tanh(l_wa).astype(BF16), w2b)
        ia = _dot(l_wa.astype(BF16), a2b)
        gate_ref[tile_rows, :] = _dot(jax.nn.sigmoid(l_g).astype(BF16), g2b).astype(BF16)

        zneg = -(w0 + dw)
        softplus = jnp.maximum(zneg, 0.0) + jnp.log(1.0 + jnp.exp(-jnp.abs(zneg)))
        logw = -jnp.exp(-softplus - 0.5)
        iclr = jax.nn.sigmoid(a0 + ia)

        kk = k * k_k
        kk = kk / jnp.maximum(jnp.sqrt(_head_sum(kk * kk, bd)), 1e-12)
        k2 = k * (1.0 + (iclr - 1.0) * k_a)
        a = -kk
        b = kk * iclr
        bonus_ref[tile_rows, :] = (_head_sum(r * k2 * r_k, bd) * v).astype(BF16)

        lw_hi, lw_lo = _split(logw)
        cl = jnp.concatenate(
            [_dot(tri_ones, lw_hi[q * PAIR:(q + 1) * PAIR]) + _dot(tri_ones, lw_lo[q * PAIR:(q + 1) * PAIR])
             for q in range(ts // PAIR)], axis=0)
        at_all = a * jnp.exp(cl - logw)
        rt_all = r * jnp.exp(cl)
        w_inv = jnp.exp(-cl)
        bt_all = b * w_inv
        kt_all = k2 * w_inv

        chunks = range(n_chunks)
        rows = [slice(c * CHUNK, (c + 1) * CHUNK) for c in chunks]
        out_rows = [slice(r0 + c * CHUNK, r0 + (c + 1) * CHUNK) for c in chunks]
        last = [cl[rows[c]][CHUNK - 1:CHUNK, :] for c in chunks]
        w_rem = [jnp.exp(last[c] - cl[rows[c]]) for c in chunks]
        a_s = [stack(at_all[rows[c]]) for c in chunks]
        r_s = [stack(rt_all[rows[c]]) for c in chunks]
        v_s = [stack(v[rows[c]]) for c in chunks]
        v_sb = [v_s[c].astype(BF16) for c in chunks]
        bk_h = [jnp.concatenate([stack(b[rows[c]] * w_rem[c]), stack(k2[rows[c]] * w_rem[c])],
                                axis=0).astype(BF16) for c in chunks]

        a_all = []
        for c in chunks:
            lhs = jnp.concatenate([a_s[c], r_s[c]], axis=0).astype(BF16)
            rhs = jnp.concatenate([stack(bt_all[rows[c]]), stack(kt_all[rows[c]])],
                                  axis=0).astype(BF16)
            a_all.append(jnp.where(tri, _dot_nt(lhs, rhs), 0.0))
        n_k = [a_all[c][:PAIR, :PAIR].astype(BF16) for c in chunks]
        a_r = [a_all[c][PAIR:, :].astype(BF16) for c in chunks]
        x = [jnp.concatenate([a_s[c], _dot(a_all[c][:PAIR, PAIR:].astype(BF16), v_sb[c])], axis=1)
             for c in chunks]
        steps = CHUNK.bit_length() - 1
        for it in range(steps):
            x = [x[c] + _dot(n_k[c], x[c].astype(BF16)) for c in chunks]
            if it + 1 < steps:
                n_k = [_dot(n_k[c], n_k[c]).astype(BF16) for c in chunks]
        zero_b = jnp.zeros((PAIR, PAIR), BF16)
        ry = [_dot(a_r[c], jnp.concatenate(
            [x[c].astype(BF16), jnp.concatenate([zero_b, v_sb[c]], axis=1)], axis=0)) for c in chunks]
        for c in chunks:
            r_c = rt_all[rows[c]]
            rp_ref[out_rows[c], :] = (ry[c][:CHUNK, :PAIR] + ry[c][CHUNK:, :PAIR] + r_c).astype(BF16)
            yq_ref[out_rows[c], :] = (ry[c][:CHUNK, PAIR:] + ry[c][CHUNK:, PAIR:]).astype(BF16)
        p_t = [x[c][:, :PAIR].T.astype(BF16) for c in chunks]
        qv_t = [jnp.concatenate([x[c][:, PAIR:], v_s[c]], axis=0).T.astype(BF16) for c in chunks]
        c0 = sub * n_chunks
        for c in chunks:
            g_ref[c0 + c, 0] = (jnp.where(eye, jnp.exp(last[c]), 0.0)
                                + _dot(p_t[c], bk_h[c][:PAIR]))
        for c in chunks:
            hbd = _dot(qv_t[c], bk_h[c])
            h_ref[c0 + c, 0] = hbd[:HEAD] + hbd[HEAD:]


def _rwkv_prep(proj, lora, mu_rkv, pvec, w2pad, a2pad, g2, seq_len, col0):
    t = proj.shape[0]
    width = mu_rkv.shape[1]
    n_pairs = width // PAIR
    tt = TT_PREP
    nl = lora.shape[1]
    cb = col0 // PAIR
    wb = width // PAIR

    def prev_rows(i):
        return jnp.maximum(i * (tt // 8) - 1, 0)

    def tok(c):
        return pl.BlockSpec((tt, PAIR), lambda i, p: (i, c + p))

    def prev(c):
        return pl.BlockSpec((8, PAIR), lambda i, p: (prev_rows(i), c + p))

    out_tok = pl.BlockSpec((tt, PAIR), lambda i, p: (i, p))
    tok_shape = jax.ShapeDtypeStruct((t, width), BF16)
    return pl.pallas_call(
        functools.partial(_rwkv_prep_kernel, seq_len),
        grid=(t // tt, n_pairs),
        in_specs=[
            tok(cb), tok(cb + wb), tok(cb + 2 * wb),
            prev(cb), prev(cb + wb), prev(cb + 2 * wb),
            pl.BlockSpec((tt, nl), lambda i, p: (i, 0)),
            pl.BlockSpec((8, nl), lambda i, p: (prev_rows(i), 0)),
            pl.BlockSpec((3, PAIR), lambda i, p: (0, p)),
            pl.BlockSpec((8, PAIR), lambda i, p: (0, p)),
            pl.BlockSpec((PAIR, PAIR), lambda i, p: (0, p)),
            pl.BlockSpec((PAIR, PAIR), lambda i, p: (0, p)),
            pl.BlockSpec((PAIR, PAIR), lambda i, p: (0, p)),
        ],
        out_specs=[
            out_tok, out_tok, out_tok, out_tok,
            pl.BlockSpec((tt // CHUNK, 1, PAIR, PAIR), lambda i, p: (i, p, 0, 0)),
            pl.BlockSpec((tt // CHUNK, 1, HEAD, PAIR), lambda i, p: (i, p, 0, 0)),
        ],
        out_shape=[
            tok_shape, tok_shape, tok_shape, tok_shape,
            jax.ShapeDtypeStruct((t // CHUNK, n_pairs, PAIR, PAIR), F32),
            jax.ShapeDtypeStruct((t // CHUNK, n_pairs, HEAD, PAIR), F32),
        ],
        compiler_params=_params(("arbitrary", "arbitrary")),
        name="rwkv_prep",
    )(proj, proj, proj, proj, proj, proj, lora, lora, mu_rkv, pvec, w2pad, a2pad, g2)


def _rwkv_scan_kernel(rp_ref, yq_ref, bonus_ref, gate_ref, g_ref, h_ref, pvec_ref, o_ref, s_scr, y_scr):
    nb, tt, width = rp_ref.shape
    n_pairs = width // PAIR

    @pl.when(pl.program_id(0) == 0)
    def _():
        s_scr[...] = jnp.zeros_like(s_scr)

    lane = lax.broadcasted_iota(I32, (HEAD, PAIR), 1)
    is_lo = lane < HEAD
    lane_r = lax.broadcasted_iota(I32, (PAIR, PAIR), 0)
    lane_c = lax.broadcasted_iota(I32, (PAIR, PAIR), 1)
    bd = jnp.where((lane_r // HEAD) == (lane_c // HEAD), 1.0, 0.0).astype(BF16)

    chains = [(b, p) for b in range(nb) for p in range(n_pairs)]
    state = {ch: s_scr[ch[0], ch[1]] for ch in chains}
    for c in range(tt // CHUNK):
        rows = slice(c * CHUNK, (c + 1) * CHUNK)
        for (b, p) in chains:
            cols = slice(p * PAIR, (p + 1) * PAIR)
            s0 = state[(b, p)]
            s_st = jnp.concatenate([jnp.where(is_lo, s0, 0.0), jnp.where(is_lo, 0.0, s0)], axis=0)
            y_scr[b, rows, cols] = (_dot_nt(rp_ref[b, rows, cols], s_st.astype(BF16))
                                    + yq_ref[b, rows, cols].astype(F32))
        state = {(b, p): _dot3(state[(b, p)], g_ref[b, c, p]) + h_ref[b, c, p] for (b, p) in chains}
    for (b, p) in chains:
        s_scr[b, p] = state[(b, p)]

    pvec = pvec_ref[...]
    for (b, p) in chains:
        cols = slice(p * PAIR, (p + 1) * PAIR)
        y = y_scr[b, :, cols]
        mu = _head_sum(y, bd) * (1.0 / HEAD)
        yc = y - mu
        var = _head_sum(yc * yc, bd) * (1.0 / HEAD)
        yn = yc * lax.rsqrt(var + LNX_EPS) * pvec[5:6, cols] + pvec[6:7, cols]
        o_ref[b, :, cols] = ((yn + bonus_ref[b, :, cols].astype(F32))
                             * gate_ref[b, :, cols].astype(F32))


def _rwkv_scan(rp, yq, bonus, gate, g, h, pvec, batch):
    t, width = rp.shape
    n_pairs = width // PAIR
    seq = t // batch
    tt = TT_RWKV
    nc = tt // CHUNK
    tok = pl.BlockSpec((batch, tt, width), lambda i: (0, i, 0))
    as_seq = lambda z: z.reshape(batch, seq, width)
    out = pl.pallas_call(
        _rwkv_scan_kernel,
        grid=(seq // tt,),
        in_specs=[
            tok, tok, tok, tok,
            pl.BlockSpec((batch, nc, n_pairs, PAIR, PAIR), lambda i: (0, i, 0, 0, 0)),
            pl.BlockSpec((batch, nc, n_pairs, HEAD, PAIR), lambda i: (0, i, 0, 0, 0)),
            pl.BlockSpec((8, width), lambda i: (0, 0)),
        ],
        out_specs=tok,
        out_shape=jax.ShapeDtypeStruct((batch, seq, width), F32),
        scratch_shapes=[pltpu.VMEM((batch, n_pairs, HEAD, PAIR), F32),
                        pltpu.VMEM((batch, tt, width), F32)],
        compiler_params=_params(("arbitrary",)),
        name="rwkv_scan",
    )(as_seq(rp), as_seq(yq), as_seq(bonus), as_seq(gate),
      g.reshape(batch, seq // CHUNK, n_pairs, PAIR, PAIR),
      h.reshape(batch, seq // CHUNK, n_pairs, HEAD, PAIR), pvec)
    return out.reshape(t, width)


def _out_proj_kernel(x_ref, ya_ref, yb_ref, wo_ref, g_ref, rwt_ref, rb_ref, h1_ref, hn_ref, lg_ref):
    wa = ya_ref.shape[1]
    h1 = (x_ref[...] + _dot(ya_ref[...].astype(BF16), wo_ref[0:wa, :])
          + _dot(yb_ref[...].astype(BF16), wo_ref[wa:, :]))
    h1_ref[...] = h1
    hn = h1 * lax.rsqrt(jnp.mean(h1 * h1, axis=-1, keepdims=True) + RMS_EPS) * g_ref[...]
    _store_token_tiles(hn_ref, hn)
    lg_ref[...] = _dot3_nt(rwt_ref[...], hn) + rb_ref[...]


def _out_proj(x2, ya, yb, wo_bf, g2, rwt, rb):
    t, d = x2.shape
    wa = ya.shape[1]
    wb = yb.shape[1]
    ne = rwt.shape[0]
    tm = TM_PROJ
    return pl.pallas_call(
        _out_proj_kernel,
        grid=(t // tm,),
        in_specs=[
            pl.BlockSpec((tm, d), lambda i: (i, 0)),
            pl.BlockSpec((tm, wa), lambda i: (i, 0)),
            pl.BlockSpec((tm, wb), lambda i: (i, 0)),
            pl.BlockSpec((wa + wb, d), lambda i: (0, 0)),
            pl.BlockSpec((1, d), lambda i: (0, 0)),
            pl.BlockSpec((ne, d), lambda i: (0, 0)),
            pl.BlockSpec((ne, 1), lambda i: (0, 0)),
        ],
        out_specs=[
            pl.BlockSpec((tm, d), lambda i: (i, 0)),
            pl.BlockSpec((tm * SUBLANES, LANES), lambda i: (i, 0)),
            pl.BlockSpec((ne, tm), lambda i: (0, i)),
        ],
        out_shape=[
            jax.ShapeDtypeStruct((t, d), F32),
            jax.ShapeDtypeStruct((t * SUBLANES, LANES), F32),
            jax.ShapeDtypeStruct((ne, t), F32),
        ],
        compiler_params=_params(("arbitrary",)),
        name="out_proj",
    )(x2, ya, yb, wo_bf, g2, rwt, rb)


def _route_kernel(lg_ref, pos_ref, gate_ref, pend_ref, carry_scr, pstart_scr):
    phase = pl.program_id(0)
    first = pl.program_id(1) == 0

    @pl.when(first & (phase == 0))
    def _():
        carry_scr[...] = jnp.zeros_like(carry_scr)
        pstart_scr[...] = jnp.zeros_like(pstart_scr)
        pend_ref[...] = jnp.zeros_like(pend_ref)

    @pl.when(first & (phase == 1))
    def _():
        counts = carry_scr[...]
        padded = jnp.ceil(counts * (1.0 / BM_MOE)) * BM_MOE
        row = lax.broadcasted_iota(I32, counts.shape, 0)
        end = padded
        s = 1
        while s < counts.shape[0]:
            end = end + jnp.where(row >= s, pltpu.roll(end, s, 0), 0.0)
            s *= 2
        pstart_scr[...] = end - padded
        pend_ref[...] = end.astype(I32)
        carry_scr[...] = jnp.zeros_like(carry_scr)

    l = lg_ref[...]
    ne, tr = l.shape
    e_iota = lax.broadcasted_iota(I32, (ne, tr), 0)
    chosen = jnp.zeros((ne, tr), F32)
    vals, sels = [], []
    for j in range(TOP_K):
        m = jnp.max(l, axis=0, keepdims=True)
        idx = jnp.min(jnp.where(l == m, e_iota, ne), axis=0, keepdims=True)
        sel = e_iota == idx
        vals.append(m)
        sels.append(sel)
        chosen = jnp.where(sel, 1.0, chosen)
        l = jnp.where(sel, -jnp.inf, l)
    ex = [jnp.exp(vj - vals[0]) for vj in vals]
    den = ex[0] + ex[1] + ex[2] + ex[3]
    for j in range(TOP_K):
        gate_ref[j:j + 1, :] = ex[j] / den

    ti = lax.broadcasted_iota(I32, (tr, tr), 0)
    tj = lax.broadcasted_iota(I32, (tr, tr), 1)
    upper = jnp.where(ti <= tj, 1.0, 0.0).astype(BF16)
    inc = _dot(chosen.astype(BF16), upper)
    carry = carry_scr[...]
    row_of = inc - chosen + carry[:, 0:1] + pstart_scr[:, 0:1]
    for j in range(TOP_K):
        pj = jnp.sum(jnp.where(sels[j], row_of, 0.0), axis=0, keepdims=True)
        pos_ref[j:j + 1, :] = pj.astype(I32)
    carry_scr[...] = carry + inc[:, tr - 1:tr]


def _route(logits_t):
    ne, t = logits_t.shape
    tr = TR_ROUTE
    tok = pl.BlockSpec((TOP_K, tr), lambda ph, i: (0, i * ph))
    return pl.pallas_call(
        _route_kernel,
        grid=(2, t // tr),
        in_specs=[pl.BlockSpec((ne, tr), lambda ph, i: (0, i))],
        out_specs=[tok, tok, pl.BlockSpec((ne, 128), lambda ph, i: (0, 0))],
        out_shape=[
            jax.ShapeDtypeStruct((TOP_K, t), I32),
            jax.ShapeDtypeStruct((TOP_K, t), F32),
            jax.ShapeDtypeStruct((ne, 128), I32),
        ],
        scratch_shapes=[pltpu.VMEM((ne, 128), F32), pltpu.VMEM((ne, 128), F32)],
        compiler_params=_params(("arbitrary", "arbitrary")),
        name="route",
    )(logits_t)


def _dispatch_kernel(pend_ref, nused_ref, pos_ref, hn_ref, xin_ref, zero_scr, sem):
    bm = zero_scr.shape[0] // SUBLANES
    td = pos_ref.shape[0] // TOP_K
    i = pl.program_id(0)

    def tile(ref, token):
        return ref.at[pl.ds(pl.multiple_of(token * SUBLANES, SUBLANES), SUBLANES)]

    def zero_block(row0):
        start = pl.multiple_of(row0 * SUBLANES, bm * SUBLANES)
        return pltpu.make_async_copy(zero_scr, xin_ref.at[pl.ds(start, bm * SUBLANES)], sem)

    @pl.when(i == 0)
    def _():
        zero_scr[...] = jnp.zeros_like(zero_scr)

        def has_rows(e):
            end = pend_ref[e]
            return jnp.where(e == 0, end >= bm, end > pend_ref[jnp.maximum(e - 1, 0)])

        def start_one(e, carry):
            @pl.when(has_rows(e))
            def _():
                zero_block(pend_ref[e] - bm).start()
            return carry

        def wait_one(e, carry):
            @pl.when(has_rows(e))
            def _():
                zero_block(0).wait()
            return carry

        def start_tail(blk, carry):
            zero_block(blk * bm).start()
            return carry

        def wait_tail(blk, carry):
            zero_block(0).wait()
            return carry

        n_blocks = xin_ref.shape[0] // (bm * SUBLANES)
        lax.fori_loop(0, pend_ref.shape[0], start_one, 0)
        lax.fori_loop(nused_ref[0], n_blocks, start_tail, 0)
        lax.fori_loop(0, pend_ref.shape[0], wait_one, 0)
        lax.fori_loop(nused_ref[0], n_blocks, wait_tail, 0)

    def issue(tk, carry):
        for j in range(TOP_K):
            pltpu.make_async_copy(tile(hn_ref, tk), tile(xin_ref, pos_ref[TOP_K * tk + j]),
                                  sem).start(priority=j % 2)
        return carry

    lax.fori_loop(0, td, issue, 0, unroll=8)

    for j in range(TOP_K):
        pltpu.make_async_copy(hn_ref, xin_ref.at[pl.ds(0, td * SUBLANES)], sem).wait()


def _dispatch(pad_end, n_used, pos, hn_tiles, n_rows):
    t = hn_tiles.shape[0] // SUBLANES
    td = TD_DISPATCH
    grid_spec = pltpu.PrefetchScalarGridSpec(
        num_scalar_prefetch=2,
        grid=(t // td,),
        in_specs=[
            pl.BlockSpec((TOP_K * td,), lambda i, pe, nu: (i,), memory_space=pltpu.SMEM),
            pl.BlockSpec((td * SUBLANES, LANES), lambda i, pe, nu: (i, 0)),
        ],
        out_specs=pl.BlockSpec(memory_space=pl.ANY),
        scratch_shapes=[pltpu.VMEM((BM_MOE * SUBLANES, LANES), F32), pltpu.SemaphoreType.DMA(())],
    )
    return pl.pallas_call(
        _dispatch_kernel,
        grid_spec=grid_spec,
        out_shape=jax.ShapeDtypeStruct((n_rows * SUBLANES, LANES), F32),
        compiler_params=_params(("arbitrary",)),
        name="dispatch",
    )(pad_end, n_used, pos, hn_tiles)


def _experts_kernel(pend_ref, nused_ref, w1_ref, b1_ref, w2_ref, b2_ref, xin_ref, yout_ref,
                    xbuf, obuf, wt_scr, wg_scr, wl_scr, w2_scr, sem_in, sem_out):
    e = pl.program_id(0)
    rows_blk = xbuf.shape[1]
    bm = rows_blk // SUBLANES
    dh = w2_ref.shape[1]
    n_used = nused_ref[0]
    blk_lo = jnp.where(e == 0, 0, pend_ref[jnp.maximum(e - 1, 0)]) // bm
    blk_hi = pend_ref[e] // bm

    def block(ref, blk):
        return ref.at[pl.ds(pl.multiple_of(blk * rows_blk, rows_blk), rows_blk)]

    def x_copy(blk, slot):
        return pltpu.make_async_copy(block(xin_ref, blk), xbuf.at[slot], sem_in.at[slot])

    def o_copy(blk, slot):
        return pltpu.make_async_copy(obuf.at[slot], block(yout_ref, blk), sem_out.at[slot])

    @pl.when((e == 0) & (n_used > 0))
    def _():
        x_copy(0, 0).start()

    @pl.when(blk_hi > blk_lo)
    def _():
        n_slab, n_t, lanes = wt_scr.shape
        for part in range(2 * dh // n_t):
            rows = slice(part * (n_t // 2), (part + 1) * (n_t // 2))
            for s in range(n_slab):
                cols = slice(s * lanes, (s + 1) * lanes)
                wt_scr[s] = w1_ref[0, cols, part * n_t:(part + 1) * n_t].T
                wg_scr[rows, cols] = wt_scr[s, pl.ds(0, n_t // 2, stride=2), :].astype(BF16)
                wl_scr[rows, cols] = wt_scr[s, pl.ds(1, n_t // 2, stride=2), :].astype(BF16)
        w2_scr[...] = w2_ref[0].astype(BF16)

        def one_block(blk, carry):
            slot = blk % 2

            @pl.when(blk + 1 < n_used)
            def _():
                x_copy(blk + 1, 1 - slot).start()

            x_copy(blk, slot).wait()

            @pl.when(blk >= 2)
            def _():
                o_copy(blk - 2, slot).wait()

            xb = _load_token_tiles(xbuf.at[slot], 0, bm).astype(BF16)
            bias = b1_ref[0]
            glu = jnp.minimum(_dot_nt(xb, wg_scr[...]) + bias[:, :dh], SWIGLU_LIMIT)
            lin = jnp.clip(_dot_nt(xb, wl_scr[...]) + bias[:, dh:], -SWIGLU_LIMIT, SWIGLU_LIMIT)
            act = glu * jax.nn.sigmoid(SWIGLU_ALPHA * glu) * (lin + 1.0)
            _store_token_tiles(obuf.at[slot], _dot(act.astype(BF16), w2_scr[...]) + b2_ref[0])
            o_copy(blk, slot).start()
            return carry

        lax.fori_loop(blk_lo, blk_hi, one_block, 0)

    @pl.when(e == pl.num_programs(0) - 1)
    def _():
        @pl.when(n_used >= 1)
        def _():
            o_copy(0, (n_used - 1) % 2).wait()

        @pl.when(n_used >= 2)
        def _():
            o_copy(0, n_used % 2).wait()

        obuf[0] = jnp.zeros(obuf.shape[1:], obuf.dtype)
        n_blocks = yout_ref.shape[0] // rows_blk

        def start_tail(blk, carry):
            o_copy(blk, 0).start()
            return carry

        def wait_tail(blk, carry):
            o_copy(0, 0).wait()
            return carry

        lax.fori_loop(n_used, n_blocks, start_tail, 0)
        lax.fori_loop(n_used, n_blocks, wait_tail, 0)


def _experts(pad_end, n_used, xin, w1, b1p, w2, b2):
    ne, d, dh2 = w1.shape
    dh = dh2 // 2
    rows_blk = BM_MOE * SUBLANES
    grid_spec = pltpu.PrefetchScalarGridSpec(
        num_scalar_prefetch=2,
        grid=(ne,),
        in_specs=[
            pl.BlockSpec((1, d, dh2), lambda e, pe, nu: (e, 0, 0)),
            pl.BlockSpec((1, 1, dh2), lambda e, pe, nu: (e, 0, 0)),
            pl.BlockSpec((1, dh, d), lambda e, pe, nu: (e, 0, 0)),
            pl.BlockSpec((1, 1, d), lambda e, pe, nu: (e, 0, 0)),
            pl.BlockSpec(memory_space=pl.ANY),
        ],
        out_specs=pl.BlockSpec(memory_space=pl.ANY),
        scratch_shapes=[
            pltpu.VMEM((2, rows_blk, LANES), F32),
            pltpu.VMEM((2, rows_blk, LANES), F32),
            pltpu.VMEM((d // 128, dh, 128), F32),
            pltpu.VMEM((dh, d), BF16),
            pltpu.VMEM((dh, d), BF16),
            pltpu.VMEM((dh, d), BF16),
            pltpu.SemaphoreType.DMA((2,)),
            pltpu.SemaphoreType.DMA((2,)),
        ],
    )
    return pl.pallas_call(
        _experts_kernel,
        grid_spec=grid_spec,
        out_shape=jax.ShapeDtypeStruct(xin.shape, F32),
        compiler_params=_params(("arbitrary",)),
        name="experts",
    )(pad_end, n_used, w1, b1p, w2, b2, xin)


def _combine_kernel(pos_ref, pos_next_ref, gate_ref, h1_ref, fg_ref, yout_ref, o_ref, ybuf, sem):
    tc = h1_ref.shape[0]
    i = pl.program_id(0)
    slot = i % 2

    def tile(ref, token):
        return ref.at[pl.ds(pl.multiple_of(token * SUBLANES, SUBLANES), SUBLANES)]

    def gather(p_ref, s):
        def issue(tk, carry):
            for j in range(TOP_K):
                pltpu.make_async_copy(tile(yout_ref, p_ref[TOP_K * tk + j]), tile(ybuf.at[s], j * tc + tk),
                                      sem.at[s]).start(priority=j % 2)
            return carry
        lax.fori_loop(0, tc, issue, 0, unroll=8)

    @pl.when(i == 0)
    def _():
        gather(pos_ref, 0)

    @pl.when(i + 1 < pl.num_programs(0))
    def _():
        gather(pos_next_ref, 1 - slot)

    pltpu.make_async_copy(yout_ref.at[pl.ds(0, ybuf.shape[1])], ybuf.at[slot], sem.at[slot]).wait()

    gates = gate_ref[...]
    h = h1_ref[...]
    for j in range(TOP_K):
        h = h + _load_token_tiles(ybuf.at[slot], j * tc * SUBLANES, tc) * gates[:, j:j + 1]
    o_ref[...] = h * lax.rsqrt(jnp.mean(h * h, axis=-1, keepdims=True) + RMS_EPS) * fg_ref[...]


def _combine(pos, gates_t, h1, fg, yout):
    t, d = h1.shape
    tc = TC_COMBINE
    n_steps = t // tc
    return pl.pallas_call(
        _combine_kernel,
        grid=(n_steps,),
        in_specs=[
            pl.BlockSpec((TOP_K * tc,), lambda i: (i,), memory_space=pltpu.SMEM),
            pl.BlockSpec((TOP_K * tc,), lambda i: (jnp.minimum(i + 1, n_steps - 1),),
                         memory_space=pltpu.SMEM),
            pl.BlockSpec((tc, TOP_K), lambda i: (i, 0)),
            pl.BlockSpec((tc, d), lambda i: (i, 0)),
            pl.BlockSpec((1, d), lambda i: (0, 0)),
            pl.BlockSpec(memory_space=pl.ANY),
        ],
        out_specs=pl.BlockSpec((tc, d), lambda i: (i, 0)),
        out_shape=jax.ShapeDtypeStruct((t, d), F32),
        scratch_shapes=[pltpu.VMEM((2, TOP_K * tc * SUBLANES, LANES), F32),
                        pltpu.SemaphoreType.DMA((2,))],
        compiler_params=_params(("arbitrary",)),
        name="combine",
    )(pos, pos, gates_t, h1, fg, yout)


def kernel(x, norm1_g, w_in, sgu_ln_g, sgu_ln_b, sgu_w, sgu_b, mu_rkv, mu_wag, decay_w0, decay_w1,
           decay_w2, iclr_a0, iclr_a1, iclr_a2, gate_g1, gate_g2, k_k, k_a, r_k, lnx_g, lnx_b, w_out,
           norm2_g, router_w, router_b, moe_w1, moe_b1, moe_w2, moe_b2, final_g):
    batch, seq, d = x.shape
    t = batch * seq
    depth = w_in.shape[0]
    sgu_width = sgu_ln_g.shape[1]
    rw = mu_rkv.shape[2]
    n_dec, n_icl, n_gate = decay_w1.shape[2], iclr_a1.shape[2], gate_g1.shape[2]
    assert n_dec == HEAD and n_icl == HEAD and n_gate == PAIR and rw % PAIR == 0
    assert seq % TT_RWKV == 0 and seq % TT_PREP == 0 and t % TM_PROJ == 0
    assert sgu_w.shape[2] == SGU_BLOCK
    assert depth == 1, "the final RMSNorm is fused into the last layer's combine kernel"
    assert d == SUBLANES * LANES, "the MoE row movers copy one (8, 128) f32 tile per token"

    h = x.reshape(t, d)
    for l in range(depth):
        win_bf = w_in[l].astype(BF16)
        wl = jnp.concatenate([decay_w1[l], iclr_a1[l], gate_g1[l]], axis=1)
        mucat = jnp.concatenate([
            jnp.broadcast_to(mu_wag[l, 0][:, None], (d, n_dec)),
            jnp.broadcast_to(mu_wag[l, 1][:, None], (d, n_icl)),
            jnp.broadcast_to(mu_wag[l, 2][:, None], (d, n_gate))], axis=1)
        zeros = jnp.zeros((HEAD, rw), F32)
        w2pad = jnp.concatenate([decay_w2[l], zeros], axis=0)
        a2pad = jnp.concatenate([zeros, iclr_a2[l]], axis=0)
        pvec = jnp.stack([decay_w0[l], iclr_a0[l], k_k[l], k_a[l], r_k[l].reshape(-1),
                          lnx_g[l], lnx_b[l], jnp.zeros((rw,), F32)], axis=0)
        bias2d = jnp.repeat(sgu_b[l].T, sgu_width // sgu_b.shape[1], axis=1)
        wo_bf = w_out[l].astype(BF16)
        b1p = jnp.concatenate([moe_b1[l][:, 0::2], moe_b1[l][:, 1::2]], axis=-1)[:, None, :]
        b2 = moe_b2[l][:, None, :]

        assert w_in.shape[2] == 2 * sgu_width + 3 * rw
        proj, lora, ya = _in_proj(h, norm1_g[l][None, :], win_bf, wl, mucat, sgu_ln_g[l][None, :],
                                  sgu_ln_b[l][None, :], sgu_w[l], bias2d)
        rp, yq, bonus, gate, g_m, h_m = _rwkv_prep(proj, lora, mu_rkv[l], pvec, w2pad, a2pad,
                                                   gate_g2[l], seq, 0)
        yb = _rwkv_scan(rp, yq, bonus, gate, g_m, h_m, pvec, batch)
        h1, hn2, logits_t = _out_proj(h, ya, yb, wo_bf, norm2_g[l][None, :], router_w[l].T,
                                      router_b[l][:, None])
        pos, gates, pend = _route(logits_t)

        pad_end = pend[:, 0]
        n_blocks = (t * TOP_K) // BM_MOE + N_EXPERTS
        n_used = pad_end[-1:] // BM_MOE

        pos = pos.T.reshape(-1)
        xin = _dispatch(pad_end, n_used, pos, hn2, n_blocks * BM_MOE)
        yout = _experts(pad_end, n_used, xin, moe_w1[l], b1p, moe_w2[l], b2)
        h = _combine(pos, gates.T, h1, final_g[None, :], yout)
    return h.reshape(batch, seq, d)
```

```python
import functools

import jax
import jax.numpy as jnp
from jax import lax
from jax.experimental import pallas as pl
from jax.experimental.pallas import tpu as pltpu

F32 = jnp.float32
BF16 = jnp.bfloat16
I32 = jnp.int32

RMS_EPS = 1e-5
LN_EPS = 1e-5
LNX_EPS = 64e-5
CHUNK = 64
SGU_BLOCK = 128
HEAD = 64
PAIR = 2 * HEAD
N_EXPERTS = 32
TOP_K = 4
SWIGLU_ALPHA = 1.702
SWIGLU_LIMIT = 7.0

V7X_VMEM_LIMIT = 56 * 1024 * 1024

TM_PROJ = 1024
TT_PREP = 1024
N_SUB_PREP = 1
TT_RWKV = 512
TR_ROUTE = 1024
BM_MOE = 256
TD_DISPATCH = 1024
TC_COMBINE = 512
WIN_COMBINE = 32


def _dot(a, b):
    return jnp.dot(a, b, preferred_element_type=F32)


def _dot_nt(a, b):
    return lax.dot_general(a, b, (((1,), (1,)), ((), ())), preferred_element_type=F32)


def _split(x):
    hi = x.astype(BF16)
    lo = (x - hi.astype(F32)).astype(BF16)
    return hi, lo


def _dot3(a, b):
    ah, al = _split(a)
    bh, bl = _split(b)
    return _dot(ah, bh) + _dot(al, bh) + _dot(ah, bl)


def _dot3_nt(a, b):
    ah, al = _split(a)
    bh, bl = _split(b)
    return _dot_nt(ah, bh) + _dot_nt(al, bh) + _dot_nt(ah, bl)


def _head_sum(x, bd):
    hi, lo = _split(x)
    return _dot(hi, bd) + _dot(lo, bd)


SUBLANES = 8
LANES = 128


def _store_token_tiles(ref, x):
    n = x.shape[0]
    for s in range(SUBLANES):
        ref[pl.ds(s, n, stride=SUBLANES), :] = x[:, s * LANES:(s + 1) * LANES]


def _load_token_tiles(ref, row0, n):
    return jnp.concatenate(
        [ref[pl.ds(row0 + s, n, stride=SUBLANES), :] for s in range(SUBLANES)], axis=1)


def _params(sem, vmem=V7X_VMEM_LIMIT):
    return pltpu.CompilerParams(dimension_semantics=sem, vmem_limit_bytes=vmem)


def _in_proj_kernel(x_ref, g_ref, win_ref, wl_ref, mucat_ref, lng_ref, lnb_ref, w_ref, bias_ref,
                    proj_ref, lora_ref, o_ref, wl_scr, wm_scr):
    n_heads = w_ref.shape[0]

    @pl.when(pl.program_id(0) == 0)
    def _():
        wl = wl_ref[...]
        mu = mucat_ref[...]
        n = wl.shape[1]
        wl_scr[:, 0:n] = (wl * (1.0 - mu)).astype(BF16)
        wl_scr[:, n:2 * n] = (wl * mu).astype(BF16)
        qi = lax.broadcasted_iota(I32, (SGU_BLOCK, SGU_BLOCK), 0) // CHUNK
        kj = lax.broadcasted_iota(I32, (SGU_BLOCK, SGU_BLOCK), 1) // CHUNK
        for h in range(n_heads):
            wm_scr[h] = jnp.where(kj <= qi, w_ref[h], 0.0).astype(BF16)

    x = x_ref[...]
    hn = x * lax.rsqrt(jnp.mean(x * x, axis=-1, keepdims=True) + RMS_EPS) * g_ref[...]
    hb = hn.astype(BF16)
    width = lng_ref.shape[1]
    two_w = 2 * width
    proj_ref[...] = _dot(hb, win_ref[:, two_w:])
    lora_ref[...] = _dot(hb, wl_scr[...])

    z = _dot(hb, win_ref[:, :two_w])
    tm = z.shape[0]
    gz = 0.5 * z * (1.0 + lax.erf(z * (2.0 ** -0.5)))
    u = gz[:, :width]
    v = gz[:, width:]
    mu = jnp.mean(v, axis=-1, keepdims=True)
    vc = v - mu
    var = jnp.mean(vc * vc, axis=-1, keepdims=True)
    vn = vc * lax.rsqrt(var + LN_EPS) * lng_ref[...] + lnb_ref[...]
    lane = lax.broadcasted_iota(I32, (SGU_BLOCK, PAIR), 1)
    is_lo = lane < HEAD
    bias = bias_ref[...]
    for blk in range(tm // SGU_BLOCK):
        rows = slice(blk * SGU_BLOCK, (blk + 1) * SGU_BLOCK)
        for p in range(width // PAIR):
            cols = slice(p * PAIR, (p + 1) * PAIR)
            vp = vn[rows, cols]
            lo = jnp.where(is_lo, vp, 0.0).astype(BF16)
            hi = jnp.where(is_lo, 0.0, vp).astype(BF16)
            sv = _dot(wm_scr[2 * p], lo) + _dot(wm_scr[2 * p + 1], hi)
            o_ref[rows, cols] = u[rows, cols] * (sv + bias[:, cols])


def _in_proj(x2, g, win_bf, wl, mucat, ln_g, ln_b, w_s, bias2d):
    t, d = x2.shape
    width = ln_g.shape[1]
    n_rkv = win_bf.shape[1] - 2 * width
    n_l = wl.shape[1]
    n_heads = w_s.shape[0]
    tm = TM_PROJ
    const = lambda *shape: pl.BlockSpec(shape, lambda i: (0,) * len(shape))
    row = lambda n: pl.BlockSpec((tm, n), lambda i: (i, 0))
    return pl.pallas_call(
        _in_proj_kernel,
        grid=(t // tm,),
        in_specs=[
            row(d), const(1, d), const(d, win_bf.shape[1]), const(d, n_l), const(d, n_l),
            const(1, width), const(1, width), const(n_heads, SGU_BLOCK, SGU_BLOCK),
            const(SGU_BLOCK, width),
        ],
        out_specs=[row(n_rkv), row(2 * n_l), row(width)],
        out_shape=[
            jax.ShapeDtypeStruct((t, n_rkv), F32),
            jax.ShapeDtypeStruct((t, 2 * n_l), F32),
            jax.ShapeDtypeStruct((t, width), F32),
        ],
        scratch_shapes=[pltpu.VMEM((d, 2 * n_l), BF16),
                        pltpu.VMEM((n_heads, SGU_BLOCK, SGU_BLOCK), BF16)],
        compiler_params=_params(("arbitrary",)),
        name="in_proj_sgu",
    )(x2, g, win_bf, wl, mucat, ln_g, ln_b, w_s, bias2d)


def _rwkv_prep_kernel(seq_len, pr_ref, pk_ref, pv_ref, ppr_ref, ppk_ref, ppv_ref, lo_ref, plo_ref,
                      mu_ref, pvec_ref, w2_ref, a2_ref, g2_ref,
                      rp_ref, yq_ref, bonus_ref, gate_ref, g_ref, h_ref):
    tt = pr_ref.shape[0]
    ts = tt // N_SUB_PREP
    n_chunks = ts // CHUNK
    i = pl.program_id(0)
    keep = jnp.where((i * tt) % seq_len == 0, 0.0, 1.0)

    mu = mu_ref[...]
    pvec = pvec_ref[...]
    w0, a0, k_k, k_a, r_k = pvec[0:1], pvec[1:2], pvec[2:3], pvec[3:4], pvec[4:5]
    w2b = w2_ref[...].astype(BF16)
    a2b = a2_ref[...].astype(BF16)
    g2b = g2_ref[...].astype(BF16)
    half = lo_ref.shape[1] // 2

    lane_r = lax.broadcasted_iota(I32, (PAIR, PAIR), 0)
    lane_c = lax.broadcasted_iota(I32, (PAIR, PAIR), 1)
    bd = jnp.where((lane_r // HEAD) == (lane_c // HEAD), 1.0, 0.0).astype(BF16)
    tri_ones = jnp.where(((lane_r // CHUNK) == (lane_c // CHUNK)) & (lane_c <= lane_r),
                         1.0, 0.0).astype(BF16)
    lane = lax.broadcasted_iota(I32, (CHUNK, PAIR), 1)
    is_lo = lane < HEAD

    def stack(x):
        return jnp.concatenate([jnp.where(is_lo, x, 0.0), jnp.where(is_lo, 0.0, x)], axis=0)

    n2 = 2 * PAIR
    ri = lax.broadcasted_iota(I32, (n2, n2), 0)
    ci = lax.broadcasted_iota(I32, (n2, n2), 1)
    same_head = ((ri // CHUNK) % 2) == ((ci // CHUNK) % 2)
    t_i = ri % CHUNK
    s_i = ci % CHUNK
    tri = same_head & (s_i < t_i + ri // PAIR)
    eye = lane_r == lane_c

    for sub in range(N_SUB_PREP):
        r0 = sub * ts
        tile_rows = slice(r0, r0 + ts)

        def shift(ref, pref, cols=slice(None)):
            x = ref[tile_rows, cols]
            if sub == 0:
                first = pref[7:8, cols] * keep
            else:
                first = ref[r0 - 1:r0, cols]
            rowc = lax.broadcasted_iota(I32, x.shape, 0)
            return x, jnp.where(rowc == 0, first, pltpu.roll(x, 1, 0))

        def shift_mix(ref, pref, m):
            x, xs = shift(ref, pref)
            return x + (xs - x) * m

        r = shift_mix(pr_ref, ppr_ref, mu[0:1])
        k = shift_mix(pk_ref, ppk_ref, mu[1:2])
        v = shift_mix(pv_ref, ppv_ref, mu[2:3])

        _, lo_b = shift(lo_ref, plo_ref, slice(half, 2 * half))
        l_all = lo_ref[tile_rows, :half] + lo_b
        l_wa = l_all[:, :PAIR]
        l_g = l_all[:, PAIR:]

        dw = _dot(jnp.tanh(l_wa).astype(BF16), w2b)
        ia = _dot(l_wa.astype(BF16), a2b)
        gate_ref[tile_rows, :] = _dot(jax.nn.sigmoid(l_g).astype(BF16), g2b).astype(BF16)

        zneg = -(w0 + dw)
        softplus = jnp.maximum(zneg, 0.0) + jnp.log(1.0 + jnp.exp(-jnp.abs(zneg)))
        logw = -jnp.exp(-softplus - 0.5)
        iclr = jax.nn.sigmoid(a0 + ia)

        kk = k * k_k
        kk = kk / jnp.maximum(jnp.sqrt(_head_sum(kk * kk, bd)), 1e-12)
        k2 = k * (1.0 + (iclr - 1.0) * k_a)
        a = -kk
        b = kk * iclr
        bonus_ref[tile_rows, :] = (_head_sum(r * k2 * r_k, bd) * v).astype(BF16)

        lw_hi, lw_lo = _split(logw)
        cl = jnp.concatenate(
            [_dot(tri_ones, lw_hi[q * PAIR:(q + 1) * PAIR]) + _dot(tri_ones, lw_lo[q * PAIR:(q + 1) * PAIR])
             for q in range(ts // PAIR)], axis=0)
        at_all = a * jnp.exp(cl - logw)
        rt_all = r * jnp.exp(cl)
        w_inv = jnp.exp(-cl)
        bt_all = b * w_inv
        kt_all = k2 * w_inv

        chunks = range(n_chunks)
        rows = [slice(c * CHUNK, (c + 1) * CHUNK) for c in chunks]
        out_rows = [slice(r0 + c * CHUNK, r0 + (c + 1) * CHUNK) for c in chunks]
        last = [cl[rows[c]][CHUNK - 1:CHUNK, :] for c in chunks]
        w_rem = [jnp.exp(last[c] - cl[rows[c]]) for c in chunks]
        a_s = [stack(at_all[rows[c]]) for c in chunks]
        r_s = [stack(rt_all[rows[c]]) for c in chunks]
        v_s = [stack(v[rows[c]]) for c in chunks]
        v_sb = [v_s[c].astype(BF16) for c in chunks]
        bk_h = [jnp.concatenate([stack(b[rows[c]] * w_rem[c]), stack(k2[rows[c]] * w_rem[c])],
                                axis=0).astype(BF16) for c in chunks]

        a_all = []
        for c in chunks:
            lhs = jnp.concatenate([a_s[c], r_s[c]], axis=0).astype(BF16)
            rhs = jnp.concatenate([stack(bt_all[rows[c]]), stack(kt_all[rows[c]])],
                                  axis=0).astype(BF16)
            a_all.append(jnp.where(tri, _dot_nt(lhs, rhs), 0.0))
        n_k = [a_all[c][:PAIR, :PAIR].astype(BF16) for c in chunks]
        a_r = [a_all[c][PAIR:, :].astype(BF16) for c in chunks]
        x = [jnp.concatenate([a_s[c], _dot(a_all[c][:PAIR, PAIR:].astype(BF16), v_sb[c])], axis=1)
             for c in chunks]
        steps = CHUNK.bit_length() - 1
        for it in range(steps):
            x = [x[c] + _dot(n_k[c], x[c].astype(BF16)) for c in chunks]
            if it + 1 < steps:
                n_k = [_dot(n_k[c], n_k[c]).astype(BF16) for c in chunks]
        zero_b = jnp.zeros((PAIR, PAIR), BF16)
        ry = [_dot(a_r[c], jnp.concatenate(
            [x[c].astype(BF16), jnp.concatenate([zero_b, v_sb[c]], axis=1)], axis=0)) for c in chunks]
        for c in chunks:
            r_c = rt_all[rows[c]]
            rp_ref[out_rows[c], :] = (ry[c][:CHUNK, :PAIR] + ry[c][CHUNK:, :PAIR] + r_c).astype(BF16)
            yq_ref[out_rows[c], :] = (ry[c][:CHUNK, PAIR:] + ry[c][CHUNK:, PAIR:]).astype(BF16)
        p_t = [x[c][:, :PAIR].T.astype(BF16) for c in chunks]
        qv_t = [jnp.concatenate([x[c][:, PAIR:], v_s[c]], axis=0).T.astype(BF16) for c in chunks]
        c0 = sub * n_chunks
        for c in chunks:
            g_ref[c0 + c, 0] = (jnp.where(eye, jnp.exp(last[c]), 0.0)
                                + _dot(p_t[c], bk_h[c][:PAIR]))
        for c in chunks:
            hbd = _dot(qv_t[c], bk_h[c])
            h_ref[c0 + c, 0] = hbd[:HEAD] + hbd[HEAD:]


def _rwkv_prep(proj, lora, mu_rkv, pvec, w2pad, a2pad, g2, seq_len, col0):
    t = proj.shape[0]
    width = mu_rkv.shape[1]
    n_pairs = width // PAIR
    tt = TT_PREP
    nl = lora.shape[1]
    cb = col0 // PAIR
    wb = width // PAIR

    def prev_rows(i):
        return jnp.maximum(i * (tt // 8) - 1, 0)

    def tok(c):
        return pl.BlockSpec((tt, PAIR), lambda i, p: (i, c + p))

    def prev(c):
        return pl.BlockSpec((8, PAIR), lambda i, p: (prev_rows(i), c + p))

    out_tok = pl.BlockSpec((tt, PAIR), lambda i, p: (i, p))
    tok_shape = jax.ShapeDtypeStruct((t, width), BF16)
    return pl.pallas_call(
        functools.partial(_rwkv_prep_kernel, seq_len),
        grid=(t // tt, n_pairs),
        in_specs=[
            tok(cb), tok(cb + wb), tok(cb + 2 * wb),
            prev(cb), prev(cb + wb), prev(cb + 2 * wb),
            pl.BlockSpec((tt, nl), lambda i, p: (i, 0)),
            pl.BlockSpec((8, nl), lambda i, p: (prev_rows(i), 0)),
            pl.BlockSpec((3, PAIR), lambda i, p: (0, p)),
            pl.BlockSpec((8, PAIR), lambda i, p: (0, p)),
            pl.BlockSpec((PAIR, PAIR), lambda i, p: (0, p)),
            pl.BlockSpec((PAIR, PAIR), lambda i, p: (0, p)),
            pl.BlockSpec((PAIR, PAIR), lambda i, p: (0, p)),
        ],
        out_specs=[
            out_tok, out_tok, out_tok, out_tok,
            pl.BlockSpec((tt // CHUNK, 1, PAIR, PAIR), lambda i, p: (i, p, 0, 0)),
            pl.BlockSpec((tt // CHUNK, 1, HEAD, PAIR), lambda i, p: (i, p, 0, 0)),
        ],
        out_shape=[
            tok_shape, tok_shape, tok_shape, tok_shape,
            jax.ShapeDtypeStruct((t // CHUNK, n_pairs, PAIR, PAIR), F32),
            jax.ShapeDtypeStruct((t // CHUNK, n_pairs, HEAD, PAIR), F32),
        ],
        compiler_params=_params(("arbitrary", "arbitrary")),
        name="rwkv_prep",
    )(proj, proj, proj, proj, proj, proj, lora, lora, mu_rkv, pvec, w2pad, a2pad, g2)


def _rwkv_scan_kernel(rp_ref, yq_ref, bonus_ref, gate_ref, g_ref, h_ref, pvec_ref, o_ref, s_scr, y_scr):
    nb, tt, width = rp_ref.shape
    n_pairs = width // PAIR

    @pl.when(pl.program_id(0) == 0)
    def _():
        s_scr[...] = jnp.zeros_like(s_scr)

    lane = lax.broadcasted_iota(I32, (HEAD, PAIR), 1)
    is_lo = lane < HEAD
    lane_r = lax.broadcasted_iota(I32, (PAIR, PAIR), 0)
    lane_c = lax.broadcasted_iota(I32, (PAIR, PAIR), 1)
    bd = jnp.where((lane_r // HEAD) == (lane_c // HEAD), 1.0, 0.0).astype(BF16)

    chains = [(b, p) for b in range(nb) for p in range(n_pairs)]
    state = {ch: s_scr[ch[0], ch[1]] for ch in chains}
    for c in range(tt // CHUNK):
        rows = slice(c * CHUNK, (c + 1) * CHUNK)
        for (b, p) in chains:
            cols = slice(p * PAIR, (p + 1) * PAIR)
            s0 = state[(b, p)]
            s_st = jnp.concatenate([jnp.where(is_lo, s0, 0.0), jnp.where(is_lo, 0.0, s0)], axis=0)
            y_scr[b, rows, cols] = (_dot_nt(rp_ref[b, rows, cols], s_st.astype(BF16))
                                    + yq_ref[b, rows, cols].astype(F32))
        state = {(b, p): _dot3(state[(b, p)], g_ref[b, c, p]) + h_ref[b, c, p] for (b, p) in chains}
    for (b, p) in chains:
        s_scr[b, p] = state[(b, p)]

    pvec = pvec_ref[...]
    for (b, p) in chains:
        cols = slice(p * PAIR, (p + 1) * PAIR)
        y = y_scr[b, :, cols]
        mu = _head_sum(y, bd) * (1.0 / HEAD)
        yc = y - mu
        var = _head_sum(yc * yc, bd) * (1.0 / HEAD)
        yn = yc * lax.rsqrt(var + LNX_EPS) * pvec[5:6, cols] + pvec[6:7, cols]
        o_ref[b, :, cols] = ((yn + bonus_ref[b, :, cols].astype(F32))
                             * gate_ref[b, :, cols].astype(F32))


def _rwkv_scan(rp, yq, bonus, gate, g, h, pvec, batch):
    t, width = rp.shape
    n_pairs = width // PAIR
    seq = t // batch
    tt = TT_RWKV
    nc = tt // CHUNK
    tok = pl.BlockSpec((batch, tt, width), lambda i: (0, i, 0))
    as_seq = lambda z: z.reshape(batch, seq, width)
    out = pl.pallas_call(
        _rwkv_scan_kernel,
        grid=(seq // tt,),
        in_specs=[
            tok, tok, tok, tok,
            pl.BlockSpec((batch, nc, n_pairs, PAIR, PAIR), lambda i: (0, i, 0, 0, 0)),
            pl.BlockSpec((batch, nc, n_pairs, HEAD, PAIR), lambda i: (0, i, 0, 0, 0)),
            pl.BlockSpec((8, width), lambda i: (0, 0)),
        ],
        out_specs=tok,
        out_shape=jax.ShapeDtypeStruct((batch, seq, width), F32),
        scratch_shapes=[pltpu.VMEM((batch, n_pairs, HEAD, PAIR), F32),
                        pltpu.VMEM((batch, tt, width), F32)],
        compiler_params=_params(("arbitrary",)),
        name="rwkv_scan",
    )(as_seq(rp), as_seq(yq), as_seq(bonus), as_seq(gate),
      g.reshape(batch, seq // CHUNK, n_pairs, PAIR, PAIR),
      h.reshape(batch, seq // CHUNK, n_pairs, HEAD, PAIR), pvec)
    return out.reshape(t, width)


def _out_proj_kernel(x_ref, ya_ref, yb_ref, wo_ref, g_ref, rwt_ref, rb_ref, h1_ref, hn_ref, lg_ref):
    wa = ya_ref.shape[1]
    h1 = (x_ref[...] + _dot(ya_ref[...].astype(BF16), wo_ref[0:wa, :])
          + _dot(yb_ref[...].astype(BF16), wo_ref[wa:, :]))
    _store_token_tiles(h1_ref, h1)
    hn = h1 * lax.rsqrt(jnp.mean(h1 * h1, axis=-1, keepdims=True) + RMS_EPS) * g_ref[...]
    _store_token_tiles(hn_ref, hn)
    lg_ref[...] = _dot3_nt(rwt_ref[...], hn) + rb_ref[...]


def _out_proj(x2, ya, yb, wo_bf, g2, rwt, rb):
    t, d = x2.shape
    wa = ya.shape[1]
    wb = yb.shape[1]
    ne = rwt.shape[0]
    tm = TM_PROJ
    return pl.pallas_call(
        _out_proj_kernel,
        grid=(t // tm,),
        in_specs=[
            pl.BlockSpec((tm, d), lambda i: (i, 0)),
            pl.BlockSpec((tm, wa), lambda i: (i, 0)),
            pl.BlockSpec((tm, wb), lambda i: (i, 0)),
            pl.BlockSpec((wa + wb, d), lambda i: (0, 0)),
            pl.BlockSpec((1, d), lambda i: (0, 0)),
            pl.BlockSpec((ne, d), lambda i: (0, 0)),
            pl.BlockSpec((ne, 1), lambda i: (0, 0)),
        ],
        out_specs=[
            pl.BlockSpec((tm * SUBLANES, LANES), lambda i: (i, 0)),
            pl.BlockSpec((tm * SUBLANES, LANES), lambda i: (i, 0)),
            pl.BlockSpec((ne, tm), lambda i: (0, i)),
        ],
        out_shape=[
            jax.ShapeDtypeStruct((t * SUBLANES, LANES), F32),
            jax.ShapeDtypeStruct((t * SUBLANES, LANES), F32),
            jax.ShapeDtypeStruct((ne, t), F32),
        ],
        compiler_params=_params(("arbitrary",)),
        name="out_proj",
    )(x2, ya, yb, wo_bf, g2, rwt, rb)


def _route_kernel(lg_ref, pos_ref, gate_ref, pend_ref, li_ref, rstart_ref, rlen_ref,
                  carry_scr, pstart_scr):
    phase = pl.program_id(0)
    first = pl.program_id(1) == 0

    @pl.when(first & (phase == 0))
    def _():
        carry_scr[...] = jnp.zeros_like(carry_scr)
        pstart_scr[...] = jnp.zeros_like(pstart_scr)
        pend_ref[...] = jnp.zeros_like(pend_ref)

    @pl.when(first & (phase == 1))
    def _():
        counts = carry_scr[...]
        padded = jnp.ceil(counts * (1.0 / BM_MOE)) * BM_MOE
        row = lax.broadcasted_iota(I32, counts.shape, 0)
        end = padded
        s = 1
        while s < counts.shape[0]:
            end = end + jnp.where(row >= s, pltpu.roll(end, s, 0), 0.0)
            s *= 2
        pstart_scr[...] = end - padded
        pend_ref[...] = end.astype(I32)
        carry_scr[...] = jnp.zeros_like(carry_scr)

    l = lg_ref[...]
    ne, tr = l.shape
    e_iota = lax.broadcasted_iota(I32, (ne, tr), 0)
    chosen = jnp.zeros((ne, tr), F32)
    vals, sels = [], []
    for j in range(TOP_K):
        m = jnp.max(l, axis=0, keepdims=True)
        idx = jnp.min(jnp.where(l == m, e_iota, ne), axis=0, keepdims=True)
        sel = e_iota == idx
        vals.append(m)
        sels.append(sel)
        chosen = jnp.where(sel, 1.0, chosen)
        l = jnp.where(sel, -jnp.inf, l)
    ex = [jnp.exp(vj - vals[0]) for vj in vals]
    den = ex[0] + ex[1] + ex[2] + ex[3]
    for j in range(TOP_K):
        gate_ref[j:j + 1, :] = ex[j] / den

    ti = lax.broadcasted_iota(I32, (tr, tr), 0)
    tj = lax.broadcasted_iota(I32, (tr, tr), 1)
    upper = jnp.where(ti <= tj, 1.0, 0.0).astype(BF16)
    inc = _dot(chosen.astype(BF16), upper)
    carry = carry_scr[...]
    first_row = carry[:, 0:1] + pstart_scr[:, 0:1]
    row_of = inc - chosen + first_row
    for j in range(TOP_K):
        pj = jnp.sum(jnp.where(sels[j], row_of, 0.0), axis=0, keepdims=True)
        pos_ref[j:j + 1, :] = pj.astype(I32)
    carry_scr[...] = carry + inc[:, tr - 1:tr]

    tc = TC_COMBINE
    erow = lax.broadcasted_iota(I32, (ne, 128), 0)
    for s in range(tr // tc):
        cols = slice(s * tc, (s + 1) * tc)
        before = inc[:, s * tc - 1:s * tc] if s else jnp.zeros((ne, 1), F32)
        n_run = inc[:, (s + 1) * tc - 1:(s + 1) * tc] - before
        win_rows = jnp.broadcast_to(jnp.ceil(n_run * (1.0 / WIN_COMBINE)) * WIN_COMBINE, (ne, 128))
        base = win_rows
        k = 1
        while k < ne:
            base = base + jnp.where(erow >= k, pltpu.roll(base, k, 0), 0.0)
            k *= 2
        base = base - win_rows
        slot_of = (inc - chosen)[:, cols] - before + base[:, 0:1]
        for j in range(TOP_K):
            lj = jnp.sum(jnp.where(sels[j][:, cols], slot_of, 0.0), axis=0, keepdims=True)
            li_ref[j:j + 1, cols] = lj.astype(I32)
        rstart_ref[s * ne:(s + 1) * ne, :] = jnp.broadcast_to(first_row + before, (ne, 128)).astype(I32)
        rlen_ref[s * ne:(s + 1) * ne, :] = jnp.broadcast_to(n_run, (ne, 128)).astype(I32)


def _route(logits_t):
    ne, t = logits_t.shape
    tr = TR_ROUTE
    tok = pl.BlockSpec((TOP_K, tr), lambda ph, i: (0, i * ph))
    n_sub = tr // TC_COMBINE
    runs = pl.BlockSpec((n_sub * ne, 128), lambda ph, i: (i * ph, 0))
    runs_shape = jax.ShapeDtypeStruct((t // TC_COMBINE * ne, 128), I32)
    return pl.pallas_call(
        _route_kernel,
        grid=(2, t // tr),
        in_specs=[pl.BlockSpec((ne, tr), lambda ph, i: (0, i))],
        out_specs=[tok, tok, pl.BlockSpec((ne, 128), lambda ph, i: (0, 0)), tok, runs, runs],
        out_shape=[
            jax.ShapeDtypeStruct((TOP_K, t), I32),
            jax.ShapeDtypeStruct((TOP_K, t), F32),
            jax.ShapeDtypeStruct((ne, 128), I32),
            jax.ShapeDtypeStruct((TOP_K, t), I32),
            runs_shape, runs_shape,
        ],
        scratch_shapes=[pltpu.VMEM((ne, 128), F32), pltpu.VMEM((ne, 128), F32)],
        compiler_params=_params(("arbitrary", "arbitrary")),
        name="route",
    )(logits_t)


def _dispatch_kernel(pend_ref, nused_ref, pos_ref, hn_ref, xin_ref, zero_scr, sem):
    bm = zero_scr.shape[0] // SUBLANES
    td = pos_ref.shape[0] // TOP_K
    i = pl.program_id(0)

    def tile(ref, token):
        return ref.at[pl.ds(pl.multiple_of(token * SUBLANES, SUBLANES), SUBLANES)]

    def zero_block(row0):
        start = pl.multiple_of(row0 * SUBLANES, bm * SUBLANES)
        return pltpu.make_async_copy(zero_scr, xin_ref.at[pl.ds(start, bm * SUBLANES)], sem)

    @pl.when(i == 0)
    def _():
        zero_scr[...] = jnp.zeros_like(zero_scr)

        def has_rows(e):
            end = pend_ref[e]
            return jnp.where(e == 0, end >= bm, end > pend_ref[jnp.maximum(e - 1, 0)])

        def start_one(e, carry):
            @pl.when(has_rows(e))
            def _():
                zero_block(pend_ref[e] - bm).start()
            return carry

        def wait_one(e, carry):
            @pl.when(has_rows(e))
            def _():
                zero_block(0).wait()
            return carry

        def start_tail(blk, carry):
            zero_block(blk * bm).start()
            return carry

        def wait_tail(blk, carry):
            zero_block(0).wait()
            return carry

        n_blocks = xin_ref.shape[0] // (bm * SUBLANES)
        lax.fori_loop(0, pend_ref.shape[0], start_one, 0)
        lax.fori_loop(nused_ref[0], n_blocks, start_tail, 0)
        lax.fori_loop(0, pend_ref.shape[0], wait_one, 0)
        lax.fori_loop(nused_ref[0], n_blocks, wait_tail, 0)

    def issue(tk, carry):
        for j in range(TOP_K):
            pltpu.make_async_copy(tile(hn_ref, tk), tile(xin_ref, pos_ref[TOP_K * tk + j]),
                                  sem).start(priority=j % 2)
        return carry

    lax.fori_loop(0, td, issue, 0, unroll=8)

    for j in range(TOP_K):
        pltpu.make_async_copy(hn_ref, xin_ref.at[pl.ds(0, td * SUBLANES)], sem).wait()


def _dispatch(pad_end, n_used, pos, hn_tiles, n_rows):
    t = hn_tiles.shape[0] // SUBLANES
    td = TD_DISPATCH
    grid_spec = pltpu.PrefetchScalarGridSpec(
        num_scalar_prefetch=2,
        grid=(t // td,),
        in_specs=[
            pl.BlockSpec((TOP_K * td,), lambda i, pe, nu: (i,), memory_space=pltpu.SMEM),
            pl.BlockSpec((td * SUBLANES, LANES), lambda i, pe, nu: (i, 0)),
        ],
        out_specs=pl.BlockSpec(memory_space=pl.ANY),
        scratch_shapes=[pltpu.VMEM((BM_MOE * SUBLANES, LANES), F32), pltpu.SemaphoreType.DMA(())],
    )
    return pl.pallas_call(
        _dispatch_kernel,
        grid_spec=grid_spec,
        out_shape=jax.ShapeDtypeStruct((n_rows * SUBLANES, LANES), F32),
        compiler_params=_params(("arbitrary",)),
        name="dispatch",
    )(pad_end, n_used, pos, hn_tiles)


def _experts_kernel(pend_ref, nused_ref, w1_ref, b1_ref, w2_ref, b2_ref, xin_ref, yout_ref,
                    xbuf, obuf, wt_scr, wg_scr, wl_scr, w2_scr, sem_in, sem_out):
    e = pl.program_id(0)
    rows_blk = xbuf.shape[1]
    bm = rows_blk // SUBLANES
    dh = w2_ref.shape[1]
    n_used = nused_ref[0]
    blk_lo = jnp.where(e == 0, 0, pend_ref[jnp.maximum(e - 1, 0)]) // bm
    blk_hi = pend_ref[e] // bm

    def block(ref, blk):
        return ref.at[pl.ds(pl.multiple_of(blk * rows_blk, rows_blk), rows_blk)]

    def x_copy(blk, slot):
        return pltpu.make_async_copy(block(xin_ref, blk), xbuf.at[slot], sem_in.at[slot])

    def o_copy(blk, slot):
        return pltpu.make_async_copy(obuf.at[slot], block(yout_ref, blk), sem_out.at[slot])

    @pl.when((e == 0) & (n_used > 0))
    def _():
        x_copy(0, 0).start()

    @pl.when(blk_hi > blk_lo)
    def _():
        n_slab, n_t, lanes = wt_scr.shape
        for part in range(2 * dh // n_t):
            rows = slice(part * (n_t // 2), (part + 1) * (n_t // 2))
            for s in range(n_slab):
                cols = slice(s * lanes, (s + 1) * lanes)
                wt_scr[s] = w1_ref[0, cols, part * n_t:(part + 1) * n_t].T
                wg_scr[rows, cols] = wt_scr[s, pl.ds(0, n_t // 2, stride=2), :].astype(BF16)
                wl_scr[rows, cols] = wt_scr[s, pl.ds(1, n_t // 2, stride=2), :].astype(BF16)
        w2_scr[...] = w2_ref[0].astype(BF16)

        def one_block(blk, carry):
            slot = blk % 2

            @pl.when(blk + 1 < n_used)
            def _():
                x_copy(blk + 1, 1 - slot).start()

            x_copy(blk, slot).wait()

            @pl.when(blk >= 2)
            def _():
                o_copy(blk - 2, slot).wait()

            xb = _load_token_tiles(xbuf.at[slot], 0, bm).astype(BF16)
            bias = b1_ref[0]
            glu = jnp.minimum(_dot_nt(xb, wg_scr[...]) + bias[:, :dh], SWIGLU_LIMIT)
            lin = jnp.clip(_dot_nt(xb, wl_scr[...]) + bias[:, dh:], -SWIGLU_LIMIT, SWIGLU_LIMIT)
            act = glu * jax.nn.sigmoid(SWIGLU_ALPHA * glu) * (lin + 1.0)
            _store_token_tiles(obuf.at[slot], _dot(act.astype(BF16), w2_scr[...]) + b2_ref[0])
            o_copy(blk, slot).start()
            return carry

        lax.fori_loop(blk_lo, blk_hi, one_block, 0)

    @pl.when(e == pl.num_programs(0) - 1)
    def _():
        @pl.when(n_used >= 1)
        def _():
            o_copy(0, (n_used - 1) % 2).wait()

        @pl.when(n_used >= 2)
        def _():
            o_copy(0, n_used % 2).wait()

        obuf[0] = jnp.zeros(obuf.shape[1:], obuf.dtype)
        n_blocks = yout_ref.shape[0] // rows_blk

        def start_tail(blk, carry):
            o_copy(blk, 0).start()
            return carry

        def wait_tail(blk, carry):
            o_copy(0, 0).wait()
            return carry

        lax.fori_loop(n_used, n_blocks, start_tail, 0)
        lax.fori_loop(n_used, n_blocks, wait_tail, 0)


def _experts(pad_end, n_used, xin, w1, b1p, w2, b2):
    ne, d, dh2 = w1.shape
    dh = dh2 // 2
    rows_blk = BM_MOE * SUBLANES
    grid_spec = pltpu.PrefetchScalarGridSpec(
        num_scalar_prefetch=2,
        grid=(ne,),
        in_specs=[
            pl.BlockSpec((1, d, dh2), lambda e, pe, nu: (e, 0, 0)),
            pl.BlockSpec((1, 1, dh2), lambda e, pe, nu: (e, 0, 0)),
            pl.BlockSpec((1, dh, d), lambda e, pe, nu: (e, 0, 0)),
            pl.BlockSpec((1, 1, d), lambda e, pe, nu: (e, 0, 0)),
            pl.BlockSpec(memory_space=pl.ANY),
        ],
        out_specs=pl.BlockSpec(memory_space=pl.ANY),
        scratch_shapes=[
            pltpu.VMEM((2, rows_blk, LANES), F32),
            pltpu.VMEM((2, rows_blk, LANES), F32),
            pltpu.VMEM((d // 128, dh, 128), F32),
            pltpu.VMEM((dh, d), BF16),
            pltpu.VMEM((dh, d), BF16),
            pltpu.VMEM((dh, d), BF16),
            pltpu.SemaphoreType.DMA((2,)),
            pltpu.SemaphoreType.DMA((2,)),
        ],
    )
    return pl.pallas_call(
        _experts_kernel,
        grid_spec=grid_spec,
        out_shape=jax.ShapeDtypeStruct(xin.shape, F32),
        compiler_params=_params(("arbitrary",)),
        name="experts",
    )(pad_end, n_used, w1, b1p, w2, b2, xin)


def _combine_kernel(rstart_ref, rlen_ref, li_ref, gate_ref, h1_ref, fg_ref, yout_ref, o_ref,
                    wbuf, hacc, sem):
    tc = o_ref.shape[0]
    i = pl.program_id(0)
    slot = i % 2
    n_experts = rstart_ref.shape[0] // pl.num_programs(0)
    win_rows = WIN_COMBINE * SUBLANES

    def tile(ref, token):
        return ref.at[pl.ds(pl.multiple_of(token * SUBLANES, SUBLANES), SUBLANES)]

    def for_windows(tile_idx, s, fn):
        def per_expert(e, base):
            start = rstart_ref[tile_idx * n_experts + e]
            n_win = (rlen_ref[tile_idx * n_experts + e] + (WIN_COMBINE - 1)) // WIN_COMBINE

            def per_window(w, carry):
                src = yout_ref.at[pl.ds(pl.multiple_of((start + w * WIN_COMBINE) * SUBLANES, SUBLANES),
                                        win_rows)]
                dst = wbuf.at[s, pl.ds(pl.multiple_of((base + w) * win_rows, win_rows), win_rows)]
                fn(pltpu.make_async_copy(src, dst, sem.at[s]))
                return carry

            lax.fori_loop(0, n_win, per_window, 0)
            return base + n_win

        lax.fori_loop(0, n_experts, per_expert, 0)

    @pl.when(i == 0)
    def _():
        for_windows(0, 0, lambda cp: cp.start())

    @pl.when(i + 1 < pl.num_programs(0))
    def _():
        for_windows(i + 1, 1 - slot, lambda cp: cp.start())

    for_windows(i, slot, lambda cp: cp.wait())

    def per_token(tk, carry):
        acc = tile(h1_ref, tk)[...]
        for j in range(TOP_K):
            row = li_ref[TOP_K * tk + j]
            acc = acc + gate_ref[TOP_K * tk + j] * tile(wbuf.at[slot], row)[...]
        tile(hacc, tk)[...] = acc
        return carry

    lax.fori_loop(0, tc, per_token, 0, unroll=8)
    h = _load_token_tiles(hacc, 0, tc)
    o_ref[...] = h * lax.rsqrt(jnp.mean(h * h, axis=-1, keepdims=True) + RMS_EPS) * fg_ref[...]


def _combine(rstart, rlen, li, gates, h1_tiles, fg, yout):
    t = h1_tiles.shape[0] // SUBLANES
    d = fg.shape[1]
    tc = TC_COMBINE
    n_steps = t // tc
    buf_rows = TOP_K * tc + (rstart.shape[0] // n_steps) * WIN_COMBINE
    grid_spec = pltpu.PrefetchScalarGridSpec(
        num_scalar_prefetch=2,
        grid=(n_steps,),
        in_specs=[
            pl.BlockSpec((TOP_K * tc,), lambda i, rs, rl: (i,), memory_space=pltpu.SMEM),
            pl.BlockSpec((TOP_K * tc,), lambda i, rs, rl: (i,), memory_space=pltpu.SMEM),
            pl.BlockSpec((tc * SUBLANES, LANES), lambda i, rs, rl: (i, 0)),
            pl.BlockSpec((1, d), lambda i, rs, rl: (0, 0)),
            pl.BlockSpec(memory_space=pl.ANY),
        ],
        out_specs=pl.BlockSpec((tc, d), lambda i, rs, rl: (i, 0)),
        scratch_shapes=[pltpu.VMEM((2, buf_rows * SUBLANES, LANES), F32),
                        pltpu.VMEM((tc * SUBLANES, LANES), F32),
                        pltpu.SemaphoreType.DMA((2,))],
    )
    return pl.pallas_call(
        _combine_kernel,
        grid_spec=grid_spec,
        out_shape=jax.ShapeDtypeStruct((t, d), F32),
        compiler_params=_params(("arbitrary",)),
        name="combine",
    )(rstart, rlen, li, gates, h1_tiles, fg, yout)


def kernel(x, norm1_g, w_in, sgu_ln_g, sgu_ln_b, sgu_w, sgu_b, mu_rkv, mu_wag, decay_w0, decay_w1,
           decay_w2, iclr_a0, iclr_a1, iclr_a2, gate_g1, gate_g2, k_k, k_a, r_k, lnx_g, lnx_b, w_out,
           norm2_g, router_w, router_b, moe_w1, moe_b1, moe_w2, moe_b2, final_g):
    batch, seq, d = x.shape
    t = batch * seq
    depth = w_in.shape[0]
    sgu_width = sgu_ln_g.shape[1]
    rw = mu_rkv.shape[2]
    n_dec, n_icl, n_gate = decay_w1.shape[2], iclr_a1.shape[2], gate_g1.shape[2]
    assert n_dec == HEAD and n_icl == HEAD and n_gate == PAIR and rw % PAIR == 0
    assert seq % TT_RWKV == 0 and seq % TT_PREP == 0 and t % TM_PROJ == 0
    assert sgu_w.shape[2] == SGU_BLOCK
    assert depth == 1, "the final RMSNorm is fused into the last layer's combine kernel"
    assert d == SUBLANES * LANES, "the MoE row movers copy one (8, 128) f32 tile per token"

    h = x.reshape(t, d)
    for l in range(depth):
        win_bf = w_in[l].astype(BF16)
        wl = jnp.concatenate([decay_w1[l], iclr_a1[l], gate_g1[l]], axis=1)
        mucat = jnp.concatenate([
            jnp.broadcast_to(mu_wag[l, 0][:, None], (d, n_dec)),
            jnp.broadcast_to(mu_wag[l, 1][:, None], (d, n_icl)),
            jnp.broadcast_to(mu_wag[l, 2][:, None], (d, n_gate))], axis=1)
        zeros = jnp.zeros((HEAD, rw), F32)
        w2pad = jnp.concatenate([decay_w2[l], zeros], axis=0)
        a2pad = jnp.concatenate([zeros, iclr_a2[l]], axis=0)
        pvec = jnp.stack([decay_w0[l], iclr_a0[l], k_k[l], k_a[l], r_k[l].reshape(-1),
                          lnx_g[l], lnx_b[l], jnp.zeros((rw,), F32)], axis=0)
        bias2d = jnp.repeat(sgu_b[l].T, sgu_width // sgu_b.shape[1], axis=1)
        wo_bf = w_out[l].astype(BF16)
        b1p = jnp.concatenate([moe_b1[l][:, 0::2], moe_b1[l][:, 1::2]], axis=-1)[:, None, :]
        b2 = moe_b2[l][:, None, :]

        assert w_in.shape[2] == 2 * sgu_width + 3 * rw
        proj, lora, ya = _in_proj(h, norm1_g[l][None, :], win_bf, wl, mucat, sgu_ln_g[l][None, :],
                                  sgu_ln_b[l][None, :], sgu_w[l], bias2d)
        rp, yq, bonus, gate, g_m, h_m = _rwkv_prep(proj, lora, mu_rkv[l], pvec, w2pad, a2pad,
                                                   gate_g2[l], seq, 0)
        yb = _rwkv_scan(rp, yq, bonus, gate, g_m, h_m, pvec, batch)
        h1, hn2, logits_t = _out_proj(h, ya, yb, wo_bf, norm2_g[l][None, :], router_w[l].T,
                                      router_b[l][:, None])
        pos, gates, pend, li, rstart, rlen = _route(logits_t)

        pad_end = pend[:, 0]
        n_blocks = (t * TOP_K) // BM_MOE + N_EXPERTS
        n_used = pad_end[-1:] // BM_MOE

        pos = pos.T.reshape(-1)
        xin = _dispatch(pad_end, n_used, pos, hn2, n_blocks * BM_MOE)
        yout = _experts(pad_end, n_used, xin, moe_w1[l], b1p, moe_w2[l], b2)
        h = _combine(rstart[:, 0], rlen[:, 0], li.T.reshape(-1), gates.T.reshape(-1), h1,
                     final_g[None, :], yout)
    return h.reshape(batch, seq, d)
```

```python
import functools

import jax
import jax.numpy as jnp
from jax import lax
from jax.experimental import pallas as pl
from jax.experimental.pallas import tpu as pltpu

F32 = jnp.float32
BF16 = jnp.bfloat16
I32 = jnp.int32

RMS_EPS = 1e-5
LN_EPS = 1e-5
LNX_EPS = 64e-5
CHUNK = 64
SGU_BLOCK = 128
HEAD = 64
PAIR = 2 * HEAD
N_EXPERTS = 32
TOP_K = 4
SWIGLU_ALPHA = 1.702
SWIGLU_LIMIT = 7.0

V7X_VMEM_LIMIT = 56 * 1024 * 1024

TM_PROJ = 1024
TT_PREP = 1024
N_SUB_PREP = 1
TT_RWKV = 512
TR_ROUTE = 1024
BM_MOE = 256
TD_DISPATCH = 1024
TC_COMBINE = 512
WIN_COMBINE = 32


def _dot(a, b):
    return jnp.dot(a, b, preferred_element_type=F32)


def _dot_nt(a, b):
    return lax.dot_general(a, b, (((1,), (1,)), ((), ())), preferred_element_type=F32)


def _split(x):
    hi = x.astype(BF16)
    lo = (x - hi.astype(F32)).astype(BF16)
    return hi, lo


def _dot3(a, b):
    ah, al = _split(a)
    bh, bl = _split(b)
    return _dot(ah, bh) + _dot(al, bh) + _dot(ah, bl)


def _dot3_nt(a, b):
    ah, al = _split(a)
    bh, bl = _split(b)
    return _dot_nt(ah, bh) + _dot_nt(al, bh) + _dot_nt(ah, bl)


def _head_sum(x, bd):
    hi, lo = _split(x)
    return _dot(hi, bd) + _dot(lo, bd)


SUBLANES = 8
LANES = 128


def _store_token_tiles(ref, x):
    n = x.shape[0]
    for s in range(SUBLANES):
        ref[pl.ds(s, n, stride=SUBLANES), :] = x[:, s * LANES:(s + 1) * LANES]


def _load_token_tiles(ref, row0, n):
    return jnp.concatenate(
        [ref[pl.ds(row0 + s, n, stride=SUBLANES), :] for s in range(SUBLANES)], axis=1)


def _params(sem, vmem=V7X_VMEM_LIMIT):
    return pltpu.CompilerParams(dimension_semantics=sem, vmem_limit_bytes=vmem)


def _in_proj_kernel(x_ref, g_ref, win_ref, wl_ref, mucat_ref, lng_ref, lnb_ref, w_ref, bias_ref,
                    proj_ref, lora_ref, o_ref, wl_scr, wm_scr):
    n_heads = w_ref.shape[0]

    @pl.when(pl.program_id(0) == 0)
    def _():
        wl = wl_ref[...]
        mu = mucat_ref[...]
        n = wl.shape[1]
        wl_scr[:, 0:n] = (wl * (1.0 - mu)).astype(BF16)
        wl_scr[:, n:2 * n] = (wl * mu).astype(BF16)
        qi = lax.broadcasted_iota(I32, (SGU_BLOCK, SGU_BLOCK), 0) // CHUNK
        kj = lax.broadcasted_iota(I32, (SGU_BLOCK, SGU_BLOCK), 1) // CHUNK
        for h in range(n_heads):
            wm_scr[h] = jnp.where(kj <= qi, w_ref[h], 0.0).astype(BF16)

    x = x_ref[...]
    hn = x * lax.rsqrt(jnp.mean(x * x, axis=-1, keepdims=True) + RMS_EPS) * g_ref[...]
    hb = hn.astype(BF16)
    width = lng_ref.shape[1]
    two_w = 2 * width
    proj_ref[...] = _dot(hb, win_ref[:, two_w:])
    lora_ref[...] = _dot(hb, wl_scr[...])

    z = _dot(hb, win_ref[:, :two_w])
    tm = z.shape[0]
    gz = 0.5 * z * (1.0 + lax.erf(z * (2.0 ** -0.5)))
    u = gz[:, :width]
    v = gz[:, width:]
    mu = jnp.mean(v, axis=-1, keepdims=True)
    vc = v - mu
    var = jnp.mean(vc * vc, axis=-1, keepdims=True)
    vn = vc * lax.rsqrt(var + LN_EPS) * lng_ref[...] + lnb_ref[...]
    lane = lax.broadcasted_iota(I32, (SGU_BLOCK, PAIR), 1)
    is_lo = lane < HEAD
    bias = bias_ref[...]
    for blk in range(tm // SGU_BLOCK):
        rows = slice(blk * SGU_BLOCK, (blk + 1) * SGU_BLOCK)
        for p in range(width // PAIR):
            cols = slice(p * PAIR, (p + 1) * PAIR)
            vp = vn[rows, cols]
            lo = jnp.where(is_lo, vp, 0.0).astype(BF16)
            hi = jnp.where(is_lo, 0.0, vp).astype(BF16)
            sv = _dot(wm_scr[2 * p], lo) + _dot(wm_scr[2 * p + 1], hi)
            o_ref[rows, cols] = u[rows, cols] * (sv + bias[:, cols])


def _in_proj(x2, g, win_bf, wl, mucat, ln_g, ln_b, w_s, bias2d):
    t, d = x2.shape
    width = ln_g.shape[1]
    n_rkv = win_bf.shape[1] - 2 * width
    n_l = wl.shape[1]
    n_heads = w_s.shape[0]
    tm = TM_PROJ
    const = lambda *shape: pl.BlockSpec(shape, lambda i: (0,) * len(shape))
    row = lambda n: pl.BlockSpec((tm, n), lambda i: (i, 0))
    return pl.pallas_call(
        _in_proj_kernel,
        grid=(t // tm,),
        in_specs=[
            row(d), const(1, d), const(d, win_bf.shape[1]), const(d, n_l), const(d, n_l),
            const(1, width), const(1, width), const(n_heads, SGU_BLOCK, SGU_BLOCK),
            const(SGU_BLOCK, width),
        ],
        out_specs=[row(n_rkv), row(2 * n_l), row(width)],
        out_shape=[
            jax.ShapeDtypeStruct((t, n_rkv), F32),
            jax.ShapeDtypeStruct((t, 2 * n_l), F32),
            jax.ShapeDtypeStruct((t, width), F32),
        ],
        scratch_shapes=[pltpu.VMEM((d, 2 * n_l), BF16),
                        pltpu.VMEM((n_heads, SGU_BLOCK, SGU_BLOCK), BF16)],
        compiler_params=_params(("arbitrary",)),
        name="in_proj_sgu",
    )(x2, g, win_bf, wl, mucat, ln_g, ln_b, w_s, bias2d)


def _rwkv_prep_kernel(seq_len, pr_ref, pk_ref, pv_ref, ppr_ref, ppk_ref, ppv_ref, lo_ref, plo_ref,
                      mu_ref, pvec_ref, w2_ref, a2_ref, g2_ref,
                      rp_ref, yq_ref, bonus_ref, gate_ref, g_ref, h_ref):
    tt = pr_ref.shape[0]
    ts = tt // N_SUB_PREP
    n_chunks = ts // CHUNK
    i = pl.program_id(0)
    keep = jnp.where((i * tt) % seq_len == 0, 0.0, 1.0)

    mu = mu_ref[...]
    pvec = pvec_ref[...]
    w0, a0, k_k, k_a, r_k = pvec[0:1], pvec[1:2], pvec[2:3], pvec[3:4], pvec[4:5]
    w2b = w2_ref[...].astype(BF16)
    a2b = a2_ref[...].astype(BF16)
    g2b = g2_ref[...].astype(BF16)
    half = lo_ref.shape[1] // 2

    lane_r = lax.broadcasted_iota(I32, (PAIR, PAIR), 0)
    lane_c = lax.broadcasted_iota(I32, (PAIR, PAIR), 1)
    bd = jnp.where((lane_r // HEAD) == (lane_c // HEAD), 1.0, 0.0).astype(BF16)
    tri_ones = jnp.where(((lane_r // CHUNK) == (lane_c // CHUNK)) & (lane_c <= lane_r),
                         1.0, 0.0).astype(BF16)
    lane = lax.broadcasted_iota(I32, (CHUNK, PAIR), 1)
    is_lo = lane < HEAD

    def stack(x):
        return jnp.concatenate([jnp.where(is_lo, x, 0.0), jnp.where(is_lo, 0.0, x)], axis=0)

    n2 = 2 * PAIR
    ri = lax.broadcasted_iota(I32, (n2, n2), 0)
    ci = lax.broadcasted_iota(I32, (n2, n2), 1)
    same_head = ((ri // CHUNK) % 2) == ((ci // CHUNK) % 2)
    t_i = ri % CHUNK
    s_i = ci % CHUNK
    tri = same_head & (s_i < t_i + ri // PAIR)
    eye = lane_r == lane_c

    for sub in range(N_SUB_PREP):
        r0 = sub * ts
        tile_rows = slice(r0, r0 + ts)

        def shift(ref, pref, cols=slice(None)):
            x = ref[tile_rows, cols]
            if sub == 0:
                first = pref[7:8, cols] * keep
            else:
                first = ref[r0 - 1:r0, cols]
            rowc = lax.broadcasted_iota(I32, x.shape, 0)
            return x, jnp.where(rowc == 0, first, pltpu.roll(x, 1, 0))

        def shift_mix(ref, pref, m):
            x, xs = shift(ref, pref)
            return x + (xs - x) * m

        r = shift_mix(pr_ref, ppr_ref, mu[0:1])
        k = shift_mix(pk_ref, ppk_ref, mu[1:2])
        v = shift_mix(pv_ref, ppv_ref, mu[2:3])

        _, lo_b = shift(lo_ref, plo_ref, slice(half, 2 * half))
        l_all = lo_ref[tile_rows, :half] + lo_b
        l_wa = l_all[:, :PAIR]
        l_g = l_all[:, PAIR:]

        dw = _dot(jnp.tanh(l_wa).astype(BF16), w2b)
        ia = _dot(l_wa.astype(BF16), a2b)
        gate_ref[tile_rows, :] = _dot(jax.nn.sigmoid(l_g).astype(BF16), g2b).astype(BF16)

        zneg = -(w0 + dw)
        softplus = jnp.maximum(zneg, 0.0) + jnp.log(1.0 + jnp.exp(-jnp.abs(zneg)))
        logw = -jnp.exp(-softplus - 0.5)
        iclr = jax.nn.sigmoid(a0 + ia)

        kk = k * k_k
        kk = kk / jnp.maximum(jnp.sqrt(_head_sum(kk * kk, bd)), 1e-12)
        k2 = k * (1.0 + (iclr - 1.0) * k_a)
        a = -kk
        b = kk * iclr
        bonus_ref[tile_rows, :] = (_head_sum(r * k2 * r_k, bd) * v).astype(BF16)

        lw_hi, lw_lo = _split(logw)
        cl = jnp.concatenate(
            [_dot(tri_ones, lw_hi[q * PAIR:(q + 1) * PAIR]) + _dot(tri_ones, lw_lo[q * PAIR:(q + 1) * PAIR])
             for q in range(ts // PAIR)], axis=0)
        at_all = a * jnp.exp(cl - logw)
        rt_all = r * jnp.exp(cl)
        w_inv = jnp.exp(-cl)
        bt_all = b * w_inv
        kt_all = k2 * w_inv

        chunks = range(n_chunks)
        rows = [slice(c * CHUNK, (c + 1) * CHUNK) for c in chunks]
        out_rows = [slice(r0 + c * CHUNK, r0 + (c + 1) * CHUNK) for c in chunks]
        last = [cl[rows[c]][CHUNK - 1:CHUNK, :] for c in chunks]
        w_rem = [jnp.exp(last[c] - cl[rows[c]]) for c in chunks]
        a_s = [stack(at_all[rows[c]]) for c in chunks]
        r_s = [stack(rt_all[rows[c]]) for c in chunks]
        v_s = [stack(v[rows[c]]) for c in chunks]
        v_sb = [v_s[c].astype(BF16) for c in chunks]
        bk_h = [jnp.concatenate([stack(b[rows[c]] * w_rem[c]), stack(k2[rows[c]] * w_rem[c])],
                                axis=0).astype(BF16) for c in chunks]

        a_all = []
        for c in chunks:
            lhs = jnp.concatenate([a_s[c], r_s[c]], axis=0).astype(BF16)
            rhs = jnp.concatenate([stack(bt_all[rows[c]]), stack(kt_all[rows[c]])],
                                  axis=0).astype(BF16)
            a_all.append(jnp.where(tri, _dot_nt(lhs, rhs), 0.0))
        n_k = [a_all[c][:PAIR, :PAIR].astype(BF16) for c in chunks]
        a_r = [a_all[c][PAIR:, :].astype(BF16) for c in chunks]
        x = [jnp.concatenate([a_s[c], _dot(a_all[c][:PAIR, PAIR:].astype(BF16), v_sb[c])], axis=1)
             for c in chunks]
        steps = CHUNK.bit_length() - 1
        for it in range(steps):
            x = [x[c] + _dot(n_k[c], x[c].astype(BF16)) for c in chunks]
            if it + 1 < steps:
                n_k = [_dot(n_k[c], n_k[c]).astype(BF16) for c in chunks]
        zero_b = jnp.zeros((PAIR, PAIR), BF16)
        ry = [_dot(a_r[c], jnp.concatenate(
            [x[c].astype(BF16), jnp.concatenate([zero_b, v_sb[c]], axis=1)], axis=0)) for c in chunks]
        for c in chunks:
            r_c = rt_all[rows[c]]
            rp_ref[out_rows[c], :] = (ry[c][:CHUNK, :PAIR] + ry[c][CHUNK:, :PAIR] + r_c).astype(BF16)
            yq_ref[out_rows[c], :] = (ry[c][:CHUNK, PAIR:] + ry[c][CHUNK:, PAIR:]).astype(BF16)
        p_t = [x[c][:, :PAIR].T.astype(BF16) for c in chunks]
        qv_t = [jnp.concatenate([x[c][:, PAIR:], v_s[c]], axis=0).T.astype(BF16) for c in chunks]
        c0 = sub * n_chunks
        for c in chunks:
            g_ref[c0 + c, 0] = (jnp.where(eye, jnp.exp(last[c]), 0.0)
                                + _dot(p_t[c], bk_h[c][:PAIR]))
        for c in chunks:
            hbd = _dot(qv_t[c], bk_h[c])
            h_ref[c0 + c, 0] = hbd[:HEAD] + hbd[HEAD:]


def _rwkv_prep(proj, lora, mu_rkv, pvec, w2pad, a2pad, g2, seq_len, col0):
    t = proj.shape[0]
    width = mu_rkv.shape[1]
    n_pairs = width // PAIR
    tt = TT_PREP
    nl = lora.shape[1]
    cb = col0 // PAIR
    wb = width // PAIR

    def prev_rows(i):
        return jnp.maximum(i * (tt // 8) - 1, 0)

    def tok(c):
        return pl.BlockSpec((tt, PAIR), lambda i, p: (i, c + p))

    def prev(c):
        return pl.BlockSpec((8, PAIR), lambda i, p: (prev_rows(i), c + p))

    out_tok = pl.BlockSpec((tt, PAIR), lambda i, p: (i, p))
    tok_shape = jax.ShapeDtypeStruct((t, width), BF16)
    return pl.pallas_call(
        functools.partial(_rwkv_prep_kernel, seq_len),
        grid=(t // tt, n_pairs),
        in_specs=[
            tok(cb), tok(cb + wb), tok(cb + 2 * wb),
            prev(cb), prev(cb + wb), prev(cb + 2 * wb),
            pl.BlockSpec((tt, nl), lambda i, p: (i, 0)),
            pl.BlockSpec((8, nl), lambda i, p: (prev_rows(i), 0)),
            pl.BlockSpec((3, PAIR), lambda i, p: (0, p)),
            pl.BlockSpec((8, PAIR), lambda i, p: (0, p)),
            pl.BlockSpec((PAIR, PAIR), lambda i, p: (0, p)),
            pl.BlockSpec((PAIR, PAIR), lambda i, p: (0, p)),
            pl.BlockSpec((PAIR, PAIR), lambda i, p: (0, p)),
        ],
        out_specs=[
            out_tok, out_tok, out_tok, out_tok,
            pl.BlockSpec((tt // CHUNK, 1, PAIR, PAIR), lambda i, p: (i, p, 0, 0)),
            pl.BlockSpec((tt // CHUNK, 1, HEAD, PAIR), lambda i, p: (i, p, 0, 0)),
        ],
        out_shape=[
            tok_shape, tok_shape, tok_shape, tok_shape,
            jax.ShapeDtypeStruct((t // CHUNK, n_pairs, PAIR, PAIR), F32),
            jax.ShapeDtypeStruct((t // CHUNK, n_pairs, HEAD, PAIR), F32),
        ],
        compiler_params=_params(("arbitrary", "arbitrary")),
        name="rwkv_prep",
    )(proj, proj, proj, proj, proj, proj, lora, lora, mu_rkv, pvec, w2pad, a2pad, g2)


def _rwkv_scan_kernel(rp_ref, yq_ref, bonus_ref, gate_ref, g_ref, h_ref, pvec_ref, o_ref, s_scr, y_scr):
    nb, tt, width = rp_ref.shape
    n_pairs = width // PAIR

    @pl.when(pl.program_id(0) == 0)
    def _():
        s_scr[...] = jnp.zeros_like(s_scr)

    lane = lax.broadcasted_iota(I32, (HEAD, PAIR), 1)
    is_lo = lane < HEAD
    lane_r = lax.broadcasted_iota(I32, (PAIR, PAIR), 0)
    lane_c = lax.broadcasted_iota(I32, (PAIR, PAIR), 1)
    bd = jnp.where((lane_r // HEAD) == (lane_c // HEAD), 1.0, 0.0).astype(BF16)

    chains = [(b, p) for b in range(nb) for p in range(n_pairs)]
    state = {ch: s_scr[ch[0], ch[1]] for ch in chains}
    for c in range(tt // CHUNK):
        rows = slice(c * CHUNK, (c + 1) * CHUNK)
        for (b, p) in chains:
            cols = slice(p * PAIR, (p + 1) * PAIR)
            s0 = state[(b, p)]
            s_st = jnp.concatenate([jnp.where(is_lo, s0, 0.0), jnp.where(is_lo, 0.0, s0)], axis=0)
            y_scr[b, rows, cols] = (_dot_nt(rp_ref[b, rows, cols], s_st.astype(BF16))
                                    + yq_ref[b, rows, cols].astype(F32))
        state = {(b, p): _dot3(state[(b, p)], g_ref[b, c, p]) + h_ref[b, c, p] for (b, p) in chains}
    for (b, p) in chains:
        s_scr[b, p] = state[(b, p)]

    pvec = pvec_ref[...]
    for (b, p) in chains:
        cols = slice(p * PAIR, (p + 1) * PAIR)
        y = y_scr[b, :, cols]
        mu = _head_sum(y, bd) * (1.0 / HEAD)
        yc = y - mu
        var = _head_sum(yc * yc, bd) * (1.0 / HEAD)
        yn = yc * lax.rsqrt(var + LNX_EPS) * pvec[5:6, cols] + pvec[6:7, cols]
        o_ref[b, :, cols] = ((yn + bonus_ref[b, :, cols].astype(F32))
                             * gate_ref[b, :, cols].astype(F32))


def _rwkv_scan(rp, yq, bonus, gate, g, h, pvec, batch):
    t, width = rp.shape
    n_pairs = width // PAIR
    seq = t // batch
    tt = TT_RWKV
    nc = tt // CHUNK
    tok = pl.BlockSpec((batch, tt, width), lambda i: (0, i, 0))
    as_seq = lambda z: z.reshape(batch, seq, width)
    out = pl.pallas_call(
        _rwkv_scan_kernel,
        grid=(seq // tt,),
        in_specs=[
            tok, tok, tok, tok,
            pl.BlockSpec((batch, nc, n_pairs, PAIR, PAIR), lambda i: (0, i, 0, 0, 0)),
            pl.BlockSpec((batch, nc, n_pairs, HEAD, PAIR), lambda i: (0, i, 0, 0, 0)),
            pl.BlockSpec((8, width), lambda i: (0, 0)),
        ],
        out_specs=tok,
        out_shape=jax.ShapeDtypeStruct((batch, seq, width), F32),
        scratch_shapes=[pltpu.VMEM((batch, n_pairs, HEAD, PAIR), F32),
                        pltpu.VMEM((batch, tt, width), F32)],
        compiler_params=_params(("arbitrary",)),
        name="rwkv_scan",
    )(as_seq(rp), as_seq(yq), as_seq(bonus), as_seq(gate),
      g.reshape(batch, seq // CHUNK, n_pairs, PAIR, PAIR),
      h.reshape(batch, seq // CHUNK, n_pairs, HEAD, PAIR), pvec)
    return out.reshape(t, width)


def _out_proj_kernel(x_ref, ya_ref, yb_ref, wo_ref, g_ref, rwt_ref, rb_ref, h1_ref, hn_ref, lg_ref):
    wa = ya_ref.shape[1]
    h1 = (x_ref[...] + _dot(ya_ref[...].astype(BF16), wo_ref[0:wa, :])
          + _dot(yb_ref[...].astype(BF16), wo_ref[wa:, :]))
    _store_token_tiles(h1_ref, h1)
    hn = h1 * lax.rsqrt(jnp.mean(h1 * h1, axis=-1, keepdims=True) + RMS_EPS) * g_ref[...]
    _store_token_tiles(hn_ref, hn)
    lg_ref[...] = _dot3_nt(rwt_ref[...], hn) + rb_ref[...]


def _out_proj(x2, ya, yb, wo_bf, g2, rwt, rb):
    t, d = x2.shape
    wa = ya.shape[1]
    wb = yb.shape[1]
    ne = rwt.shape[0]
    tm = TM_PROJ
    return pl.pallas_call(
        _out_proj_kernel,
        grid=(t // tm,),
        in_specs=[
            pl.BlockSpec((tm, d), lambda i: (i, 0)),
            pl.BlockSpec((tm, wa), lambda i: (i, 0)),
            pl.BlockSpec((tm, wb), lambda i: (i, 0)),
            pl.BlockSpec((wa + wb, d), lambda i: (0, 0)),
            pl.BlockSpec((1, d), lambda i: (0, 0)),
            pl.BlockSpec((ne, d), lambda i: (0, 0)),
            pl.BlockSpec((ne, 1), lambda i: (0, 0)),
        ],
        out_specs=[
            pl.BlockSpec((tm * SUBLANES, LANES), lambda i: (i, 0)),
            pl.BlockSpec((tm * SUBLANES, LANES), lambda i: (i, 0)),
            pl.BlockSpec((ne, tm), lambda i: (0, i)),
        ],
        out_shape=[
            jax.ShapeDtypeStruct((t * SUBLANES, LANES), F32),
            jax.ShapeDtypeStruct((t * SUBLANES, LANES), F32),
            jax.ShapeDtypeStruct((ne, t), F32),
        ],
        compiler_params=_params(("arbitrary",)),
        name="out_proj",
    )(x2, ya, yb, wo_bf, g2, rwt, rb)


def _route_kernel(lg_ref, pos_ref, gate_ref, pend_ref, li_ref, rstart_ref, rlen_ref,
                  carry_scr, pstart_scr):
    phase = pl.program_id(0)
    first = pl.program_id(1) == 0

    @pl.when(first & (phase == 0))
    def _():
        carry_scr[...] = jnp.zeros_like(carry_scr)
        pstart_scr[...] = jnp.zeros_like(pstart_scr)
        pend_ref[...] = jnp.zeros_like(pend_ref)

    @pl.when(first & (phase == 1))
    def _():
        counts = carry_scr[...]
        padded = jnp.ceil(counts * (1.0 / BM_MOE)) * BM_MOE
        row = lax.broadcasted_iota(I32, counts.shape, 0)
        end = padded
        s = 1
        while s < counts.shape[0]:
            end = end + jnp.where(row >= s, pltpu.roll(end, s, 0), 0.0)
            s *= 2
        pstart_scr[...] = end - padded
        pend_ref[...] = end.astype(I32)
        carry_scr[...] = jnp.zeros_like(carry_scr)

    l = lg_ref[...]
    ne, tr = l.shape
    e_iota = lax.broadcasted_iota(I32, (ne, tr), 0)
    chosen = jnp.zeros((ne, tr), F32)
    vals, sels = [], []
    for j in range(TOP_K):
        m = jnp.max(l, axis=0, keepdims=True)
        idx = jnp.min(jnp.where(l == m, e_iota, ne), axis=0, keepdims=True)
        sel = e_iota == idx
        vals.append(m)
        sels.append(sel)
        chosen = jnp.where(sel, 1.0, chosen)
        l = jnp.where(sel, -jnp.inf, l)
    ex = [jnp.exp(vj - vals[0]) for vj in vals]
    den = ex[0] + ex[1] + ex[2] + ex[3]
    for j in range(TOP_K):
        gate_ref[j:j + 1, :] = ex[j] / den

    ti = lax.broadcasted_iota(I32, (tr, tr), 0)
    tj = lax.broadcasted_iota(I32, (tr, tr), 1)
    upper = jnp.where(ti <= tj, 1.0, 0.0).astype(BF16)
    inc = _dot(chosen.astype(BF16), upper)
    carry = carry_scr[...]
    first_row = carry[:, 0:1] + pstart_scr[:, 0:1]
    row_of = inc - chosen + first_row
    for j in range(TOP_K):
        pj = jnp.sum(jnp.where(sels[j], row_of, 0.0), axis=0, keepdims=True)
        pos_ref[j:j + 1, :] = pj.astype(I32)
    carry_scr[...] = carry + inc[:, tr - 1:tr]

    tc = TC_COMBINE
    erow = lax.broadcasted_iota(I32, (ne, 128), 0)
    for s in range(tr // tc):
        cols = slice(s * tc, (s + 1) * tc)
        before = inc[:, s * tc - 1:s * tc] if s else jnp.zeros((ne, 1), F32)
        n_run = inc[:, (s + 1) * tc - 1:(s + 1) * tc] - before
        win_rows = jnp.broadcast_to(jnp.ceil(n_run * (1.0 / WIN_COMBINE)) * WIN_COMBINE, (ne, 128))
        base = win_rows
        k = 1
        while k < ne:
            base = base + jnp.where(erow >= k, pltpu.roll(base, k, 0), 0.0)
            k *= 2
        base = base - win_rows
        slot_of = (inc - chosen)[:, cols] - before + base[:, 0:1]
        for j in range(TOP_K):
            lj = jnp.sum(jnp.where(sels[j][:, cols], slot_of, 0.0), axis=0, keepdims=True)
            li_ref[j:j + 1, cols] = lj.astype(I32)
        rstart_ref[s * ne:(s + 1) * ne, :] = jnp.broadcast_to(first_row + before, (ne, 128)).astype(I32)
        rlen_ref[s * ne:(s + 1) * ne, :] = jnp.broadcast_to(n_run, (ne, 128)).astype(I32)


def _route(logits_t):
    ne, t = logits_t.shape
    tr = TR_ROUTE
    tok = pl.BlockSpec((TOP_K, tr), lambda ph, i: (0, i * ph))
    n_sub = tr // TC_COMBINE
    runs = pl.BlockSpec((n_sub * ne, 128), lambda ph, i: (i * ph, 0))
    runs_shape = jax.ShapeDtypeStruct((t // TC_COMBINE * ne, 128), I32)
    return pl.pallas_call(
        _route_kernel,
        grid=(2, t // tr),
        in_specs=[pl.BlockSpec((ne, tr), lambda ph, i: (0, i))],
        out_specs=[tok, tok, pl.BlockSpec((ne, 128), lambda ph, i: (0, 0)), tok, runs, runs],
        out_shape=[
            jax.ShapeDtypeStruct((TOP_K, t), I32),
            jax.ShapeDtypeStruct((TOP_K, t), F32),
            jax.ShapeDtypeStruct((ne, 128), I32),
            jax.ShapeDtypeStruct((TOP_K, t), I32),
            runs_shape, runs_shape,
        ],
        scratch_shapes=[pltpu.VMEM((ne, 128), F32), pltpu.VMEM((ne, 128), F32)],
        compiler_params=_params(("arbitrary", "arbitrary")),
        name="route",
    )(logits_t)


def _dispatch_kernel(pend_ref, nused_ref, *refs):
    pos_refs = refs[:TOP_K]
    hn_ref, xin_ref, zero_scr, sem = refs[TOP_K:]
    bm = zero_scr.shape[0] // SUBLANES
    td = pos_refs[0].shape[0]
    i = pl.program_id(0)

    def tile(ref, token):
        return ref.at[pl.ds(pl.multiple_of(token * SUBLANES, SUBLANES), SUBLANES)]

    def zero_block(row0):
        start = pl.multiple_of(row0 * SUBLANES, bm * SUBLANES)
        return pltpu.make_async_copy(zero_scr, xin_ref.at[pl.ds(start, bm * SUBLANES)], sem)

    @pl.when(i == 0)
    def _():
        zero_scr[...] = jnp.zeros_like(zero_scr)

        def has_rows(e):
            end = pend_ref[e]
            return jnp.where(e == 0, end >= bm, end > pend_ref[jnp.maximum(e - 1, 0)])

        def start_one(e, carry):
            @pl.when(has_rows(e))
            def _():
                zero_block(pend_ref[e] - bm).start()
            return carry

        def wait_one(e, carry):
            @pl.when(has_rows(e))
            def _():
                zero_block(0).wait()
            return carry

        def start_tail(blk, carry):
            zero_block(blk * bm).start()
            return carry

        def wait_tail(blk, carry):
            zero_block(0).wait()
            return carry

        n_blocks = xin_ref.shape[0] // (bm * SUBLANES)
        lax.fori_loop(0, pend_ref.shape[0], start_one, 0)
        lax.fori_loop(nused_ref[0], n_blocks, start_tail, 0)
        lax.fori_loop(0, pend_ref.shape[0], wait_one, 0)
        lax.fori_loop(nused_ref[0], n_blocks, wait_tail, 0)

    def issue(tk, carry):
        for j in range(TOP_K):
            pltpu.make_async_copy(tile(hn_ref, tk), tile(xin_ref, pos_refs[j][tk]),
                                  sem).start(priority=j % 2)
        return carry

    lax.fori_loop(0, td, issue, 0, unroll=8)

    for j in range(TOP_K):
        pltpu.make_async_copy(hn_ref, xin_ref.at[pl.ds(0, td * SUBLANES)], sem).wait()


def _choice_specs(n_steps, tile):
    return [pl.BlockSpec((tile,), functools.partial(lambda j, i, *_: (j * n_steps + i,), j),
                         memory_space=pltpu.SMEM) for j in range(TOP_K)]


def _dispatch(pad_end, n_used, pos, hn_tiles, n_rows):
    t = hn_tiles.shape[0] // SUBLANES
    td = TD_DISPATCH
    pos = pos.reshape(-1)
    grid_spec = pltpu.PrefetchScalarGridSpec(
        num_scalar_prefetch=2,
        grid=(t // td,),
        in_specs=_choice_specs(t // td, td) + [
            pl.BlockSpec((td * SUBLANES, LANES), lambda i, pe, nu: (i, 0)),
        ],
        out_specs=pl.BlockSpec(memory_space=pl.ANY),
        scratch_shapes=[pltpu.VMEM((BM_MOE * SUBLANES, LANES), F32), pltpu.SemaphoreType.DMA(())],
    )
    return pl.pallas_call(
        _dispatch_kernel,
        grid_spec=grid_spec,
        out_shape=jax.ShapeDtypeStruct((n_rows * SUBLANES, LANES), F32),
        compiler_params=_params(("arbitrary",)),
        name="dispatch",
    )(pad_end, n_used, *([pos] * TOP_K), hn_tiles)


def _experts_kernel(pend_ref, nused_ref, w1_ref, b1_ref, w2_ref, b2_ref, xin_ref, yout_ref,
                    xbuf, obuf, wt_scr, wg_scr, wl_scr, w2_scr, sem_in, sem_out):
    e = pl.program_id(0)
    rows_blk = xbuf.shape[1]
    bm = rows_blk // SUBLANES
    dh = w2_ref.shape[1]
    n_used = nused_ref[0]
    blk_lo = jnp.where(e == 0, 0, pend_ref[jnp.maximum(e - 1, 0)]) // bm
    blk_hi = pend_ref[e] // bm

    def block(ref, blk):
        return ref.at[pl.ds(pl.multiple_of(blk * rows_blk, rows_blk), rows_blk)]

    def x_copy(blk, slot):
        return pltpu.make_async_copy(block(xin_ref, blk), xbuf.at[slot], sem_in.at[slot])

    def o_copy(blk, slot):
        return pltpu.make_async_copy(obuf.at[slot], block(yout_ref, blk), sem_out.at[slot])

    @pl.when((e == 0) & (n_used > 0))
    def _():
        x_copy(0, 0).start()

    @pl.when(blk_hi > blk_lo)
    def _():
        n_slab, n_t, lanes = wt_scr.shape
        for part in range(2 * dh // n_t):
            rows = slice(part * (n_t // 2), (part + 1) * (n_t // 2))
            for s in range(n_slab):
                cols = slice(s * lanes, (s + 1) * lanes)
                wt_scr[s] = w1_ref[0, cols, part * n_t:(part + 1) * n_t].T
                wg_scr[rows, cols] = wt_scr[s, pl.ds(0, n_t // 2, stride=2), :].astype(BF16)
                wl_scr[rows, cols] = wt_scr[s, pl.ds(1, n_t // 2, stride=2), :].astype(BF16)
        w2_scr[...] = w2_ref[0].astype(BF16)

        def one_block(blk, carry):
            slot = blk % 2

            @pl.when(blk + 1 < n_used)
            def _():
                x_copy(blk + 1, 1 - slot).start()

            x_copy(blk, slot).wait()

            @pl.when(blk >= 2)
            def _():
                o_copy(blk - 2, slot).wait()

            xb = _load_token_tiles(xbuf.at[slot], 0, bm).astype(BF16)
            bias = b1_ref[0]
            glu = jnp.minimum(_dot_nt(xb, wg_scr[...]) + bias[:, :dh], SWIGLU_LIMIT)
            lin = jnp.clip(_dot_nt(xb, wl_scr[...]) + bias[:, dh:], -SWIGLU_LIMIT, SWIGLU_LIMIT)
            act = glu * jax.nn.sigmoid(SWIGLU_ALPHA * glu) * (lin + 1.0)
            _store_token_tiles(obuf.at[slot], _dot(act.astype(BF16), w2_scr[...]) + b2_ref[0])
            o_copy(blk, slot).start()
            return carry

        lax.fori_loop(blk_lo, blk_hi, one_block, 0)

    @pl.when(e == pl.num_programs(0) - 1)
    def _():
        @pl.when(n_used >= 1)
        def _():
            o_copy(0, (n_used - 1) % 2).wait()

        @pl.when(n_used >= 2)
        def _():
            o_copy(0, n_used % 2).wait()

        obuf[0] = jnp.zeros(obuf.shape[1:], obuf.dtype)
        n_blocks = yout_ref.shape[0] // rows_blk

        def start_tail(blk, carry):
            o_copy(blk, 0).start()
            return carry

        def wait_tail(blk, carry):
            o_copy(0, 0).wait()
            return carry

        lax.fori_loop(n_used, n_blocks, start_tail, 0)
        lax.fori_loop(n_used, n_blocks, wait_tail, 0)


def _experts(pad_end, n_used, xin, w1, b1p, w2, b2):
    ne, d, dh2 = w1.shape
    dh = dh2 // 2
    rows_blk = BM_MOE * SUBLANES
    grid_spec = pltpu.PrefetchScalarGridSpec(
        num_scalar_prefetch=2,
        grid=(ne,),
        in_specs=[
            pl.BlockSpec((1, d, dh2), lambda e, pe, nu: (e, 0, 0)),
            pl.BlockSpec((1, 1, dh2), lambda e, pe, nu: (e, 0, 0)),
            pl.BlockSpec((1, dh, d), lambda e, pe, nu: (e, 0, 0)),
            pl.BlockSpec((1, 1, d), lambda e, pe, nu: (e, 0, 0)),
            pl.BlockSpec(memory_space=pl.ANY),
        ],
        out_specs=pl.BlockSpec(memory_space=pl.ANY),
        scratch_shapes=[
            pltpu.VMEM((2, rows_blk, LANES), F32),
            pltpu.VMEM((2, rows_blk, LANES), F32),
            pltpu.VMEM((d // 128, dh, 128), F32),
            pltpu.VMEM((dh, d), BF16),
            pltpu.VMEM((dh, d), BF16),
            pltpu.VMEM((dh, d), BF16),
            pltpu.SemaphoreType.DMA((2,)),
            pltpu.SemaphoreType.DMA((2,)),
        ],
    )
    return pl.pallas_call(
        _experts_kernel,
        grid_spec=grid_spec,
        out_shape=jax.ShapeDtypeStruct(xin.shape, F32),
        compiler_params=_params(("arbitrary",)),
        name="experts",
    )(pad_end, n_used, w1, b1p, w2, b2, xin)


def _combine_kernel(*refs):
    rstart_ref, rlen_ref, rstart_next_ref, rlen_next_ref = refs[:4]
    li_refs = refs[4:4 + TOP_K]
    gate_refs = refs[4 + TOP_K:4 + 2 * TOP_K]
    h1_ref, fg_ref, yout_ref, o_ref, wbuf, hacc, sem = refs[4 + 2 * TOP_K:]
    tc = o_ref.shape[0]
    i = pl.program_id(0)
    slot = i % 2
    n_experts = rstart_ref.shape[0]
    win_rows = WIN_COMBINE * SUBLANES

    def tile(ref, token):
        return ref.at[pl.ds(pl.multiple_of(token * SUBLANES, SUBLANES), SUBLANES)]

    def for_windows(rs_ref, rl_ref, s, fn):
        def per_expert(e, base):
            start = rs_ref[e, 0]
            n_win = (rl_ref[e, 0] + (WIN_COMBINE - 1)) // WIN_COMBINE

            def per_window(w, carry):
                src = yout_ref.at[pl.ds(pl.multiple_of((start + w * WIN_COMBINE) * SUBLANES, SUBLANES),
                                        win_rows)]
                dst = wbuf.at[s, pl.ds(pl.multiple_of((base + w) * win_rows, win_rows), win_rows)]
                fn(pltpu.make_async_copy(src, dst, sem.at[s]))
                return carry

            lax.fori_loop(0, n_win, per_window, 0)
            return base + n_win

        lax.fori_loop(0, n_experts, per_expert, 0)

    @pl.when(i == 0)
    def _():
        for_windows(rstart_ref, rlen_ref, 0, lambda cp: cp.start())

    @pl.when(i + 1 < pl.num_programs(0))
    def _():
        for_windows(rstart_next_ref, rlen_next_ref, 1 - slot, lambda cp: cp.start())

    for_windows(rstart_ref, rlen_ref, slot, lambda cp: cp.wait())

    def per_token(tk, carry):
        acc = tile(h1_ref, tk)[...]
        for j in range(TOP_K):
            acc = acc + gate_refs[j][tk] * tile(wbuf.at[slot], li_refs[j][tk])[...]
        tile(hacc, tk)[...] = acc
        return carry

    lax.fori_loop(0, tc, per_token, 0, unroll=8)
    h = _load_token_tiles(hacc, 0, tc)
    o_ref[...] = h * lax.rsqrt(jnp.mean(h * h, axis=-1, keepdims=True) + RMS_EPS) * fg_ref[...]


def _combine(rstart, rlen, li, gates, h1_tiles, fg, yout):
    t = h1_tiles.shape[0] // SUBLANES
    d = fg.shape[1]
    tc = TC_COMBINE
    n_steps = t // tc
    n_experts = rstart.shape[0] // n_steps
    buf_rows = TOP_K * tc + n_experts * WIN_COMBINE
    runs = pl.BlockSpec((n_experts, 128), lambda i: (i, 0), memory_space=pltpu.SMEM)
    runs_next = pl.BlockSpec((n_experts, 128), lambda i: (jnp.minimum(i + 1, n_steps - 1), 0),
                             memory_space=pltpu.SMEM)
    return pl.pallas_call(
        _combine_kernel,
        grid=(n_steps,),
        in_specs=[runs, runs, runs_next, runs_next]
        + _choice_specs(n_steps, tc) + _choice_specs(n_steps, tc) + [
            pl.BlockSpec((tc * SUBLANES, LANES), lambda i: (i, 0)),
            pl.BlockSpec((1, d), lambda i: (0, 0)),
            pl.BlockSpec(memory_space=pl.ANY),
        ],
        out_specs=pl.BlockSpec((tc, d), lambda i: (i, 0)),
        out_shape=jax.ShapeDtypeStruct((t, d), F32),
        scratch_shapes=[pltpu.VMEM((2, buf_rows * SUBLANES, LANES), F32),
                        pltpu.VMEM((tc * SUBLANES, LANES), F32),
                        pltpu.SemaphoreType.DMA((2,))],
        compiler_params=_params(("arbitrary",)),
        name="combine",
    )(rstart, rlen, rstart, rlen, *([li.reshape(-1)] * TOP_K), *([gates.reshape(-1)] * TOP_K),
      h1_tiles, fg, yout)


def kernel(x, norm1_g, w_in, sgu_ln_g, sgu_ln_b, sgu_w, sgu_b, mu_rkv, mu_wag, decay_w0, decay_w1,
           decay_w2, iclr_a0, iclr_a1, iclr_a2, gate_g1, gate_g2, k_k, k_a, r_k, lnx_g, lnx_b, w_out,
           norm2_g, router_w, router_b, moe_w1, moe_b1, moe_w2, moe_b2, final_g):
    batch, seq, d = x.shape
    t = batch * seq
    depth = w_in.shape[0]
    sgu_width = sgu_ln_g.shape[1]
    rw = mu_rkv.shape[2]
    n_dec, n_icl, n_gate = decay_w1.shape[2], iclr_a1.shape[2], gate_g1.shape[2]
    assert n_dec == HEAD and n_icl == HEAD and n_gate == PAIR and rw % PAIR == 0
    assert seq % TT_RWKV == 0 and seq % TT_PREP == 0 and t % TM_PROJ == 0
    assert sgu_w.shape[2] == SGU_BLOCK
    assert depth == 1, "the final RMSNorm is fused into the last layer's combine kernel"
    assert d == SUBLANES * LANES, "the MoE row movers copy one (8, 128) f32 tile per token"

    h = x.reshape(t, d)
    for l in range(depth):
        win_bf = w_in[l].astype(BF16)
        wl = jnp.concatenate([decay_w1[l], iclr_a1[l], gate_g1[l]], axis=1)
        mucat = jnp.concatenate([
            jnp.broadcast_to(mu_wag[l, 0][:, None], (d, n_dec)),
            jnp.broadcast_to(mu_wag[l, 1][:, None], (d, n_icl)),
            jnp.broadcast_to(mu_wag[l, 2][:, None], (d, n_gate))], axis=1)
        zeros = jnp.zeros((HEAD, rw), F32)
        w2pad = jnp.concatenate([decay_w2[l], zeros], axis=0)
        a2pad = jnp.concatenate([zeros, iclr_a2[l]], axis=0)
        pvec = jnp.stack([decay_w0[l], iclr_a0[l], k_k[l], k_a[l], r_k[l].reshape(-1),
                          lnx_g[l], lnx_b[l], jnp.zeros((rw,), F32)], axis=0)
        bias2d = jnp.repeat(sgu_b[l].T, sgu_width // sgu_b.shape[1], axis=1)
        wo_bf = w_out[l].astype(BF16)
        b1p = jnp.concatenate([moe_b1[l][:, 0::2], moe_b1[l][:, 1::2]], axis=-1)[:, None, :]
        b2 = moe_b2[l][:, None, :]

        assert w_in.shape[2] == 2 * sgu_width + 3 * rw
        proj, lora, ya = _in_proj(h, norm1_g[l][None, :], win_bf, wl, mucat, sgu_ln_g[l][None, :],
                                  sgu_ln_b[l][None, :], sgu_w[l], bias2d)
        rp, yq, bonus, gate, g_m, h_m = _rwkv_prep(proj, lora, mu_rkv[l], pvec, w2pad, a2pad,
                                                   gate_g2[l], seq, 0)
        yb = _rwkv_scan(rp, yq, bonus, gate, g_m, h_m, pvec, batch)
        h1, hn2, logits_t = _out_proj(h, ya, yb, wo_bf, norm2_g[l][None, :], router_w[l].T,
                                      router_b[l][:, None])
        pos, gates, pend, li, rstart, rlen = _route(logits_t)

        pad_end = pend[:, 0]
        n_blocks = (t * TOP_K) // BM_MOE + N_EXPERTS
        n_used = pad_end[-1:] // BM_MOE

        xin = _dispatch(pad_end, n_used, pos, hn2, n_blocks * BM_MOE)
        yout = _experts(pad_end, n_used, xin, moe_w1[l], b1p, moe_w2[l], b2)
        h = _combine(rstart, rlen, li, gates, h1, final_g[None, :], yout)
    return h.reshape(batch, seq, d)
```

```python
import functools

import jax
import jax.numpy as jnp
from jax import lax
from jax.experimental import pallas as pl
from jax.experimental.pallas import tpu as pltpu

F32 = jnp.float32
BF16 = jnp.bfloat16
I32 = jnp.int32

RMS_EPS = 1e-5
LN_EPS = 1e-5
LNX_EPS = 64e-5
CHUNK = 64
SGU_BLOCK = 128
HEAD = 64
PAIR = 2 * HEAD
N_EXPERTS = 32
TOP_K = 4
SWIGLU_ALPHA = 1.702
SWIGLU_LIMIT = 7.0

V7X_VMEM_LIMIT = 56 * 1024 * 1024

TM_PROJ = 1024
TT_PREP = 1024
N_SUB_PREP = 1
TT_RWKV = 512
TR_ROUTE = 1024
BM_MOE = 256
TD_DISPATCH = 1024
TC_COMBINE = 512
WIN_COMBINE = 64


def _dot(a, b):
    return jnp.dot(a, b, preferred_element_type=F32)


def _dot_nt(a, b):
    return lax.dot_general(a, b, (((1,), (1,)), ((), ())), preferred_element_type=F32)


def _split(x):
    hi = x.astype(BF16)
    lo = (x - hi.astype(F32)).astype(BF16)
    return hi, lo


def _dot3(a, b):
    ah, al = _split(a)
    bh, bl = _split(b)
    return _dot(ah, bh) + _dot(al, bh) + _dot(ah, bl)


def _dot3_nt(a, b):
    ah, al = _split(a)
    bh, bl = _split(b)
    return _dot_nt(ah, bh) + _dot_nt(al, bh) + _dot_nt(ah, bl)


def _head_sum(x, bd):
    hi, lo = _split(x)
    return _dot(hi, bd) + _dot(lo, bd)


SUBLANES = 8
LANES = 128


def _store_token_tiles(ref, x):
    n = x.shape[0]
    for s in range(SUBLANES):
        ref[pl.ds(s, n, stride=SUBLANES), :] = x[:, s * LANES:(s + 1) * LANES]


def _load_token_tiles(ref, row0, n):
    return jnp.concatenate(
        [ref[pl.ds(row0 + s, n, stride=SUBLANES), :] for s in range(SUBLANES)], axis=1)


def _params(sem, vmem=V7X_VMEM_LIMIT):
    return pltpu.CompilerParams(dimension_semantics=sem, vmem_limit_bytes=vmem)


def _in_proj_kernel(x_ref, g_ref, win_ref, wl_ref, mucat_ref, lng_ref, lnb_ref, w_ref, bias_ref,
                    proj_ref, lora_ref, o_ref, wl_scr, wm_scr):
    n_heads = w_ref.shape[0]

    @pl.when(pl.program_id(0) == 0)
    def _():
        wl = wl_ref[...]
        mu = mucat_ref[...]
        n = wl.shape[1]
        wl_scr[:, 0:n] = (wl * (1.0 - mu)).astype(BF16)
        wl_scr[:, n:2 * n] = (wl * mu).astype(BF16)
        qi = lax.broadcasted_iota(I32, (SGU_BLOCK, SGU_BLOCK), 0) // CHUNK
        kj = lax.broadcasted_iota(I32, (SGU_BLOCK, SGU_BLOCK), 1) // CHUNK
        for h in range(n_heads):
            wm_scr[h] = jnp.where(kj <= qi, w_ref[h], 0.0).astype(BF16)

    x = x_ref[...]
    hn = x * lax.rsqrt(jnp.mean(x * x, axis=-1, keepdims=True) + RMS_EPS) * g_ref[...]
    hb = hn.astype(BF16)
    width = lng_ref.shape[1]
    two_w = 2 * width
    proj_ref[...] = _dot(hb, win_ref[:, two_w:])
    lora_ref[...] = _dot(hb, wl_scr[...])

    z = _dot(hb, win_ref[:, :two_w])
    tm = z.shape[0]
    gz = 0.5 * z * (1.0 + lax.erf(z * (2.0 ** -0.5)))
    u = gz[:, :width]
    v = gz[:, width:]
    mu = jnp.mean(v, axis=-1, keepdims=True)
    vc = v - mu
    var = jnp.mean(vc * vc, axis=-1, keepdims=True)
    vn = vc * lax.rsqrt(var + LN_EPS) * lng_ref[...] + lnb_ref[...]
    lane = lax.broadcasted_iota(I32, (SGU_BLOCK, PAIR), 1)
    is_lo = lane < HEAD
    bias = bias_ref[...]
    for blk in range(tm // SGU_BLOCK):
        rows = slice(blk * SGU_BLOCK, (blk + 1) * SGU_BLOCK)
        for p in range(width // PAIR):
            cols = slice(p * PAIR, (p + 1) * PAIR)
            vp = vn[rows, cols]
            lo = jnp.where(is_lo, vp, 0.0).astype(BF16)
            hi = jnp.where(is_lo, 0.0, vp).astype(BF16)
            sv = _dot(wm_scr[2 * p], lo) + _dot(wm_scr[2 * p + 1], hi)
            o_ref[rows, cols] = u[rows, cols] * (sv + bias[:, cols])


def _in_proj(x2, g, win_bf, wl, mucat, ln_g, ln_b, w_s, bias2d):
    t, d = x2.shape
    width = ln_g.shape[1]
    n_rkv = win_bf.shape[1] - 2 * width
    n_l = wl.shape[1]
    n_heads = w_s.shape[0]
    tm = TM_PROJ
    const = lambda *shape: pl.BlockSpec(shape, lambda i: (0,) * len(shape))
    row = lambda n: pl.BlockSpec((tm, n), lambda i: (i, 0))
    return pl.pallas_call(
        _in_proj_kernel,
        grid=(t // tm,),
        in_specs=[
            row(d), const(1, d), const(d, win_bf.shape[1]), const(d, n_l), const(d, n_l),
            const(1, width), const(1, width), const(n_heads, SGU_BLOCK, SGU_BLOCK),
            const(SGU_BLOCK, width),
        ],
        out_specs=[row(n_rkv), row(2 * n_l), row(width)],
        out_shape=[
            jax.ShapeDtypeStruct((t, n_rkv), F32),
            jax.ShapeDtypeStruct((t, 2 * n_l), F32),
            jax.ShapeDtypeStruct((t, width), F32),
        ],
        scratch_shapes=[pltpu.VMEM((d, 2 * n_l), BF16),
                        pltpu.VMEM((n_heads, SGU_BLOCK, SGU_BLOCK), BF16)],
        compiler_params=_params(("arbitrary",)),
        name="in_proj_sgu",
    )(x2, g, win_bf, wl, mucat, ln_g, ln_b, w_s, bias2d)


def _rwkv_prep_kernel(seq_len, pr_ref, pk_ref, pv_ref, ppr_ref, ppk_ref, ppv_ref, lo_ref, plo_ref,
                      mu_ref, pvec_ref, w2_ref, a2_ref, g2_ref,
                      rp_ref, yq_ref, bonus_ref, gate_ref, g_ref, h_ref):
    tt = pr_ref.shape[0]
    ts = tt // N_SUB_PREP
    n_chunks = ts // CHUNK
    i = pl.program_id(0)
    keep = jnp.where((i * tt) % seq_len == 0, 0.0, 1.0)

    mu = mu_ref[...]
    pvec = pvec_ref[...]
    w0, a0, k_k, k_a, r_k = pvec[0:1], pvec[1:2], pvec[2:3], pvec[3:4], pvec[4:5]
    w2b = w2_ref[...].astype(BF16)
    a2b = a2_ref[...].astype(BF16)
    g2b = g2_ref[...].astype(BF16)
    half = lo_ref.shape[1] // 2

    lane_r = lax.broadcasted_iota(I32, (PAIR, PAIR), 0)
    lane_c = lax.broadcasted_iota(I32, (PAIR, PAIR), 1)
    bd = jnp.where((lane_r // HEAD) == (lane_c // HEAD), 1.0, 0.0).astype(BF16)
    tri_ones = jnp.where(((lane_r // CHUNK) == (lane_c // CHUNK)) & (lane_c <= lane_r),
                         1.0, 0.0).astype(BF16)
    lane = lax.broadcasted_iota(I32, (CHUNK, PAIR), 1)
    is_lo = lane < HEAD

    def stack(x):
        return jnp.concatenate([jnp.where(is_lo, x, 0.0), jnp.where(is_lo, 0.0, x)], axis=0)

    n2 = 2 * PAIR
    ri = lax.broadcasted_iota(I32, (n2, n2), 0)
    ci = lax.broadcasted_iota(I32, (n2, n2), 1)
    same_head = ((ri // CHUNK) % 2) == ((ci // CHUNK) % 2)
    t_i = ri % CHUNK
    s_i = ci % CHUNK
    tri = same_head & (s_i < t_i + ri // PAIR)
    eye = lane_r == lane_c

    for sub in range(N_SUB_PREP):
        r0 = sub * ts
        tile_rows = slice(r0, r0 + ts)

        def shift(ref, pref, cols=slice(None)):
            x = ref[tile_rows, cols]
            if sub == 0:
                first = pref[7:8, cols] * keep
            else:
                first = ref[r0 - 1:r0, cols]
            rowc = lax.broadcasted_iota(I32, x.shape, 0)
            return x, jnp.where(rowc == 0, first, pltpu.roll(x, 1, 0))

        def shift_mix(ref, pref, m):
            x, xs = shift(ref, pref)
            return x + (xs - x) * m

        r = shift_mix(pr_ref, ppr_ref, mu[0:1])
        k = shift_mix(pk_ref, ppk_ref, mu[1:2])
        v = shift_mix(pv_ref, ppv_ref, mu[2:3])

        _, lo_b = shift(lo_ref, plo_ref, slice(half, 2 * half))
        l_all = lo_ref[tile_rows, :half] + lo_b
        l_wa = l_all[:, :PAIR]
        l_g = l_all[:, PAIR:]

        dw = _dot(jnp.tanh(l_wa).astype(BF16), w2b)
        ia = _dot(l_wa.astype(BF16), a2b)
        gate_ref[tile_rows, :] = _dot(jax.nn.sigmoid(l_g).astype(BF16), g2b).astype(BF16)

        zneg = -(w0 + dw)
        softplus = jnp.maximum(zneg, 0.0) + jnp.log(1.0 + jnp.exp(-jnp.abs(zneg)))
        logw = -jnp.exp(-softplus - 0.5)
        iclr = jax.nn.sigmoid(a0 + ia)

        kk = k * k_k
        kk = kk / jnp.maximum(jnp.sqrt(_head_sum(kk * kk, bd)), 1e-12)
        k2 = k * (1.0 + (iclr - 1.0) * k_a)
        a = -kk
        b = kk * iclr
        bonus_ref[tile_rows, :] = (_head_sum(r * k2 * r_k, bd) * v).astype(BF16)

        lw_hi, lw_lo = _split(logw)
        cl = jnp.concatenate(
            [_dot(tri_ones, lw_hi[q * PAIR:(q + 1) * PAIR]) + _dot(tri_ones, lw_lo[q * PAIR:(q + 1) * PAIR])
             for q in range(ts // PAIR)], axis=0)
        at_all = a * jnp.exp(cl - logw)
        rt_all = r * jnp.exp(cl)
        w_inv = jnp.exp(-cl)
        bt_all = b * w_inv
        kt_all = k2 * w_inv

        chunks = range(n_chunks)
        rows = [slice(c * CHUNK, (c + 1) * CHUNK) for c in chunks]
        out_rows = [slice(r0 + c * CHUNK, r0 + (c + 1) * CHUNK) for c in chunks]
        last = [cl[rows[c]][CHUNK - 1:CHUNK, :] for c in chunks]
        w_rem = [jnp.exp(last[c] - cl[rows[c]]) for c in chunks]
        a_s = [stack(at_all[rows[c]]) for c in chunks]
        r_s = [stack(rt_all[rows[c]]) for c in chunks]
        v_s = [stack(v[rows[c]]) for c in chunks]
        v_sb = [v_s[c].astype(BF16) for c in chunks]
        bk_h = [jnp.concatenate([stack(b[rows[c]] * w_rem[c]), stack(k2[rows[c]] * w_rem[c])],
                                axis=0).astype(BF16) for c in chunks]

        a_all = []
        for c in chunks:
            lhs = jnp.concatenate([a_s[c], r_s[c]], axis=0).astype(BF16)
            rhs = jnp.concatenate([stack(bt_all[rows[c]]), stack(kt_all[rows[c]])],
                                  axis=0).astype(BF16)
            a_all.append(jnp.where(tri, _dot_nt(lhs, rhs), 0.0))
        n_k = [a_all[c][:PAIR, :PAIR].astype(BF16) for c in chunks]
        a_r = [a_all[c][PAIR:, :].astype(BF16) for c in chunks]
        x = [jnp.concatenate([a_s[c], _dot(a_all[c][:PAIR, PAIR:].astype(BF16), v_sb[c])], axis=1)
             for c in chunks]
        steps = CHUNK.bit_length() - 1
        for it in range(steps):
            x = [x[c] + _dot(n_k[c], x[c].astype(BF16)) for c in chunks]
            if it + 1 < steps:
                n_k = [_dot(n_k[c], n_k[c]).astype(BF16) for c in chunks]
        zero_b = jnp.zeros((PAIR, PAIR), BF16)
        ry = [_dot(a_r[c], jnp.concatenate(
            [x[c].astype(BF16), jnp.concatenate([zero_b, v_sb[c]], axis=1)], axis=0)) for c in chunks]
        for c in chunks:
            r_c = rt_all[rows[c]]
            rp_ref[out_rows[c], :] = (ry[c][:CHUNK, :PAIR] + ry[c][CHUNK:, :PAIR] + r_c).astype(BF16)
            yq_ref[out_rows[c], :] = (ry[c][:CHUNK, PAIR:] + ry[c][CHUNK:, PAIR:]).astype(BF16)
        p_t = [x[c][:, :PAIR].T.astype(BF16) for c in chunks]
        qv_t = [jnp.concatenate([x[c][:, PAIR:], v_s[c]], axis=0).T.astype(BF16) for c in chunks]
        c0 = sub * n_chunks
        for c in chunks:
            g_ref[c0 + c, 0] = (jnp.where(eye, jnp.exp(last[c]), 0.0)
                                + _dot(p_t[c], bk_h[c][:PAIR]))
        for c in chunks:
            hbd = _dot(qv_t[c], bk_h[c])
            h_ref[c0 + c, 0] = hbd[:HEAD] + hbd[HEAD:]


def _rwkv_prep(proj, lora, mu_rkv, pvec, w2pad, a2pad, g2, seq_len, col0):
    t = proj.shape[0]
    width = mu_rkv.shape[1]
    n_pairs = width // PAIR
    tt = TT_PREP
    nl = lora.shape[1]
    cb = col0 // PAIR
    wb = width // PAIR

    def prev_rows(i):
        return jnp.maximum(i * (tt // 8) - 1, 0)

    def tok(c):
        return pl.BlockSpec((tt, PAIR), lambda i, p: (i, c + p))

    def prev(c):
        return pl.BlockSpec((8, PAIR), lambda i, p: (prev_rows(i), c + p))

    out_tok = pl.BlockSpec((tt, PAIR), lambda i, p: (i, p))
    tok_shape = jax.ShapeDtypeStruct((t, width), BF16)
    return pl.pallas_call(
        functools.partial(_rwkv_prep_kernel, seq_len),
        grid=(t // tt, n_pairs),
        in_specs=[
            tok(cb), tok(cb + wb), tok(cb + 2 * wb),
            prev(cb), prev(cb + wb), prev(cb + 2 * wb),
            pl.BlockSpec((tt, nl), lambda i, p: (i, 0)),
            pl.BlockSpec((8, nl), lambda i, p: (prev_rows(i), 0)),
            pl.BlockSpec((3, PAIR), lambda i, p: (0, p)),
            pl.BlockSpec((8, PAIR), lambda i, p: (0, p)),
            pl.BlockSpec((PAIR, PAIR), lambda i, p: (0, p)),
            pl.BlockSpec((PAIR, PAIR), lambda i, p: (0, p)),
            pl.BlockSpec((PAIR, PAIR), lambda i, p: (0, p)),
        ],
        out_specs=[
            out_tok, out_tok, out_tok, out_tok,
            pl.BlockSpec((tt // CHUNK, 1, PAIR, PAIR), lambda i, p: (i, p, 0, 0)),
            pl.BlockSpec((tt // CHUNK, 1, HEAD, PAIR), lambda i, p: (i, p, 0, 0)),
        ],
        out_shape=[
            tok_shape, tok_shape, tok_shape, tok_shape,
            jax.ShapeDtypeStruct((t // CHUNK, n_pairs, PAIR, PAIR), F32),
            jax.ShapeDtypeStruct((t // CHUNK, n_pairs, HEAD, PAIR), F32),
        ],
        compiler_params=_params(("arbitrary", "arbitrary")),
        name="rwkv_prep",
    )(proj, proj, proj, proj, proj, proj, lora, lora, mu_rkv, pvec, w2pad, a2pad, g2)


def _rwkv_scan_kernel(rp_ref, yq_ref, bonus_ref, gate_ref, g_ref, h_ref, pvec_ref, o_ref, s_scr, y_scr):
    nb, tt, width = rp_ref.shape
    n_pairs = width // PAIR

    @pl.when(pl.program_id(0) == 0)
    def _():
        s_scr[...] = jnp.zeros_like(s_scr)

    lane = lax.broadcasted_iota(I32, (HEAD, PAIR), 1)
    is_lo = lane < HEAD
    lane_r = lax.broadcasted_iota(I32, (PAIR, PAIR), 0)
    lane_c = lax.broadcasted_iota(I32, (PAIR, PAIR), 1)
    bd = jnp.where((lane_r // HEAD) == (lane_c // HEAD), 1.0, 0.0).astype(BF16)

    chains = [(b, p) for b in range(nb) for p in range(n_pairs)]
    state = {ch: s_scr[ch[0], ch[1]] for ch in chains}
    for c in range(tt // CHUNK):
        rows = slice(c * CHUNK, (c + 1) * CHUNK)
        for (b, p) in chains:
            cols = slice(p * PAIR, (p + 1) * PAIR)
            s0 = state[(b, p)]
            s_st = jnp.concatenate([jnp.where(is_lo, s0, 0.0), jnp.where(is_lo, 0.0, s0)], axis=0)
            y_scr[b, rows, cols] = (_dot_nt(rp_ref[b, rows, cols], s_st.astype(BF16))
                                    + yq_ref[b, rows, cols].astype(F32))
        state = {(b, p): _dot3(state[(b, p)], g_ref[b, c, p]) + h_ref[b, c, p] for (b, p) in chains}
    for (b, p) in chains:
        s_scr[b, p] = state[(b, p)]

    pvec = pvec_ref[...]
    for (b, p) in chains:
        cols = slice(p * PAIR, (p + 1) * PAIR)
        y = y_scr[b, :, cols]
        mu = _head_sum(y, bd) * (1.0 / HEAD)
        yc = y - mu
        var = _head_sum(yc * yc, bd) * (1.0 / HEAD)
        yn = yc * lax.rsqrt(var + LNX_EPS) * pvec[5:6, cols] + pvec[6:7, cols]
        o_ref[b, :, cols] = ((yn + bonus_ref[b, :, cols].astype(F32))
                             * gate_ref[b, :, cols].astype(F32))


def _rwkv_scan(rp, yq, bonus, gate, g, h, pvec, batch):
    t, width = rp.shape
    n_pairs = width // PAIR
    seq = t // batch
    tt = TT_RWKV
    nc = tt // CHUNK
    tok = pl.BlockSpec((batch, tt, width), lambda i: (0, i, 0))
    as_seq = lambda z: z.reshape(batch, seq, width)
    out = pl.pallas_call(
        _rwkv_scan_kernel,
        grid=(seq // tt,),
        in_specs=[
            tok, tok, tok, tok,
            pl.BlockSpec((batch, nc, n_pairs, PAIR, PAIR), lambda i: (0, i, 0, 0, 0)),
            pl.BlockSpec((batch, nc, n_pairs, HEAD, PAIR), lambda i: (0, i, 0, 0, 0)),
            pl.BlockSpec((8, width), lambda i: (0, 0)),
        ],
        out_specs=tok,
        out_shape=jax.ShapeDtypeStruct((batch, seq, width), F32),
        scratch_shapes=[pltpu.VMEM((batch, n_pairs, HEAD, PAIR), F32),
                        pltpu.VMEM((batch, tt, width), F32)],
        compiler_params=_params(("arbitrary",)),
        name="rwkv_scan",
    )(as_seq(rp), as_seq(yq), as_seq(bonus), as_seq(gate),
      g.reshape(batch, seq // CHUNK, n_pairs, PAIR, PAIR),
      h.reshape(batch, seq // CHUNK, n_pairs, HEAD, PAIR), pvec)
    return out.reshape(t, width)


def _out_proj_kernel(x_ref, ya_ref, yb_ref, wo_ref, g_ref, rwt_ref, rb_ref, h1_ref, hn_ref, lg_ref):
    wa = ya_ref.shape[1]
    h1 = (x_ref[...] + _dot(ya_ref[...].astype(BF16), wo_ref[0:wa, :])
          + _dot(yb_ref[...].astype(BF16), wo_ref[wa:, :]))
    h1_ref[...] = h1
    hn = h1 * lax.rsqrt(jnp.mean(h1 * h1, axis=-1, keepdims=True) + RMS_EPS) * g_ref[...]
    _store_token_tiles(hn_ref, hn)
    lg_ref[...] = _dot3_nt(rwt_ref[...], hn) + rb_ref[...]


def _out_proj(x2, ya, yb, wo_bf, g2, rwt, rb):
    t, d = x2.shape
    wa = ya.shape[1]
    wb = yb.shape[1]
    ne = rwt.shape[0]
    tm = TM_PROJ
    return pl.pallas_call(
        _out_proj_kernel,
        grid=(t // tm,),
        in_specs=[
            pl.BlockSpec((tm, d), lambda i: (i, 0)),
            pl.BlockSpec((tm, wa), lambda i: (i, 0)),
            pl.BlockSpec((tm, wb), lambda i: (i, 0)),
            pl.BlockSpec((wa + wb, d), lambda i: (0, 0)),
            pl.BlockSpec((1, d), lambda i: (0, 0)),
            pl.BlockSpec((ne, d), lambda i: (0, 0)),
            pl.BlockSpec((ne, 1), lambda i: (0, 0)),
        ],
        out_specs=[
            pl.BlockSpec((tm, d), lambda i: (i, 0)),
            pl.BlockSpec((tm * SUBLANES, LANES), lambda i: (i, 0)),
            pl.BlockSpec((ne, tm), lambda i: (0, i)),
        ],
        out_shape=[
            jax.ShapeDtypeStruct((t, d), F32),
            jax.ShapeDtypeStruct((t * SUBLANES, LANES), F32),
            jax.ShapeDtypeStruct((ne, t), F32),
        ],
        compiler_params=_params(("arbitrary",)),
        name="out_proj",
    )(x2, ya, yb, wo_bf, g2, rwt, rb)


def _route_kernel(lg_ref, pos_ref, gate_ref, pend_ref, li_ref, rstart_ref, rlen_ref,
                  carry_scr, pstart_scr):
    phase = pl.program_id(0)
    first = pl.program_id(1) == 0

    @pl.when(first & (phase == 0))
    def _():
        carry_scr[...] = jnp.zeros_like(carry_scr)
        pstart_scr[...] = jnp.zeros_like(pstart_scr)
        pend_ref[...] = jnp.zeros_like(pend_ref)

    @pl.when(first & (phase == 1))
    def _():
        counts = carry_scr[...]
        padded = jnp.ceil(counts * (1.0 / BM_MOE)) * BM_MOE
        row = lax.broadcasted_iota(I32, counts.shape, 0)
        end = padded
        s = 1
        while s < counts.shape[0]:
            end = end + jnp.where(row >= s, pltpu.roll(end, s, 0), 0.0)
            s *= 2
        pstart_scr[...] = end - padded
        pend_ref[...] = end.astype(I32)
        carry_scr[...] = jnp.zeros_like(carry_scr)

    l = lg_ref[...]
    ne, tr = l.shape
    e_iota = lax.broadcasted_iota(I32, (ne, tr), 0)
    chosen = jnp.zeros((ne, tr), F32)
    vals, sels = [], []
    for j in range(TOP_K):
        m = jnp.max(l, axis=0, keepdims=True)
        idx = jnp.min(jnp.where(l == m, e_iota, ne), axis=0, keepdims=True)
        sel = e_iota == idx
        vals.append(m)
        sels.append(sel)
        chosen = jnp.where(sel, 1.0, chosen)
        l = jnp.where(sel, -jnp.inf, l)
    ex = [jnp.exp(vj - vals[0]) for vj in vals]
    den = ex[0] + ex[1] + ex[2] + ex[3]
    for j in range(TOP_K):
        gate_ref[j:j + 1, :] = ex[j] / den

    ti = lax.broadcasted_iota(I32, (tr, tr), 0)
    tj = lax.broadcasted_iota(I32, (tr, tr), 1)
    upper = jnp.where(ti <= tj, 1.0, 0.0).astype(BF16)
    inc = _dot(chosen.astype(BF16), upper)
    carry = carry_scr[...]
    first_row = carry[:, 0:1] + pstart_scr[:, 0:1]
    row_of = inc - chosen + first_row
    for j in range(TOP_K):
        pj = jnp.sum(jnp.where(sels[j], row_of, 0.0), axis=0, keepdims=True)
        pos_ref[j:j + 1, :] = pj.astype(I32)
    carry_scr[...] = carry + inc[:, tr - 1:tr]

    tc = TC_COMBINE
    erow = lax.broadcasted_iota(I32, (ne, 128), 0)
    for s in range(tr // tc):
        cols = slice(s * tc, (s + 1) * tc)
        before = inc[:, s * tc - 1:s * tc] if s else jnp.zeros((ne, 1), F32)
        n_run = inc[:, (s + 1) * tc - 1:(s + 1) * tc] - before
        win_rows = jnp.broadcast_to(jnp.ceil(n_run * (1.0 / WIN_COMBINE)) * WIN_COMBINE, (ne, 128))
        base = win_rows
        k = 1
        while k < ne:
            base = base + jnp.where(erow >= k, pltpu.roll(base, k, 0), 0.0)
            k *= 2
        base = base - win_rows
        slot_of = (inc - chosen)[:, cols] - before + base[:, 0:1]
        for j in range(TOP_K):
            lj = jnp.sum(jnp.where(sels[j][:, cols], slot_of, 0.0), axis=0, keepdims=True)
            li_ref[j:j + 1, cols] = lj.astype(I32)
        rstart_ref[s * ne:(s + 1) * ne, :] = jnp.broadcast_to(first_row + before, (ne, 128)).astype(I32)
        rlen_ref[s * ne:(s + 1) * ne, :] = jnp.broadcast_to(n_run, (ne, 128)).astype(I32)


def _route(logits_t):
    ne, t = logits_t.shape
    tr = TR_ROUTE
    tok = pl.BlockSpec((TOP_K, tr), lambda ph, i: (0, i * ph))
    n_sub = tr // TC_COMBINE
    runs = pl.BlockSpec((n_sub * ne, 128), lambda ph, i: (i * ph, 0))
    runs_shape = jax.ShapeDtypeStruct((t // TC_COMBINE * ne, 128), I32)
    return pl.pallas_call(
        _route_kernel,
        grid=(2, t // tr),
        in_specs=[pl.BlockSpec((ne, tr), lambda ph, i: (0, i))],
        out_specs=[tok, tok, pl.BlockSpec((ne, 128), lambda ph, i: (0, 0)), tok, runs, runs],
        out_shape=[
            jax.ShapeDtypeStruct((TOP_K, t), I32),
            jax.ShapeDtypeStruct((TOP_K, t), F32),
            jax.ShapeDtypeStruct((ne, 128), I32),
            jax.ShapeDtypeStruct((TOP_K, t), I32),
            runs_shape, runs_shape,
        ],
        scratch_shapes=[pltpu.VMEM((ne, 128), F32), pltpu.VMEM((ne, 128), F32)],
        compiler_params=_params(("arbitrary", "arbitrary")),
        name="route",
    )(logits_t)


def _dispatch_kernel(pend_ref, nused_ref, *refs):
    pos_refs = refs[:TOP_K]
    hn_ref, xin_ref, zero_scr, sem = refs[TOP_K:]
    bm = zero_scr.shape[0] // SUBLANES
    td = pos_refs[0].shape[0]
    i = pl.program_id(0)

    def tile(ref, token):
        return ref.at[pl.ds(pl.multiple_of(token * SUBLANES, SUBLANES), SUBLANES)]

    def zero_block(row0):
        start = pl.multiple_of(row0 * SUBLANES, bm * SUBLANES)
        return pltpu.make_async_copy(zero_scr, xin_ref.at[pl.ds(start, bm * SUBLANES)], sem)

    @pl.when(i == 0)
    def _():
        zero_scr[...] = jnp.zeros_like(zero_scr)

        def has_rows(e):
            end = pend_ref[e]
            return jnp.where(e == 0, end >= bm, end > pend_ref[jnp.maximum(e - 1, 0)])

        def start_one(e, carry):
            @pl.when(has_rows(e))
            def _():
                zero_block(pend_ref[e] - bm).start()
            return carry

        def wait_one(e, carry):
            @pl.when(has_rows(e))
            def _():
                zero_block(0).wait()
            return carry

        def start_tail(blk, carry):
            zero_block(blk * bm).start()
            return carry

        def wait_tail(blk, carry):
            zero_block(0).wait()
            return carry

        n_blocks = xin_ref.shape[0] // (bm * SUBLANES)
        lax.fori_loop(0, pend_ref.shape[0], start_one, 0)
        lax.fori_loop(nused_ref[0], n_blocks, start_tail, 0)
        lax.fori_loop(0, pend_ref.shape[0], wait_one, 0)
        lax.fori_loop(nused_ref[0], n_blocks, wait_tail, 0)

    def issue(tk, carry):
        for j in range(TOP_K):
            pltpu.make_async_copy(tile(hn_ref, tk), tile(xin_ref, pos_refs[j][tk]),
                                  sem).start(priority=j % 2)
        return carry

    lax.fori_loop(0, td, issue, 0, unroll=8)

    for j in range(TOP_K):
        pltpu.make_async_copy(hn_ref, xin_ref.at[pl.ds(0, td * SUBLANES)], sem).wait()


def _choice_specs(n_steps, tile):
    return [pl.BlockSpec((tile,), functools.partial(lambda j, i, *_: (j * n_steps + i,), j),
                         memory_space=pltpu.SMEM) for j in range(TOP_K)]


def _dispatch(pad_end, n_used, pos, hn_tiles, n_rows):
    t = hn_tiles.shape[0] // SUBLANES
    td = TD_DISPATCH
    pos = pos.reshape(-1)
    grid_spec = pltpu.PrefetchScalarGridSpec(
        num_scalar_prefetch=2,
        grid=(t // td,),
        in_specs=_choice_specs(t // td, td) + [
            pl.BlockSpec((td * SUBLANES, LANES), lambda i, pe, nu: (i, 0)),
        ],
        out_specs=pl.BlockSpec(memory_space=pl.ANY),
        scratch_shapes=[pltpu.VMEM((BM_MOE * SUBLANES, LANES), F32), pltpu.SemaphoreType.DMA(())],
    )
    return pl.pallas_call(
        _dispatch_kernel,
        grid_spec=grid_spec,
        out_shape=jax.ShapeDtypeStruct((n_rows * SUBLANES, LANES), F32),
        compiler_params=_params(("arbitrary",)),
        name="dispatch",
    )(pad_end, n_used, *([pos] * TOP_K), hn_tiles)


def _experts_kernel(pend_ref, nused_ref, w1_ref, b1_ref, w2_ref, b2_ref, xin_ref, yout_ref,
                    xbuf, obuf, wt_scr, wg_scr, wl_scr, w2_scr, sem_in, sem_out):
    e = pl.program_id(0)
    rows_blk = xbuf.shape[1]
    bm = rows_blk // SUBLANES
    dh = w2_ref.shape[1]
    n_used = nused_ref[0]
    blk_lo = jnp.where(e == 0, 0, pend_ref[jnp.maximum(e - 1, 0)]) // bm
    blk_hi = pend_ref[e] // bm

    def block(ref, blk):
        return ref.at[pl.ds(pl.multiple_of(blk * rows_blk, rows_blk), rows_blk)]

    def x_copy(blk, slot):
        return pltpu.make_async_copy(block(xin_ref, blk), xbuf.at[slot], sem_in.at[slot])

    def o_copy(blk, slot):
        return pltpu.make_async_copy(obuf.at[slot], block(yout_ref, blk), sem_out.at[slot])

    @pl.when((e == 0) & (n_used > 0))
    def _():
        x_copy(0, 0).start()

    @pl.when(blk_hi > blk_lo)
    def _():
        n_slab, n_t, lanes = wt_scr.shape
        for part in range(2 * dh // n_t):
            rows = slice(part * (n_t // 2), (part + 1) * (n_t // 2))
            for s in range(n_slab):
                cols = slice(s * lanes, (s + 1) * lanes)
                wt_scr[s] = w1_ref[0, cols, part * n_t:(part + 1) * n_t].T
                wg_scr[rows, cols] = wt_scr[s, pl.ds(0, n_t // 2, stride=2), :].astype(BF16)
                wl_scr[rows, cols] = wt_scr[s, pl.ds(1, n_t // 2, stride=2), :].astype(BF16)
        w2_scr[...] = w2_ref[0].astype(BF16)

        def one_block(blk, carry):
            slot = blk % 2

            @pl.when(blk + 1 < n_used)
            def _():
                x_copy(blk + 1, 1 - slot).start()

            x_copy(blk, slot).wait()

            @pl.when(blk >= 2)
            def _():
                o_copy(blk - 2, slot).wait()

            xb = _load_token_tiles(xbuf.at[slot], 0, bm).astype(BF16)
            bias = b1_ref[0]
            glu = jnp.minimum(_dot_nt(xb, wg_scr[...]) + bias[:, :dh], SWIGLU_LIMIT)
            lin = jnp.clip(_dot_nt(xb, wl_scr[...]) + bias[:, dh:], -SWIGLU_LIMIT, SWIGLU_LIMIT)
            act = glu * jax.nn.sigmoid(SWIGLU_ALPHA * glu) * (lin + 1.0)
            _store_token_tiles(obuf.at[slot], _dot(act.astype(BF16), w2_scr[...]) + b2_ref[0])
            o_copy(blk, slot).start()
            return carry

        lax.fori_loop(blk_lo, blk_hi, one_block, 0)

    @pl.when(e == pl.num_programs(0) - 1)
    def _():
        @pl.when(n_used >= 1)
        def _():
            o_copy(0, (n_used - 1) % 2).wait()

        @pl.when(n_used >= 2)
        def _():
            o_copy(0, n_used % 2).wait()

        obuf[0] = jnp.zeros(obuf.shape[1:], obuf.dtype)
        n_blocks = yout_ref.shape[0] // rows_blk

        def start_tail(blk, carry):
            o_copy(blk, 0).start()
            return carry

        def wait_tail(blk, carry):
            o_copy(0, 0).wait()
            return carry

        lax.fori_loop(n_used, n_blocks, start_tail, 0)
        lax.fori_loop(n_used, n_blocks, wait_tail, 0)


def _experts(pad_end, n_used, xin, w1, b1p, w2, b2):
    ne, d, dh2 = w1.shape
    dh = dh2 // 2
    rows_blk = BM_MOE * SUBLANES
    grid_spec = pltpu.PrefetchScalarGridSpec(
        num_scalar_prefetch=2,
        grid=(ne,),
        in_specs=[
            pl.BlockSpec((1, d, dh2), lambda e, pe, nu: (e, 0, 0)),
            pl.BlockSpec((1, 1, dh2), lambda e, pe, nu: (e, 0, 0)),
            pl.BlockSpec((1, dh, d), lambda e, pe, nu: (e, 0, 0)),
            pl.BlockSpec((1, 1, d), lambda e, pe, nu: (e, 0, 0)),
            pl.BlockSpec(memory_space=pl.ANY),
        ],
        out_specs=pl.BlockSpec(memory_space=pl.ANY),
        scratch_shapes=[
            pltpu.VMEM((2, rows_blk, LANES), F32),
            pltpu.VMEM((2, rows_blk, LANES), F32),
            pltpu.VMEM((d // 128, dh, 128), F32),
            pltpu.VMEM((dh, d), BF16),
            pltpu.VMEM((dh, d), BF16),
            pltpu.VMEM((dh, d), BF16),
            pltpu.SemaphoreType.DMA((2,)),
            pltpu.SemaphoreType.DMA((2,)),
        ],
    )
    return pl.pallas_call(
        _experts_kernel,
        grid_spec=grid_spec,
        out_shape=jax.ShapeDtypeStruct(xin.shape, F32),
        compiler_params=_params(("arbitrary",)),
        name="experts",
    )(pad_end, n_used, w1, b1p, w2, b2, xin)


def _combine_kernel(*refs):
    rstart_ref, rlen_ref, rstart_next_ref, rlen_next_ref = refs[:4]
    li_refs = refs[4:4 + TOP_K]
    gate_refs = refs[4 + TOP_K:4 + 2 * TOP_K]
    h1_ref, fg_ref, yout_ref, o_ref, wbuf, hacc, sem = refs[4 + 2 * TOP_K:]
    tc = o_ref.shape[0]
    i = pl.program_id(0)
    slot = i % 2
    n_experts = rstart_ref.shape[0]
    win_rows = WIN_COMBINE * SUBLANES

    def tile(ref, token):
        return ref.at[pl.ds(pl.multiple_of(token * SUBLANES, SUBLANES), SUBLANES)]

    def for_windows(rs_ref, rl_ref, s, fn):
        def per_expert(e, base):
            start = rs_ref[e, 0]
            n_win = (rl_ref[e, 0] + (WIN_COMBINE - 1)) // WIN_COMBINE

            def per_window(w, carry):
                src = yout_ref.at[pl.ds(pl.multiple_of((start + w * WIN_COMBINE) * SUBLANES, SUBLANES),
                                        win_rows)]
                dst = wbuf.at[s, pl.ds(pl.multiple_of((base + w) * win_rows, win_rows), win_rows)]
                fn(pltpu.make_async_copy(src, dst, sem.at[s]))
                return carry

            lax.fori_loop(0, n_win, per_window, 0)
            return base + n_win

        lax.fori_loop(0, n_experts, per_expert, 0)

    @pl.when(i == 0)
    def _():
        for_windows(rstart_ref, rlen_ref, 0, lambda cp: cp.start())

    @pl.when(i + 1 < pl.num_programs(0))
    def _():
        for_windows(rstart_next_ref, rlen_next_ref, 1 - slot, lambda cp: cp.start())

    for_windows(rstart_ref, rlen_ref, slot, lambda cp: cp.wait())

    def per_token(tk, carry):
        acc = gate_refs[0][tk] * tile(wbuf.at[slot], li_refs[0][tk])[...]
        for j in range(1, TOP_K):
            acc = acc + gate_refs[j][tk] * tile(wbuf.at[slot], li_refs[j][tk])[...]
        tile(hacc, tk)[...] = acc
        return carry

    lax.fori_loop(0, tc, per_token, 0, unroll=8)
    h = h1_ref[...] + _load_token_tiles(hacc, 0, tc)
    o_ref[...] = h * lax.rsqrt(jnp.mean(h * h, axis=-1, keepdims=True) + RMS_EPS) * fg_ref[...]


def _combine(rstart, rlen, li, gates, h1, fg, yout):
    t, d = h1.shape
    tc = TC_COMBINE
    n_steps = t // tc
    n_experts = rstart.shape[0] // n_steps
    buf_rows = TOP_K * tc + n_experts * WIN_COMBINE
    runs = pl.BlockSpec((n_experts, 128), lambda i: (i, 0), memory_space=pltpu.SMEM)
    runs_next = pl.BlockSpec((n_experts, 128), lambda i: (jnp.minimum(i + 1, n_steps - 1), 0),
                             memory_space=pltpu.SMEM)
    return pl.pallas_call(
        _combine_kernel,
        grid=(n_steps,),
        in_specs=[runs, runs, runs_next, runs_next]
        + _choice_specs(n_steps, tc) + _choice_specs(n_steps, tc) + [
            pl.BlockSpec((tc, d), lambda i: (i, 0)),
            pl.BlockSpec((1, d), lambda i: (0, 0)),
            pl.BlockSpec(memory_space=pl.ANY),
        ],
        out_specs=pl.BlockSpec((tc, d), lambda i: (i, 0)),
        out_shape=jax.ShapeDtypeStruct((t, d), F32),
        scratch_shapes=[pltpu.VMEM((2, buf_rows * SUBLANES, LANES), F32),
                        pltpu.VMEM((tc * SUBLANES, LANES), F32),
                        pltpu.SemaphoreType.DMA((2,))],
        compiler_params=_params(("arbitrary",)),
        name="combine",
    )(rstart, rlen, rstart, rlen, *([li.reshape(-1)] * TOP_K), *([gates.reshape(-1)] * TOP_K),
      h1, fg, yout)


def kernel(x, norm1_g, w_in, sgu_ln_g, sgu_ln_b, sgu_w, sgu_b, mu_rkv, mu_wag, decay_w0, decay_w1,
           decay_w2, iclr_a0, iclr_a1, iclr_a2, gate_g1, gate_g2, k_k, k_a, r_k, lnx_g, lnx_b, w_out,
           norm2_g, router_w, router_b, moe_w1, moe_b1, moe_w2, moe_b2, final_g):
    batch, seq, d = x.shape
    t = batch * seq
    depth = w_in.shape[0]
    sgu_width = sgu_ln_g.shape[1]
    rw = mu_rkv.shape[2]
    n_dec, n_icl, n_gate = decay_w1.shape[2], iclr_a1.shape[2], gate_g1.shape[2]
    assert n_dec == HEAD and n_icl == HEAD and n_gate == PAIR and rw % PAIR == 0
    assert seq % TT_RWKV == 0 and seq % TT_PREP == 0 and t % TM_PROJ == 0
    assert sgu_w.shape[2] == SGU_BLOCK
    assert depth == 1, "the final RMSNorm is fused into the last layer's combine kernel"
    assert d == SUBLANES * LANES, "the MoE row movers copy one (8, 128) f32 tile per token"
    assert BM_MOE + N_EXPERTS - 1 >= WIN_COMBINE

    h = x.reshape(t, d)
    for l in range(depth):
        win_bf = w_in[l].astype(BF16)
        wl = jnp.concatenate([decay_w1[l], iclr_a1[l], gate_g1[l]], axis=1)
        mucat = jnp.concatenate([
            jnp.broadcast_to(mu_wag[l, 0][:, None], (d, n_dec)),
            jnp.broadcast_to(mu_wag[l, 1][:, None], (d, n_icl)),
            jnp.broadcast_to(mu_wag[l, 2][:, None], (d, n_gate))], axis=1)
        zeros = jnp.zeros((HEAD, rw), F32)
        w2pad = jnp.concatenate([decay_w2[l], zeros], axis=0)
        a2pad = jnp.concatenate([zeros, iclr_a2[l]], axis=0)
        pvec = jnp.stack([decay_w0[l], iclr_a0[l], k_k[l], k_a[l], r_k[l].reshape(-1),
                          lnx_g[l], lnx_b[l], jnp.zeros((rw,), F32)], axis=0)
        bias2d = jnp.repeat(sgu_b[l].T, sgu_width // sgu_b.shape[1], axis=1)
        wo_bf = w_out[l].astype(BF16)
        b1p = jnp.concatenate([moe_b1[l][:, 0::2], moe_b1[l][:, 1::2]], axis=-1)[:, None, :]
        b2 = moe_b2[l][:, None, :]

        assert w_in.shape[2] == 2 * sgu_width + 3 * rw
        proj, lora, ya = _in_proj(h, norm1_g[l][None, :], win_bf, wl, mucat, sgu_ln_g[l][None, :],
                                  sgu_ln_b[l][None, :], sgu_w[l], bias2d)
        rp, yq, bonus, gate, g_m, h_m = _rwkv_prep(proj, lora, mu_rkv[l], pvec, w2pad, a2pad,
                                                   gate_g2[l], seq, 0)
        yb = _rwkv_scan(rp, yq, bonus, gate, g_m, h_m, pvec, batch)
        h1, hn2, logits_t = _out_proj(h, ya, yb, wo_bf, norm2_g[l][None, :], router_w[l].T,
                                      router_b[l][:, None])
        pos, gates, pend, li, rstart, rlen = _route(logits_t)

        pad_end = pend[:, 0]
        n_blocks = (t * TOP_K) // BM_MOE + N_EXPERTS
        n_used = pad_end[-1:] // BM_MOE

        xin = _dispatch(pad_end, n_used, pos, hn2, n_blocks * BM_MOE)
        yout = _experts(pad_end, n_used, xin, moe_w1[l], b1p, moe_w2[l], b2)
        h = _combine(rstart, rlen, li, gates, h1, final_g[None, :], yout)
    return h.reshape(batch, seq, d)
```

```python
import functools

import jax
import jax.numpy as jnp
from jax import lax
from jax.experimental import pallas as pl
from jax.experimental.pallas import tpu as pltpu

F32 = jnp.float32
BF16 = jnp.bfloat16
I32 = jnp.int32

RMS_EPS = 1e-5
LN_EPS = 1e-5
LNX_EPS = 64e-5
CHUNK = 64
SGU_BLOCK = 128
HEAD = 64
PAIR = 2 * HEAD
N_EXPERTS = 32
TOP_K = 4
SWIGLU_ALPHA = 1.702
SWIGLU_LIMIT = 7.0

V7X_VMEM_LIMIT = 56 * 1024 * 1024

TM_PROJ = 1024
TT_PREP = 1024
N_SUB_PREP = 1
TT_RWKV = 512
TR_ROUTE = 1024
BM_MOE = 256
TD_DISPATCH = 1024
TC_COMBINE = 512
WIN_COMBINE = 64


def _dot(a, b):
    return jnp.dot(a, b, preferred_element_type=F32)


def _dot_nt(a, b):
    return lax.dot_general(a, b, (((1,), (1,)), ((), ())), preferred_element_type=F32)


def _split(x):
    hi = x.astype(BF16)
    lo = (x - hi.astype(F32)).astype(BF16)
    return hi, lo


def _dot3(a, b):
    ah, al = _split(a)
    bh, bl = _split(b)
    return _dot(ah, bh) + _dot(al, bh) + _dot(ah, bl)


def _dot3_nt(a, b):
    ah, al = _split(a)
    bh, bl = _split(b)
    return _dot_nt(ah, bh) + _dot_nt(al, bh) + _dot_nt(ah, bl)


def _head_sum(x, bd):
    hi, lo = _split(x)
    return _dot(hi, bd) + _dot(lo, bd)


SUBLANES = 8
LANES = 128


def _store_token_tiles(ref, x):
    n = x.shape[0]
    for s in range(SUBLANES):
        ref[pl.ds(s, n, stride=SUBLANES), :] = x[:, s * LANES:(s + 1) * LANES]


def _load_token_tiles(ref, row0, n):
    return jnp.concatenate(
        [ref[pl.ds(row0 + s, n, stride=SUBLANES), :] for s in range(SUBLANES)], axis=1)


def _params(sem, vmem=V7X_VMEM_LIMIT):
    return pltpu.CompilerParams(dimension_semantics=sem, vmem_limit_bytes=vmem)


def _in_proj_kernel(x_ref, g_ref, win_ref, wl_ref, mucat_ref, lng_ref, lnb_ref, w_ref, bias_ref,
                    proj_ref, lora_ref, o_ref, wl_scr, wm_scr):
    n_heads = w_ref.shape[0]

    @pl.when(pl.program_id(0) == 0)
    def _():
        wl = wl_ref[...]
        mu = mucat_ref[...]
        n = wl.shape[1]
        wl_scr[:, 0:n] = (wl * (1.0 - mu)).astype(BF16)
        wl_scr[:, n:2 * n] = (wl * mu).astype(BF16)
        qi = lax.broadcasted_iota(I32, (SGU_BLOCK, SGU_BLOCK), 0) // CHUNK
        kj = lax.broadcasted_iota(I32, (SGU_BLOCK, SGU_BLOCK), 1) // CHUNK
        for h in range(n_heads):
            wm_scr[h] = jnp.where(kj <= qi, w_ref[h], 0.0).astype(BF16)

    x = x_ref[...]
    hn = x * lax.rsqrt(jnp.mean(x * x, axis=-1, keepdims=True) + RMS_EPS) * g_ref[...]
    hb = hn.astype(BF16)
    width = lng_ref.shape[1]
    two_w = 2 * width
    proj_ref[...] = _dot(hb, win_ref[:, two_w:])
    lora_ref[...] = _dot(hb, wl_scr[...])

    z = _dot(hb, win_ref[:, :two_w])
    tm = z.shape[0]
    gz = 0.5 * z * (1.0 + lax.erf(z * (2.0 ** -0.5)))
    u = gz[:, :width]
    v = gz[:, width:]
    mu = jnp.mean(v, axis=-1, keepdims=True)
    vc = v - mu
    var = jnp.mean(vc * vc, axis=-1, keepdims=True)
    vn = vc * lax.rsqrt(var + LN_EPS) * lng_ref[...] + lnb_ref[...]
    lane = lax.broadcasted_iota(I32, (SGU_BLOCK, PAIR), 1)
    is_lo = lane < HEAD
    bias = bias_ref[...]
    for blk in range(tm // SGU_BLOCK):
        rows = slice(blk * SGU_BLOCK, (blk + 1) * SGU_BLOCK)
        for p in range(width // PAIR):
            cols = slice(p * PAIR, (p + 1) * PAIR)
            vp = vn[rows, cols]
            lo = jnp.where(is_lo, vp, 0.0).astype(BF16)
            hi = jnp.where(is_lo, 0.0, vp).astype(BF16)
            sv = _dot(wm_scr[2 * p], lo) + _dot(wm_scr[2 * p + 1], hi)
            o_ref[rows, cols] = u[rows, cols] * (sv + bias[:, cols])


def _in_proj(x2, g, win_bf, wl, mucat, ln_g, ln_b, w_s, bias2d):
    t, d = x2.shape
    width = ln_g.shape[1]
    n_rkv = win_bf.shape[1] - 2 * width
    n_l = wl.shape[1]
    n_heads = w_s.shape[0]
    tm = TM_PROJ
    const = lambda *shape: pl.BlockSpec(shape, lambda i: (0,) * len(shape))
    row = lambda n: pl.BlockSpec((tm, n), lambda i: (i, 0))
    return pl.pallas_call(
        _in_proj_kernel,
        grid=(t // tm,),
        in_specs=[
            row(d), const(1, d), const(d, win_bf.shape[1]), const(d, n_l), const(d, n_l),
            const(1, width), const(1, width), const(n_heads, SGU_BLOCK, SGU_BLOCK),
            const(SGU_BLOCK, width),
        ],
        out_specs=[row(n_rkv), row(2 * n_l), row(width)],
        out_shape=[
            jax.ShapeDtypeStruct((t, n_rkv), F32),
            jax.ShapeDtypeStruct((t, 2 * n_l), F32),
            jax.ShapeDtypeStruct((t, width), F32),
        ],
        scratch_shapes=[pltpu.VMEM((d, 2 * n_l), BF16),
                        pltpu.VMEM((n_heads, SGU_BLOCK, SGU_BLOCK), BF16)],
        compiler_params=_params(("arbitrary",)),
        name="in_proj_sgu",
    )(x2, g, win_bf, wl, mucat, ln_g, ln_b, w_s, bias2d)


def _rwkv_prep_kernel(seq_len, pr_ref, pk_ref, pv_ref, ppr_ref, ppk_ref, ppv_ref, lo_ref, plo_ref,
                      mu_ref, pvec_ref, w2_ref, a2_ref, g2_ref,
                      rp_ref, yq_ref, bonus_ref, gate_ref, g_ref, h_ref):
    tt = pr_ref.shape[0]
    ts = tt // N_SUB_PREP
    n_chunks = ts // CHUNK
    i = pl.program_id(0)
    keep = jnp.where((i * tt) % seq_len == 0, 0.0, 1.0)

    mu = mu_ref[...]
    pvec = pvec_ref[...]
    w0, a0, k_k, k_a, r_k = pvec[0:1], pvec[1:2], pvec[2:3], pvec[3:4], pvec[4:5]
    w2b = w2_ref[...].astype(BF16)
    a2b = a2_ref[...].astype(BF16)
    g2b = g2_ref[...].astype(BF16)
    half = lo_ref.shape[1] // 2

    lane_r = lax.broadcasted_iota(I32, (PAIR, PAIR), 0)
    lane_c = lax.broadcasted_iota(I32, (PAIR, PAIR), 1)
    bd = jnp.where((lane_r // HEAD) == (lane_c // HEAD), 1.0, 0.0).astype(BF16)
    tri_ones = jnp.where(((lane_r // CHUNK) == (lane_c // CHUNK)) & (lane_c <= lane_r),
                         1.0, 0.0).astype(BF16)
    lane = lax.broadcasted_iota(I32, (CHUNK, PAIR), 1)
    is_lo = lane < HEAD

    def stack(x):
        return jnp.concatenate([jnp.where(is_lo, x, 0.0), jnp.where(is_lo, 0.0, x)], axis=0)

    n2 = 2 * PAIR
    ri = lax.broadcasted_iota(I32, (n2, n2), 0)
    ci = lax.broadcasted_iota(I32, (n2, n2), 1)
    same_head = ((ri // CHUNK) % 2) == ((ci // CHUNK) % 2)
    t_i = ri % CHUNK
    s_i = ci % CHUNK
    tri = same_head & (s_i < t_i + ri // PAIR)
    eye = lane_r == lane_c

    for sub in range(N_SUB_PREP):
        r0 = sub * ts
        tile_rows = slice(r0, r0 + ts)

        def shift(ref, pref, cols=slice(None)):
            x = ref[tile_rows, cols]
            if sub == 0:
                first = pref[7:8, cols] * keep
            else:
                first = ref[r0 - 1:r0, cols]
            rowc = lax.broadcasted_iota(I32, x.shape, 0)
            return x, jnp.where(rowc == 0, first, pltpu.roll(x, 1, 0))

        def shift_mix(ref, pref, m):
            x, xs = shift(ref, pref)
            return x + (xs - x) * m

        r = shift_mix(pr_ref, ppr_ref, mu[0:1])
        k = shift_mix(pk_ref, ppk_ref, mu[1:2])
        v = shift_mix(pv_ref, ppv_ref, mu[2:3])

        _, lo_b = shift(lo_ref, plo_ref, slice(half, 2 * half))
        l_all = lo_ref[tile_rows, :half] + lo_b
        l_wa = l_all[:, :PAIR]
        l_g = l_all[:, PAIR:]

        dw = _dot(jnp.tanh(l_wa).astype(BF16), w2b)
        ia = _dot(l_wa.astype(BF16), a2b)
        gate_ref[tile_rows, :] = _dot(jax.nn.sigmoid(l_g).astype(BF16), g2b).astype(BF16)

        zneg = -(w0 + dw)
        softplus = jnp.maximum(zneg, 0.0) + jnp.log(1.0 + jnp.exp(-jnp.abs(zneg)))
        logw = -jnp.exp(-softplus - 0.5)
        iclr = jax.nn.sigmoid(a0 + ia)

        kk = k * k_k
        kk = kk / jnp.maximum(jnp.sqrt(_head_sum(kk * kk, bd)), 1e-12)
        k2 = k * (1.0 + (iclr - 1.0) * k_a)
        a = -kk
        b = kk * iclr
        bonus_ref[tile_rows, :] = (_head_sum(r * k2 * r_k, bd) * v).astype(BF16)

        lw_hi, lw_lo = _split(logw)
        cl = jnp.concatenate(
            [_dot(tri_ones, lw_hi[q * PAIR:(q + 1) * PAIR]) + _dot(tri_ones, lw_lo[q * PAIR:(q + 1) * PAIR])
             for q in range(ts // PAIR)], axis=0)
        at_all = a * jnp.exp(cl - logw)
        rt_all = r * jnp.exp(cl)
        w_inv = jnp.exp(-cl)
        bt_all = b * w_inv
        kt_all = k2 * w_inv

        chunks = range(n_chunks)
        rows = [slice(c * CHUNK, (c + 1) * CHUNK) for c in chunks]
        out_rows = [slice(r0 + c * CHUNK, r0 + (c + 1) * CHUNK) for c in chunks]
        last = [cl[rows[c]][CHUNK - 1:CHUNK, :] for c in chunks]
        w_rem = [jnp.exp(last[c] - cl[rows[c]]) for c in chunks]
        a_s = [stack(at_all[rows[c]]) for c in chunks]
        r_s = [stack(rt_all[rows[c]]) for c in chunks]
        v_s = [stack(v[rows[c]]) for c in chunks]
        v_sb = [v_s[c].astype(BF16) for c in chunks]
        bk_h = [jnp.concatenate([stack(b[rows[c]] * w_rem[c]), stack(k2[rows[c]] * w_rem[c])],
                                axis=0).astype(BF16) for c in chunks]

        a_all = []
        for c in chunks:
            lhs = jnp.concatenate([a_s[c], r_s[c]], axis=0).astype(BF16)
            rhs = jnp.concatenate([stack(bt_all[rows[c]]), stack(kt_all[rows[c]])],
                                  axis=0).astype(BF16)
            a_all.append(jnp.where(tri, _dot_nt(lhs, rhs), 0.0))
        n_k = [a_all[c][:PAIR, :PAIR].astype(BF16) for c in chunks]
        a_r = [a_all[c][PAIR:, :].astype(BF16) for c in chunks]
        x = [jnp.concatenate([a_s[c], _dot(a_all[c][:PAIR, PAIR:].astype(BF16), v_sb[c])], axis=1)
             for c in chunks]
        steps = CHUNK.bit_length() - 1
        for it in range(steps):
            x = [x[c] + _dot(n_k[c], x[c].astype(BF16)) for c in chunks]
            if it + 1 < steps:
                n_k = [_dot(n_k[c], n_k[c]).astype(BF16) for c in chunks]
        zero_b = jnp.zeros((PAIR, PAIR), BF16)
        ry = [_dot(a_r[c], jnp.concatenate(
            [x[c].astype(BF16), jnp.concatenate([zero_b, v_sb[c]], axis=1)], axis=0)) for c in chunks]
        for c in chunks:
            r_c = rt_all[rows[c]]
            rp_ref[out_rows[c], :] = (ry[c][:CHUNK, :PAIR] + ry[c][CHUNK:, :PAIR] + r_c).astype(BF16)
            yq_ref[out_rows[c], :] = (ry[c][:CHUNK, PAIR:] + ry[c][CHUNK:, PAIR:]).astype(BF16)
        p_t = [x[c][:, :PAIR].T.astype(BF16) for c in chunks]
        qv_t = [jnp.concatenate([x[c][:, PAIR:], v_s[c]], axis=0).T.astype(BF16) for c in chunks]
        c0 = sub * n_chunks
        for c in chunks:
            g_ref[c0 + c, 0] = (jnp.where(eye, jnp.exp(last[c]), 0.0)
                                + _dot(p_t[c], bk_h[c][:PAIR]))
        for c in chunks:
            hbd = _dot(qv_t[c], bk_h[c])
            h_ref[c0 + c, 0] = hbd[:HEAD] + hbd[HEAD:]


def _rwkv_prep(proj, lora, mu_rkv, pvec, w2pad, a2pad, g2, seq_len, col0):
    t = proj.shape[0]
    width = mu_rkv.shape[1]
    n_pairs = width // PAIR
    tt = TT_PREP
    nl = lora.shape[1]
    cb = col0 // PAIR
    wb = width // PAIR

    def prev_rows(i):
        return jnp.maximum(i * (tt // 8) - 1, 0)

    def tok(c):
        return pl.BlockSpec((tt, PAIR), lambda i, p: (i, c + p))

    def prev(c):
        return pl.BlockSpec((8, PAIR), lambda i, p: (prev_rows(i), c + p))

    out_tok = pl.BlockSpec((tt, PAIR), lambda i, p: (i, p))
    tok_shape = jax.ShapeDtypeStruct((t, width), BF16)
    return pl.pallas_call(
        functools.partial(_rwkv_prep_kernel, seq_len),
        grid=(t // tt, n_pairs),
        in_specs=[
            tok(cb), tok(cb + wb), tok(cb + 2 * wb),
            prev(cb), prev(cb + wb), prev(cb + 2 * wb),
            pl.BlockSpec((tt, nl), lambda i, p: (i, 0)),
            pl.BlockSpec((8, nl), lambda i, p: (prev_rows(i), 0)),
            pl.BlockSpec((3, PAIR), lambda i, p: (0, p)),
            pl.BlockSpec((8, PAIR), lambda i, p: (0, p)),
            pl.BlockSpec((PAIR, PAIR), lambda i, p: (0, p)),
            pl.BlockSpec((PAIR, PAIR), lambda i, p: (0, p)),
            pl.BlockSpec((PAIR, PAIR), lambda i, p: (0, p)),
        ],
        out_specs=[
            out_tok, out_tok, out_tok, out_tok,
            pl.BlockSpec((tt // CHUNK, 1, PAIR, PAIR), lambda i, p: (i, p, 0, 0)),
            pl.BlockSpec((tt // CHUNK, 1, HEAD, PAIR), lambda i, p: (i, p, 0, 0)),
        ],
        out_shape=[
            tok_shape, tok_shape, tok_shape, tok_shape,
            jax.ShapeDtypeStruct((t // CHUNK, n_pairs, PAIR, PAIR), F32),
            jax.ShapeDtypeStruct((t // CHUNK, n_pairs, HEAD, PAIR), F32),
        ],
        compiler_params=_params(("arbitrary", "arbitrary")),
        name="rwkv_prep",
    )(proj, proj, proj, proj, proj, proj, lora, lora, mu_rkv, pvec, w2pad, a2pad, g2)


def _rwkv_scan_kernel(rp_ref, yq_ref, bonus_ref, gate_ref, g_ref, h_ref, pvec_ref, o_ref, s_scr, y_scr):
    nb, tt, width = rp_ref.shape
    n_pairs = width // PAIR

    @pl.when(pl.program_id(0) == 0)
    def _():
        s_scr[...] = jnp.zeros_like(s_scr)

    lane = lax.broadcasted_iota(I32, (HEAD, PAIR), 1)
    is_lo = lane < HEAD
    lane_r = lax.broadcasted_iota(I32, (PAIR, PAIR), 0)
    lane_c = lax.broadcasted_iota(I32, (PAIR, PAIR), 1)
    bd = jnp.where((lane_r // HEAD) == (lane_c // HEAD), 1.0, 0.0).astype(BF16)

    chains = [(b, p) for b in range(nb) for p in range(n_pairs)]
    state = {ch: s_scr[ch[0], ch[1]] for ch in chains}
    for c in range(tt // CHUNK):
        rows = slice(c * CHUNK, (c + 1) * CHUNK)
        for (b, p) in chains:
            cols = slice(p * PAIR, (p + 1) * PAIR)
            s0 = state[(b, p)]
            s_st = jnp.concatenate([jnp.where(is_lo, s0, 0.0), jnp.where(is_lo, 0.0, s0)], axis=0)
            y_scr[b, rows, cols] = (_dot_nt(rp_ref[b, rows, cols], s_st.astype(BF16))
                                    + yq_ref[b, rows, cols].astype(F32))
        state = {(b, p): _dot3(state[(b, p)], g_ref[b, c, p]) + h_ref[b, c, p] for (b, p) in chains}
    for (b, p) in chains:
        s_scr[b, p] = state[(b, p)]

    pvec = pvec_ref[...]
    for (b, p) in chains:
        cols = slice(p * PAIR, (p + 1) * PAIR)
        y = y_scr[b, :, cols]
        mu = _head_sum(y, bd) * (1.0 / HEAD)
        yc = y - mu
        var = _head_sum(yc * yc, bd) * (1.0 / HEAD)
        yn = yc * lax.rsqrt(var + LNX_EPS) * pvec[5:6, cols] + pvec[6:7, cols]
        o_ref[b, :, cols] = ((yn + bonus_ref[b, :, cols].astype(F32))
                             * gate_ref[b, :, cols].astype(F32))


def _rwkv_scan(rp, yq, bonus, gate, g, h, pvec, batch):
    t, width = rp.shape
    n_pairs = width // PAIR
    seq = t // batch
    tt = TT_RWKV
    nc = tt // CHUNK
    tok = pl.BlockSpec((batch, tt, width), lambda i: (0, i, 0))
    as_seq = lambda z: z.reshape(batch, seq, width)
    out = pl.pallas_call(
        _rwkv_scan_kernel,
        grid=(seq // tt,),
        in_specs=[
            tok, tok, tok, tok,
            pl.BlockSpec((batch, nc, n_pairs, PAIR, PAIR), lambda i: (0, i, 0, 0, 0)),
            pl.BlockSpec((batch, nc, n_pairs, HEAD, PAIR), lambda i: (0, i, 0, 0, 0)),
            pl.BlockSpec((8, width), lambda i: (0, 0)),
        ],
        out_specs=tok,
        out_shape=jax.ShapeDtypeStruct((batch, seq, width), F32),
        scratch_shapes=[pltpu.VMEM((batch, n_pairs, HEAD, PAIR), F32),
                        pltpu.VMEM((batch, tt, width), F32)],
        compiler_params=_params(("arbitrary",)),
        name="rwkv_scan",
    )(as_seq(rp), as_seq(yq), as_seq(bonus), as_seq(gate),
      g.reshape(batch, seq // CHUNK, n_pairs, PAIR, PAIR),
      h.reshape(batch, seq // CHUNK, n_pairs, HEAD, PAIR), pvec)
    return out.reshape(t, width)


def _out_proj_kernel(x_ref, ya_ref, yb_ref, wo_ref, g_ref, rwt_ref, rb_ref, h1_ref, hn_ref, lg_ref):
    wa = ya_ref.shape[1]
    h1 = (x_ref[...] + _dot(ya_ref[...].astype(BF16), wo_ref[0:wa, :])
          + _dot(yb_ref[...].astype(BF16), wo_ref[wa:, :]))
    h1_ref[...] = h1
    hn = h1 * lax.rsqrt(jnp.mean(h1 * h1, axis=-1, keepdims=True) + RMS_EPS) * g_ref[...]
    _store_token_tiles(hn_ref, hn)
    lg_ref[...] = _dot3_nt(rwt_ref[...], hn) + rb_ref[...]


def _out_proj(x2, ya, yb, wo_bf, g2, rwt, rb):
    t, d = x2.shape
    wa = ya.shape[1]
    wb = yb.shape[1]
    ne = rwt.shape[0]
    tm = TM_PROJ
    return pl.pallas_call(
        _out_proj_kernel,
        grid=(t // tm,),
        in_specs=[
            pl.BlockSpec((tm, d), lambda i: (i, 0)),
            pl.BlockSpec((tm, wa), lambda i: (i, 0)),
            pl.BlockSpec((tm, wb), lambda i: (i, 0)),
            pl.BlockSpec((wa + wb, d), lambda i: (0, 0)),
            pl.BlockSpec((1, d), lambda i: (0, 0)),
            pl.BlockSpec((ne, d), lambda i: (0, 0)),
            pl.BlockSpec((ne, 1), lambda i: (0, 0)),
        ],
        out_specs=[
            pl.BlockSpec((tm, d), lambda i: (i, 0)),
            pl.BlockSpec((tm * SUBLANES, LANES), lambda i: (i, 0)),
            pl.BlockSpec((ne, tm), lambda i: (0, i)),
        ],
        out_shape=[
            jax.ShapeDtypeStruct((t, d), F32),
            jax.ShapeDtypeStruct((t * SUBLANES, LANES), F32),
            jax.ShapeDtypeStruct((ne, t), F32),
        ],
        compiler_params=_params(("arbitrary",)),
        name="out_proj",
    )(x2, ya, yb, wo_bf, g2, rwt, rb)


def _route_kernel(lg_ref, pos_ref, gate_ref, pend_ref, li_ref, rstart_ref, rlen_ref,
                  carry_scr, pstart_scr):
    phase = pl.program_id(0)
    first = pl.program_id(1) == 0

    @pl.when(first & (phase == 0))
    def _():
        carry_scr[...] = jnp.zeros_like(carry_scr)
        pstart_scr[...] = jnp.zeros_like(pstart_scr)
        pend_ref[...] = jnp.zeros_like(pend_ref)

    @pl.when(first & (phase == 1))
    def _():
        counts = carry_scr[...]
        padded = jnp.ceil(counts * (1.0 / BM_MOE)) * BM_MOE
        row = lax.broadcasted_iota(I32, counts.shape, 0)
        end = padded
        s = 1
        while s < counts.shape[0]:
            end = end + jnp.where(row >= s, pltpu.roll(end, s, 0), 0.0)
            s *= 2
        pstart_scr[...] = end - padded
        pend_ref[...] = end.astype(I32)
        carry_scr[...] = jnp.zeros_like(carry_scr)

    l = lg_ref[...]
    ne, tr = l.shape
    e_iota = lax.broadcasted_iota(I32, (ne, tr), 0)
    chosen = jnp.zeros((ne, tr), F32)
    vals, sels = [], []
    for j in range(TOP_K):
        m = jnp.max(l, axis=0, keepdims=True)
        idx = jnp.min(jnp.where(l == m, e_iota, ne), axis=0, keepdims=True)
        sel = e_iota == idx
        vals.append(m)
        sels.append(sel)
        chosen = jnp.where(sel, 1.0, chosen)
        l = jnp.where(sel, -jnp.inf, l)
    ex = [jnp.exp(vj - vals[0]) for vj in vals]
    den = ex[0] + ex[1] + ex[2] + ex[3]
    for j in range(TOP_K):
        gate_ref[j:j + 1, :] = ex[j] / den

    ti = lax.broadcasted_iota(I32, (tr, tr), 0)
    tj = lax.broadcasted_iota(I32, (tr, tr), 1)
    upper = jnp.where(ti <= tj, 1.0, 0.0).astype(BF16)
    inc = _dot(chosen.astype(BF16), upper)
    carry = carry_scr[...]
    first_row = carry[:, 0:1] + pstart_scr[:, 0:1]
    row_of = inc - chosen + first_row
    for j in range(TOP_K):
        pj = jnp.sum(jnp.where(sels[j], row_of, 0.0), axis=0, keepdims=True)
        pos_ref[j:j + 1, :] = (pj * SUBLANES).astype(I32)
    carry_scr[...] = carry + inc[:, tr - 1:tr]

    tc = TC_COMBINE
    erow = lax.broadcasted_iota(I32, (ne, 128), 0)
    for s in range(tr // tc):
        cols = slice(s * tc, (s + 1) * tc)
        before = inc[:, s * tc - 1:s * tc] if s else jnp.zeros((ne, 1), F32)
        n_run = inc[:, (s + 1) * tc - 1:(s + 1) * tc] - before
        win_rows = jnp.broadcast_to(jnp.ceil(n_run * (1.0 / WIN_COMBINE)) * WIN_COMBINE, (ne, 128))
        base = win_rows
        k = 1
        while k < ne:
            base = base + jnp.where(erow >= k, pltpu.roll(base, k, 0), 0.0)
            k *= 2
        base = base - win_rows
        slot_of = (inc - chosen)[:, cols] - before + base[:, 0:1]
        for j in range(TOP_K):
            lj = jnp.sum(jnp.where(sels[j][:, cols], slot_of, 0.0), axis=0, keepdims=True)
            li_ref[j:j + 1, cols] = (lj * SUBLANES).astype(I32)
        rstart_ref[s * ne:(s + 1) * ne, :] = jnp.broadcast_to(first_row + before, (ne, 128)).astype(I32)
        rlen_ref[s * ne:(s + 1) * ne, :] = jnp.broadcast_to(n_run, (ne, 128)).astype(I32)


def _route(logits_t):
    ne, t = logits_t.shape
    tr = TR_ROUTE
    tok = pl.BlockSpec((TOP_K, tr), lambda ph, i: (0, i * ph))
    n_sub = tr // TC_COMBINE
    runs = pl.BlockSpec((n_sub * ne, 128), lambda ph, i: (i * ph, 0))
    runs_shape = jax.ShapeDtypeStruct((t // TC_COMBINE * ne, 128), I32)
    return pl.pallas_call(
        _route_kernel,
        grid=(2, t // tr),
        in_specs=[pl.BlockSpec((ne, tr), lambda ph, i: (0, i))],
        out_specs=[tok, tok, pl.BlockSpec((ne, 128), lambda ph, i: (0, 0)), tok, runs, runs],
        out_shape=[
            jax.ShapeDtypeStruct((TOP_K, t), I32),
            jax.ShapeDtypeStruct((TOP_K, t), F32),
            jax.ShapeDtypeStruct((ne, 128), I32),
            jax.ShapeDtypeStruct((TOP_K, t), I32),
            runs_shape, runs_shape,
        ],
        scratch_shapes=[pltpu.VMEM((ne, 128), F32), pltpu.VMEM((ne, 128), F32)],
        compiler_params=_params(("arbitrary", "arbitrary")),
        name="route",
    )(logits_t)


def _dispatch_kernel(pend_ref, nused_ref, *refs):
    pos_refs = refs[:TOP_K]
    hn_ref, xin_ref, zero_scr, sem = refs[TOP_K:]
    bm = zero_scr.shape[0] // SUBLANES
    td = pos_refs[0].shape[0]
    i = pl.program_id(0)

    def tile(ref, token):
        return ref.at[pl.ds(pl.multiple_of(token * SUBLANES, SUBLANES), SUBLANES)]

    def zero_block(row0):
        start = pl.multiple_of(row0 * SUBLANES, bm * SUBLANES)
        return pltpu.make_async_copy(zero_scr, xin_ref.at[pl.ds(start, bm * SUBLANES)], sem)

    @pl.when(i == 0)
    def _():
        zero_scr[...] = jnp.zeros_like(zero_scr)

        def has_rows(e):
            end = pend_ref[e]
            return jnp.where(e == 0, end >= bm, end > pend_ref[jnp.maximum(e - 1, 0)])

        def start_one(e, carry):
            @pl.when(has_rows(e))
            def _():
                zero_block(pend_ref[e] - bm).start()
            return carry

        def wait_one(e, carry):
            @pl.when(has_rows(e))
            def _():
                zero_block(0).wait()
            return carry

        def start_tail(blk, carry):
            zero_block(blk * bm).start()
            return carry

        def wait_tail(blk, carry):
            zero_block(0).wait()
            return carry

        n_blocks = xin_ref.shape[0] // (bm * SUBLANES)
        lax.fori_loop(0, pend_ref.shape[0], start_one, 0)
        lax.fori_loop(nused_ref[0], n_blocks, start_tail, 0)
        lax.fori_loop(0, pend_ref.shape[0], wait_one, 0)
        lax.fori_loop(nused_ref[0], n_blocks, wait_tail, 0)

    def issue(tk, carry):
        for j in range(TOP_K):
            dst = xin_ref.at[pl.ds(pl.multiple_of(pos_refs[j][tk], SUBLANES), SUBLANES)]
            pltpu.make_async_copy(tile(hn_ref, tk), dst, sem).start(priority=j % 2)
        return carry

    lax.fori_loop(0, td, issue, 0, unroll=8)

    for j in range(TOP_K):
        pltpu.make_async_copy(hn_ref, xin_ref.at[pl.ds(0, td * SUBLANES)], sem).wait()


def _choice_specs(n_steps, tile):
    return [pl.BlockSpec((tile,), functools.partial(lambda j, i, *_: (j * n_steps + i,), j),
                         memory_space=pltpu.SMEM) for j in range(TOP_K)]


def _dispatch(pad_end, n_used, pos, hn_tiles, n_rows):
    t = hn_tiles.shape[0] // SUBLANES
    td = TD_DISPATCH
    pos = pos.reshape(-1)
    grid_spec = pltpu.PrefetchScalarGridSpec(
        num_scalar_prefetch=2,
        grid=(t // td,),
        in_specs=_choice_specs(t // td, td) + [
            pl.BlockSpec((td * SUBLANES, LANES), lambda i, pe, nu: (i, 0)),
        ],
        out_specs=pl.BlockSpec(memory_space=pl.ANY),
        scratch_shapes=[pltpu.VMEM((BM_MOE * SUBLANES, LANES), F32), pltpu.SemaphoreType.DMA(())],
    )
    return pl.pallas_call(
        _dispatch_kernel,
        grid_spec=grid_spec,
        out_shape=jax.ShapeDtypeStruct((n_rows * SUBLANES, LANES), F32),
        compiler_params=_params(("arbitrary",)),
        name="dispatch",
    )(pad_end, n_used, *([pos] * TOP_K), hn_tiles)


def _experts_kernel(pend_ref, nused_ref, w1_ref, b1_ref, w2_ref, b2_ref, xin_ref, yout_ref,
                    xbuf, obuf, wt_scr, wg_scr, wl_scr, w2_scr, sem_in, sem_out):
    e = pl.program_id(0)
    rows_blk = xbuf.shape[1]
    bm = rows_blk // SUBLANES
    dh = w2_ref.shape[1]
    n_used = nused_ref[0]
    blk_lo = jnp.where(e == 0, 0, pend_ref[jnp.maximum(e - 1, 0)]) // bm
    blk_hi = pend_ref[e] // bm

    def block(ref, blk):
        return ref.at[pl.ds(pl.multiple_of(blk * rows_blk, rows_blk), rows_blk)]

    def x_copy(blk, slot):
        return pltpu.make_async_copy(block(xin_ref, blk), xbuf.at[slot], sem_in.at[slot])

    def o_copy(blk, slot):
        return pltpu.make_async_copy(obuf.at[slot], block(yout_ref, blk), sem_out.at[slot])

    @pl.when((e == 0) & (n_used > 0))
    def _():
        x_copy(0, 0).start()

    @pl.when(blk_hi > blk_lo)
    def _():
        n_slab, n_t, lanes = wt_scr.shape
        for part in range(2 * dh // n_t):
            rows = slice(part * (n_t // 2), (part + 1) * (n_t // 2))
            for s in range(n_slab):
                cols = slice(s * lanes, (s + 1) * lanes)
                wt_scr[s] = w1_ref[0, cols, part * n_t:(part + 1) * n_t].T
                wg_scr[rows, cols] = wt_scr[s, pl.ds(0, n_t // 2, stride=2), :].astype(BF16)
                wl_scr[rows, cols] = wt_scr[s, pl.ds(1, n_t // 2, stride=2), :].astype(BF16)
        w2_scr[...] = w2_ref[0].astype(BF16)

        def one_block(blk, carry):
            slot = blk % 2

            @pl.when(blk + 1 < n_used)
            def _():
                x_copy(blk + 1, 1 - slot).start()

            x_copy(blk, slot).wait()

            @pl.when(blk >= 2)
            def _():
                o_copy(blk - 2, slot).wait()

            xb = _load_token_tiles(xbuf.at[slot], 0, bm).astype(BF16)
            bias = b1_ref[0]
            glu = jnp.minimum(_dot_nt(xb, wg_scr[...]) + bias[:, :dh], SWIGLU_LIMIT)
            lin = jnp.clip(_dot_nt(xb, wl_scr[...]) + bias[:, dh:], -SWIGLU_LIMIT, SWIGLU_LIMIT)
            act = glu * jax.nn.sigmoid(SWIGLU_ALPHA * glu) * (lin + 1.0)
            _store_token_tiles(obuf.at[slot], _dot(act.astype(BF16), w2_scr[...]) + b2_ref[0])
            o_copy(blk, slot).start()
            return carry

        lax.fori_loop(blk_lo, blk_hi, one_block, 0)

    @pl.when(e == pl.num_programs(0) - 1)
    def _():
        @pl.when(n_used >= 1)
        def _():
            o_copy(0, (n_used - 1) % 2).wait()

        @pl.when(n_used >= 2)
        def _():
            o_copy(0, n_used % 2).wait()

        obuf[0] = jnp.zeros(obuf.shape[1:], obuf.dtype)
        n_blocks = yout_ref.shape[0] // rows_blk

        def start_tail(blk, carry):
            o_copy(blk, 0).start()
            return carry

        def wait_tail(blk, carry):
            o_copy(0, 0).wait()
            return carry

        lax.fori_loop(n_used, n_blocks, start_tail, 0)
        lax.fori_loop(n_used, n_blocks, wait_tail, 0)


def _experts(pad_end, n_used, xin, w1, b1p, w2, b2):
    ne, d, dh2 = w1.shape
    dh = dh2 // 2
    rows_blk = BM_MOE * SUBLANES
    grid_spec = pltpu.PrefetchScalarGridSpec(
        num_scalar_prefetch=2,
        grid=(ne,),
        in_specs=[
            pl.BlockSpec((1, d, dh2), lambda e, pe, nu: (e, 0, 0)),
            pl.BlockSpec((1, 1, dh2), lambda e, pe, nu: (e, 0, 0)),
            pl.BlockSpec((1, dh, d), lambda e, pe, nu: (e, 0, 0)),
            pl.BlockSpec((1, 1, d), lambda e, pe, nu: (e, 0, 0)),
            pl.BlockSpec(memory_space=pl.ANY),
        ],
        out_specs=pl.BlockSpec(memory_space=pl.ANY),
        scratch_shapes=[
            pltpu.VMEM((2, rows_blk, LANES), F32),
            pltpu.VMEM((2, rows_blk, LANES), F32),
            pltpu.VMEM((d // 128, dh, 128), F32),
            pltpu.VMEM((dh, d), BF16),
            pltpu.VMEM((dh, d), BF16),
            pltpu.VMEM((dh, d), BF16),
            pltpu.SemaphoreType.DMA((2,)),
            pltpu.SemaphoreType.DMA((2,)),
        ],
    )
    return pl.pallas_call(
        _experts_kernel,
        grid_spec=grid_spec,
        out_shape=jax.ShapeDtypeStruct(xin.shape, F32),
        compiler_params=_params(("arbitrary",)),
        name="experts",
    )(pad_end, n_used, w1, b1p, w2, b2, xin)


def _combine_kernel(*refs):
    rstart_ref, rlen_ref, rstart_next_ref, rlen_next_ref = refs[:4]
    li_refs = refs[4:4 + TOP_K]
    gate_refs = refs[4 + TOP_K:4 + 2 * TOP_K]
    h1_ref, fg_ref, yout_ref, o_ref, wbuf, hacc, sem = refs[4 + 2 * TOP_K:]
    tc = o_ref.shape[0]
    i = pl.program_id(0)
    slot = i % 2
    n_experts = rstart_ref.shape[0]
    win_rows = WIN_COMBINE * SUBLANES

    def tile(ref, token):
        return ref.at[pl.ds(pl.multiple_of(token * SUBLANES, SUBLANES), SUBLANES)]

    def windows_of(rl_ref, e):
        return (rl_ref[e, 0] + (WIN_COMBINE - 1)) // WIN_COMBINE

    def start_windows(rs_ref, rl_ref, s):
        def per_expert(e, base):
            start = rs_ref[e, 0]
            n_win = windows_of(rl_ref, e)

            def per_window(w, carry):
                src = yout_ref.at[pl.ds(pl.multiple_of((start + w * WIN_COMBINE) * SUBLANES, SUBLANES),
                                        win_rows)]
                dst = wbuf.at[s, pl.ds(pl.multiple_of((base + w) * win_rows, win_rows), win_rows)]
                pltpu.make_async_copy(src, dst, sem.at[s]).start()
                return carry

            lax.fori_loop(0, n_win, per_window, 0)
            return base + n_win

        lax.fori_loop(0, n_experts, per_expert, 0)

    @pl.when(i == 0)
    def _():
        start_windows(rstart_ref, rlen_ref, 0)

    @pl.when(i + 1 < pl.num_programs(0))
    def _():
        start_windows(rstart_next_ref, rlen_next_ref, 1 - slot)

    n_windows = lax.fori_loop(0, n_experts, lambda e, n: n + windows_of(rlen_ref, e), 0)

    def wait_window(w, carry):
        pltpu.make_async_copy(yout_ref.at[pl.ds(0, win_rows)], wbuf.at[slot, pl.ds(0, win_rows)],
                              sem.at[slot]).wait()
        return carry

    lax.fori_loop(0, n_windows, wait_window, 0)

    def window_tile(j, tk):
        return wbuf[slot, pl.ds(pl.multiple_of(li_refs[j][tk], SUBLANES), SUBLANES), :]

    def per_token(tk, carry):
        acc = gate_refs[0][tk] * window_tile(0, tk)
        for j in range(1, TOP_K):
            acc = acc + gate_refs[j][tk] * window_tile(j, tk)
        tile(hacc, tk)[...] = acc
        return carry

    lax.fori_loop(0, tc, per_token, 0, unroll=8)
    h = h1_ref[...] + _load_token_tiles(hacc, 0, tc)
    o_ref[...] = h * lax.rsqrt(jnp.mean(h * h, axis=-1, keepdims=True) + RMS_EPS) * fg_ref[...]


def _combine(rstart, rlen, li, gates, h1, fg, yout):
    t, d = h1.shape
    tc = TC_COMBINE
    n_steps = t // tc
    n_experts = rstart.shape[0] // n_steps
    buf_rows = TOP_K * tc + n_experts * WIN_COMBINE
    runs = pl.BlockSpec((n_experts, 128), lambda i: (i, 0), memory_space=pltpu.SMEM)
    runs_next = pl.BlockSpec((n_experts, 128), lambda i: (jnp.minimum(i + 1, n_steps - 1), 0),
                             memory_space=pltpu.SMEM)
    return pl.pallas_call(
        _combine_kernel,
        grid=(n_steps,),
        in_specs=[runs, runs, runs_next, runs_next]
        + _choice_specs(n_steps, tc) + _choice_specs(n_steps, tc) + [
            pl.BlockSpec((tc, d), lambda i: (i, 0)),
            pl.BlockSpec((1, d), lambda i: (0, 0)),
            pl.BlockSpec(memory_space=pl.ANY),
        ],
        out_specs=pl.BlockSpec((tc, d), lambda i: (i, 0)),
        out_shape=jax.ShapeDtypeStruct((t, d), F32),
        scratch_shapes=[pltpu.VMEM((2, buf_rows * SUBLANES, LANES), F32),
                        pltpu.VMEM((tc * SUBLANES, LANES), F32),
                        pltpu.SemaphoreType.DMA((2,))],
        compiler_params=_params(("arbitrary",)),
        name="combine",
    )(rstart, rlen, rstart, rlen, *([li.reshape(-1)] * TOP_K), *([gates.reshape(-1)] * TOP_K),
      h1, fg, yout)


def kernel(x, norm1_g, w_in, sgu_ln_g, sgu_ln_b, sgu_w, sgu_b, mu_rkv, mu_wag, decay_w0, decay_w1,
           decay_w2, iclr_a0, iclr_a1, iclr_a2, gate_g1, gate_g2, k_k, k_a, r_k, lnx_g, lnx_b, w_out,
           norm2_g, router_w, router_b, moe_w1, moe_b1, moe_w2, moe_b2, final_g):
    batch, seq, d = x.shape
    t = batch * seq
    depth = w_in.shape[0]
    sgu_width = sgu_ln_g.shape[1]
    rw = mu_rkv.shape[2]
    n_dec, n_icl, n_gate = decay_w1.shape[2], iclr_a1.shape[2], gate_g1.shape[2]
    assert n_dec == HEAD and n_icl == HEAD and n_gate == PAIR and rw % PAIR == 0
    assert seq % TT_RWKV == 0 and seq % TT_PREP == 0 and t % TM_PROJ == 0
    assert sgu_w.shape[2] == SGU_BLOCK
    assert depth == 1, "the final RMSNorm is fused into the last layer's combine kernel"
    assert d == SUBLANES * LANES, "the MoE row movers copy one (8, 128) f32 tile per token"
    assert BM_MOE + N_EXPERTS - 1 >= WIN_COMBINE

    h = x.reshape(t, d)
    for l in range(depth):
        win_bf = w_in[l].astype(BF16)
        wl = jnp.concatenate([decay_w1[l], iclr_a1[l], gate_g1[l]], axis=1)
        mucat = jnp.concatenate([
            jnp.broadcast_to(mu_wag[l, 0][:, None], (d, n_dec)),
            jnp.broadcast_to(mu_wag[l, 1][:, None], (d, n_icl)),
            jnp.broadcast_to(mu_wag[l, 2][:, None], (d, n_gate))], axis=1)
        zeros = jnp.zeros((HEAD, rw), F32)
        w2pad = jnp.concatenate([decay_w2[l], zeros], axis=0)
        a2pad = jnp.concatenate([zeros, iclr_a2[l]], axis=0)
        pvec = jnp.stack([decay_w0[l], iclr_a0[l], k_k[l], k_a[l], r_k[l].reshape(-1),
                          lnx_g[l], lnx_b[l], jnp.zeros((rw,), F32)], axis=0)
        bias2d = jnp.repeat(sgu_b[l].T, sgu_width // sgu_b.shape[1], axis=1)
        wo_bf = w_out[l].astype(BF16)
        b1p = jnp.concatenate([moe_b1[l][:, 0::2], moe_b1[l][:, 1::2]], axis=-1)[:, None, :]
        b2 = moe_b2[l][:, None, :]

        assert w_in.shape[2] == 2 * sgu_width + 3 * rw
        proj, lora, ya = _in_proj(h, norm1_g[l][None, :], win_bf, wl, mucat, sgu_ln_g[l][None, :],
                                  sgu_ln_b[l][None, :], sgu_w[l], bias2d)
        rp, yq, bonus, gate, g_m, h_m = _rwkv_prep(proj, lora, mu_rkv[l], pvec, w2pad, a2pad,
                                                   gate_g2[l], seq, 0)
        yb = _rwkv_scan(rp, yq, bonus, gate, g_m, h_m, pvec, batch)
        h1, hn2, logits_t = _out_proj(h, ya, yb, wo_bf, norm2_g[l][None, :], router_w[l].T,
                                      router_b[l][:, None])
        pos, gates, pend, li, rstart, rlen = _route(logits_t)

        pad_end = pend[:, 0]
        n_blocks = (t * TOP_K) // BM_MOE + N_EXPERTS
        n_used = pad_end[-1:] // BM_MOE

        xin = _dispatch(pad_end, n_used, pos, hn2, n_blocks * BM_MOE)
        yout = _experts(pad_end, n_used, xin, moe_w1[l], b1p, moe_w2[l], b2)
        h = _combine(rstart, rlen, li, gates, h1, final_g[None, :], yout)
    return h.reshape(batch, seq, d)
```

```python
import functools

import jax
import jax.numpy as jnp
from jax import lax
from jax.experimental import pallas as pl
from jax.experimental.pallas import tpu as pltpu

F32 = jnp.float32
BF16 = jnp.bfloat16
I32 = jnp.int32

RMS_EPS = 1e-5
LN_EPS = 1e-5
LNX_EPS = 64e-5
CHUNK = 64
SGU_BLOCK = 128
HEAD = 64
PAIR = 2 * HEAD
N_EXPERTS = 32
TOP_K = 4
SWIGLU_ALPHA = 1.702
SWIGLU_LIMIT = 7.0

V7X_VMEM_LIMIT = 56 * 1024 * 1024

TM_PROJ = 1024
TT_PREP = 1024
N_SUB_PREP = 1
TT_RWKV = 512
TR_ROUTE = 1024
BM_MOE = 256
TD_DISPATCH = 1024
TC_COMBINE = 512
WIN_COMBINE = 64


def _dot(a, b):
    return jnp.dot(a, b, preferred_element_type=F32)


def _dot_nt(a, b):
    return lax.dot_general(a, b, (((1,), (1,)), ((), ())), preferred_element_type=F32)


def _split(x):
    hi = x.astype(BF16)
    lo = (x - hi.astype(F32)).astype(BF16)
    return hi, lo


def _dot3(a, b):
    ah, al = _split(a)
    bh, bl = _split(b)
    return _dot(ah, bh) + _dot(al, bh) + _dot(ah, bl)


def _dot3_nt(a, b):
    ah, al = _split(a)
    bh, bl = _split(b)
    return _dot_nt(ah, bh) + _dot_nt(al, bh) + _dot_nt(ah, bl)


def _head_sum(x, bd):
    hi, lo = _split(x)
    return _dot(hi, bd) + _dot(lo, bd)


SUBLANES = 8
LANES = 128


def _store_token_tiles(ref, x):
    n = x.shape[0]
    for s in range(SUBLANES):
        ref[pl.ds(s, n, stride=SUBLANES), :] = x[:, s * LANES:(s + 1) * LANES]


def _load_token_tiles(ref, row0, n):
    return jnp.concatenate(
        [ref[pl.ds(row0 + s, n, stride=SUBLANES), :] for s in range(SUBLANES)], axis=1)


def _params(sem, vmem=V7X_VMEM_LIMIT):
    return pltpu.CompilerParams(dimension_semantics=sem, vmem_limit_bytes=vmem)


def _in_proj_kernel(x_ref, g_ref, win_ref, wl_ref, mucat_ref, lng_ref, lnb_ref, w_ref, bias_ref,
                    proj_ref, lora_ref, o_ref, wl_scr, wm_scr):
    n_heads = w_ref.shape[0]

    @pl.when(pl.program_id(0) == 0)
    def _():
        wl = wl_ref[...]
        mu = mucat_ref[...]
        n = wl.shape[1]
        wl_scr[:, 0:n] = (wl * (1.0 - mu)).astype(BF16)
        wl_scr[:, n:2 * n] = (wl * mu).astype(BF16)
        qi = lax.broadcasted_iota(I32, (SGU_BLOCK, SGU_BLOCK), 0) // CHUNK
        kj = lax.broadcasted_iota(I32, (SGU_BLOCK, SGU_BLOCK), 1) // CHUNK
        for h in range(n_heads):
            wm_scr[h] = jnp.where(kj <= qi, w_ref[h], 0.0).astype(BF16)

    x = x_ref[...]
    hn = x * lax.rsqrt(jnp.mean(x * x, axis=-1, keepdims=True) + RMS_EPS) * g_ref[...]
    hb = hn.astype(BF16)
    width = lng_ref.shape[1]
    two_w = 2 * width
    proj_ref[...] = _dot(hb, win_ref[:, two_w:])
    lora_ref[...] = _dot(hb, wl_scr[...])

    z = _dot(hb, win_ref[:, :two_w])
    tm = z.shape[0]
    gz = 0.5 * z * (1.0 + lax.erf(z * (2.0 ** -0.5)))
    u = gz[:, :width]
    v = gz[:, width:]
    mu = jnp.mean(v, axis=-1, keepdims=True)
    vc = v - mu
    var = jnp.mean(vc * vc, axis=-1, keepdims=True)
    vn = vc * lax.rsqrt(var + LN_EPS) * lng_ref[...] + lnb_ref[...]
    lane = lax.broadcasted_iota(I32, (SGU_BLOCK, PAIR), 1)
    is_lo = lane < HEAD
    bias = bias_ref[...]
    for blk in range(tm // SGU_BLOCK):
        rows = slice(blk * SGU_BLOCK, (blk + 1) * SGU_BLOCK)
        for p in range(width // PAIR):
            cols = slice(p * PAIR, (p + 1) * PAIR)
            vp = vn[rows, cols]
            lo = jnp.where(is_lo, vp, 0.0).astype(BF16)
            hi = jnp.where(is_lo, 0.0, vp).astype(BF16)
            sv = _dot(wm_scr[2 * p], lo) + _dot(wm_scr[2 * p + 1], hi)
            o_ref[rows, cols] = u[rows, cols] * (sv + bias[:, cols])


def _in_proj(x2, g, win_bf, wl, mucat, ln_g, ln_b, w_s, bias2d):
    t, d = x2.shape
    width = ln_g.shape[1]
    n_rkv = win_bf.shape[1] - 2 * width
    n_l = wl.shape[1]
    n_heads = w_s.shape[0]
    tm = TM_PROJ
    const = lambda *shape: pl.BlockSpec(shape, lambda i: (0,) * len(shape))
    row = lambda n: pl.BlockSpec((tm, n), lambda i: (i, 0))
    return pl.pallas_call(
        _in_proj_kernel,
        grid=(t // tm,),
        in_specs=[
            row(d), const(1, d), const(d, win_bf.shape[1]), const(d, n_l), const(d, n_l),
            const(1, width), const(1, width), const(n_heads, SGU_BLOCK, SGU_BLOCK),
            const(SGU_BLOCK, width),
        ],
        out_specs=[row(n_rkv), row(2 * n_l), row(width)],
        out_shape=[
            jax.ShapeDtypeStruct((t, n_rkv), F32),
            jax.ShapeDtypeStruct((t, 2 * n_l), F32),
            jax.ShapeDtypeStruct((t, width), F32),
        ],
        scratch_shapes=[pltpu.VMEM((d, 2 * n_l), BF16),
                        pltpu.VMEM((n_heads, SGU_BLOCK, SGU_BLOCK), BF16)],
        compiler_params=_params(("arbitrary",)),
        name="in_proj_sgu",
    )(x2, g, win_bf, wl, mucat, ln_g, ln_b, w_s, bias2d)


def _rwkv_prep_kernel(seq_len, pr_ref, pk_ref, pv_ref, ppr_ref, ppk_ref, ppv_ref, lo_ref, plo_ref,
                      mu_ref, pvec_ref, w2_ref, a2_ref, g2_ref,
                      rp_ref, yq_ref, bonus_ref, gate_ref, g_ref, h_ref):
    tt = pr_ref.shape[0]
    ts = tt // N_SUB_PREP
    n_chunks = ts // CHUNK
    i = pl.program_id(0)
    keep = jnp.where((i * tt) % seq_len == 0, 0.0, 1.0)

    mu = mu_ref[...]
    pvec = pvec_ref[...]
    w0, a0, k_k, k_a, r_k = pvec[0:1], pvec[1:2], pvec[2:3], pvec[3:4], pvec[4:5]
    w2b = w2_ref[...].astype(BF16)
    a2b = a2_ref[...].astype(BF16)
    g2b = g2_ref[...].astype(BF16)
    half = lo_ref.shape[1] // 2

    lane_r = lax.broadcasted_iota(I32, (PAIR, PAIR), 0)
    lane_c = lax.broadcasted_iota(I32, (PAIR, PAIR), 1)
    bd = jnp.where((lane_r // HEAD) == (lane_c // HEAD), 1.0, 0.0).astype(BF16)
    tri_ones = jnp.where(((lane_r // CHUNK) == (lane_c // CHUNK)) & (lane_c <= lane_r),
                         1.0, 0.0).astype(BF16)
    lane = lax.broadcasted_iota(I32, (CHUNK, PAIR), 1)
    is_lo = lane < HEAD

    def stack(x):
        return jnp.concatenate([jnp.where(is_lo, x, 0.0), jnp.where(is_lo, 0.0, x)], axis=0)

    n2 = 2 * PAIR
    ri = lax.broadcasted_iota(I32, (n2, n2), 0)
    ci = lax.broadcasted_iota(I32, (n2, n2), 1)
    same_head = ((ri // CHUNK) % 2) == ((ci // CHUNK) % 2)
    t_i = ri % CHUNK
    s_i = ci % CHUNK
    tri = same_head & (s_i < t_i + ri // PAIR)
    eye = lane_r == lane_c

    for sub in range(N_SUB_PREP):
        r0 = sub * ts
        tile_rows = slice(r0, r0 + ts)

        def shift(ref, pref, cols=slice(None)):
            x = ref[tile_rows, cols]
            if sub == 0:
                first = pref[7:8, cols] * keep
            else:
                first = ref[r0 - 1:r0, cols]
            rowc = lax.broadcasted_iota(I32, x.shape, 0)
            return x, jnp.where(rowc == 0, first, pltpu.roll(x, 1, 0))

        def shift_mix(ref, pref, m):
            x, xs = shift(ref, pref)
            return x + (xs - x) * m

        r = shift_mix(pr_ref, ppr_ref, mu[0:1])
        k = shift_mix(pk_ref, ppk_ref, mu[1:2])
        v = shift_mix(pv_ref, ppv_ref, mu[2:3])

        _, lo_b = shift(lo_ref, plo_ref, slice(half, 2 * half))
        l_all = lo_ref[tile_rows, :half] + lo_b
        l_wa = l_all[:, :PAIR]
        l_g = l_all[:, PAIR:]

        dw = _dot(jnp.tanh(l_wa).astype(BF16), w2b)
        ia = _dot(l_wa.astype(BF16), a2b)
        gate_ref[tile_rows, :] = _dot(jax.nn.sigmoid(l_g).astype(BF16), g2b).astype(BF16)

        zneg = -(w0 + dw)
        softplus = jnp.maximum(zneg, 0.0) + jnp.log(1.0 + jnp.exp(-jnp.abs(zneg)))
        logw = -jnp.exp(-softplus - 0.5)
        iclr = jax.nn.sigmoid(a0 + ia)

        kk = k * k_k
        kk = kk / jnp.maximum(jnp.sqrt(_head_sum(kk * kk, bd)), 1e-12)
        k2 = k * (1.0 + (iclr - 1.0) * k_a)
        a = -kk
        b = kk * iclr
        bonus_ref[tile_rows, :] = (_head_sum(r * k2 * r_k, bd) * v).astype(BF16)

        lw_hi, lw_lo = _split(logw)
        cl = jnp.concatenate(
            [_dot(tri_ones, lw_hi[q * PAIR:(q + 1) * PAIR]) + _dot(tri_ones, lw_lo[q * PAIR:(q + 1) * PAIR])
             for q in range(ts // PAIR)], axis=0)
        at_all = a * jnp.exp(cl - logw)
        rt_all = r * jnp.exp(cl)
        w_inv = jnp.exp(-cl)
        bt_all = b * w_inv
        kt_all = k2 * w_inv

        chunks = range(n_chunks)
        rows = [slice(c * CHUNK, (c + 1) * CHUNK) for c in chunks]
        out_rows = [slice(r0 + c * CHUNK, r0 + (c + 1) * CHUNK) for c in chunks]
        last = [cl[rows[c]][CHUNK - 1:CHUNK, :] for c in chunks]
        w_rem = [jnp.exp(last[c] - cl[rows[c]]) for c in chunks]
        a_s = [stack(at_all[rows[c]]) for c in chunks]
        r_s = [stack(rt_all[rows[c]]) for c in chunks]
        v_s = [stack(v[rows[c]]) for c in chunks]
        v_sb = [v_s[c].astype(BF16) for c in chunks]
        bk_h = [jnp.concatenate([stack(b[rows[c]] * w_rem[c]), stack(k2[rows[c]] * w_rem[c])],
                                axis=0).astype(BF16) for c in chunks]

        a_all = []
        for c in chunks:
            lhs = jnp.concatenate([a_s[c], r_s[c]], axis=0).astype(BF16)
            rhs = jnp.concatenate([stack(bt_all[rows[c]]), stack(kt_all[rows[c]])],
                                  axis=0).astype(BF16)
            a_all.append(jnp.where(tri, _dot_nt(lhs, rhs), 0.0))
        n_k = [a_all[c][:PAIR, :PAIR].astype(BF16) for c in chunks]
        a_r = [a_all[c][PAIR:, :].astype(BF16) for c in chunks]
        x = [jnp.concatenate([a_s[c], _dot(a_all[c][:PAIR, PAIR:].astype(BF16), v_sb[c])], axis=1)
             for c in chunks]
        steps = CHUNK.bit_length() - 1
        for it in range(steps):
            x = [x[c] + _dot(n_k[c], x[c].astype(BF16)) for c in chunks]
            if it + 1 < steps:
                n_k = [_dot(n_k[c], n_k[c]).astype(BF16) for c in chunks]
        zero_b = jnp.zeros((PAIR, PAIR), BF16)
        ry = [_dot(a_r[c], jnp.concatenate(
            [x[c].astype(BF16), jnp.concatenate([zero_b, v_sb[c]], axis=1)], axis=0)) for c in chunks]
        for c in chunks:
            r_c = rt_all[rows[c]]
            rp_ref[out_rows[c], :] = (ry[c][:CHUNK, :PAIR] + ry[c][CHUNK:, :PAIR] + r_c).astype(BF16)
            yq_ref[out_rows[c], :] = (ry[c][:CHUNK, PAIR:] + ry[c][CHUNK:, PAIR:]).astype(BF16)
        p_t = [x[c][:, :PAIR].T.astype(BF16) for c in chunks]
        qv_t = [jnp.concatenate([x[c][:, PAIR:], v_s[c]], axis=0).T.astype(BF16) for c in chunks]
        c0 = sub * n_chunks
        for c in chunks:
            g_ref[c0 + c, 0] = (jnp.where(eye, jnp.exp(last[c]), 0.0)
                                + _dot(p_t[c], bk_h[c][:PAIR]))
        for c in chunks:
            hbd = _dot(qv_t[c], bk_h[c])
            h_ref[c0 + c, 0] = hbd[:HEAD] + hbd[HEAD:]


def _rwkv_prep(proj, lora, mu_rkv, pvec, w2pad, a2pad, g2, seq_len, col0):
    t = proj.shape[0]
    width = mu_rkv.shape[1]
    n_pairs = width // PAIR
    tt = TT_PREP
    nl = lora.shape[1]
    cb = col0 // PAIR
    wb = width // PAIR

    def prev_rows(i):
        return jnp.maximum(i * (tt // 8) - 1, 0)

    def tok(c):
        return pl.BlockSpec((tt, PAIR), lambda i, p: (i, c + p))

    def prev(c):
        return pl.BlockSpec((8, PAIR), lambda i, p: (prev_rows(i), c + p))

    out_tok = pl.BlockSpec((tt, PAIR), lambda i, p: (i, p))
    tok_shape = jax.ShapeDtypeStruct((t, width), BF16)
    return pl.pallas_call(
        functools.partial(_rwkv_prep_kernel, seq_len),
        grid=(t // tt, n_pairs),
        in_specs=[
            tok(cb), tok(cb + wb), tok(cb + 2 * wb),
            prev(cb), prev(cb + wb), prev(cb + 2 * wb),
            pl.BlockSpec((tt, nl), lambda i, p: (i, 0)),
            pl.BlockSpec((8, nl), lambda i, p: (prev_rows(i), 0)),
            pl.BlockSpec((3, PAIR), lambda i, p: (0, p)),
            pl.BlockSpec((8, PAIR), lambda i, p: (0, p)),
            pl.BlockSpec((PAIR, PAIR), lambda i, p: (0, p)),
            pl.BlockSpec((PAIR, PAIR), lambda i, p: (0, p)),
            pl.BlockSpec((PAIR, PAIR), lambda i, p: (0, p)),
        ],
        out_specs=[
            out_tok, out_tok, out_tok, out_tok,
            pl.BlockSpec((tt // CHUNK, 1, PAIR, PAIR), lambda i, p: (i, p, 0, 0)),
            pl.BlockSpec((tt // CHUNK, 1, HEAD, PAIR), lambda i, p: (i, p, 0, 0)),
        ],
        out_shape=[
            tok_shape, tok_shape, tok_shape, tok_shape,
            jax.ShapeDtypeStruct((t // CHUNK, n_pairs, PAIR, PAIR), F32),
            jax.ShapeDtypeStruct((t // CHUNK, n_pairs, HEAD, PAIR), F32),
        ],
        compiler_params=_params(("arbitrary", "arbitrary")),
        name="rwkv_prep",
    )(proj, proj, proj, proj, proj, proj, lora, lora, mu_rkv, pvec, w2pad, a2pad, g2)


def _rwkv_scan_kernel(rp_ref, yq_ref, bonus_ref, gate_ref, g_ref, h_ref, pvec_ref, o_ref, s_scr, y_scr):
    nb, tt, width = rp_ref.shape
    n_pairs = width // PAIR

    @pl.when(pl.program_id(0) == 0)
    def _():
        s_scr[...] = jnp.zeros_like(s_scr)

    lane = lax.broadcasted_iota(I32, (HEAD, PAIR), 1)
    is_lo = lane < HEAD
    lane_r = lax.broadcasted_iota(I32, (PAIR, PAIR), 0)
    lane_c = lax.broadcasted_iota(I32, (PAIR, PAIR), 1)
    bd = jnp.where((lane_r // HEAD) == (lane_c // HEAD), 1.0, 0.0).astype(BF16)

    chains = [(b, p) for b in range(nb) for p in range(n_pairs)]
    state = {ch: s_scr[ch[0], ch[1]] for ch in chains}
    for c in range(tt // CHUNK):
        rows = slice(c * CHUNK, (c + 1) * CHUNK)
        for (b, p) in chains:
            cols = slice(p * PAIR, (p + 1) * PAIR)
            s0 = state[(b, p)]
            s_st = jnp.concatenate([jnp.where(is_lo, s0, 0.0), jnp.where(is_lo, 0.0, s0)], axis=0)
            y_scr[b, rows, cols] = (_dot_nt(rp_ref[b, rows, cols], s_st.astype(BF16))
                                    + yq_ref[b, rows, cols].astype(F32))
        state = {(b, p): _dot3(state[(b, p)], g_ref[b, c, p]) + h_ref[b, c, p] for (b, p) in chains}
    for (b, p) in chains:
        s_scr[b, p] = state[(b, p)]

    pvec = pvec_ref[...]
    for (b, p) in chains:
        cols = slice(p * PAIR, (p + 1) * PAIR)
        y = y_scr[b, :, cols]
        mu = _head_sum(y, bd) * (1.0 / HEAD)
        yc = y - mu
        var = _head_sum(yc * yc, bd) * (1.0 / HEAD)
        yn = yc * lax.rsqrt(var + LNX_EPS) * pvec[5:6, cols] + pvec[6:7, cols]
        o_ref[b, :, cols] = ((yn + bonus_ref[b, :, cols].astype(F32))
                             * gate_ref[b, :, cols].astype(F32))


def _rwkv_scan(rp, yq, bonus, gate, g, h, pvec, batch):
    t, width = rp.shape
    n_pairs = width // PAIR
    seq = t // batch
    tt = TT_RWKV
    nc = tt // CHUNK
    tok = pl.BlockSpec((batch, tt, width), lambda i: (0, i, 0))
    as_seq = lambda z: z.reshape(batch, seq, width)
    out = pl.pallas_call(
        _rwkv_scan_kernel,
        grid=(seq // tt,),
        in_specs=[
            tok, tok, tok, tok,
            pl.BlockSpec((batch, nc, n_pairs, PAIR, PAIR), lambda i: (0, i, 0, 0, 0)),
            pl.BlockSpec((batch, nc, n_pairs, HEAD, PAIR), lambda i: (0, i, 0, 0, 0)),
            pl.BlockSpec((8, width), lambda i: (0, 0)),
        ],
        out_specs=tok,
        out_shape=jax.ShapeDtypeStruct((batch, seq, width), F32),
        scratch_shapes=[pltpu.VMEM((batch, n_pairs, HEAD, PAIR), F32),
                        pltpu.VMEM((batch, tt, width), F32)],
        compiler_params=_params(("arbitrary",)),
        name="rwkv_scan",
    )(as_seq(rp), as_seq(yq), as_seq(bonus), as_seq(gate),
      g.reshape(batch, seq // CHUNK, n_pairs, PAIR, PAIR),
      h.reshape(batch, seq // CHUNK, n_pairs, HEAD, PAIR), pvec)
    return out.reshape(t, width)


def _out_proj_kernel(x_ref, ya_ref, yb_ref, wo_ref, g_ref, rwt_ref, rb_ref, h1_ref, hn_ref, lg_ref):
    wa = ya_ref.shape[1]
    h1 = (x_ref[...] + _dot(ya_ref[...].astype(BF16), wo_ref[0:wa, :])
          + _dot(yb_ref[...].astype(BF16), wo_ref[wa:, :]))
    h1_ref[...] = h1
    hn = h1 * lax.rsqrt(jnp.mean(h1 * h1, axis=-1, keepdims=True) + RMS_EPS) * g_ref[...]
    _store_token_tiles(hn_ref, hn)
    lg_ref[...] = _dot3_nt(rwt_ref[...], hn) + rb_ref[...]


def _out_proj(x2, ya, yb, wo_bf, g2, rwt, rb):
    t, d = x2.shape
    wa = ya.shape[1]
    wb = yb.shape[1]
    ne = rwt.shape[0]
    tm = TM_PROJ
    return pl.pallas_call(
        _out_proj_kernel,
        grid=(t // tm,),
        in_specs=[
            pl.BlockSpec((tm, d), lambda i: (i, 0)),
            pl.BlockSpec((tm, wa), lambda i: (i, 0)),
            pl.BlockSpec((tm, wb), lambda i: (i, 0)),
            pl.BlockSpec((wa + wb, d), lambda i: (0, 0)),
            pl.BlockSpec((1, d), lambda i: (0, 0)),
            pl.BlockSpec((ne, d), lambda i: (0, 0)),
            pl.BlockSpec((ne, 1), lambda i: (0, 0)),
        ],
        out_specs=[
            pl.BlockSpec((tm, d), lambda i: (i, 0)),
            pl.BlockSpec((tm * SUBLANES, LANES), lambda i: (i, 0)),
            pl.BlockSpec((ne, tm), lambda i: (0, i)),
        ],
        out_shape=[
            jax.ShapeDtypeStruct((t, d), F32),
            jax.ShapeDtypeStruct((t * SUBLANES, LANES), F32),
            jax.ShapeDtypeStruct((ne, t), F32),
        ],
        compiler_params=_params(("arbitrary",)),
        name="out_proj",
    )(x2, ya, yb, wo_bf, g2, rwt, rb)


def _route_kernel(lg_ref, pos_ref, gate_ref, pend_ref, li_ref, rstart_ref, rlen_ref,
                  carry_scr, pstart_scr):
    phase = pl.program_id(0)
    first = pl.program_id(1) == 0

    @pl.when(first & (phase == 0))
    def _():
        carry_scr[...] = jnp.zeros_like(carry_scr)
        pstart_scr[...] = jnp.zeros_like(pstart_scr)
        pend_ref[...] = jnp.zeros_like(pend_ref)

    @pl.when(first & (phase == 1))
    def _():
        counts = carry_scr[...]
        padded = jnp.ceil(counts * (1.0 / BM_MOE)) * BM_MOE
        row = lax.broadcasted_iota(I32, counts.shape, 0)
        end = padded
        s = 1
        while s < counts.shape[0]:
            end = end + jnp.where(row >= s, pltpu.roll(end, s, 0), 0.0)
            s *= 2
        pstart_scr[...] = end - padded
        pend_ref[...] = end.astype(I32)
        carry_scr[...] = jnp.zeros_like(carry_scr)

    l = lg_ref[...]
    ne, tr = l.shape
    e_iota = lax.broadcasted_iota(I32, (ne, tr), 0)
    chosen = jnp.zeros((ne, tr), F32)
    vals, sels = [], []
    for j in range(TOP_K):
        m = jnp.max(l, axis=0, keepdims=True)
        idx = jnp.min(jnp.where(l == m, e_iota, ne), axis=0, keepdims=True)
        sel = e_iota == idx
        vals.append(m)
        sels.append(sel)
        chosen = jnp.where(sel, 1.0, chosen)
        l = jnp.where(sel, -jnp.inf, l)
    ex = [jnp.exp(vj - vals[0]) for vj in vals]
    den = ex[0] + ex[1] + ex[2] + ex[3]
    for j in range(TOP_K):
        gate_ref[j:j + 1, :] = ex[j] / den

    ti = lax.broadcasted_iota(I32, (tr, tr), 0)
    tj = lax.broadcasted_iota(I32, (tr, tr), 1)
    upper = jnp.where(ti <= tj, 1.0, 0.0).astype(BF16)
    inc = _dot(chosen.astype(BF16), upper)
    carry = carry_scr[...]
    first_row = carry[:, 0:1] + pstart_scr[:, 0:1]
    row_of = inc - chosen + first_row
    for j in range(TOP_K):
        pj = jnp.sum(jnp.where(sels[j], row_of, 0.0), axis=0, keepdims=True)
        pos_ref[j:j + 1, :] = (pj * SUBLANES).astype(I32)
    carry_scr[...] = carry + inc[:, tr - 1:tr]

    tc = TC_COMBINE
    erow = lax.broadcasted_iota(I32, (ne, 128), 0)
    for s in range(tr // tc):
        cols = slice(s * tc, (s + 1) * tc)
        before = inc[:, s * tc - 1:s * tc] if s else jnp.zeros((ne, 1), F32)
        n_run = inc[:, (s + 1) * tc - 1:(s + 1) * tc] - before
        win_rows = jnp.broadcast_to(jnp.ceil(n_run * (1.0 / WIN_COMBINE)) * WIN_COMBINE, (ne, 128))
        base = win_rows
        k = 1
        while k < ne:
            base = base + jnp.where(erow >= k, pltpu.roll(base, k, 0), 0.0)
            k *= 2
        base = base - win_rows
        slot_of = (inc - chosen)[:, cols] - before + base[:, 0:1]
        for j in range(TOP_K):
            lj = jnp.sum(jnp.where(sels[j][:, cols], slot_of, 0.0), axis=0, keepdims=True)
            li_ref[j:j + 1, cols] = (lj * SUBLANES).astype(I32)
        rstart_ref[s * ne:(s + 1) * ne, :] = jnp.broadcast_to(first_row + before, (ne, 128)).astype(I32)
        rlen_ref[s * ne:(s + 1) * ne, :] = jnp.broadcast_to(n_run, (ne, 128)).astype(I32)


def _route(logits_t):
    ne, t = logits_t.shape
    tr = TR_ROUTE
    tok = pl.BlockSpec((TOP_K, tr), lambda ph, i: (0, i * ph))
    n_sub = tr // TC_COMBINE
    runs = pl.BlockSpec((n_sub * ne, 128), lambda ph, i: (i * ph, 0))
    runs_shape = jax.ShapeDtypeStruct((t // TC_COMBINE * ne, 128), I32)
    return pl.pallas_call(
        _route_kernel,
        grid=(2, t // tr),
        in_specs=[pl.BlockSpec((ne, tr), lambda ph, i: (0, i))],
        out_specs=[tok, tok, pl.BlockSpec((ne, 128), lambda ph, i: (0, 0)), tok, runs, runs],
        out_shape=[
            jax.ShapeDtypeStruct((TOP_K, t), I32),
            jax.ShapeDtypeStruct((TOP_K, t), F32),
            jax.ShapeDtypeStruct((ne, 128), I32),
            jax.ShapeDtypeStruct((TOP_K, t), I32),
            runs_shape, runs_shape,
        ],
        scratch_shapes=[pltpu.VMEM((ne, 128), F32), pltpu.VMEM((ne, 128), F32)],
        compiler_params=_params(("arbitrary", "arbitrary")),
        name="route",
    )(logits_t)


def _dispatch_kernel(pend_ref, nused_ref, *refs):
    pos_refs = refs[:TOP_K]
    hn_ref, xin_ref, zero_scr, sem = refs[TOP_K:]
    bm = zero_scr.shape[0] // SUBLANES
    td = pos_refs[0].shape[0]
    i = pl.program_id(0)

    def tile(ref, token):
        return ref.at[pl.ds(pl.multiple_of(token * SUBLANES, SUBLANES), SUBLANES)]

    def zero_block(row0):
        start = pl.multiple_of(row0 * SUBLANES, bm * SUBLANES)
        return pltpu.make_async_copy(zero_scr, xin_ref.at[pl.ds(start, bm * SUBLANES)], sem)

    @pl.when(i == 0)
    def _():
        zero_scr[...] = jnp.zeros_like(zero_scr)

        def has_rows(e):
            end = pend_ref[e]
            return jnp.where(e == 0, end >= bm, end > pend_ref[jnp.maximum(e - 1, 0)])

        def start_one(e, carry):
            @pl.when(has_rows(e))
            def _():
                zero_block(pend_ref[e] - bm).start()
            return carry

        def wait_one(e, carry):
            @pl.when(has_rows(e))
            def _():
                zero_block(0).wait()
            return carry

        def start_tail(blk, carry):
            zero_block(blk * bm).start()
            return carry

        def wait_tail(blk, carry):
            zero_block(0).wait()
            return carry

        n_blocks = xin_ref.shape[0] // (bm * SUBLANES)
        lax.fori_loop(0, pend_ref.shape[0], start_one, 0)
        lax.fori_loop(nused_ref[0], n_blocks, start_tail, 0)
        lax.fori_loop(0, pend_ref.shape[0], wait_one, 0)
        lax.fori_loop(nused_ref[0], n_blocks, wait_tail, 0)

    def issue(tk, carry):
        for j in range(TOP_K):
            dst = xin_ref.at[pl.ds(pl.multiple_of(pos_refs[j][tk], SUBLANES), SUBLANES)]
            pltpu.make_async_copy(tile(hn_ref, tk), dst, sem).start(priority=j % 2)
        return carry

    lax.fori_loop(0, td, issue, 0, unroll=8)

    for j in range(TOP_K):
        pltpu.make_async_copy(hn_ref, xin_ref.at[pl.ds(0, td * SUBLANES)], sem).wait()


def _choice_specs(n_steps, tile):
    return [pl.BlockSpec((tile,), functools.partial(lambda j, i, *_: (j * n_steps + i,), j),
                         memory_space=pltpu.SMEM) for j in range(TOP_K)]


def _dispatch(pad_end, n_used, pos, hn_tiles, n_rows):
    t = hn_tiles.shape[0] // SUBLANES
    td = TD_DISPATCH
    pos = pos.reshape(-1)
    grid_spec = pltpu.PrefetchScalarGridSpec(
        num_scalar_prefetch=2,
        grid=(t // td,),
        in_specs=_choice_specs(t // td, td) + [
            pl.BlockSpec((td * SUBLANES, LANES), lambda i, pe, nu: (i, 0)),
        ],
        out_specs=pl.BlockSpec(memory_space=pl.ANY),
        scratch_shapes=[pltpu.VMEM((BM_MOE * SUBLANES, LANES), F32), pltpu.SemaphoreType.DMA(())],
    )
    return pl.pallas_call(
        _dispatch_kernel,
        grid_spec=grid_spec,
        out_shape=jax.ShapeDtypeStruct((n_rows * SUBLANES, LANES), F32),
        compiler_params=_params(("arbitrary",)),
        name="dispatch",
    )(pad_end, n_used, *([pos] * TOP_K), hn_tiles)


def _experts_kernel(pend_ref, nused_ref, w1_ref, b1_ref, w2_ref, b2_ref, xin_ref, yout_ref,
                    xbuf, obuf, wt_scr, wg_scr, wl_scr, w2_scr, sem_in, sem_out):
    e = pl.program_id(0)
    rows_blk = xbuf.shape[1]
    bm = rows_blk // SUBLANES
    dh = w2_ref.shape[1]
    n_used = nused_ref[0]
    blk_lo = jnp.where(e == 0, 0, pend_ref[jnp.maximum(e - 1, 0)]) // bm
    blk_hi = pend_ref[e] // bm

    def block(ref, blk):
        return ref.at[pl.ds(pl.multiple_of(blk * rows_blk, rows_blk), rows_blk)]

    def x_copy(blk, slot):
        return pltpu.make_async_copy(block(xin_ref, blk), xbuf.at[slot], sem_in.at[slot])

    def o_copy(blk, slot):
        return pltpu.make_async_copy(obuf.at[slot], block(yout_ref, blk), sem_out.at[slot])

    @pl.when((e == 0) & (n_used > 0))
    def _():
        x_copy(0, 0).start()

    @pl.when(blk_hi > blk_lo)
    def _():
        n_slab, n_t, lanes = wt_scr.shape
        for part in range(2 * dh // n_t):
            rows = slice(part * (n_t // 2), (part + 1) * (n_t // 2))
            for s in range(n_slab):
                cols = slice(s * lanes, (s + 1) * lanes)
                wt_scr[s] = w1_ref[0, cols, part * n_t:(part + 1) * n_t].T
                wg_scr[rows, cols] = wt_scr[s, pl.ds(0, n_t // 2, stride=2), :].astype(BF16)
                wl_scr[rows, cols] = wt_scr[s, pl.ds(1, n_t // 2, stride=2), :].astype(BF16)
        w2_scr[...] = w2_ref[0].astype(BF16)

        def one_block(blk, carry):
            slot = blk % 2

            @pl.when(blk + 1 < n_used)
            def _():
                x_copy(blk + 1, 1 - slot).start()

            x_copy(blk, slot).wait()

            @pl.when(blk >= 2)
            def _():
                o_copy(blk - 2, slot).wait()

            xb = _load_token_tiles(xbuf.at[slot], 0, bm).astype(BF16)
            bias = b1_ref[0]
            glu = jnp.minimum(_dot_nt(xb, wg_scr[...]) + bias[:, :dh], SWIGLU_LIMIT)
            lin = jnp.clip(_dot_nt(xb, wl_scr[...]) + bias[:, dh:], -SWIGLU_LIMIT, SWIGLU_LIMIT)
            act = glu * jax.nn.sigmoid(SWIGLU_ALPHA * glu) * (lin + 1.0)
            _store_token_tiles(obuf.at[slot], _dot(act.astype(BF16), w2_scr[...]) + b2_ref[0])
            o_copy(blk, slot).start()
            return carry

        lax.fori_loop(blk_lo, blk_hi, one_block, 0)

    @pl.when(e == pl.num_programs(0) - 1)
    def _():
        @pl.when(n_used >= 1)
        def _():
            o_copy(0, (n_used - 1) % 2).wait()

        @pl.when(n_used >= 2)
        def _():
            o_copy(0, n_used % 2).wait()

        obuf[0] = jnp.zeros(obuf.shape[1:], obuf.dtype)
        n_blocks = yout_ref.shape[0] // rows_blk

        def start_tail(blk, carry):
            o_copy(blk, 0).start()
            return carry

        def wait_tail(blk, carry):
            o_copy(0, 0).wait()
            return carry

        lax.fori_loop(n_used, n_blocks, start_tail, 0)
        lax.fori_loop(n_used, n_blocks, wait_tail, 0)


def _experts(pad_end, n_used, xin, w1, b1p, w2, b2):
    ne, d, dh2 = w1.shape
    dh = dh2 // 2
    rows_blk = BM_MOE * SUBLANES
    grid_spec = pltpu.PrefetchScalarGridSpec(
        num_scalar_prefetch=2,
        grid=(ne,),
        in_specs=[
            pl.BlockSpec((1, d, dh2), lambda e, pe, nu: (e, 0, 0)),
            pl.BlockSpec((1, 1, dh2), lambda e, pe, nu: (e, 0, 0)),
            pl.BlockSpec((1, dh, d), lambda e, pe, nu: (e, 0, 0)),
            pl.BlockSpec((1, 1, d), lambda e, pe, nu: (e, 0, 0)),
            pl.BlockSpec(memory_space=pl.ANY),
        ],
        out_specs=pl.BlockSpec(memory_space=pl.ANY),
        scratch_shapes=[
            pltpu.VMEM((2, rows_blk, LANES), F32),
            pltpu.VMEM((2, rows_blk, LANES), F32),
            pltpu.VMEM((d // 128, dh, 128), F32),
            pltpu.VMEM((dh, d), BF16),
            pltpu.VMEM((dh, d), BF16),
            pltpu.VMEM((dh, d), BF16),
            pltpu.SemaphoreType.DMA((2,)),
            pltpu.SemaphoreType.DMA((2,)),
        ],
    )
    return pl.pallas_call(
        _experts_kernel,
        grid_spec=grid_spec,
        out_shape=jax.ShapeDtypeStruct(xin.shape, F32),
        compiler_params=_params(("arbitrary",)),
        name="experts",
    )(pad_end, n_used, w1, b1p, w2, b2, xin)


def _combine_kernel(*refs):
    rstart_ref, rlen_ref, rstart_next_ref, rlen_next_ref = refs[:4]
    li_refs = refs[4:4 + TOP_K]
    gate_refs = refs[4 + TOP_K:4 + 2 * TOP_K]
    h1_ref, fg_ref, yout_ref, o_ref, wbuf, hacc, sem = refs[4 + 2 * TOP_K:]
    tc = o_ref.shape[0]
    i = pl.program_id(0)
    slot = i % 2
    n_experts = rstart_ref.shape[0]
    win_rows = WIN_COMBINE * SUBLANES

    def tile(ref, token):
        return ref.at[pl.ds(pl.multiple_of(token * SUBLANES, SUBLANES), SUBLANES)]

    def windows_of(rl_ref, e):
        return (rl_ref[e, 0] + (WIN_COMBINE - 1)) // WIN_COMBINE

    def start_windows(rs_ref, rl_ref, s):
        def per_expert(e, base):
            start = rs_ref[e, 0]
            n_win = windows_of(rl_ref, e)

            def per_window(w, carry):
                src = yout_ref.at[pl.ds(pl.multiple_of((start + w * WIN_COMBINE) * SUBLANES, SUBLANES),
                                        win_rows)]
                dst = wbuf.at[s, pl.ds(pl.multiple_of((base + w) * win_rows, win_rows), win_rows)]
                pltpu.make_async_copy(src, dst, sem.at[s]).start()
                return carry

            lax.fori_loop(0, n_win, per_window, 0)
            return base + n_win

        lax.fori_loop(0, n_experts, per_expert, 0)

    @pl.when(i == 0)
    def _():
        start_windows(rstart_ref, rlen_ref, 0)

    @pl.when(i + 1 < pl.num_programs(0))
    def _():
        start_windows(rstart_next_ref, rlen_next_ref, 1 - slot)

    n_windows = lax.fori_loop(0, n_experts, lambda e, n: n + windows_of(rlen_ref, e), 0)

    def wait_window(w, carry):
        pltpu.make_async_copy(yout_ref.at[pl.ds(0, win_rows)], wbuf.at[slot, pl.ds(0, win_rows)],
                              sem.at[slot]).wait()
        return carry

    lax.fori_loop(0, n_windows, wait_window, 0)

    def window_tile(j, tk):
        return wbuf[slot, pl.ds(pl.multiple_of(li_refs[j][tk], SUBLANES), SUBLANES), :]

    def per_token(tk, carry):
        acc = gate_refs[0][tk] * window_tile(0, tk)
        for j in range(1, TOP_K):
            acc = acc + gate_refs[j][tk] * window_tile(j, tk)
        tile(hacc, tk)[...] = acc
        return carry

    lax.fori_loop(0, tc, per_token, 0, unroll=32)
    h = h1_ref[...] + _load_token_tiles(hacc, 0, tc)
    o_ref[...] = h * lax.rsqrt(jnp.mean(h * h, axis=-1, keepdims=True) + RMS_EPS) * fg_ref[...]


def _combine(rstart, rlen, li, gates, h1, fg, yout):
    t, d = h1.shape
    tc = TC_COMBINE
    n_steps = t // tc
    n_experts = rstart.shape[0] // n_steps
    buf_rows = TOP_K * tc + n_experts * WIN_COMBINE
    runs = pl.BlockSpec((n_experts, 128), lambda i: (i, 0), memory_space=pltpu.SMEM)
    runs_next = pl.BlockSpec((n_experts, 128), lambda i: (jnp.minimum(i + 1, n_steps - 1), 0),
                             memory_space=pltpu.SMEM)
    return pl.pallas_call(
        _combine_kernel,
        grid=(n_steps,),
        in_specs=[runs, runs, runs_next, runs_next]
        + _choice_specs(n_steps, tc) + _choice_specs(n_steps, tc) + [
            pl.BlockSpec((tc, d), lambda i: (i, 0)),
            pl.BlockSpec((1, d), lambda i: (0, 0)),
            pl.BlockSpec(memory_space=pl.ANY),
        ],
        out_specs=pl.BlockSpec((tc, d), lambda i: (i, 0)),
        out_shape=jax.ShapeDtypeStruct((t, d), F32),
        scratch_shapes=[pltpu.VMEM((2, buf_rows * SUBLANES, LANES), F32),
                        pltpu.VMEM((tc * SUBLANES, LANES), F32),
                        pltpu.SemaphoreType.DMA((2,))],
        compiler_params=_params(("arbitrary",)),
        name="combine",
    )(rstart, rlen, rstart, rlen, *([li.reshape(-1)] * TOP_K), *([gates.reshape(-1)] * TOP_K),
      h1, fg, yout)


def kernel(x, norm1_g, w_in, sgu_ln_g, sgu_ln_b, sgu_w, sgu_b, mu_rkv, mu_wag, decay_w0, decay_w1,
           decay_w2, iclr_a0, iclr_a1, iclr_a2, gate_g1, gate_g2, k_k, k_a, r_k, lnx_g, lnx_b, w_out,
           norm2_g, router_w, router_b, moe_w1, moe_b1, moe_w2, moe_b2, final_g):
    batch, seq, d = x.shape
    t = batch * seq
    depth = w_in.shape[0]
    sgu_width = sgu_ln_g.shape[1]
    rw = mu_rkv.shape[2]
    n_dec, n_icl, n_gate = decay_w1.shape[2], iclr_a1.shape[2], gate_g1.shape[2]
    assert n_dec == HEAD and n_icl == HEAD and n_gate == PAIR and rw % PAIR == 0
    assert seq % TT_RWKV == 0 and seq % TT_PREP == 0 and t % TM_PROJ == 0
    assert sgu_w.shape[2] == SGU_BLOCK
    assert depth == 1, "the final RMSNorm is fused into the last layer's combine kernel"
    assert d == SUBLANES * LANES, "the MoE row movers copy one (8, 128) f32 tile per token"
    assert BM_MOE + N_EXPERTS - 1 >= WIN_COMBINE

    h = x.reshape(t, d)
    for l in range(depth):
        win_bf = w_in[l].astype(BF16)
        wl = jnp.concatenate([decay_w1[l], iclr_a1[l], gate_g1[l]], axis=1)
        mucat = jnp.concatenate([
            jnp.broadcast_to(mu_wag[l, 0][:, None], (d, n_dec)),
            jnp.broadcast_to(mu_wag[l, 1][:, None], (d, n_icl)),
            jnp.broadcast_to(mu_wag[l, 2][:, None], (d, n_gate))], axis=1)
        zeros = jnp.zeros((HEAD, rw), F32)
        w2pad = jnp.concatenate([decay_w2[l], zeros], axis=0)
        a2pad = jnp.concatenate([zeros, iclr_a2[l]], axis=0)
        pvec = jnp.stack([decay_w0[l], iclr_a0[l], k_k[l], k_a[l], r_k[l].reshape(-1),
                          lnx_g[l], lnx_b[l], jnp.zeros((rw,), F32)], axis=0)
        bias2d = jnp.repeat(sgu_b[l].T, sgu_width // sgu_b.shape[1], axis=1)
        wo_bf = w_out[l].astype(BF16)
        b1p = jnp.concatenate([moe_b1[l][:, 0::2], moe_b1[l][:, 1::2]], axis=-1)[:, None, :]
        b2 = moe_b2[l][:, None, :]

        assert w_in.shape[2] == 2 * sgu_width + 3 * rw
        proj, lora, ya = _in_proj(h, norm1_g[l][None, :], win_bf, wl, mucat, sgu_ln_g[l][None, :],
                                  sgu_ln_b[l][None, :], sgu_w[l], bias2d)
        rp, yq, bonus, gate, g_m, h_m = _rwkv_prep(proj, lora, mu_rkv[l], pvec, w2pad, a2pad,
                                                   gate_g2[l], seq, 0)
        yb = _rwkv_scan(rp, yq, bonus, gate, g_m, h_m, pvec, batch)
        h1, hn2, logits_t = _out_proj(h, ya, yb, wo_bf, norm2_g[l][None, :], router_w[l].T,
                                      router_b[l][:, None])
        pos, gates, pend, li, rstart, rlen = _route(logits_t)

        pad_end = pend[:, 0]
        n_blocks = (t * TOP_K) // BM_MOE + N_EXPERTS
        n_used = pad_end[-1:] // BM_MOE

        xin = _dispatch(pad_end, n_used, pos, hn2, n_blocks * BM_MOE)
        yout = _experts(pad_end, n_used, xin, moe_w1[l], b1p, moe_w2[l], b2)
        h = _combine(rstart, rlen, li, gates, h1, final_g[None, :], yout)
    return h.reshape(batch, seq, d)
```

```python
import functools

import jax
import jax.numpy as jnp
from jax import lax
from jax.experimental import pallas as pl
from jax.experimental.pallas import tpu as pltpu

F32 = jnp.float32
BF16 = jnp.bfloat16
I32 = jnp.int32

RMS_EPS = 1e-5
LN_EPS = 1e-5
LNX_EPS = 64e-5
CHUNK = 64
SGU_BLOCK = 128
HEAD = 64
PAIR = 2 * HEAD
N_EXPERTS = 32
TOP_K = 4
SWIGLU_ALPHA = 1.702
SWIGLU_LIMIT = 7.0

V7X_VMEM_LIMIT = 56 * 1024 * 1024

TM_PROJ = 1024
TT_PREP = 1024
N_SUB_PREP = 1
TT_RWKV = 512
TR_ROUTE = 1024
BM_MOE = 256
TD_DISPATCH = 2048
TC_COMBINE = 512
WIN_COMBINE = 64


def _dot(a, b):
    return jnp.dot(a, b, preferred_element_type=F32)


def _dot_nt(a, b):
    return lax.dot_general(a, b, (((1,), (1,)), ((), ())), preferred_element_type=F32)


def _split(x):
    hi = x.astype(BF16)
    lo = (x - hi.astype(F32)).astype(BF16)
    return hi, lo


def _dot3(a, b):
    ah, al = _split(a)
    bh, bl = _split(b)
    return _dot(ah, bh) + _dot(al, bh) + _dot(ah, bl)


def _dot3_nt(a, b):
    ah, al = _split(a)
    bh, bl = _split(b)
    return _dot_nt(ah, bh) + _dot_nt(al, bh) + _dot_nt(ah, bl)


def _head_sum(x, bd):
    hi, lo = _split(x)
    return _dot(hi, bd) + _dot(lo, bd)


SUBLANES = 8
LANES = 128


def _store_token_tiles(ref, x):
    n = x.shape[0]
    for s in range(SUBLANES):
        ref[pl.ds(s, n, stride=SUBLANES), :] = x[:, s * LANES:(s + 1) * LANES]


def _load_token_tiles(ref, row0, n):
    return jnp.concatenate(
        [ref[pl.ds(row0 + s, n, stride=SUBLANES), :] for s in range(SUBLANES)], axis=1)


def _params(sem, vmem=V7X_VMEM_LIMIT):
    return pltpu.CompilerParams(dimension_semantics=sem, vmem_limit_bytes=vmem)


def _in_proj_kernel(x_ref, g_ref, win_ref, wl_ref, mucat_ref, lng_ref, lnb_ref, w_ref, bias_ref,
                    proj_ref, lora_ref, o_ref, wl_scr, wm_scr):
    n_heads = w_ref.shape[0]

    @pl.when(pl.program_id(0) == 0)
    def _():
        wl = wl_ref[...]
        mu = mucat_ref[...]
        n = wl.shape[1]
        wl_scr[:, 0:n] = (wl * (1.0 - mu)).astype(BF16)
        wl_scr[:, n:2 * n] = (wl * mu).astype(BF16)
        qi = lax.broadcasted_iota(I32, (SGU_BLOCK, SGU_BLOCK), 0) // CHUNK
        kj = lax.broadcasted_iota(I32, (SGU_BLOCK, SGU_BLOCK), 1) // CHUNK
        for h in range(n_heads):
            wm_scr[h] = jnp.where(kj <= qi, w_ref[h], 0.0).astype(BF16)

    x = x_ref[...]
    hn = x * lax.rsqrt(jnp.mean(x * x, axis=-1, keepdims=True) + RMS_EPS) * g_ref[...]
    hb = hn.astype(BF16)
    width = lng_ref.shape[1]
    two_w = 2 * width
    proj_ref[...] = _dot(hb, win_ref[:, two_w:])
    lora_ref[...] = _dot(hb, wl_scr[...])

    z = _dot(hb, win_ref[:, :two_w])
    tm = z.shape[0]
    gz = 0.5 * z * (1.0 + lax.erf(z * (2.0 ** -0.5)))
    u = gz[:, :width]
    v = gz[:, width:]
    mu = jnp.mean(v, axis=-1, keepdims=True)
    vc = v - mu
    var = jnp.mean(vc * vc, axis=-1, keepdims=True)
    vn = vc * lax.rsqrt(var + LN_EPS) * lng_ref[...] + lnb_ref[...]
    lane = lax.broadcasted_iota(I32, (SGU_BLOCK, PAIR), 1)
    is_lo = lane < HEAD
    bias = bias_ref[...]
    for blk in range(tm // SGU_BLOCK):
        rows = slice(blk * SGU_BLOCK, (blk + 1) * SGU_BLOCK)
        for p in range(width // PAIR):
            cols = slice(p * PAIR, (p + 1) * PAIR)
            vp = vn[rows, cols]
            lo = jnp.where(is_lo, vp, 0.0).astype(BF16)
            hi = jnp.where(is_lo, 0.0, vp).astype(BF16)
            sv = _dot(wm_scr[2 * p], lo) + _dot(wm_scr[2 * p + 1], hi)
            o_ref[rows, cols] = u[rows, cols] * (sv + bias[:, cols])


def _in_proj(x2, g, win_bf, wl, mucat, ln_g, ln_b, w_s, bias2d):
    t, d = x2.shape
    width = ln_g.shape[1]
    n_rkv = win_bf.shape[1] - 2 * width
    n_l = wl.shape[1]
    n_heads = w_s.shape[0]
    tm = TM_PROJ
    const = lambda *shape: pl.BlockSpec(shape, lambda i: (0,) * len(shape))
    row = lambda n: pl.BlockSpec((tm, n), lambda i: (i, 0))
    return pl.pallas_call(
        _in_proj_kernel,
        grid=(t // tm,),
        in_specs=[
            row(d), const(1, d), const(d, win_bf.shape[1]), const(d, n_l), const(d, n_l),
            const(1, width), const(1, width), const(n_heads, SGU_BLOCK, SGU_BLOCK),
            const(SGU_BLOCK, width),
        ],
        out_specs=[row(n_rkv), row(2 * n_l), row(width)],
        out_shape=[
            jax.ShapeDtypeStruct((t, n_rkv), F32),
            jax.ShapeDtypeStruct((t, 2 * n_l), F32),
            jax.ShapeDtypeStruct((t, width), F32),
        ],
        scratch_shapes=[pltpu.VMEM((d, 2 * n_l), BF16),
                        pltpu.VMEM((n_heads, SGU_BLOCK, SGU_BLOCK), BF16)],
        compiler_params=_params(("arbitrary",)),
        name="in_proj_sgu",
    )(x2, g, win_bf, wl, mucat, ln_g, ln_b, w_s, bias2d)


def _rwkv_prep_kernel(seq_len, pr_ref, pk_ref, pv_ref, ppr_ref, ppk_ref, ppv_ref, lo_ref, plo_ref,
                      mu_ref, pvec_ref, w2_ref, a2_ref, g2_ref,
                      rp_ref, yq_ref, bonus_ref, gate_ref, g_ref, h_ref):
    tt = pr_ref.shape[0]
    ts = tt // N_SUB_PREP
    n_chunks = ts // CHUNK
    i = pl.program_id(0)
    keep = jnp.where((i * tt) % seq_len == 0, 0.0, 1.0)

    mu = mu_ref[...]
    pvec = pvec_ref[...]
    w0, a0, k_k, k_a, r_k = pvec[0:1], pvec[1:2], pvec[2:3], pvec[3:4], pvec[4:5]
    w2b = w2_ref[...].astype(BF16)
    a2b = a2_ref[...].astype(BF16)
    g2b = g2_ref[...].astype(BF16)
    half = lo_ref.shape[1] // 2

    lane_r = lax.broadcasted_iota(I32, (PAIR, PAIR), 0)
    lane_c = lax.broadcasted_iota(I32, (PAIR, PAIR), 1)
    bd = jnp.where((lane_r // HEAD) == (lane_c // HEAD), 1.0, 0.0).astype(BF16)
    tri_ones = jnp.where(((lane_r // CHUNK) == (lane_c // CHUNK)) & (lane_c <= lane_r),
                         1.0, 0.0).astype(BF16)
    lane = lax.broadcasted_iota(I32, (CHUNK, PAIR), 1)
    is_lo = lane < HEAD

    def stack(x):
        return jnp.concatenate([jnp.where(is_lo, x, 0.0), jnp.where(is_lo, 0.0, x)], axis=0)

    n2 = 2 * PAIR
    ri = lax.broadcasted_iota(I32, (n2, n2), 0)
    ci = lax.broadcasted_iota(I32, (n2, n2), 1)
    same_head = ((ri // CHUNK) % 2) == ((ci // CHUNK) % 2)
    t_i = ri % CHUNK
    s_i = ci % CHUNK
    tri = same_head & (s_i < t_i + ri // PAIR)
    eye = lane_r == lane_c

    for sub in range(N_SUB_PREP):
        r0 = sub * ts
        tile_rows = slice(r0, r0 + ts)

        def shift(ref, pref, cols=slice(None)):
            x = ref[tile_rows, cols]
            if sub == 0:
                first = pref[7:8, cols] * keep
            else:
                first = ref[r0 - 1:r0, cols]
            rowc = lax.broadcasted_iota(I32, x.shape, 0)
            return x, jnp.where(rowc == 0, first, pltpu.roll(x, 1, 0))

        def shift_mix(ref, pref, m):
            x, xs = shift(ref, pref)
            return x + (xs - x) * m

        r = shift_mix(pr_ref, ppr_ref, mu[0:1])
        k = shift_mix(pk_ref, ppk_ref, mu[1:2])
        v = shift_mix(pv_ref, ppv_ref, mu[2:3])

        _, lo_b = shift(lo_ref, plo_ref, slice(half, 2 * half))
        l_all = lo_ref[tile_rows, :half] + lo_b
        l_wa = l_all[:, :PAIR]
        l_g = l_all[:, PAIR:]

        dw = _dot(jnp.tanh(l_wa).astype(BF16), w2b)
        ia = _dot(l_wa.astype(BF16), a2b)
        gate_ref[tile_rows, :] = _dot(jax.nn.sigmoid(l_g).astype(BF16), g2b).astype(BF16)

        zneg = -(w0 + dw)
        softplus = jnp.maximum(zneg, 0.0) + jnp.log(1.0 + jnp.exp(-jnp.abs(zneg)))
        logw = -jnp.exp(-softplus - 0.5)
        iclr = jax.nn.sigmoid(a0 + ia)

        kk = k * k_k
        kk = kk / jnp.maximum(jnp.sqrt(_head_sum(kk * kk, bd)), 1e-12)
        k2 = k * (1.0 + (iclr - 1.0) * k_a)
        a = -kk
        b = kk * iclr
        bonus_ref[tile_rows, :] = (_head_sum(r * k2 * r_k, bd) * v).astype(BF16)

        lw_hi, lw_lo = _split(logw)
        cl = jnp.concatenate(
            [_dot(tri_ones, lw_hi[q * PAIR:(q + 1) * PAIR]) + _dot(tri_ones, lw_lo[q * PAIR:(q + 1) * PAIR])
             for q in range(ts // PAIR)], axis=0)
        at_all = a * jnp.exp(cl - logw)
        rt_all = r * jnp.exp(cl)
        w_inv = jnp.exp(-cl)
        bt_all = b * w_inv
        kt_all = k2 * w_inv

        chunks = range(n_chunks)
        rows = [slice(c * CHUNK, (c + 1) * CHUNK) for c in chunks]
        out_rows = [slice(r0 + c * CHUNK, r0 + (c + 1) * CHUNK) for c in chunks]
        last = [cl[rows[c]][CHUNK - 1:CHUNK, :] for c in chunks]
        w_rem = [jnp.exp(last[c] - cl[rows[c]]) for c in chunks]
        a_s = [stack(at_all[rows[c]]) for c in chunks]
        r_s = [stack(rt_all[rows[c]]) for c in chunks]
        v_s = [stack(v[rows[c]]) for c in chunks]
        v_sb = [v_s[c].astype(BF16) for c in chunks]
        bk_h = [jnp.concatenate([stack(b[rows[c]] * w_rem[c]), stack(k2[rows[c]] * w_rem[c])],
                                axis=0).astype(BF16) for c in chunks]

        a_all = []
        for c in chunks:
            lhs = jnp.concatenate([a_s[c], r_s[c]], axis=0).astype(BF16)
            rhs = jnp.concatenate([stack(bt_all[rows[c]]), stack(kt_all[rows[c]])],
                                  axis=0).astype(BF16)
            a_all.append(jnp.where(tri, _dot_nt(lhs, rhs), 0.0))
        n_k = [a_all[c][:PAIR, :PAIR].astype(BF16) for c in chunks]
        a_r = [a_all[c][PAIR:, :].astype(BF16) for c in chunks]
        x = [jnp.concatenate([a_s[c], _dot(a_all[c][:PAIR, PAIR:].astype(BF16), v_sb[c])], axis=1)
             for c in chunks]
        steps = CHUNK.bit_length() - 1
        for it in range(steps):
            x = [x[c] + _dot(n_k[c], x[c].astype(BF16)) for c in chunks]
            if it + 1 < steps:
                n_k = [_dot(n_k[c], n_k[c]).astype(BF16) for c in chunks]
        zero_b = jnp.zeros((PAIR, PAIR), BF16)
        ry = [_dot(a_r[c], jnp.concatenate(
            [x[c].astype(BF16), jnp.concatenate([zero_b, v_sb[c]], axis=1)], axis=0)) for c in chunks]
        for c in chunks:
            r_c = rt_all[rows[c]]
            rp_ref[out_rows[c], :] = (ry[c][:CHUNK, :PAIR] + ry[c][CHUNK:, :PAIR] + r_c).astype(BF16)
            yq_ref[out_rows[c], :] = (ry[c][:CHUNK, PAIR:] + ry[c][CHUNK:, PAIR:]).astype(BF16)
        p_t = [x[c][:, :PAIR].T.astype(BF16) for c in chunks]
        qv_t = [jnp.concatenate([x[c][:, PAIR:], v_s[c]], axis=0).T.astype(BF16) for c in chunks]
        c0 = sub * n_chunks
        for c in chunks:
            g_ref[c0 + c, 0] = (jnp.where(eye, jnp.exp(last[c]), 0.0)
                                + _dot(p_t[c], bk_h[c][:PAIR]))
        for c in chunks:
            hbd = _dot(qv_t[c], bk_h[c])
            h_ref[c0 + c, 0] = hbd[:HEAD] + hbd[HEAD:]


def _rwkv_prep(proj, lora, mu_rkv, pvec, w2pad, a2pad, g2, seq_len, col0):
    t = proj.shape[0]
    width = mu_rkv.shape[1]
    n_pairs = width // PAIR
    tt = TT_PREP
    nl = lora.shape[1]
    cb = col0 // PAIR
    wb = width // PAIR

    def prev_rows(i):
        return jnp.maximum(i * (tt // 8) - 1, 0)

    def tok(c):
        return pl.BlockSpec((tt, PAIR), lambda i, p: (i, c + p))

    def prev(c):
        return pl.BlockSpec((8, PAIR), lambda i, p: (prev_rows(i), c + p))

    out_tok = pl.BlockSpec((tt, PAIR), lambda i, p: (i, p))
    tok_shape = jax.ShapeDtypeStruct((t, width), BF16)
    return pl.pallas_call(
        functools.partial(_rwkv_prep_kernel, seq_len),
        grid=(t // tt, n_pairs),
        in_specs=[
            tok(cb), tok(cb + wb), tok(cb + 2 * wb),
            prev(cb), prev(cb + wb), prev(cb + 2 * wb),
            pl.BlockSpec((tt, nl), lambda i, p: (i, 0)),
            pl.BlockSpec((8, nl), lambda i, p: (prev_rows(i), 0)),
            pl.BlockSpec((3, PAIR), lambda i, p: (0, p)),
            pl.BlockSpec((8, PAIR), lambda i, p: (0, p)),
            pl.BlockSpec((PAIR, PAIR), lambda i, p: (0, p)),
            pl.BlockSpec((PAIR, PAIR), lambda i, p: (0, p)),
            pl.BlockSpec((PAIR, PAIR), lambda i, p: (0, p)),
        ],
        out_specs=[
            out_tok, out_tok, out_tok, out_tok,
            pl.BlockSpec((tt // CHUNK, 1, PAIR, PAIR), lambda i, p: (i, p, 0, 0)),
            pl.BlockSpec((tt // CHUNK, 1, HEAD, PAIR), lambda i, p: (i, p, 0, 0)),
        ],
        out_shape=[
            tok_shape, tok_shape, tok_shape, tok_shape,
            jax.ShapeDtypeStruct((t // CHUNK, n_pairs, PAIR, PAIR), F32),
            jax.ShapeDtypeStruct((t // CHUNK, n_pairs, HEAD, PAIR), F32),
        ],
        compiler_params=_params(("arbitrary", "arbitrary")),
        name="rwkv_prep",
    )(proj, proj, proj, proj, proj, proj, lora, lora, mu_rkv, pvec, w2pad, a2pad, g2)


def _rwkv_scan_kernel(rp_ref, yq_ref, bonus_ref, gate_ref, g_ref, h_ref, pvec_ref, o_ref, s_scr, y_scr):
    nb, tt, width = rp_ref.shape
    n_pairs = width // PAIR

    @pl.when(pl.program_id(0) == 0)
    def _():
        s_scr[...] = jnp.zeros_like(s_scr)

    lane = lax.broadcasted_iota(I32, (HEAD, PAIR), 1)
    is_lo = lane < HEAD
    lane_r = lax.broadcasted_iota(I32, (PAIR, PAIR), 0)
    lane_c = lax.broadcasted_iota(I32, (PAIR, PAIR), 1)
    bd = jnp.where((lane_r // HEAD) == (lane_c // HEAD), 1.0, 0.0).astype(BF16)

    chains = [(b, p) for b in range(nb) for p in range(n_pairs)]
    state = {ch: s_scr[ch[0], ch[1]] for ch in chains}
    for c in range(tt // CHUNK):
        rows = slice(c * CHUNK, (c + 1) * CHUNK)
        for (b, p) in chains:
            cols = slice(p * PAIR, (p + 1) * PAIR)
            s0 = state[(b, p)]
            s_st = jnp.concatenate([jnp.where(is_lo, s0, 0.0), jnp.where(is_lo, 0.0, s0)], axis=0)
            y_scr[b, rows, cols] = (_dot_nt(rp_ref[b, rows, cols], s_st.astype(BF16))
                                    + yq_ref[b, rows, cols].astype(F32))
        state = {(b, p): _dot3(state[(b, p)], g_ref[b, c, p]) + h_ref[b, c, p] for (b, p) in chains}
    for (b, p) in chains:
        s_scr[b, p] = state[(b, p)]

    pvec = pvec_ref[...]
    for (b, p) in chains:
        cols = slice(p * PAIR, (p + 1) * PAIR)
        y = y_scr[b, :, cols]
        mu = _head_sum(y, bd) * (1.0 / HEAD)
        yc = y - mu
        var = _head_sum(yc * yc, bd) * (1.0 / HEAD)
        yn = yc * lax.rsqrt(var + LNX_EPS) * pvec[5:6, cols] + pvec[6:7, cols]
        o_ref[b, :, cols] = ((yn + bonus_ref[b, :, cols].astype(F32))
                             * gate_ref[b, :, cols].astype(F32))


def _rwkv_scan(rp, yq, bonus, gate, g, h, pvec, batch):
    t, width = rp.shape
    n_pairs = width // PAIR
    seq = t // batch
    tt = TT_RWKV
    nc = tt // CHUNK
    tok = pl.BlockSpec((batch, tt, width), lambda i: (0, i, 0))
    as_seq = lambda z: z.reshape(batch, seq, width)
    out = pl.pallas_call(
        _rwkv_scan_kernel,
        grid=(seq // tt,),
        in_specs=[
            tok, tok, tok, tok,
            pl.BlockSpec((batch, nc, n_pairs, PAIR, PAIR), lambda i: (0, i, 0, 0, 0)),
            pl.BlockSpec((batch, nc, n_pairs, HEAD, PAIR), lambda i: (0, i, 0, 0, 0)),
            pl.BlockSpec((8, width), lambda i: (0, 0)),
        ],
        out_specs=tok,
        out_shape=jax.ShapeDtypeStruct((batch, seq, width), F32),
        scratch_shapes=[pltpu.VMEM((batch, n_pairs, HEAD, PAIR), F32),
                        pltpu.VMEM((batch, tt, width), F32)],
        compiler_params=_params(("arbitrary",)),
        name="rwkv_scan",
    )(as_seq(rp), as_seq(yq), as_seq(bonus), as_seq(gate),
      g.reshape(batch, seq // CHUNK, n_pairs, PAIR, PAIR),
      h.reshape(batch, seq // CHUNK, n_pairs, HEAD, PAIR), pvec)
    return out.reshape(t, width)


def _out_proj_kernel(x_ref, ya_ref, yb_ref, wo_ref, g_ref, rwt_ref, rb_ref, h1_ref, hn_ref, lg_ref):
    wa = ya_ref.shape[1]
    h1 = (x_ref[...] + _dot(ya_ref[...].astype(BF16), wo_ref[0:wa, :])
          + _dot(yb_ref[...].astype(BF16), wo_ref[wa:, :]))
    h1_ref[...] = h1
    hn = h1 * lax.rsqrt(jnp.mean(h1 * h1, axis=-1, keepdims=True) + RMS_EPS) * g_ref[...]
    _store_token_tiles(hn_ref, hn)
    lg_ref[...] = _dot3_nt(rwt_ref[...], hn) + rb_ref[...]


def _out_proj(x2, ya, yb, wo_bf, g2, rwt, rb):
    t, d = x2.shape
    wa = ya.shape[1]
    wb = yb.shape[1]
    ne = rwt.shape[0]
    tm = TM_PROJ
    return pl.pallas_call(
        _out_proj_kernel,
        grid=(t // tm,),
        in_specs=[
            pl.BlockSpec((tm, d), lambda i: (i, 0)),
            pl.BlockSpec((tm, wa), lambda i: (i, 0)),
            pl.BlockSpec((tm, wb), lambda i: (i, 0)),
            pl.BlockSpec((wa + wb, d), lambda i: (0, 0)),
            pl.BlockSpec((1, d), lambda i: (0, 0)),
            pl.BlockSpec((ne, d), lambda i: (0, 0)),
            pl.BlockSpec((ne, 1), lambda i: (0, 0)),
        ],
        out_specs=[
            pl.BlockSpec((tm, d), lambda i: (i, 0)),
            pl.BlockSpec((tm * SUBLANES, LANES), lambda i: (i, 0)),
            pl.BlockSpec((ne, tm), lambda i: (0, i)),
        ],
        out_shape=[
            jax.ShapeDtypeStruct((t, d), F32),
            jax.ShapeDtypeStruct((t * SUBLANES, LANES), F32),
            jax.ShapeDtypeStruct((ne, t), F32),
        ],
        compiler_params=_params(("arbitrary",)),
        name="out_proj",
    )(x2, ya, yb, wo_bf, g2, rwt, rb)


def _route_kernel(lg_ref, pos_ref, gate_ref, pend_ref, li_ref, rstart_ref, rlen_ref,
                  carry_scr, pstart_scr):
    phase = pl.program_id(0)
    first = pl.program_id(1) == 0

    @pl.when(first & (phase == 0))
    def _():
        carry_scr[...] = jnp.zeros_like(carry_scr)
        pstart_scr[...] = jnp.zeros_like(pstart_scr)
        pend_ref[...] = jnp.zeros_like(pend_ref)

    @pl.when(first & (phase == 1))
    def _():
        counts = carry_scr[...]
        padded = jnp.ceil(counts * (1.0 / BM_MOE)) * BM_MOE
        row = lax.broadcasted_iota(I32, counts.shape, 0)
        end = padded
        s = 1
        while s < counts.shape[0]:
            end = end + jnp.where(row >= s, pltpu.roll(end, s, 0), 0.0)
            s *= 2
        pstart_scr[...] = end - padded
        pend_ref[...] = end.astype(I32)
        carry_scr[...] = jnp.zeros_like(carry_scr)

    l = lg_ref[...]
    ne, tr = l.shape
    e_iota = lax.broadcasted_iota(I32, (ne, tr), 0)
    chosen = jnp.zeros((ne, tr), F32)
    vals, sels = [], []
    for j in range(TOP_K):
        m = jnp.max(l, axis=0, keepdims=True)
        idx = jnp.min(jnp.where(l == m, e_iota, ne), axis=0, keepdims=True)
        sel = e_iota == idx
        vals.append(m)
        sels.append(sel)
        chosen = jnp.where(sel, 1.0, chosen)
        l = jnp.where(sel, -jnp.inf, l)
    ex = [jnp.exp(vj - vals[0]) for vj in vals]
    den = ex[0] + ex[1] + ex[2] + ex[3]
    for j in range(TOP_K):
        gate_ref[j:j + 1, :] = ex[j] / den

    ti = lax.broadcasted_iota(I32, (tr, tr), 0)
    tj = lax.broadcasted_iota(I32, (tr, tr), 1)
    upper = jnp.where(ti <= tj, 1.0, 0.0).astype(BF16)
    inc = _dot(chosen.astype(BF16), upper)
    carry = carry_scr[...]
    first_row = carry[:, 0:1] + pstart_scr[:, 0:1]
    row_of = inc - chosen + first_row
    for j in range(TOP_K):
        pj = jnp.sum(jnp.where(sels[j], row_of, 0.0), axis=0, keepdims=True)
        pos_ref[j:j + 1, :] = (pj * SUBLANES).astype(I32)
    carry_scr[...] = carry + inc[:, tr - 1:tr]

    tc = TC_COMBINE
    erow = lax.broadcasted_iota(I32, (ne, 128), 0)
    for s in range(tr // tc):
        cols = slice(s * tc, (s + 1) * tc)
        before = inc[:, s * tc - 1:s * tc] if s else jnp.zeros((ne, 1), F32)
        n_run = inc[:, (s + 1) * tc - 1:(s + 1) * tc] - before
        win_rows = jnp.broadcast_to(jnp.ceil(n_run * (1.0 / WIN_COMBINE)) * WIN_COMBINE, (ne, 128))
        base = win_rows
        k = 1
        while k < ne:
            base = base + jnp.where(erow >= k, pltpu.roll(base, k, 0), 0.0)
            k *= 2
        base = base - win_rows
        slot_of = (inc - chosen)[:, cols] - before + base[:, 0:1]
        for j in range(TOP_K):
            lj = jnp.sum(jnp.where(sels[j][:, cols], slot_of, 0.0), axis=0, keepdims=True)
            li_ref[j:j + 1, cols] = (lj * SUBLANES).astype(I32)
        rstart_ref[s * ne:(s + 1) * ne, :] = jnp.broadcast_to(first_row + before, (ne, 128)).astype(I32)
        rlen_ref[s * ne:(s + 1) * ne, :] = jnp.broadcast_to(n_run, (ne, 128)).astype(I32)


def _route(logits_t):
    ne, t = logits_t.shape
    tr = TR_ROUTE
    tok = pl.BlockSpec((TOP_K, tr), lambda ph, i: (0, i * ph))
    n_sub = tr // TC_COMBINE
    runs = pl.BlockSpec((n_sub * ne, 128), lambda ph, i: (i * ph, 0))
    runs_shape = jax.ShapeDtypeStruct((t // TC_COMBINE * ne, 128), I32)
    return pl.pallas_call(
        _route_kernel,
        grid=(2, t // tr),
        in_specs=[pl.BlockSpec((ne, tr), lambda ph, i: (0, i))],
        out_specs=[tok, tok, pl.BlockSpec((ne, 128), lambda ph, i: (0, 0)), tok, runs, runs],
        out_shape=[
            jax.ShapeDtypeStruct((TOP_K, t), I32),
            jax.ShapeDtypeStruct((TOP_K, t), F32),
            jax.ShapeDtypeStruct((ne, 128), I32),
            jax.ShapeDtypeStruct((TOP_K, t), I32),
            runs_shape, runs_shape,
        ],
        scratch_shapes=[pltpu.VMEM((ne, 128), F32), pltpu.VMEM((ne, 128), F32)],
        compiler_params=_params(("arbitrary", "arbitrary")),
        name="route",
    )(logits_t)


def _dispatch_kernel(pend_ref, nused_ref, *refs):
    pos_refs = refs[:TOP_K]
    hn_ref, xin_ref, zero_scr, sem = refs[TOP_K:]
    bm = zero_scr.shape[0] // SUBLANES
    td = pos_refs[0].shape[0]
    i = pl.program_id(0)

    def tile(ref, token):
        return ref.at[pl.ds(pl.multiple_of(token * SUBLANES, SUBLANES), SUBLANES)]

    def zero_block(row0):
        start = pl.multiple_of(row0 * SUBLANES, bm * SUBLANES)
        return pltpu.make_async_copy(zero_scr, xin_ref.at[pl.ds(start, bm * SUBLANES)], sem)

    @pl.when(i == 0)
    def _():
        zero_scr[...] = jnp.zeros_like(zero_scr)

        def has_rows(e):
            end = pend_ref[e]
            return jnp.where(e == 0, end >= bm, end > pend_ref[jnp.maximum(e - 1, 0)])

        def start_one(e, carry):
            @pl.when(has_rows(e))
            def _():
                zero_block(pend_ref[e] - bm).start()
            return carry

        def wait_one(e, carry):
            @pl.when(has_rows(e))
            def _():
                zero_block(0).wait()
            return carry

        def start_tail(blk, carry):
            zero_block(blk * bm).start()
            return carry

        def wait_tail(blk, carry):
            zero_block(0).wait()
            return carry

        n_blocks = xin_ref.shape[0] // (bm * SUBLANES)
        lax.fori_loop(0, pend_ref.shape[0], start_one, 0)
        lax.fori_loop(nused_ref[0], n_blocks, start_tail, 0)
        lax.fori_loop(0, pend_ref.shape[0], wait_one, 0)
        lax.fori_loop(nused_ref[0], n_blocks, wait_tail, 0)

    def issue(tk, carry):
        for j in range(TOP_K):
            dst = xin_ref.at[pl.ds(pl.multiple_of(pos_refs[j][tk], SUBLANES), SUBLANES)]
            pltpu.make_async_copy(tile(hn_ref, tk), dst, sem).start(priority=j % 2)
        return carry

    lax.fori_loop(0, td, issue, 0, unroll=8)

    for j in range(TOP_K):
        pltpu.make_async_copy(hn_ref, xin_ref.at[pl.ds(0, td * SUBLANES)], sem).wait()


def _choice_specs(n_steps, tile):
    return [pl.BlockSpec((tile,), functools.partial(lambda j, i, *_: (j * n_steps + i,), j),
                         memory_space=pltpu.SMEM) for j in range(TOP_K)]


def _dispatch(pad_end, n_used, pos, hn_tiles, n_rows):
    t = hn_tiles.shape[0] // SUBLANES
    td = TD_DISPATCH
    pos = pos.reshape(-1)
    grid_spec = pltpu.PrefetchScalarGridSpec(
        num_scalar_prefetch=2,
        grid=(t // td,),
        in_specs=_choice_specs(t // td, td) + [
            pl.BlockSpec((td * SUBLANES, LANES), lambda i, pe, nu: (i, 0)),
        ],
        out_specs=pl.BlockSpec(memory_space=pl.ANY),
        scratch_shapes=[pltpu.VMEM((BM_MOE * SUBLANES, LANES), F32), pltpu.SemaphoreType.DMA(())],
    )
    return pl.pallas_call(
        _dispatch_kernel,
        grid_spec=grid_spec,
        out_shape=jax.ShapeDtypeStruct((n_rows * SUBLANES, LANES), F32),
        compiler_params=_params(("arbitrary",)),
        name="dispatch",
    )(pad_end, n_used, *([pos] * TOP_K), hn_tiles)


def _experts_kernel(pend_ref, nused_ref, w1_ref, b1_ref, w2_ref, b2_ref, xin_ref, yout_ref,
                    xbuf, obuf, wt_scr, wg_scr, wl_scr, w2_scr, sem_in, sem_out):
    e = pl.program_id(0)
    rows_blk = xbuf.shape[1]
    bm = rows_blk // SUBLANES
    dh = w2_ref.shape[1]
    n_used = nused_ref[0]
    blk_lo = jnp.where(e == 0, 0, pend_ref[jnp.maximum(e - 1, 0)]) // bm
    blk_hi = pend_ref[e] // bm

    def block(ref, blk):
        return ref.at[pl.ds(pl.multiple_of(blk * rows_blk, rows_blk), rows_blk)]

    def x_copy(blk, slot):
        return pltpu.make_async_copy(block(xin_ref, blk), xbuf.at[slot], sem_in.at[slot])

    def o_copy(blk, slot):
        return pltpu.make_async_copy(obuf.at[slot], block(yout_ref, blk), sem_out.at[slot])

    @pl.when((e == 0) & (n_used > 0))
    def _():
        x_copy(0, 0).start()

    @pl.when(blk_hi > blk_lo)
    def _():
        n_slab, n_t, lanes = wt_scr.shape
        for part in range(2 * dh // n_t):
            rows = slice(part * (n_t // 2), (part + 1) * (n_t // 2))
            for s in range(n_slab):
                cols = slice(s * lanes, (s + 1) * lanes)
                wt_scr[s] = w1_ref[0, cols, part * n_t:(part + 1) * n_t].T
                wg_scr[rows, cols] = wt_scr[s, pl.ds(0, n_t // 2, stride=2), :].astype(BF16)
                wl_scr[rows, cols] = wt_scr[s, pl.ds(1, n_t // 2, stride=2), :].astype(BF16)
        w2_scr[...] = w2_ref[0].astype(BF16)

        def one_block(blk, carry):
            slot = blk % 2

            @pl.when(blk + 1 < n_used)
            def _():
                x_copy(blk + 1, 1 - slot).start()

            x_copy(blk, slot).wait()

            @pl.when(blk >= 2)
            def _():
                o_copy(blk - 2, slot).wait()

            xb = _load_token_tiles(xbuf.at[slot], 0, bm).astype(BF16)
            bias = b1_ref[0]
            glu = jnp.minimum(_dot_nt(xb, wg_scr[...]) + bias[:, :dh], SWIGLU_LIMIT)
            lin = jnp.clip(_dot_nt(xb, wl_scr[...]) + bias[:, dh:], -SWIGLU_LIMIT, SWIGLU_LIMIT)
            act = glu * jax.nn.sigmoid(SWIGLU_ALPHA * glu) * (lin + 1.0)
            _store_token_tiles(obuf.at[slot], _dot(act.astype(BF16), w2_scr[...]) + b2_ref[0])
            o_copy(blk, slot).start()
            return carry

        lax.fori_loop(blk_lo, blk_hi, one_block, 0)

    @pl.when(e == pl.num_programs(0) - 1)
    def _():
        @pl.when(n_used >= 1)
        def _():
            o_copy(0, (n_used - 1) % 2).wait()

        @pl.when(n_used >= 2)
        def _():
            o_copy(0, n_used % 2).wait()

        obuf[0] = jnp.zeros(obuf.shape[1:], obuf.dtype)
        n_blocks = yout_ref.shape[0] // rows_blk

        def start_tail(blk, carry):
            o_copy(blk, 0).start()
            return carry

        def wait_tail(blk, carry):
            o_copy(0, 0).wait()
            return carry

        lax.fori_loop(n_used, n_blocks, start_tail, 0)
        lax.fori_loop(n_used, n_blocks, wait_tail, 0)


def _experts(pad_end, n_used, xin, w1, b1p, w2, b2):
    ne, d, dh2 = w1.shape
    dh = dh2 // 2
    rows_blk = BM_MOE * SUBLANES
    grid_spec = pltpu.PrefetchScalarGridSpec(
        num_scalar_prefetch=2,
        grid=(ne,),
        in_specs=[
            pl.BlockSpec((1, d, dh2), lambda e, pe, nu: (e, 0, 0)),
            pl.BlockSpec((1, 1, dh2), lambda e, pe, nu: (e, 0, 0)),
            pl.BlockSpec((1, dh, d), lambda e, pe, nu: (e, 0, 0)),
            pl.BlockSpec((1, 1, d), lambda e, pe, nu: (e, 0, 0)),
            pl.BlockSpec(memory_space=pl.ANY),
        ],
        out_specs=pl.BlockSpec(memory_space=pl.ANY),
        scratch_shapes=[
            pltpu.VMEM((2, rows_blk, LANES), F32),
            pltpu.VMEM((2, rows_blk, LANES), F32),
            pltpu.VMEM((d // 128, dh, 128), F32),
            pltpu.VMEM((dh, d), BF16),
            pltpu.VMEM((dh, d), BF16),
            pltpu.VMEM((dh, d), BF16),
            pltpu.SemaphoreType.DMA((2,)),
            pltpu.SemaphoreType.DMA((2,)),
        ],
    )
    return pl.pallas_call(
        _experts_kernel,
        grid_spec=grid_spec,
        out_shape=jax.ShapeDtypeStruct(xin.shape, F32),
        compiler_params=_params(("arbitrary",)),
        name="experts",
    )(pad_end, n_used, w1, b1p, w2, b2, xin)


def _combine_kernel(*refs):
    rstart_ref, rlen_ref, rstart_next_ref, rlen_next_ref = refs[:4]
    li_refs = refs[4:4 + TOP_K]
    gate_refs = refs[4 + TOP_K:4 + 2 * TOP_K]
    h1_ref, fg_ref, yout_ref, o_ref, wbuf, hacc, sem = refs[4 + 2 * TOP_K:]
    tc = o_ref.shape[0]
    i = pl.program_id(0)
    slot = i % 2
    n_experts = rstart_ref.shape[0]
    win_rows = WIN_COMBINE * SUBLANES

    def tile(ref, token):
        return ref.at[pl.ds(pl.multiple_of(token * SUBLANES, SUBLANES), SUBLANES)]

    def windows_of(rl_ref, e):
        return (rl_ref[e, 0] + (WIN_COMBINE - 1)) // WIN_COMBINE

    def start_windows(rs_ref, rl_ref, s):
        def per_expert(e, base):
            start = rs_ref[e, 0]
            n_win = windows_of(rl_ref, e)

            def per_window(w, carry):
                src = yout_ref.at[pl.ds(pl.multiple_of((start + w * WIN_COMBINE) * SUBLANES, SUBLANES),
                                        win_rows)]
                dst = wbuf.at[s, pl.ds(pl.multiple_of((base + w) * win_rows, win_rows), win_rows)]
                pltpu.make_async_copy(src, dst, sem.at[s]).start()
                return carry

            lax.fori_loop(0, n_win, per_window, 0)
            return base + n_win

        lax.fori_loop(0, n_experts, per_expert, 0)

    @pl.when(i == 0)
    def _():
        start_windows(rstart_ref, rlen_ref, 0)

    @pl.when(i + 1 < pl.num_programs(0))
    def _():
        start_windows(rstart_next_ref, rlen_next_ref, 1 - slot)

    n_windows = lax.fori_loop(0, n_experts, lambda e, n: n + windows_of(rlen_ref, e), 0)

    def wait_window(w, carry):
        pltpu.make_async_copy(yout_ref.at[pl.ds(0, win_rows)], wbuf.at[slot, pl.ds(0, win_rows)],
                              sem.at[slot]).wait()
        return carry

    lax.fori_loop(0, n_windows, wait_window, 0)

    def window_tile(j, tk):
        return wbuf[slot, pl.ds(pl.multiple_of(li_refs[j][tk], SUBLANES), SUBLANES), :]

    def per_token(tk, carry):
        acc = gate_refs[0][tk] * window_tile(0, tk)
        for j in range(1, TOP_K):
            acc = acc + gate_refs[j][tk] * window_tile(j, tk)
        tile(hacc, tk)[...] = acc
        return carry

    lax.fori_loop(0, tc, per_token, 0, unroll=32)
    h = h1_ref[...] + _load_token_tiles(hacc, 0, tc)
    o_ref[...] = h * lax.rsqrt(jnp.mean(h * h, axis=-1, keepdims=True) + RMS_EPS) * fg_ref[...]


def _combine(rstart, rlen, li, gates, h1, fg, yout):
    t, d = h1.shape
    tc = TC_COMBINE
    n_steps = t // tc
    n_experts = rstart.shape[0] // n_steps
    buf_rows = TOP_K * tc + n_experts * WIN_COMBINE
    runs = pl.BlockSpec((n_experts, 128), lambda i: (i, 0), memory_space=pltpu.SMEM)
    runs_next = pl.BlockSpec((n_experts, 128), lambda i: (jnp.minimum(i + 1, n_steps - 1), 0),
                             memory_space=pltpu.SMEM)
    return pl.pallas_call(
        _combine_kernel,
        grid=(n_steps,),
        in_specs=[runs, runs, runs_next, runs_next]
        + _choice_specs(n_steps, tc) + _choice_specs(n_steps, tc) + [
            pl.BlockSpec((tc, d), lambda i: (i, 0)),
            pl.BlockSpec((1, d), lambda i: (0, 0)),
            pl.BlockSpec(memory_space=pl.ANY),
        ],
        out_specs=pl.BlockSpec((tc, d), lambda i: (i, 0)),
        out_shape=jax.ShapeDtypeStruct((t, d), F32),
        scratch_shapes=[pltpu.VMEM((2, buf_rows * SUBLANES, LANES), F32),
                        pltpu.VMEM((tc * SUBLANES, LANES), F32),
                        pltpu.SemaphoreType.DMA((2,))],
        compiler_params=_params(("arbitrary",)),
        name="combine",
    )(rstart, rlen, rstart, rlen, *([li.reshape(-1)] * TOP_K), *([gates.reshape(-1)] * TOP_K),
      h1, fg, yout)


def kernel(x, norm1_g, w_in, sgu_ln_g, sgu_ln_b, sgu_w, sgu_b, mu_rkv, mu_wag, decay_w0, decay_w1,
           decay_w2, iclr_a0, iclr_a1, iclr_a2, gate_g1, gate_g2, k_k, k_a, r_k, lnx_g, lnx_b, w_out,
           norm2_g, router_w, router_b, moe_w1, moe_b1, moe_w2, moe_b2, final_g):
    batch, seq, d = x.shape
    t = batch * seq
    depth = w_in.shape[0]
    sgu_width = sgu_ln_g.shape[1]
    rw = mu_rkv.shape[2]
    n_dec, n_icl, n_gate = decay_w1.shape[2], iclr_a1.shape[2], gate_g1.shape[2]
    assert n_dec == HEAD and n_icl == HEAD and n_gate == PAIR and rw % PAIR == 0
    assert seq % TT_RWKV == 0 and seq % TT_PREP == 0 and t % TM_PROJ == 0
    assert sgu_w.shape[2] == SGU_BLOCK and TM_PROJ % SGU_BLOCK == 0
    assert t % TR_ROUTE == 0 and t % TD_DISPATCH == 0 and TR_ROUTE % TC_COMBINE == 0
    assert (t * TOP_K) % BM_MOE == 0 and router_w.shape[2] == N_EXPERTS
    assert depth == 1, "the final RMSNorm is fused into the last layer's combine kernel"
    assert d == SUBLANES * LANES, "the MoE row movers copy one (8, 128) f32 tile per token"
    assert BM_MOE + N_EXPERTS - 1 >= WIN_COMBINE

    h = x.reshape(t, d)
    for l in range(depth):
        win_bf = w_in[l].astype(BF16)
        wl = jnp.concatenate([decay_w1[l], iclr_a1[l], gate_g1[l]], axis=1)
        mucat = jnp.concatenate([
            jnp.broadcast_to(mu_wag[l, 0][:, None], (d, n_dec)),
            jnp.broadcast_to(mu_wag[l, 1][:, None], (d, n_icl)),
            jnp.broadcast_to(mu_wag[l, 2][:, None], (d, n_gate))], axis=1)
        zeros = jnp.zeros((HEAD, rw), F32)
        w2pad = jnp.concatenate([decay_w2[l], zeros], axis=0)
        a2pad = jnp.concatenate([zeros, iclr_a2[l]], axis=0)
        pvec = jnp.stack([decay_w0[l], iclr_a0[l], k_k[l], k_a[l], r_k[l].reshape(-1),
                          lnx_g[l], lnx_b[l], jnp.zeros((rw,), F32)], axis=0)
        bias2d = jnp.repeat(sgu_b[l].T, sgu_width // sgu_b.shape[1], axis=1)
        wo_bf = w_out[l].astype(BF16)
        b1p = jnp.concatenate([moe_b1[l][:, 0::2], moe_b1[l][:, 1::2]], axis=-1)[:, None, :]
        b2 = moe_b2[l][:, None, :]

        assert w_in.shape[2] == 2 * sgu_width + 3 * rw
        proj, lora, ya = _in_proj(h, norm1_g[l][None, :], win_bf, wl, mucat, sgu_ln_g[l][None, :],
                                  sgu_ln_b[l][None, :], sgu_w[l], bias2d)
        rp, yq, bonus, gate, g_m, h_m = _rwkv_prep(proj, lora, mu_rkv[l], pvec, w2pad, a2pad,
                                                   gate_g2[l], seq, 0)
        yb = _rwkv_scan(rp, yq, bonus, gate, g_m, h_m, pvec, batch)
        h1, hn2, logits_t = _out_proj(h, ya, yb, wo_bf, norm2_g[l][None, :], router_w[l].T,
                                      router_b[l][:, None])
        pos, gates, pend, li, rstart, rlen = _route(logits_t)

        pad_end = pend[:, 0]
        n_blocks = (t * TOP_K) // BM_MOE + N_EXPERTS
        n_used = pad_end[-1:] // BM_MOE

        xin = _dispatch(pad_end, n_used, pos, hn2, n_blocks * BM_MOE)
        yout = _experts(pad_end, n_used, xin, moe_w1[l], b1p, moe_w2[l], b2)
        h = _combine(rstart, rlen, li, gates, h1, final_g[None, :], yout)
    return h.reshape(batch, seq, d)
```

```python
import functools

import jax
import jax.numpy as jnp
from jax import lax
from jax.experimental import pallas as pl
from jax.experimental.pallas import tpu as pltpu

F32 = jnp.float32
BF16 = jnp.bfloat16
I32 = jnp.int32

RMS_EPS = 1e-5
LN_EPS = 1e-5
LNX_EPS = 64e-5
CHUNK = 64
SGU_BLOCK = 128
HEAD = 64
PAIR = 2 * HEAD
N_EXPERTS = 32
TOP_K = 4
SWIGLU_ALPHA = 1.702
SWIGLU_LIMIT = 7.0

V7X_VMEM_LIMIT = 56 * 1024 * 1024

TM_PROJ = 1024
TT_PREP = 1024
N_SUB_PREP = 1
TT_RWKV = 512
TR_ROUTE = 1024
BM_MOE = 256
TD_DISPATCH = 2048
TC_COMBINE = 512
WIN_COMBINE = 64


def _dot(a, b):
    return jnp.dot(a, b, preferred_element_type=F32)


def _dot_nt(a, b):
    return lax.dot_general(a, b, (((1,), (1,)), ((), ())), preferred_element_type=F32)


def _split(x):
    hi = x.astype(BF16)
    lo = (x - hi.astype(F32)).astype(BF16)
    return hi, lo


def _dot3(a, b):
    ah, al = _split(a)
    bh, bl = _split(b)
    return _dot(ah, bh) + _dot(al, bh) + _dot(ah, bl)


def _dot3_nt(a, b):
    ah, al = _split(a)
    bh, bl = _split(b)
    return _dot_nt(ah, bh) + _dot_nt(al, bh) + _dot_nt(ah, bl)


def _head_sum(x, bd):
    hi, lo = _split(x)
    return _dot(hi, bd) + _dot(lo, bd)


SUBLANES = 8
LANES = 128


def _store_token_tiles(ref, x):
    n = x.shape[0]
    for s in range(SUBLANES):
        ref[pl.ds(s, n, stride=SUBLANES), :] = x[:, s * LANES:(s + 1) * LANES]


def _load_token_tiles(ref, row0, n):
    return jnp.concatenate(
        [ref[pl.ds(row0 + s, n, stride=SUBLANES), :] for s in range(SUBLANES)], axis=1)


def _params(sem, vmem=V7X_VMEM_LIMIT):
    return pltpu.CompilerParams(dimension_semantics=sem, vmem_limit_bytes=vmem)


def _in_proj_kernel(x_ref, g_ref, win_ref, wl_ref, mucat_ref, lng_ref, lnb_ref, w_ref, bias_ref,
                    proj_ref, lora_ref, o_ref, wl_scr, wm_scr):
    n_heads = w_ref.shape[0]

    @pl.when(pl.program_id(0) == 0)
    def _():
        wl = wl_ref[...]
        mu = mucat_ref[...]
        n = wl.shape[1]
        wl_scr[:, 0:n] = (wl * (1.0 - mu)).astype(BF16)
        wl_scr[:, n:2 * n] = (wl * mu).astype(BF16)
        qi = lax.broadcasted_iota(I32, (SGU_BLOCK, SGU_BLOCK), 0) // CHUNK
        kj = lax.broadcasted_iota(I32, (SGU_BLOCK, SGU_BLOCK), 1) // CHUNK
        for h in range(n_heads):
            wm_scr[h] = jnp.where(kj <= qi, w_ref[h], 0.0).astype(BF16)

    x = x_ref[...]
    hn = x * lax.rsqrt(jnp.mean(x * x, axis=-1, keepdims=True) + RMS_EPS) * g_ref[...]
    hb = hn.astype(BF16)
    width = lng_ref.shape[1]
    two_w = 2 * width
    proj_ref[...] = _dot(hb, win_ref[:, two_w:])
    lora_ref[...] = _dot(hb, wl_scr[...])

    z = _dot(hb, win_ref[:, :two_w])
    tm = z.shape[0]
    gz = 0.5 * z * (1.0 + lax.erf(z * (2.0 ** -0.5)))
    u = gz[:, :width]
    v = gz[:, width:]
    mu = jnp.mean(v, axis=-1, keepdims=True)
    vc = v - mu
    var = jnp.mean(vc * vc, axis=-1, keepdims=True)
    vn = vc * lax.rsqrt(var + LN_EPS) * lng_ref[...] + lnb_ref[...]
    lane = lax.broadcasted_iota(I32, (SGU_BLOCK, PAIR), 1)
    is_lo = lane < HEAD
    bias = bias_ref[...]
    for blk in range(tm // SGU_BLOCK):
        rows = slice(blk * SGU_BLOCK, (blk + 1) * SGU_BLOCK)
        for p in range(width // PAIR):
            cols = slice(p * PAIR, (p + 1) * PAIR)
            vp = vn[rows, cols]
            lo = jnp.where(is_lo, vp, 0.0).astype(BF16)
            hi = jnp.where(is_lo, 0.0, vp).astype(BF16)
            sv = _dot(wm_scr[2 * p], lo) + _dot(wm_scr[2 * p + 1], hi)
            o_ref[rows, cols] = u[rows, cols] * (sv + bias[:, cols])


def _in_proj(x2, g, win_bf, wl, mucat, ln_g, ln_b, w_s, bias2d):
    t, d = x2.shape
    width = ln_g.shape[1]
    n_rkv = win_bf.shape[1] - 2 * width
    n_l = wl.shape[1]
    n_heads = w_s.shape[0]
    tm = TM_PROJ
    const = lambda *shape: pl.BlockSpec(shape, lambda i: (0,) * len(shape))
    row = lambda n: pl.BlockSpec((tm, n), lambda i: (i, 0))
    return pl.pallas_call(
        _in_proj_kernel,
        grid=(t // tm,),
        in_specs=[
            row(d), const(1, d), const(d, win_bf.shape[1]), const(d, n_l), const(d, n_l),
            const(1, width), const(1, width), const(n_heads, SGU_BLOCK, SGU_BLOCK),
            const(SGU_BLOCK, width),
        ],
        out_specs=[row(n_rkv), row(2 * n_l), row(width)],
        out_shape=[
            jax.ShapeDtypeStruct((t, n_rkv), F32),
            jax.ShapeDtypeStruct((t, 2 * n_l), F32),
            jax.ShapeDtypeStruct((t, width), F32),
        ],
        scratch_shapes=[pltpu.VMEM((d, 2 * n_l), BF16),
                        pltpu.VMEM((n_heads, SGU_BLOCK, SGU_BLOCK), BF16)],
        compiler_params=_params(("arbitrary",)),
        name="in_proj_sgu",
    )(x2, g, win_bf, wl, mucat, ln_g, ln_b, w_s, bias2d)


def _rwkv_prep_kernel(seq_len, pr_ref, pk_ref, pv_ref, ppr_ref, ppk_ref, ppv_ref, lo_ref, plo_ref,
                      mu_ref, pvec_ref, w2_ref, a2_ref, g2_ref,
                      rp_ref, yq_ref, bonus_ref, gate_ref, g_ref, h_ref):
    tt = pr_ref.shape[0]
    ts = tt // N_SUB_PREP
    n_chunks = ts // CHUNK
    i = pl.program_id(0)
    keep = jnp.where((i * tt) % seq_len == 0, 0.0, 1.0)

    mu = mu_ref[...]
    pvec = pvec_ref[...]
    w0, a0, k_k, k_a, r_k = pvec[0:1], pvec[1:2], pvec[2:3], pvec[3:4], pvec[4:5]
    w2b = w2_ref[...].astype(BF16)
    a2b = a2_ref[...].astype(BF16)
    g2b = g2_ref[...].astype(BF16)
    half = lo_ref.shape[1] // 2

    lane_r = lax.broadcasted_iota(I32, (PAIR, PAIR), 0)
    lane_c = lax.broadcasted_iota(I32, (PAIR, PAIR), 1)
    bd = jnp.where((lane_r // HEAD) == (lane_c // HEAD), 1.0, 0.0).astype(BF16)
    tri_ones = jnp.where(((lane_r // CHUNK) == (lane_c // CHUNK)) & (lane_c <= lane_r),
                         1.0, 0.0).astype(BF16)
    lane = lax.broadcasted_iota(I32, (CHUNK, PAIR), 1)
    is_lo = lane < HEAD

    def stack(x):
        return jnp.concatenate([jnp.where(is_lo, x, 0.0), jnp.where(is_lo, 0.0, x)], axis=0)

    n2 = 2 * PAIR
    ri = lax.broadcasted_iota(I32, (n2, n2), 0)
    ci = lax.broadcasted_iota(I32, (n2, n2), 1)
    same_head = ((ri // CHUNK) % 2) == ((ci // CHUNK) % 2)
    t_i = ri % CHUNK
    s_i = ci % CHUNK
    tri = same_head & (s_i < t_i + ri // PAIR)
    eye = lane_r == lane_c

    for sub in range(N_SUB_PREP):
        r0 = sub * ts
        tile_rows = slice(r0, r0 + ts)

        def shift(ref, pref, cols=slice(None)):
            x = ref[tile_rows, cols]
            if sub == 0:
                first = pref[7:8, cols] * keep
            else:
                first = ref[r0 - 1:r0, cols]
            rowc = lax.broadcasted_iota(I32, x.shape, 0)
            return x, jnp.where(rowc == 0, first, pltpu.roll(x, 1, 0))

        def shift_mix(ref, pref, m):
            x, xs = shift(ref, pref)
            return x + (xs - x) * m

        r = shift_mix(pr_ref, ppr_ref, mu[0:1])
        k = shift_mix(pk_ref, ppk_ref, mu[1:2])
        v = shift_mix(pv_ref, ppv_ref, mu[2:3])

        _, lo_b = shift(lo_ref, plo_ref, slice(half, 2 * half))
        l_all = lo_ref[tile_rows, :half] + lo_b
        l_wa = l_all[:, :PAIR]
        l_g = l_all[:, PAIR:]

        dw = _dot(jnp.tanh(l_wa).astype(BF16), w2b)
        ia = _dot(l_wa.astype(BF16), a2b)
        gate_ref[tile_rows, :] = _dot(jax.nn.sigmoid(l_g).astype(BF16), g2b).astype(BF16)

        zneg = -(w0 + dw)
        softplus = jnp.maximum(zneg, 0.0) + jnp.log(1.0 + jnp.exp(-jnp.abs(zneg)))
        logw = -jnp.exp(-softplus - 0.5)
        iclr = jax.nn.sigmoid(a0 + ia)

        kk = k * k_k
        kk = kk / jnp.maximum(jnp.sqrt(_head_sum(kk * kk, bd)), 1e-12)
        k2 = k * (1.0 + (iclr - 1.0) * k_a)
        a = -kk
        b = kk * iclr
        bonus_ref[tile_rows, :] = (_head_sum(r * k2 * r_k, bd) * v).astype(BF16)

        lw_hi, lw_lo = _split(logw)
        cl = jnp.concatenate(
            [_dot(tri_ones, lw_hi[q * PAIR:(q + 1) * PAIR]) + _dot(tri_ones, lw_lo[q * PAIR:(q + 1) * PAIR])
             for q in range(ts // PAIR)], axis=0)
        at_all = a * jnp.exp(cl - logw)
        rt_all = r * jnp.exp(cl)
        w_inv = jnp.exp(-cl)
        bt_all = b * w_inv
        kt_all = k2 * w_inv

        chunks = range(n_chunks)
        rows = [slice(c * CHUNK, (c + 1) * CHUNK) for c in chunks]
        out_rows = [slice(r0 + c * CHUNK, r0 + (c + 1) * CHUNK) for c in chunks]
        last = [cl[rows[c]][CHUNK - 1:CHUNK, :] for c in chunks]
        w_rem = [jnp.exp(last[c] - cl[rows[c]]) for c in chunks]
        a_s = [stack(at_all[rows[c]]) for c in chunks]
        r_s = [stack(rt_all[rows[c]]) for c in chunks]
        v_s = [stack(v[rows[c]]) for c in chunks]
        v_sb = [v_s[c].astype(BF16) for c in chunks]
        bk_h = [jnp.concatenate([stack(b[rows[c]] * w_rem[c]), stack(k2[rows[c]] * w_rem[c])],
                                axis=0).astype(BF16) for c in chunks]

        a_all = []
        for c in chunks:
            lhs = jnp.concatenate([a_s[c], r_s[c]], axis=0).astype(BF16)
            rhs = jnp.concatenate([stack(bt_all[rows[c]]), stack(kt_all[rows[c]])],
                                  axis=0).astype(BF16)
            a_all.append(jnp.where(tri, _dot_nt(lhs, rhs), 0.0))
        n_k = [a_all[c][:PAIR, :PAIR].astype(BF16) for c in chunks]
        a_r = [a_all[c][PAIR:, :].astype(BF16) for c in chunks]
        x = [jnp.concatenate([a_s[c], _dot(a_all[c][:PAIR, PAIR:].astype(BF16), v_sb[c])], axis=1)
             for c in chunks]
        steps = CHUNK.bit_length() - 1
        for it in range(steps):
            x = [x[c] + _dot(n_k[c], x[c].astype(BF16)) for c in chunks]
            if it + 1 < steps:
                n_k = [_dot(n_k[c], n_k[c]).astype(BF16) for c in chunks]
        zero_b = jnp.zeros((PAIR, PAIR), BF16)
        ry = [_dot(a_r[c], jnp.concatenate(
            [x[c].astype(BF16), jnp.concatenate([zero_b, v_sb[c]], axis=1)], axis=0)) for c in chunks]
        for c in chunks:
            r_c = rt_all[rows[c]]
            rp_ref[out_rows[c], :] = (ry[c][:CHUNK, :PAIR] + ry[c][CHUNK:, :PAIR] + r_c).astype(BF16)
            yq_ref[out_rows[c], :] = (ry[c][:CHUNK, PAIR:] + ry[c][CHUNK:, PAIR:]).astype(BF16)
        p_t = [x[c][:, :PAIR].T.astype(BF16) for c in chunks]
        qv_t = [jnp.concatenate([x[c][:, PAIR:], v_s[c]], axis=0).T.astype(BF16) for c in chunks]
        c0 = sub * n_chunks
        for c in chunks:
            g_ref[c0 + c, 0] = (jnp.where(eye, jnp.exp(last[c]), 0.0)
                                + _dot(p_t[c], bk_h[c][:PAIR]))
        for c in chunks:
            hbd = _dot(qv_t[c], bk_h[c])
            h_ref[c0 + c, 0] = hbd[:HEAD] + hbd[HEAD:]


def _rwkv_prep(proj, lora, mu_rkv, pvec, w2pad, a2pad, g2, seq_len, col0):
    t = proj.shape[0]
    width = mu_rkv.shape[1]
    n_pairs = width // PAIR
    tt = TT_PREP
    nl = lora.shape[1]
    cb = col0 // PAIR
    wb = width // PAIR

    def prev_rows(i):
        return jnp.maximum(i * (tt // 8) - 1, 0)

    def tok(c):
        return pl.BlockSpec((tt, PAIR), lambda i, p: (i, c + p))

    def prev(c):
        return pl.BlockSpec((8, PAIR), lambda i, p: (prev_rows(i), c + p))

    out_tok = pl.BlockSpec((tt, PAIR), lambda i, p: (i, p))
    tok_shape = jax.ShapeDtypeStruct((t, width), BF16)
    return pl.pallas_call(
        functools.partial(_rwkv_prep_kernel, seq_len),
        grid=(t // tt, n_pairs),
        in_specs=[
            tok(cb), tok(cb + wb), tok(cb + 2 * wb),
            prev(cb), prev(cb + wb), prev(cb + 2 * wb),
            pl.BlockSpec((tt, nl), lambda i, p: (i, 0)),
            pl.BlockSpec((8, nl), lambda i, p: (prev_rows(i), 0)),
            pl.BlockSpec((3, PAIR), lambda i, p: (0, p)),
            pl.BlockSpec((8, PAIR), lambda i, p: (0, p)),
            pl.BlockSpec((PAIR, PAIR), lambda i, p: (0, p)),
            pl.BlockSpec((PAIR, PAIR), lambda i, p: (0, p)),
            pl.BlockSpec((PAIR, PAIR), lambda i, p: (0, p)),
        ],
        out_specs=[
            out_tok, out_tok, out_tok, out_tok,
            pl.BlockSpec((tt // CHUNK, 1, PAIR, PAIR), lambda i, p: (i, p, 0, 0)),
            pl.BlockSpec((tt // CHUNK, 1, HEAD, PAIR), lambda i, p: (i, p, 0, 0)),
        ],
        out_shape=[
            tok_shape, tok_shape, tok_shape, tok_shape,
            jax.ShapeDtypeStruct((t // CHUNK, n_pairs, PAIR, PAIR), F32),
            jax.ShapeDtypeStruct((t // CHUNK, n_pairs, HEAD, PAIR), F32),
        ],
        compiler_params=_params(("arbitrary", "arbitrary")),
        name="rwkv_prep",
    )(proj, proj, proj, proj, proj, proj, lora, lora, mu_rkv, pvec, w2pad, a2pad, g2)


def _rwkv_scan_kernel(rp_ref, yq_ref, bonus_ref, gate_ref, g_ref, h_ref, pvec_ref, o_ref, s_scr, y_scr):
    nb, tt, width = rp_ref.shape
    n_pairs = width // PAIR

    @pl.when(pl.program_id(0) == 0)
    def _():
        s_scr[...] = jnp.zeros_like(s_scr)

    lane = lax.broadcasted_iota(I32, (HEAD, PAIR), 1)
    is_lo = lane < HEAD
    lane_r = lax.broadcasted_iota(I32, (PAIR, PAIR), 0)
    lane_c = lax.broadcasted_iota(I32, (PAIR, PAIR), 1)
    bd = jnp.where((lane_r // HEAD) == (lane_c // HEAD), 1.0, 0.0).astype(BF16)

    chains = [(b, p) for b in range(nb) for p in range(n_pairs)]
    state = {ch: s_scr[ch[0], ch[1]] for ch in chains}
    for c in range(tt // CHUNK):
        rows = slice(c * CHUNK, (c + 1) * CHUNK)
        for (b, p) in chains:
            cols = slice(p * PAIR, (p + 1) * PAIR)
            s0 = state[(b, p)]
            s_st = jnp.concatenate([jnp.where(is_lo, s0, 0.0), jnp.where(is_lo, 0.0, s0)], axis=0)
            y_scr[b, rows, cols] = (_dot_nt(rp_ref[b, rows, cols], s_st.astype(BF16))
                                    + yq_ref[b, rows, cols].astype(F32))
        state = {(b, p): _dot3(state[(b, p)], g_ref[b, c, p]) + h_ref[b, c, p] for (b, p) in chains}
    for (b, p) in chains:
        s_scr[b, p] = state[(b, p)]

    pvec = pvec_ref[...]
    for (b, p) in chains:
        cols = slice(p * PAIR, (p + 1) * PAIR)
        y = y_scr[b, :, cols]
        mu = _head_sum(y, bd) * (1.0 / HEAD)
        yc = y - mu
        var = _head_sum(yc * yc, bd) * (1.0 / HEAD)
        yn = yc * lax.rsqrt(var + LNX_EPS) * pvec[5:6, cols] + pvec[6:7, cols]
        o_ref[b, :, cols] = ((yn + bonus_ref[b, :, cols].astype(F32))
                             * gate_ref[b, :, cols].astype(F32))


def _rwkv_scan(rp, yq, bonus, gate, g, h, pvec, batch):
    t, width = rp.shape
    n_pairs = width // PAIR
    seq = t // batch
    tt = TT_RWKV
    nc = tt // CHUNK
    tok = pl.BlockSpec((batch, tt, width), lambda i: (0, i, 0))
    as_seq = lambda z: z.reshape(batch, seq, width)
    out = pl.pallas_call(
        _rwkv_scan_kernel,
        grid=(seq // tt,),
        in_specs=[
            tok, tok, tok, tok,
            pl.BlockSpec((batch, nc, n_pairs, PAIR, PAIR), lambda i: (0, i, 0, 0, 0)),
            pl.BlockSpec((batch, nc, n_pairs, HEAD, PAIR), lambda i: (0, i, 0, 0, 0)),
            pl.BlockSpec((8, width), lambda i: (0, 0)),
        ],
        out_specs=tok,
        out_shape=jax.ShapeDtypeStruct((batch, seq, width), F32),
        scratch_shapes=[pltpu.VMEM((batch, n_pairs, HEAD, PAIR), F32),
                        pltpu.VMEM((batch, tt, width), F32)],
        compiler_params=_params(("arbitrary",)),
        name="rwkv_scan",
    )(as_seq(rp), as_seq(yq), as_seq(bonus), as_seq(gate),
      g.reshape(batch, seq // CHUNK, n_pairs, PAIR, PAIR),
      h.reshape(batch, seq // CHUNK, n_pairs, HEAD, PAIR), pvec)
    return out.reshape(t, width)


def _out_proj_kernel(x_ref, ya_ref, yb_ref, wo_ref, g_ref, rwt_ref, rb_ref, h1_ref, hn_ref, lg_ref):
    wa = ya_ref.shape[1]
    h1 = (x_ref[...] + _dot(ya_ref[...].astype(BF16), wo_ref[0:wa, :])
          + _dot(yb_ref[...].astype(BF16), wo_ref[wa:, :]))
    h1_ref[...] = h1
    hn = h1 * lax.rsqrt(jnp.mean(h1 * h1, axis=-1, keepdims=True) + RMS_EPS) * g_ref[...]
    _store_token_tiles(hn_ref, hn)
    lg_ref[...] = _dot3_nt(rwt_ref[...], hn) + rb_ref[...]


def _out_proj(x2, ya, yb, wo_bf, g2, rwt, rb):
    t, d = x2.shape
    wa = ya.shape[1]
    wb = yb.shape[1]
    ne = rwt.shape[0]
    tm = TM_PROJ
    return pl.pallas_call(
        _out_proj_kernel,
        grid=(t // tm,),
        in_specs=[
            pl.BlockSpec((tm, d), lambda i: (i, 0)),
            pl.BlockSpec((tm, wa), lambda i: (i, 0)),
            pl.BlockSpec((tm, wb), lambda i: (i, 0)),
            pl.BlockSpec((wa + wb, d), lambda i: (0, 0)),
            pl.BlockSpec((1, d), lambda i: (0, 0)),
            pl.BlockSpec((ne, d), lambda i: (0, 0)),
            pl.BlockSpec((ne, 1), lambda i: (0, 0)),
        ],
        out_specs=[
            pl.BlockSpec((tm, d), lambda i: (i, 0)),
            pl.BlockSpec((tm * SUBLANES, LANES), lambda i: (i, 0)),
            pl.BlockSpec((ne, tm), lambda i: (0, i)),
        ],
        out_shape=[
            jax.ShapeDtypeStruct((t, d), F32),
            jax.ShapeDtypeStruct((t * SUBLANES, LANES), F32),
            jax.ShapeDtypeStruct((ne, t), F32),
        ],
        compiler_params=_params(("arbitrary",)),
        name="out_proj",
    )(x2, ya, yb, wo_bf, g2, rwt, rb)


def _route_kernel(lg_ref, pos_ref, gate_ref, pend_ref, li_ref, rstart_ref, rlen_ref,
                  carry_scr, pstart_scr):
    phase = pl.program_id(0)
    first = pl.program_id(1) == 0

    @pl.when(first & (phase == 0))
    def _():
        carry_scr[...] = jnp.zeros_like(carry_scr)
        pstart_scr[...] = jnp.zeros_like(pstart_scr)
        pend_ref[...] = jnp.zeros_like(pend_ref)

    @pl.when(first & (phase == 1))
    def _():
        counts = carry_scr[...]
        padded = jnp.ceil(counts * (1.0 / BM_MOE)) * BM_MOE
        row = lax.broadcasted_iota(I32, counts.shape, 0)
        end = padded
        s = 1
        while s < counts.shape[0]:
            end = end + jnp.where(row >= s, pltpu.roll(end, s, 0), 0.0)
            s *= 2
        pstart_scr[...] = end - padded
        pend_ref[...] = end.astype(I32)
        carry_scr[...] = jnp.zeros_like(carry_scr)

    l = lg_ref[...]
    ne, tr = l.shape
    e_iota = lax.broadcasted_iota(I32, (ne, tr), 0)
    chosen = jnp.zeros((ne, tr), F32)
    vals, sels = [], []
    for j in range(TOP_K):
        m = jnp.max(l, axis=0, keepdims=True)
        idx = jnp.min(jnp.where(l == m, e_iota, ne), axis=0, keepdims=True)
        sel = e_iota == idx
        vals.append(m)
        sels.append(sel)
        chosen = jnp.where(sel, 1.0, chosen)
        l = jnp.where(sel, -jnp.inf, l)
    @pl.when(phase == 0)
    def _():
        carry_scr[...] = carry_scr[...] + jnp.sum(chosen, axis=1, keepdims=True)

    @pl.when(phase == 1)
    def _():
        _route_emit(vals, sels, chosen, pos_ref, gate_ref, li_ref, rstart_ref, rlen_ref,
                    carry_scr, pstart_scr)


def _route_emit(vals, sels, chosen, pos_ref, gate_ref, li_ref, rstart_ref, rlen_ref,
                carry_scr, pstart_scr):
    ne, tr = chosen.shape
    ex = [jnp.exp(vj - vals[0]) for vj in vals]
    den = ex[0] + ex[1] + ex[2] + ex[3]
    for j in range(TOP_K):
        gate_ref[j:j + 1, :] = ex[j] / den

    ti = lax.broadcasted_iota(I32, (tr, tr), 0)
    tj = lax.broadcasted_iota(I32, (tr, tr), 1)
    upper = jnp.where(ti <= tj, 1.0, 0.0).astype(BF16)
    inc = _dot(chosen.astype(BF16), upper)
    carry = carry_scr[...]
    first_row = carry[:, 0:1] + pstart_scr[:, 0:1]
    row_of = inc - chosen + first_row
    for j in range(TOP_K):
        pj = jnp.sum(jnp.where(sels[j], row_of, 0.0), axis=0, keepdims=True)
        pos_ref[j:j + 1, :] = (pj * SUBLANES).astype(I32)
    carry_scr[...] = carry + inc[:, tr - 1:tr]

    tc = TC_COMBINE
    erow = lax.broadcasted_iota(I32, (ne, 128), 0)
    for s in range(tr // tc):
        cols = slice(s * tc, (s + 1) * tc)
        before = inc[:, s * tc - 1:s * tc] if s else jnp.zeros((ne, 1), F32)
        n_run = inc[:, (s + 1) * tc - 1:(s + 1) * tc] - before
        win_rows = jnp.broadcast_to(jnp.ceil(n_run * (1.0 / WIN_COMBINE)) * WIN_COMBINE, (ne, 128))
        base = win_rows
        k = 1
        while k < ne:
            base = base + jnp.where(erow >= k, pltpu.roll(base, k, 0), 0.0)
            k *= 2
        base = base - win_rows
        slot_of = (inc - chosen)[:, cols] - before + base[:, 0:1]
        for j in range(TOP_K):
            lj = jnp.sum(jnp.where(sels[j][:, cols], slot_of, 0.0), axis=0, keepdims=True)
            li_ref[j:j + 1, cols] = (lj * SUBLANES).astype(I32)
        rstart_ref[s * ne:(s + 1) * ne, :] = jnp.broadcast_to(first_row + before, (ne, 128)).astype(I32)
        rlen_ref[s * ne:(s + 1) * ne, :] = jnp.broadcast_to(n_run, (ne, 128)).astype(I32)


def _route(logits_t):
    ne, t = logits_t.shape
    tr = TR_ROUTE
    tok = pl.BlockSpec((TOP_K, tr), lambda ph, i: (0, i * ph))
    n_sub = tr // TC_COMBINE
    runs = pl.BlockSpec((n_sub * ne, 128), lambda ph, i: (i * ph, 0))
    runs_shape = jax.ShapeDtypeStruct((t // TC_COMBINE * ne, 128), I32)
    return pl.pallas_call(
        _route_kernel,
        grid=(2, t // tr),
        in_specs=[pl.BlockSpec((ne, tr), lambda ph, i: (0, i))],
        out_specs=[tok, tok, pl.BlockSpec((ne, 128), lambda ph, i: (0, 0)), tok, runs, runs],
        out_shape=[
            jax.ShapeDtypeStruct((TOP_K, t), I32),
            jax.ShapeDtypeStruct((TOP_K, t), F32),
            jax.ShapeDtypeStruct((ne, 128), I32),
            jax.ShapeDtypeStruct((TOP_K, t), I32),
            runs_shape, runs_shape,
        ],
        scratch_shapes=[pltpu.VMEM((ne, 128), F32), pltpu.VMEM((ne, 128), F32)],
        compiler_params=_params(("arbitrary", "arbitrary")),
        name="route",
    )(logits_t)


def _dispatch_kernel(pend_ref, nused_ref, *refs):
    pos_refs = refs[:TOP_K]
    hn_ref, xin_ref, zero_scr, sem = refs[TOP_K:]
    bm = zero_scr.shape[0] // SUBLANES
    td = pos_refs[0].shape[0]
    i = pl.program_id(0)

    def tile(ref, token):
        return ref.at[pl.ds(pl.multiple_of(token * SUBLANES, SUBLANES), SUBLANES)]

    def zero_block(row0):
        start = pl.multiple_of(row0 * SUBLANES, bm * SUBLANES)
        return pltpu.make_async_copy(zero_scr, xin_ref.at[pl.ds(start, bm * SUBLANES)], sem)

    @pl.when(i == 0)
    def _():
        zero_scr[...] = jnp.zeros_like(zero_scr)

        def has_rows(e):
            end = pend_ref[e]
            return jnp.where(e == 0, end >= bm, end > pend_ref[jnp.maximum(e - 1, 0)])

        def start_one(e, carry):
            @pl.when(has_rows(e))
            def _():
                zero_block(pend_ref[e] - bm).start()
            return carry

        def wait_one(e, carry):
            @pl.when(has_rows(e))
            def _():
                zero_block(0).wait()
            return carry

        def start_tail(blk, carry):
            zero_block(blk * bm).start()
            return carry

        def wait_tail(blk, carry):
            zero_block(0).wait()
            return carry

        n_blocks = xin_ref.shape[0] // (bm * SUBLANES)
        lax.fori_loop(0, pend_ref.shape[0], start_one, 0)
        lax.fori_loop(nused_ref[0], n_blocks, start_tail, 0)
        lax.fori_loop(0, pend_ref.shape[0], wait_one, 0)
        lax.fori_loop(nused_ref[0], n_blocks, wait_tail, 0)

    def issue(tk, carry):
        for j in range(TOP_K):
            dst = xin_ref.at[pl.ds(pl.multiple_of(pos_refs[j][tk], SUBLANES), SUBLANES)]
            pltpu.make_async_copy(tile(hn_ref, tk), dst, sem).start(priority=j % 2)
        return carry

    lax.fori_loop(0, td, issue, 0, unroll=8)

    for j in range(TOP_K):
        pltpu.make_async_copy(hn_ref, xin_ref.at[pl.ds(0, td * SUBLANES)], sem).wait()


def _choice_specs(n_steps, tile):
    return [pl.BlockSpec((tile,), functools.partial(lambda j, i, *_: (j * n_steps + i,), j),
                         memory_space=pltpu.SMEM) for j in range(TOP_K)]


def _dispatch(pad_end, n_used, pos, hn_tiles, n_rows):
    t = hn_tiles.shape[0] // SUBLANES
    td = TD_DISPATCH
    pos = pos.reshape(-1)
    grid_spec = pltpu.PrefetchScalarGridSpec(
        num_scalar_prefetch=2,
        grid=(t // td,),
        in_specs=_choice_specs(t // td, td) + [
            pl.BlockSpec((td * SUBLANES, LANES), lambda i, pe, nu: (i, 0)),
        ],
        out_specs=pl.BlockSpec(memory_space=pl.ANY),
        scratch_shapes=[pltpu.VMEM((BM_MOE * SUBLANES, LANES), F32), pltpu.SemaphoreType.DMA(())],
    )
    return pl.pallas_call(
        _dispatch_kernel,
        grid_spec=grid_spec,
        out_shape=jax.ShapeDtypeStruct((n_rows * SUBLANES, LANES), F32),
        compiler_params=_params(("arbitrary",)),
        name="dispatch",
    )(pad_end, n_used, *([pos] * TOP_K), hn_tiles)


def _experts_kernel(pend_ref, nused_ref, w1_ref, b1_ref, w2_ref, b2_ref, xin_ref, yout_ref,
                    xbuf, obuf, wt_scr, wg_scr, wl_scr, w2_scr, sem_in, sem_out):
    e = pl.program_id(0)
    rows_blk = xbuf.shape[1]
    bm = rows_blk // SUBLANES
    dh = w2_ref.shape[1]
    n_used = nused_ref[0]
    blk_lo = jnp.where(e == 0, 0, pend_ref[jnp.maximum(e - 1, 0)]) // bm
    blk_hi = pend_ref[e] // bm

    def block(ref, blk):
        return ref.at[pl.ds(pl.multiple_of(blk * rows_blk, rows_blk), rows_blk)]

    def x_copy(blk, slot):
        return pltpu.make_async_copy(block(xin_ref, blk), xbuf.at[slot], sem_in.at[slot])

    def o_copy(blk, slot):
        return pltpu.make_async_copy(obuf.at[slot], block(yout_ref, blk), sem_out.at[slot])

    @pl.when((e == 0) & (n_used > 0))
    def _():
        x_copy(0, 0).start()

    @pl.when(blk_hi > blk_lo)
    def _():
        n_slab, n_t, lanes = wt_scr.shape
        for part in range(2 * dh // n_t):
            rows = slice(part * (n_t // 2), (part + 1) * (n_t // 2))
            for s in range(n_slab):
                cols = slice(s * lanes, (s + 1) * lanes)
                wt_scr[s] = w1_ref[0, cols, part * n_t:(part + 1) * n_t].T
                wg_scr[rows, cols] = wt_scr[s, pl.ds(0, n_t // 2, stride=2), :].astype(BF16)
                wl_scr[rows, cols] = wt_scr[s, pl.ds(1, n_t // 2, stride=2), :].astype(BF16)
        w2_scr[...] = w2_ref[0].astype(BF16)

        def one_block(blk, carry):
            slot = blk % 2

            @pl.when(blk + 1 < n_used)
            def _():
                x_copy(blk + 1, 1 - slot).start()

            x_copy(blk, slot).wait()

            @pl.when(blk >= 2)
            def _():
                o_copy(blk - 2, slot).wait()

            xb = _load_token_tiles(xbuf.at[slot], 0, bm).astype(BF16)
            bias = b1_ref[0]
            glu = jnp.minimum(_dot_nt(xb, wg_scr[...]) + bias[:, :dh], SWIGLU_LIMIT)
            lin = jnp.clip(_dot_nt(xb, wl_scr[...]) + bias[:, dh:], -SWIGLU_LIMIT, SWIGLU_LIMIT)
            act = glu * jax.nn.sigmoid(SWIGLU_ALPHA * glu) * (lin + 1.0)
            _store_token_tiles(obuf.at[slot], _dot(act.astype(BF16), w2_scr[...]) + b2_ref[0])
            o_copy(blk, slot).start()
            return carry

        lax.fori_loop(blk_lo, blk_hi, one_block, 0)

    @pl.when(e == pl.num_programs(0) - 1)
    def _():
        @pl.when(n_used >= 1)
        def _():
            o_copy(0, (n_used - 1) % 2).wait()

        @pl.when(n_used >= 2)
        def _():
            o_copy(0, n_used % 2).wait()

        obuf[0] = jnp.zeros(obuf.shape[1:], obuf.dtype)
        n_blocks = yout_ref.shape[0] // rows_blk

        def start_tail(blk, carry):
            o_copy(blk, 0).start()
            return carry

        def wait_tail(blk, carry):
            o_copy(0, 0).wait()
            return carry

        lax.fori_loop(n_used, n_blocks, start_tail, 0)
        lax.fori_loop(n_used, n_blocks, wait_tail, 0)


def _experts(pad_end, n_used, xin, w1, b1p, w2, b2):
    ne, d, dh2 = w1.shape
    dh = dh2 // 2
    rows_blk = BM_MOE * SUBLANES
    grid_spec = pltpu.PrefetchScalarGridSpec(
        num_scalar_prefetch=2,
        grid=(ne,),
        in_specs=[
            pl.BlockSpec((1, d, dh2), lambda e, pe, nu: (e, 0, 0)),
            pl.BlockSpec((1, 1, dh2), lambda e, pe, nu: (e, 0, 0)),
            pl.BlockSpec((1, dh, d), lambda e, pe, nu: (e, 0, 0)),
            pl.BlockSpec((1, 1, d), lambda e, pe, nu: (e, 0, 0)),
            pl.BlockSpec(memory_space=pl.ANY),
        ],
        out_specs=pl.BlockSpec(memory_space=pl.ANY),
        scratch_shapes=[
            pltpu.VMEM((2, rows_blk, LANES), F32),
            pltpu.VMEM((2, rows_blk, LANES), F32),
            pltpu.VMEM((d // 128, dh, 128), F32),
            pltpu.VMEM((dh, d), BF16),
            pltpu.VMEM((dh, d), BF16),
            pltpu.VMEM((dh, d), BF16),
            pltpu.SemaphoreType.DMA((2,)),
            pltpu.SemaphoreType.DMA((2,)),
        ],
    )
    return pl.pallas_call(
        _experts_kernel,
        grid_spec=grid_spec,
        out_shape=jax.ShapeDtypeStruct(xin.shape, F32),
        compiler_params=_params(("arbitrary",)),
        name="experts",
    )(pad_end, n_used, w1, b1p, w2, b2, xin)


def _combine_kernel(*refs):
    rstart_ref, rlen_ref, rstart_next_ref, rlen_next_ref = refs[:4]
    li_refs = refs[4:4 + TOP_K]
    gate_refs = refs[4 + TOP_K:4 + 2 * TOP_K]
    h1_ref, fg_ref, yout_ref, o_ref, wbuf, hacc, sem = refs[4 + 2 * TOP_K:]
    tc = o_ref.shape[0]
    i = pl.program_id(0)
    slot = i % 2
    n_experts = rstart_ref.shape[0]
    win_rows = WIN_COMBINE * SUBLANES

    def tile(ref, token):
        return ref.at[pl.ds(pl.multiple_of(token * SUBLANES, SUBLANES), SUBLANES)]

    def windows_of(rl_ref, e):
        return (rl_ref[e, 0] + (WIN_COMBINE - 1)) // WIN_COMBINE

    def start_windows(rs_ref, rl_ref, s):
        def per_expert(e, base):
            start = rs_ref[e, 0]
            n_win = windows_of(rl_ref, e)

            def per_window(w, carry):
                src = yout_ref.at[pl.ds(pl.multiple_of((start + w * WIN_COMBINE) * SUBLANES, SUBLANES),
                                        win_rows)]
                dst = wbuf.at[s, pl.ds(pl.multiple_of((base + w) * win_rows, win_rows), win_rows)]
                pltpu.make_async_copy(src, dst, sem.at[s]).start()
                return carry

            lax.fori_loop(0, n_win, per_window, 0)
            return base + n_win

        lax.fori_loop(0, n_experts, per_expert, 0)

    @pl.when(i == 0)
    def _():
        start_windows(rstart_ref, rlen_ref, 0)

    @pl.when(i + 1 < pl.num_programs(0))
    def _():
        start_windows(rstart_next_ref, rlen_next_ref, 1 - slot)

    n_windows = lax.fori_loop(0, n_experts, lambda e, n: n + windows_of(rlen_ref, e), 0)

    def wait_window(w, carry):
        pltpu.make_async_copy(yout_ref.at[pl.ds(0, win_rows)], wbuf.at[slot, pl.ds(0, win_rows)],
                              sem.at[slot]).wait()
        return carry

    lax.fori_loop(0, n_windows, wait_window, 0)

    def window_tile(j, tk):
        return wbuf[slot, pl.ds(pl.multiple_of(li_refs[j][tk], SUBLANES), SUBLANES), :]

    def per_token(tk, carry):
        acc = gate_refs[0][tk] * window_tile(0, tk)
        for j in range(1, TOP_K):
            acc = acc + gate_refs[j][tk] * window_tile(j, tk)
        tile(hacc, tk)[...] = acc
        return carry

    lax.fori_loop(0, tc, per_token, 0, unroll=32)
    h = h1_ref[...] + _load_token_tiles(hacc, 0, tc)
    o_ref[...] = h * lax.rsqrt(jnp.mean(h * h, axis=-1, keepdims=True) + RMS_EPS) * fg_ref[...]


def _combine(rstart, rlen, li, gates, h1, fg, yout):
    t, d = h1.shape
    tc = TC_COMBINE
    n_steps = t // tc
    n_experts = rstart.shape[0] // n_steps
    buf_rows = TOP_K * tc + n_experts * WIN_COMBINE
    runs = pl.BlockSpec((n_experts, 128), lambda i: (i, 0), memory_space=pltpu.SMEM)
    runs_next = pl.BlockSpec((n_experts, 128), lambda i: (jnp.minimum(i + 1, n_steps - 1), 0),
                             memory_space=pltpu.SMEM)
    return pl.pallas_call(
        _combine_kernel,
        grid=(n_steps,),
        in_specs=[runs, runs, runs_next, runs_next]
        + _choice_specs(n_steps, tc) + _choice_specs(n_steps, tc) + [
            pl.BlockSpec((tc, d), lambda i: (i, 0)),
            pl.BlockSpec((1, d), lambda i: (0, 0)),
            pl.BlockSpec(memory_space=pl.ANY),
        ],
        out_specs=pl.BlockSpec((tc, d), lambda i: (i, 0)),
        out_shape=jax.ShapeDtypeStruct((t, d), F32),
        scratch_shapes=[pltpu.VMEM((2, buf_rows * SUBLANES, LANES), F32),
                        pltpu.VMEM((tc * SUBLANES, LANES), F32),
                        pltpu.SemaphoreType.DMA((2,))],
        compiler_params=_params(("arbitrary",)),
        name="combine",
    )(rstart, rlen, rstart, rlen, *([li.reshape(-1)] * TOP_K), *([gates.reshape(-1)] * TOP_K),
      h1, fg, yout)


def kernel(x, norm1_g, w_in, sgu_ln_g, sgu_ln_b, sgu_w, sgu_b, mu_rkv, mu_wag, decay_w0, decay_w1,
           decay_w2, iclr_a0, iclr_a1, iclr_a2, gate_g1, gate_g2, k_k, k_a, r_k, lnx_g, lnx_b, w_out,
           norm2_g, router_w, router_b, moe_w1, moe_b1, moe_w2, moe_b2, final_g):
    batch, seq, d = x.shape
    t = batch * seq
    depth = w_in.shape[0]
    sgu_width = sgu_ln_g.shape[1]
    rw = mu_rkv.shape[2]
    n_dec, n_icl, n_gate = decay_w1.shape[2], iclr_a1.shape[2], gate_g1.shape[2]
    assert n_dec == HEAD and n_icl == HEAD and n_gate == PAIR and rw % PAIR == 0
    assert seq % TT_RWKV == 0 and seq % TT_PREP == 0 and t % TM_PROJ == 0
    assert sgu_w.shape[2] == SGU_BLOCK and TM_PROJ % SGU_BLOCK == 0
    assert t % TR_ROUTE == 0 and t % TD_DISPATCH == 0 and TR_ROUTE % TC_COMBINE == 0
    assert (t * TOP_K) % BM_MOE == 0 and router_w.shape[2] == N_EXPERTS
    assert depth == 1, "the final RMSNorm is fused into the last layer's combine kernel"
    assert d == SUBLANES * LANES, "the MoE row movers copy one (8, 128) f32 tile per token"
    assert BM_MOE + N_EXPERTS - 1 >= WIN_COMBINE

    h = x.reshape(t, d)
    for l in range(depth):
        win_bf = w_in[l].astype(BF16)
        wl = jnp.concatenate([decay_w1[l], iclr_a1[l], gate_g1[l]], axis=1)
        mucat = jnp.concatenate([
            jnp.broadcast_to(mu_wag[l, 0][:, None], (d, n_dec)),
            jnp.broadcast_to(mu_wag[l, 1][:, None], (d, n_icl)),
            jnp.broadcast_to(mu_wag[l, 2][:, None], (d, n_gate))], axis=1)
        zeros = jnp.zeros((HEAD, rw), F32)
        w2pad = jnp.concatenate([decay_w2[l], zeros], axis=0)
        a2pad = jnp.concatenate([zeros, iclr_a2[l]], axis=0)
        pvec = jnp.stack([decay_w0[l], iclr_a0[l], k_k[l], k_a[l], r_k[l].reshape(-1),
                          lnx_g[l], lnx_b[l], jnp.zeros((rw,), F32)], axis=0)
        bias2d = jnp.repeat(sgu_b[l].T, sgu_width // sgu_b.shape[1], axis=1)
        wo_bf = w_out[l].astype(BF16)
        b1p = jnp.concatenate([moe_b1[l][:, 0::2], moe_b1[l][:, 1::2]], axis=-1)[:, None, :]
        b2 = moe_b2[l][:, None, :]

        assert w_in.shape[2] == 2 * sgu_width + 3 * rw
        proj, lora, ya = _in_proj(h, norm1_g[l][None, :], win_bf, wl, mucat, sgu_ln_g[l][None, :],
                                  sgu_ln_b[l][None, :], sgu_w[l], bias2d)
        rp, yq, bonus, gate, g_m, h_m = _rwkv_prep(proj, lora, mu_rkv[l], pvec, w2pad, a2pad,
                                                   gate_g2[l], seq, 0)
        yb = _rwkv_scan(rp, yq, bonus, gate, g_m, h_m, pvec, batch)
        h1, hn2, logits_t = _out_proj(h, ya, yb, wo_bf, norm2_g[l][None, :], router_w[l].T,
                                      router_b[l][:, None])
        pos, gates, pend, li, rstart, rlen = _route(logits_t)

        pad_end = pend[:, 0]
        n_blocks = (t * TOP_K) // BM_MOE + N_EXPERTS
        n_used = pad_end[-1:] // BM_MOE

        xin = _dispatch(pad_end, n_used, pos, hn2, n_blocks * BM_MOE)
        yout = _experts(pad_end, n_used, xin, moe_w1[l], b1p, moe_w2[l], b2)
        h = _combine(rstart, rlen, li, gates, h1, final_g[None, :], yout)
    return h.reshape(batch, seq, d)
```

```python
import functools

import jax
import jax.numpy as jnp
from jax import lax
from jax.experimental import pallas as pl
from jax.experimental.pallas import tpu as pltpu

F32 = jnp.float32
BF16 = jnp.bfloat16
I32 = jnp.int32

RMS_EPS = 1e-5
LN_EPS = 1e-5
LNX_EPS = 64e-5
CHUNK = 64
SGU_BLOCK = 128
HEAD = 64
PAIR = 2 * HEAD
N_EXPERTS = 32
TOP_K = 4
SWIGLU_ALPHA = 1.702
SWIGLU_LIMIT = 7.0

V7X_VMEM_LIMIT = 56 * 1024 * 1024

TM_PROJ = 1024
TT_PREP = 1024
N_SUB_PREP = 1
TT_RWKV = 512
TR_ROUTE = 1024
BM_MOE = 256
TD_DISPATCH = 2048
TC_COMBINE = 512
WIN_COMBINE = 64


def _dot(a, b):
    return jnp.dot(a, b, preferred_element_type=F32)


def _dot_nt(a, b):
    return lax.dot_general(a, b, (((1,), (1,)), ((), ())), preferred_element_type=F32)


def _split(x):
    hi = x.astype(BF16)
    lo = (x - hi.astype(F32)).astype(BF16)
    return hi, lo


def _dot3(a, b):
    ah, al = _split(a)
    bh, bl = _split(b)
    return _dot(ah, bh) + _dot(al, bh) + _dot(ah, bl)


def _dot3_nt(a, b):
    ah, al = _split(a)
    bh, bl = _split(b)
    return _dot_nt(ah, bh) + _dot_nt(al, bh) + _dot_nt(ah, bl)


def _head_sum(x, bd):
    hi, lo = _split(x)
    return _dot(hi, bd) + _dot(lo, bd)


SUBLANES = 8
LANES = 128


def _store_token_tiles(ref, x):
    n = x.shape[0]
    for s in range(SUBLANES):
        ref[pl.ds(s, n, stride=SUBLANES), :] = x[:, s * LANES:(s + 1) * LANES]


def _load_token_tiles(ref, row0, n):
    return jnp.concatenate(
        [ref[pl.ds(row0 + s, n, stride=SUBLANES), :] for s in range(SUBLANES)], axis=1)


def _params(sem, vmem=V7X_VMEM_LIMIT):
    return pltpu.CompilerParams(dimension_semantics=sem, vmem_limit_bytes=vmem)


def _in_proj_kernel(x_ref, g_ref, win_ref, wl_ref, mucat_ref, lng_ref, lnb_ref, w_ref, bias_ref,
                    proj_ref, lora_ref, o_ref, wl_scr, wm_scr):
    n_heads = w_ref.shape[0]

    @pl.when(pl.program_id(0) == 0)
    def _():
        wl = wl_ref[...]
        mu = mucat_ref[...]
        n = wl.shape[1]
        wl_scr[:, 0:n] = (wl * (1.0 - mu)).astype(BF16)
        wl_scr[:, n:2 * n] = (wl * mu).astype(BF16)
        qi = lax.broadcasted_iota(I32, (SGU_BLOCK, SGU_BLOCK), 0) // CHUNK
        kj = lax.broadcasted_iota(I32, (SGU_BLOCK, SGU_BLOCK), 1) // CHUNK
        for h in range(n_heads):
            wm_scr[h] = jnp.where(kj <= qi, w_ref[h], 0.0).astype(BF16)

    x = x_ref[...]
    hn = x * lax.rsqrt(jnp.mean(x * x, axis=-1, keepdims=True) + RMS_EPS) * g_ref[...]
    hb = hn.astype(BF16)
    width = lng_ref.shape[1]
    two_w = 2 * width
    proj_ref[...] = _dot(hb, win_ref[:, two_w:])
    lora_ref[...] = _dot(hb, wl_scr[...])

    z = _dot(hb, win_ref[:, :two_w])
    tm = z.shape[0]
    gz = 0.5 * z * (1.0 + lax.erf(z * (2.0 ** -0.5)))
    u = gz[:, :width]
    v = gz[:, width:]
    mu = jnp.mean(v, axis=-1, keepdims=True)
    vc = v - mu
    var = jnp.mean(vc * vc, axis=-1, keepdims=True)
    vn = vc * lax.rsqrt(var + LN_EPS) * lng_ref[...] + lnb_ref[...]
    lane = lax.broadcasted_iota(I32, (SGU_BLOCK, PAIR), 1)
    is_lo = lane < HEAD
    bias = bias_ref[...]
    for blk in range(tm // SGU_BLOCK):
        rows = slice(blk * SGU_BLOCK, (blk + 1) * SGU_BLOCK)
        for p in range(width // PAIR):
            cols = slice(p * PAIR, (p + 1) * PAIR)
            vp = vn[rows, cols]
            lo = jnp.where(is_lo, vp, 0.0).astype(BF16)
            hi = jnp.where(is_lo, 0.0, vp).astype(BF16)
            sv = _dot(wm_scr[2 * p], lo) + _dot(wm_scr[2 * p + 1], hi)
            o_ref[rows, cols] = u[rows, cols] * (sv + bias[:, cols])


def _in_proj(x2, g, win_bf, wl, mucat, ln_g, ln_b, w_s, bias2d):
    t, d = x2.shape
    width = ln_g.shape[1]
    n_rkv = win_bf.shape[1] - 2 * width
    n_l = wl.shape[1]
    n_heads = w_s.shape[0]
    tm = TM_PROJ
    const = lambda *shape: pl.BlockSpec(shape, lambda i: (0,) * len(shape))
    row = lambda n: pl.BlockSpec((tm, n), lambda i: (i, 0))
    return pl.pallas_call(
        _in_proj_kernel,
        grid=(t // tm,),
        in_specs=[
            row(d), const(1, d), const(d, win_bf.shape[1]), const(d, n_l), const(d, n_l),
            const(1, width), const(1, width), const(n_heads, SGU_BLOCK, SGU_BLOCK),
            const(SGU_BLOCK, width),
        ],
        out_specs=[row(n_rkv), row(2 * n_l), row(width)],
        out_shape=[
            jax.ShapeDtypeStruct((t, n_rkv), F32),
            jax.ShapeDtypeStruct((t, 2 * n_l), F32),
            jax.ShapeDtypeStruct((t, width), F32),
        ],
        scratch_shapes=[pltpu.VMEM((d, 2 * n_l), BF16),
                        pltpu.VMEM((n_heads, SGU_BLOCK, SGU_BLOCK), BF16)],
        compiler_params=_params(("arbitrary",)),
        name="in_proj_sgu",
    )(x2, g, win_bf, wl, mucat, ln_g, ln_b, w_s, bias2d)


def _rwkv_prep_kernel(seq_len, pr_ref, pk_ref, pv_ref, ppr_ref, ppk_ref, ppv_ref, lo_ref, plo_ref,
                      mu_ref, pvec_ref, w2_ref, a2_ref, g2_ref,
                      rp_ref, yq_ref, bonus_ref, gate_ref, g_ref, h_ref):
    tt = pr_ref.shape[0]
    ts = tt // N_SUB_PREP
    n_chunks = ts // CHUNK
    i = pl.program_id(0)
    keep = jnp.where((i * tt) % seq_len == 0, 0.0, 1.0)

    mu = mu_ref[...]
    pvec = pvec_ref[...]
    w0, a0, k_k, k_a, r_k = pvec[0:1], pvec[1:2], pvec[2:3], pvec[3:4], pvec[4:5]
    w2b = w2_ref[...].astype(BF16)
    a2b = a2_ref[...].astype(BF16)
    g2b = g2_ref[...].astype(BF16)
    half = lo_ref.shape[1] // 2

    lane_r = lax.broadcasted_iota(I32, (PAIR, PAIR), 0)
    lane_c = lax.broadcasted_iota(I32, (PAIR, PAIR), 1)
    bd = jnp.where((lane_r // HEAD) == (lane_c // HEAD), 1.0, 0.0).astype(BF16)
    tri_ones = jnp.where(((lane_r // CHUNK) == (lane_c // CHUNK)) & (lane_c <= lane_r),
                         1.0, 0.0).astype(BF16)
    lane = lax.broadcasted_iota(I32, (CHUNK, PAIR), 1)
    is_lo = lane < HEAD

    def stack(x):
        return jnp.concatenate([jnp.where(is_lo, x, 0.0), jnp.where(is_lo, 0.0, x)], axis=0)

    n2 = 2 * PAIR
    ri = lax.broadcasted_iota(I32, (n2, n2), 0)
    ci = lax.broadcasted_iota(I32, (n2, n2), 1)
    same_head = ((ri // CHUNK) % 2) == ((ci // CHUNK) % 2)
    t_i = ri % CHUNK
    s_i = ci % CHUNK
    tri = same_head & (s_i < t_i + ri // PAIR)
    eye = lane_r == lane_c

    for sub in range(N_SUB_PREP):
        r0 = sub * ts
        tile_rows = slice(r0, r0 + ts)

        def shift(ref, pref, cols=slice(None)):
            x = ref[tile_rows, cols]
            if sub == 0:
                first = pref[7:8, cols] * keep
            else:
                first = ref[r0 - 1:r0, cols]
            rowc = lax.broadcasted_iota(I32, x.shape, 0)
            return x, jnp.where(rowc == 0, first, pltpu.roll(x, 1, 0))

        def shift_mix(ref, pref, m):
            x, xs = shift(ref, pref)
            return x + (xs - x) * m

        r = shift_mix(pr_ref, ppr_ref, mu[0:1])
        k = shift_mix(pk_ref, ppk_ref, mu[1:2])
        v = shift_mix(pv_ref, ppv_ref, mu[2:3])

        _, lo_b = shift(lo_ref, plo_ref, slice(half, 2 * half))
        l_all = lo_ref[tile_rows, :half] + lo_b
        l_wa = l_all[:, :PAIR]
        l_g = l_all[:, PAIR:]

        dw = _dot(jnp.tanh(l_wa).astype(BF16), w2b)
        ia = _dot(l_wa.astype(BF16), a2b)
        gate_ref[tile_rows, :] = _dot(jax.nn.sigmoid(l_g).astype(BF16), g2b).astype(BF16)

        zneg = -(w0 + dw)
        softplus = jnp.maximum(zneg, 0.0) + jnp.log(1.0 + jnp.exp(-jnp.abs(zneg)))
        logw = -jnp.exp(-softplus - 0.5)
        iclr = jax.nn.sigmoid(a0 + ia)

        kk = k * k_k
        kk = kk / jnp.maximum(jnp.sqrt(_head_sum(kk * kk, bd)), 1e-12)
        k2 = k * (1.0 + (iclr - 1.0) * k_a)
        a = -kk
        b = kk * iclr
        bonus_ref[tile_rows, :] = (_head_sum(r * k2 * r_k, bd) * v).astype(BF16)

        lw_hi, lw_lo = _split(logw)
        cl = jnp.concatenate(
            [_dot(tri_ones, lw_hi[q * PAIR:(q + 1) * PAIR]) + _dot(tri_ones, lw_lo[q * PAIR:(q + 1) * PAIR])
             for q in range(ts // PAIR)], axis=0)
        at_all = a * jnp.exp(cl - logw)
        rt_all = r * jnp.exp(cl)
        w_inv = jnp.exp(-cl)
        bt_all = b * w_inv
        kt_all = k2 * w_inv

        chunks = range(n_chunks)
        rows = [slice(c * CHUNK, (c + 1) * CHUNK) for c in chunks]
        out_rows = [slice(r0 + c * CHUNK, r0 + (c + 1) * CHUNK) for c in chunks]
        last = [cl[rows[c]][CHUNK - 1:CHUNK, :] for c in chunks]
        w_rem = [jnp.exp(last[c] - cl[rows[c]]) for c in chunks]
        a_s = [stack(at_all[rows[c]]) for c in chunks]
        r_s = [stack(rt_all[rows[c]]) for c in chunks]
        v_s = [stack(v[rows[c]]) for c in chunks]
        v_sb = [v_s[c].astype(BF16) for c in chunks]
        bk_h = [jnp.concatenate([stack(b[rows[c]] * w_rem[c]), stack(k2[rows[c]] * w_rem[c])],
                                axis=0).astype(BF16) for c in chunks]

        a_all = []
        for c in chunks:
            lhs = jnp.concatenate([a_s[c], r_s[c]], axis=0).astype(BF16)
            rhs = jnp.concatenate([stack(bt_all[rows[c]]), stack(kt_all[rows[c]])],
                                  axis=0).astype(BF16)
            a_all.append(jnp.where(tri, _dot_nt(lhs, rhs), 0.0))
        n_k = [a_all[c][:PAIR, :PAIR].astype(BF16) for c in chunks]
        a_r = [a_all[c][PAIR:, :].astype(BF16) for c in chunks]
        x = [jnp.concatenate([a_s[c], _dot(a_all[c][:PAIR, PAIR:].astype(BF16), v_sb[c])], axis=1)
             for c in chunks]
        steps = CHUNK.bit_length() - 1
        for it in range(steps):
            x = [x[c] + _dot(n_k[c], x[c].astype(BF16)) for c in chunks]
            if it + 1 < steps:
                n_k = [_dot(n_k[c], n_k[c]).astype(BF16) for c in chunks]
        zero_b = jnp.zeros((PAIR, PAIR), BF16)
        ry = [_dot(a_r[c], jnp.concatenate(
            [x[c].astype(BF16), jnp.concatenate([zero_b, v_sb[c]], axis=1)], axis=0)) for c in chunks]
        for c in chunks:
            r_c = rt_all[rows[c]]
            rp_ref[out_rows[c], :] = (ry[c][:CHUNK, :PAIR] + ry[c][CHUNK:, :PAIR] + r_c).astype(BF16)
            yq_ref[out_rows[c], :] = (ry[c][:CHUNK, PAIR:] + ry[c][CHUNK:, PAIR:]).astype(BF16)
        p_t = [x[c][:, :PAIR].T.astype(BF16) for c in chunks]
        qv_t = [jnp.concatenate([x[c][:, PAIR:], v_s[c]], axis=0).T.astype(BF16) for c in chunks]
        c0 = sub * n_chunks
        for c in chunks:
            g_ref[c0 + c, 0] = (jnp.where(eye, jnp.exp(last[c]), 0.0)
                                + _dot(p_t[c], bk_h[c][:PAIR]))
        for c in chunks:
            hbd = _dot(qv_t[c], bk_h[c])
            h_ref[c0 + c, 0] = hbd[:HEAD] + hbd[HEAD:]


def _rwkv_prep(proj, lora, mu_rkv, pvec, w2pad, a2pad, g2, seq_len, col0):
    t = proj.shape[0]
    width = mu_rkv.shape[1]
    n_pairs = width // PAIR
    tt = TT_PREP
    nl = lora.shape[1]
    cb = col0 // PAIR
    wb = width // PAIR

    def prev_rows(i):
        return jnp.maximum(i * (tt // 8) - 1, 0)

    def tok(c):
        return pl.BlockSpec((tt, PAIR), lambda i, p: (i, c + p))

    def prev(c):
        return pl.BlockSpec((8, PAIR), lambda i, p: (prev_rows(i), c + p))

    out_tok = pl.BlockSpec((tt, PAIR), lambda i, p: (i, p))
    tok_shape = jax.ShapeDtypeStruct((t, width), BF16)
    return pl.pallas_call(
        functools.partial(_rwkv_prep_kernel, seq_len),
        grid=(t // tt, n_pairs),
        in_specs=[
            tok(cb), tok(cb + wb), tok(cb + 2 * wb),
            prev(cb), prev(cb + wb), prev(cb + 2 * wb),
            pl.BlockSpec((tt, nl), lambda i, p: (i, 0)),
            pl.BlockSpec((8, nl), lambda i, p: (prev_rows(i), 0)),
            pl.BlockSpec((3, PAIR), lambda i, p: (0, p)),
            pl.BlockSpec((8, PAIR), lambda i, p: (0, p)),
            pl.BlockSpec((PAIR, PAIR), lambda i, p: (0, p)),
            pl.BlockSpec((PAIR, PAIR), lambda i, p: (0, p)),
            pl.BlockSpec((PAIR, PAIR), lambda i, p: (0, p)),
        ],
        out_specs=[
            out_tok, out_tok, out_tok, out_tok,
            pl.BlockSpec((tt // CHUNK, 1, PAIR, PAIR), lambda i, p: (i, p, 0, 0)),
            pl.BlockSpec((tt // CHUNK, 1, HEAD, PAIR), lambda i, p: (i, p, 0, 0)),
        ],
        out_shape=[
            tok_shape, tok_shape, tok_shape, tok_shape,
            jax.ShapeDtypeStruct((t // CHUNK, n_pairs, PAIR, PAIR), F32),
            jax.ShapeDtypeStruct((t // CHUNK, n_pairs, HEAD, PAIR), F32),
        ],
        compiler_params=_params(("arbitrary", "arbitrary")),
        name="rwkv_prep",
    )(proj, proj, proj, proj, proj, proj, lora, lora, mu_rkv, pvec, w2pad, a2pad, g2)


def _rwkv_scan_kernel(rp_ref, yq_ref, bonus_ref, gate_ref, g_ref, h_ref, pvec_ref, o_ref, s_scr, y_scr):
    nb, tt, width = rp_ref.shape
    n_pairs = width // PAIR

    @pl.when(pl.program_id(0) == 0)
    def _():
        s_scr[...] = jnp.zeros_like(s_scr)

    lane = lax.broadcasted_iota(I32, (HEAD, PAIR), 1)
    is_lo = lane < HEAD
    lane_r = lax.broadcasted_iota(I32, (PAIR, PAIR), 0)
    lane_c = lax.broadcasted_iota(I32, (PAIR, PAIR), 1)
    bd = jnp.where((lane_r // HEAD) == (lane_c // HEAD), 1.0, 0.0).astype(BF16)

    chains = [(b, p) for b in range(nb) for p in range(n_pairs)]
    state = {ch: s_scr[ch[0], ch[1]] for ch in chains}
    for c in range(tt // CHUNK):
        rows = slice(c * CHUNK, (c + 1) * CHUNK)
        for (b, p) in chains:
            cols = slice(p * PAIR, (p + 1) * PAIR)
            s0 = state[(b, p)]
            s_st = jnp.concatenate([jnp.where(is_lo, s0, 0.0), jnp.where(is_lo, 0.0, s0)], axis=0)
            y_scr[b, rows, cols] = (_dot_nt(rp_ref[b, rows, cols], s_st.astype(BF16))
                                    + yq_ref[b, rows, cols].astype(F32))
        state = {(b, p): _dot3(state[(b, p)], g_ref[b, c, p]) + h_ref[b, c, p] for (b, p) in chains}
    for (b, p) in chains:
        s_scr[b, p] = state[(b, p)]

    pvec = pvec_ref[...]
    for (b, p) in chains:
        cols = slice(p * PAIR, (p + 1) * PAIR)
        y = y_scr[b, :, cols]
        mu = _head_sum(y, bd) * (1.0 / HEAD)
        yc = y - mu
        var = _head_sum(yc * yc, bd) * (1.0 / HEAD)
        yn = yc * lax.rsqrt(var + LNX_EPS) * pvec[5:6, cols] + pvec[6:7, cols]
        o_ref[b, :, cols] = ((yn + bonus_ref[b, :, cols].astype(F32))
                             * gate_ref[b, :, cols].astype(F32))


def _rwkv_scan(rp, yq, bonus, gate, g, h, pvec, batch):
    t, width = rp.shape
    n_pairs = width // PAIR
    seq = t // batch
    tt = TT_RWKV
    nc = tt // CHUNK
    tok = pl.BlockSpec((batch, tt, width), lambda i: (0, i, 0))
    as_seq = lambda z: z.reshape(batch, seq, width)
    out = pl.pallas_call(
        _rwkv_scan_kernel,
        grid=(seq // tt,),
        in_specs=[
            tok, tok, tok, tok,
            pl.BlockSpec((batch, nc, n_pairs, PAIR, PAIR), lambda i: (0, i, 0, 0, 0)),
            pl.BlockSpec((batch, nc, n_pairs, HEAD, PAIR), lambda i: (0, i, 0, 0, 0)),
            pl.BlockSpec((8, width), lambda i: (0, 0)),
        ],
        out_specs=tok,
        out_shape=jax.ShapeDtypeStruct((batch, seq, width), F32),
        scratch_shapes=[pltpu.VMEM((batch, n_pairs, HEAD, PAIR), F32),
                        pltpu.VMEM((batch, tt, width), F32)],
        compiler_params=_params(("arbitrary",)),
        name="rwkv_scan",
    )(as_seq(rp), as_seq(yq), as_seq(bonus), as_seq(gate),
      g.reshape(batch, seq // CHUNK, n_pairs, PAIR, PAIR),
      h.reshape(batch, seq // CHUNK, n_pairs, HEAD, PAIR), pvec)
    return out.reshape(t, width)


def _out_proj_kernel(x_ref, ya_ref, yb_ref, wo_ref, g_ref, rwt_ref, rb_ref, h1_ref, hn_ref, lg_ref):
    wa = ya_ref.shape[1]
    h1 = (x_ref[...] + _dot(ya_ref[...].astype(BF16), wo_ref[0:wa, :])
          + _dot(yb_ref[...].astype(BF16), wo_ref[wa:, :]))
    h1_ref[...] = h1
    hn = h1 * lax.rsqrt(jnp.mean(h1 * h1, axis=-1, keepdims=True) + RMS_EPS) * g_ref[...]
    _store_token_tiles(hn_ref, hn)
    lg_ref[...] = _dot3_nt(rwt_ref[...], hn) + rb_ref[...]


def _out_proj(x2, ya, yb, wo_bf, g2, rwt, rb):
    t, d = x2.shape
    wa = ya.shape[1]
    wb = yb.shape[1]
    ne = rwt.shape[0]
    tm = TM_PROJ
    return pl.pallas_call(
        _out_proj_kernel,
        grid=(t // tm,),
        in_specs=[
            pl.BlockSpec((tm, d), lambda i: (i, 0)),
            pl.BlockSpec((tm, wa), lambda i: (i, 0)),
            pl.BlockSpec((tm, wb), lambda i: (i, 0)),
            pl.BlockSpec((wa + wb, d), lambda i: (0, 0)),
            pl.BlockSpec((1, d), lambda i: (0, 0)),
            pl.BlockSpec((ne, d), lambda i: (0, 0)),
            pl.BlockSpec((ne, 1), lambda i: (0, 0)),
        ],
        out_specs=[
            pl.BlockSpec((tm, d), lambda i: (i, 0)),
            pl.BlockSpec((tm * SUBLANES, LANES), lambda i: (i, 0)),
            pl.BlockSpec((ne, tm), lambda i: (0, i)),
        ],
        out_shape=[
            jax.ShapeDtypeStruct((t, d), F32),
            jax.ShapeDtypeStruct((t * SUBLANES, LANES), F32),
            jax.ShapeDtypeStruct((ne, t), F32),
        ],
        compiler_params=_params(("arbitrary",)),
        name="out_proj",
    )(x2, ya, yb, wo_bf, g2, rwt, rb)


def _route_kernel(lg_ref, pos_ref, gate_ref, pend_ref, li_ref, rstart_ref, rlen_ref,
                  carry_scr, pstart_scr):
    phase = pl.program_id(0)
    first = pl.program_id(1) == 0

    @pl.when(first & (phase == 0))
    def _():
        carry_scr[...] = jnp.zeros_like(carry_scr)
        pstart_scr[...] = jnp.zeros_like(pstart_scr)
        pend_ref[...] = jnp.zeros_like(pend_ref)

    @pl.when(first & (phase == 1))
    def _():
        counts = carry_scr[...]
        padded = jnp.ceil(counts * (1.0 / BM_MOE)) * BM_MOE
        row = lax.broadcasted_iota(I32, counts.shape, 0)
        end = padded
        s = 1
        while s < counts.shape[0]:
            end = end + jnp.where(row >= s, pltpu.roll(end, s, 0), 0.0)
            s *= 2
        pstart_scr[...] = end - padded
        pend_ref[...] = end.astype(I32)
        carry_scr[...] = jnp.zeros_like(carry_scr)

    l = lg_ref[...]
    ne, tr = l.shape
    e_iota = lax.broadcasted_iota(I32, (ne, tr), 0)
    chosen = jnp.zeros((ne, tr), F32)
    vals, sels = [], []
    for j in range(TOP_K):
        m = jnp.max(l, axis=0, keepdims=True)
        idx = jnp.min(jnp.where(l == m, e_iota, ne), axis=0, keepdims=True)
        sel = e_iota == idx
        vals.append(m)
        sels.append(sel)
        chosen = jnp.where(sel, 1.0, chosen)
        l = jnp.where(sel, -jnp.inf, l)
    @pl.when(phase == 0)
    def _():
        carry_scr[...] = carry_scr[...] + jnp.sum(chosen, axis=1, keepdims=True)

    @pl.when(phase == 1)
    def _():
        _route_emit(vals, sels, chosen, pos_ref, gate_ref, li_ref, rstart_ref, rlen_ref,
                    carry_scr, pstart_scr)


def _route_emit(vals, sels, chosen, pos_ref, gate_ref, li_ref, rstart_ref, rlen_ref,
                carry_scr, pstart_scr):
    ne, tr = chosen.shape
    ex = [jnp.exp(vj - vals[0]) for vj in vals]
    den = ex[0] + ex[1] + ex[2] + ex[3]
    for j in range(TOP_K):
        gate_ref[j:j + 1, :] = ex[j] / den

    ti = lax.broadcasted_iota(I32, (tr, tr), 0)
    tj = lax.broadcasted_iota(I32, (tr, tr), 1)
    upper = jnp.where(ti <= tj, 1.0, 0.0).astype(BF16)
    inc = _dot(chosen.astype(BF16), upper)
    carry = carry_scr[...]
    first_row = carry[:, 0:1] + pstart_scr[:, 0:1]
    row_of = inc - chosen + first_row
    for j in range(TOP_K):
        pj = jnp.sum(jnp.where(sels[j], row_of, 0.0), axis=0, keepdims=True)
        pos_ref[j:j + 1, :] = (pj * SUBLANES).astype(I32)
    carry_scr[...] = carry + inc[:, tr - 1:tr]

    tc = TC_COMBINE
    erow = lax.broadcasted_iota(I32, (ne, 128), 0)
    for s in range(tr // tc):
        cols = slice(s * tc, (s + 1) * tc)
        before = inc[:, s * tc - 1:s * tc] if s else jnp.zeros((ne, 1), F32)
        n_run = inc[:, (s + 1) * tc - 1:(s + 1) * tc] - before
        win_rows = jnp.broadcast_to(jnp.ceil(n_run * (1.0 / WIN_COMBINE)) * WIN_COMBINE, (ne, 128))
        base = win_rows
        k = 1
        while k < ne:
            base = base + jnp.where(erow >= k, pltpu.roll(base, k, 0), 0.0)
            k *= 2
        base = base - win_rows
        slot_of = (inc - chosen)[:, cols] - before + base[:, 0:1]
        for j in range(TOP_K):
            lj = jnp.sum(jnp.where(sels[j][:, cols], slot_of, 0.0), axis=0, keepdims=True)
            li_ref[j:j + 1, cols] = (lj * SUBLANES).astype(I32)
        rstart_ref[s * ne:(s + 1) * ne, :] = jnp.broadcast_to(first_row + before, (ne, 128)).astype(I32)
        rlen_ref[s * ne:(s + 1) * ne, :] = jnp.broadcast_to(n_run, (ne, 128)).astype(I32)


def _route(logits_t):
    ne, t = logits_t.shape
    tr = TR_ROUTE
    tok = pl.BlockSpec((TOP_K, tr), lambda ph, i: (0, i * ph))
    n_sub = tr // TC_COMBINE
    runs = pl.BlockSpec((n_sub * ne, 128), lambda ph, i: (i * ph, 0))
    runs_shape = jax.ShapeDtypeStruct((t // TC_COMBINE * ne, 128), I32)
    return pl.pallas_call(
        _route_kernel,
        grid=(2, t // tr),
        in_specs=[pl.BlockSpec((ne, tr), lambda ph, i: (0, i))],
        out_specs=[tok, tok, pl.BlockSpec((ne, 128), lambda ph, i: (0, 0)), tok, runs, runs],
        out_shape=[
            jax.ShapeDtypeStruct((TOP_K, t), I32),
            jax.ShapeDtypeStruct((TOP_K, t), F32),
            jax.ShapeDtypeStruct((ne, 128), I32),
            jax.ShapeDtypeStruct((TOP_K, t), I32),
            runs_shape, runs_shape,
        ],
        scratch_shapes=[pltpu.VMEM((ne, 128), F32), pltpu.VMEM((ne, 128), F32)],
        compiler_params=_params(("arbitrary", "arbitrary")),
        name="route",
    )(logits_t)


def _dispatch_kernel(pend_ref, nused_ref, *refs):
    pos_refs = refs[:TOP_K]
    hn_ref, xin_ref, zero_scr, sem = refs[TOP_K:]
    bm = zero_scr.shape[0] // SUBLANES
    td = pos_refs[0].shape[0]
    i = pl.program_id(0)

    def tile(ref, token):
        return ref.at[pl.ds(pl.multiple_of(token * SUBLANES, SUBLANES), SUBLANES)]

    def zero_block(row0):
        start = pl.multiple_of(row0 * SUBLANES, bm * SUBLANES)
        return pltpu.make_async_copy(zero_scr, xin_ref.at[pl.ds(start, bm * SUBLANES)], sem)

    @pl.when(i == 0)
    def _():
        zero_scr[...] = jnp.zeros_like(zero_scr)

        def has_rows(e):
            end = pend_ref[e]
            return jnp.where(e == 0, end >= bm, end > pend_ref[jnp.maximum(e - 1, 0)])

        def start_one(e, carry):
            @pl.when(has_rows(e))
            def _():
                zero_block(pend_ref[e] - bm).start()
            return carry

        def wait_one(e, carry):
            @pl.when(has_rows(e))
            def _():
                zero_block(0).wait()
            return carry

        def start_tail(blk, carry):
            zero_block(blk * bm).start()
            return carry

        def wait_tail(blk, carry):
            zero_block(0).wait()
            return carry

        n_blocks = xin_ref.shape[0] // (bm * SUBLANES)
        lax.fori_loop(0, pend_ref.shape[0], start_one, 0)
        lax.fori_loop(nused_ref[0], n_blocks, start_tail, 0)
        lax.fori_loop(0, pend_ref.shape[0], wait_one, 0)
        lax.fori_loop(nused_ref[0], n_blocks, wait_tail, 0)

    def issue(tk, carry):
        for j in range(TOP_K):
            dst = xin_ref.at[pl.ds(pl.multiple_of(pos_refs[j][tk], SUBLANES), SUBLANES)]
            pltpu.make_async_copy(tile(hn_ref, tk), dst, sem).start(priority=j % 2)
        return carry

    lax.fori_loop(0, td, issue, 0, unroll=8)

    for j in range(TOP_K):
        pltpu.make_async_copy(hn_ref, xin_ref.at[pl.ds(0, td * SUBLANES)], sem).wait()


def _choice_specs(n_steps, tile):
    return [pl.BlockSpec((tile,), functools.partial(lambda j, i, *_: (j * n_steps + i,), j),
                         memory_space=pltpu.SMEM) for j in range(TOP_K)]


def _dispatch(pad_end, n_used, pos, hn_tiles, n_rows):
    t = hn_tiles.shape[0] // SUBLANES
    td = TD_DISPATCH
    pos = pos.reshape(-1)
    grid_spec = pltpu.PrefetchScalarGridSpec(
        num_scalar_prefetch=2,
        grid=(t // td,),
        in_specs=_choice_specs(t // td, td) + [
            pl.BlockSpec((td * SUBLANES, LANES), lambda i, pe, nu: (i, 0)),
        ],
        out_specs=pl.BlockSpec(memory_space=pl.ANY),
        scratch_shapes=[pltpu.VMEM((BM_MOE * SUBLANES, LANES), F32), pltpu.SemaphoreType.DMA(())],
    )
    return pl.pallas_call(
        _dispatch_kernel,
        grid_spec=grid_spec,
        out_shape=jax.ShapeDtypeStruct((n_rows * SUBLANES, LANES), F32),
        compiler_params=_params(("arbitrary",)),
        name="dispatch",
    )(pad_end, n_used, *([pos] * TOP_K), hn_tiles)


def _experts_kernel(pend_ref, nused_ref, w1_ref, b1_ref, w2_ref, b2_ref, xin_ref, yout_ref,
                    xbuf, obuf, wt_scr, wg_scr, wl_scr, w2_scr, sem_in, sem_out):
    e = pl.program_id(0)
    rows_blk = xbuf.shape[1]
    bm = rows_blk // SUBLANES
    dh = w2_ref.shape[1]
    n_used = nused_ref[0]
    blk_lo = jnp.where(e == 0, 0, pend_ref[jnp.maximum(e - 1, 0)]) // bm
    blk_hi = pend_ref[e] // bm

    def block(ref, blk):
        return ref.at[pl.ds(pl.multiple_of(blk * rows_blk, rows_blk), rows_blk)]

    def x_copy(blk, slot):
        return pltpu.make_async_copy(block(xin_ref, blk), xbuf.at[slot], sem_in.at[slot])

    def o_copy(blk, slot):
        return pltpu.make_async_copy(obuf.at[slot], block(yout_ref, blk), sem_out.at[slot])

    @pl.when((e == 0) & (n_used > 0))
    def _():
        x_copy(0, 0).start()

    @pl.when(blk_hi > blk_lo)
    def _():
        n_slab, n_t, lanes = wt_scr.shape
        for part in range(2 * dh // n_t):
            rows = slice(part * (n_t // 2), (part + 1) * (n_t // 2))
            for s in range(n_slab):
                cols = slice(s * lanes, (s + 1) * lanes)
                wt_scr[s] = w1_ref[0, cols, part * n_t:(part + 1) * n_t].T
                wg_scr[rows, cols] = wt_scr[s, pl.ds(0, n_t // 2, stride=2), :].astype(BF16)
                wl_scr[rows, cols] = wt_scr[s, pl.ds(1, n_t // 2, stride=2), :].astype(BF16)
        w2_scr[...] = w2_ref[0].astype(BF16)

        def one_block(blk, carry):
            slot = blk % 2

            @pl.when(blk + 1 < n_used)
            def _():
                x_copy(blk + 1, 1 - slot).start()

            x_copy(blk, slot).wait()

            @pl.when(blk >= 2)
            def _():
                o_copy(blk - 2, slot).wait()

            xb = _load_token_tiles(xbuf.at[slot], 0, bm).astype(BF16)
            bias = b1_ref[0]
            half = dh // 2
            acts = []
            for q in range(2):
                hs = slice(q * half, (q + 1) * half)
                glu = jnp.minimum(_dot_nt(xb, wg_scr[hs, :]) + bias[:, hs], SWIGLU_LIMIT)
                lin = jnp.clip(_dot_nt(xb, wl_scr[hs, :]) + bias[:, dh + q * half:dh + (q + 1) * half],
                               -SWIGLU_LIMIT, SWIGLU_LIMIT)
                acts.append((glu * jax.nn.sigmoid(SWIGLU_ALPHA * glu) * (lin + 1.0)).astype(BF16))
            y = _dot(acts[0], w2_scr[:half, :]) + _dot(acts[1], w2_scr[half:, :]) + b2_ref[0]
            _store_token_tiles(obuf.at[slot], y)
            o_copy(blk, slot).start()
            return carry

        lax.fori_loop(blk_lo, blk_hi, one_block, 0)

    @pl.when(e == pl.num_programs(0) - 1)
    def _():
        @pl.when(n_used >= 1)
        def _():
            o_copy(0, (n_used - 1) % 2).wait()

        @pl.when(n_used >= 2)
        def _():
            o_copy(0, n_used % 2).wait()

        obuf[0] = jnp.zeros(obuf.shape[1:], obuf.dtype)
        n_blocks = yout_ref.shape[0] // rows_blk

        def start_tail(blk, carry):
            o_copy(blk, 0).start()
            return carry

        def wait_tail(blk, carry):
            o_copy(0, 0).wait()
            return carry

        lax.fori_loop(n_used, n_blocks, start_tail, 0)
        lax.fori_loop(n_used, n_blocks, wait_tail, 0)


def _experts(pad_end, n_used, xin, w1, b1p, w2, b2):
    ne, d, dh2 = w1.shape
    dh = dh2 // 2
    rows_blk = BM_MOE * SUBLANES
    grid_spec = pltpu.PrefetchScalarGridSpec(
        num_scalar_prefetch=2,
        grid=(ne,),
        in_specs=[
            pl.BlockSpec((1, d, dh2), lambda e, pe, nu: (e, 0, 0)),
            pl.BlockSpec((1, 1, dh2), lambda e, pe, nu: (e, 0, 0)),
            pl.BlockSpec((1, dh, d), lambda e, pe, nu: (e, 0, 0)),
            pl.BlockSpec((1, 1, d), lambda e, pe, nu: (e, 0, 0)),
            pl.BlockSpec(memory_space=pl.ANY),
        ],
        out_specs=pl.BlockSpec(memory_space=pl.ANY),
        scratch_shapes=[
            pltpu.VMEM((2, rows_blk, LANES), F32),
            pltpu.VMEM((2, rows_blk, LANES), F32),
            pltpu.VMEM((d // 128, dh, 128), F32),
            pltpu.VMEM((dh, d), BF16),
            pltpu.VMEM((dh, d), BF16),
            pltpu.VMEM((dh, d), BF16),
            pltpu.SemaphoreType.DMA((2,)),
            pltpu.SemaphoreType.DMA((2,)),
        ],
    )
    return pl.pallas_call(
        _experts_kernel,
        grid_spec=grid_spec,
        out_shape=jax.ShapeDtypeStruct(xin.shape, F32),
        compiler_params=_params(("arbitrary",)),
        name="experts",
    )(pad_end, n_used, w1, b1p, w2, b2, xin)


def _combine_kernel(*refs):
    rstart_ref, rlen_ref, rstart_next_ref, rlen_next_ref = refs[:4]
    li_refs = refs[4:4 + TOP_K]
    gate_refs = refs[4 + TOP_K:4 + 2 * TOP_K]
    h1_ref, fg_ref, yout_ref, o_ref, wbuf, hacc, sem = refs[4 + 2 * TOP_K:]
    tc = o_ref.shape[0]
    i = pl.program_id(0)
    slot = i % 2
    n_experts = rstart_ref.shape[0]
    win_rows = WIN_COMBINE * SUBLANES

    def tile(ref, token):
        return ref.at[pl.ds(pl.multiple_of(token * SUBLANES, SUBLANES), SUBLANES)]

    def windows_of(rl_ref, e):
        return (rl_ref[e, 0] + (WIN_COMBINE - 1)) // WIN_COMBINE

    def start_windows(rs_ref, rl_ref, s):
        def per_expert(e, base):
            start = rs_ref[e, 0]
            n_win = windows_of(rl_ref, e)

            def per_window(w, carry):
                src = yout_ref.at[pl.ds(pl.multiple_of((start + w * WIN_COMBINE) * SUBLANES, SUBLANES),
                                        win_rows)]
                dst = wbuf.at[s, pl.ds(pl.multiple_of((base + w) * win_rows, win_rows), win_rows)]
                pltpu.make_async_copy(src, dst, sem.at[s]).start()
                return carry

            lax.fori_loop(0, n_win, per_window, 0)
            return base + n_win

        lax.fori_loop(0, n_experts, per_expert, 0)

    @pl.when(i == 0)
    def _():
        start_windows(rstart_ref, rlen_ref, 0)

    @pl.when(i + 1 < pl.num_programs(0))
    def _():
        start_windows(rstart_next_ref, rlen_next_ref, 1 - slot)

    n_windows = lax.fori_loop(0, n_experts, lambda e, n: n + windows_of(rlen_ref, e), 0)

    def wait_window(w, carry):
        pltpu.make_async_copy(yout_ref.at[pl.ds(0, win_rows)], wbuf.at[slot, pl.ds(0, win_rows)],
                              sem.at[slot]).wait()
        return carry

    lax.fori_loop(0, n_windows, wait_window, 0)

    def window_tile(j, tk):
        return wbuf[slot, pl.ds(pl.multiple_of(li_refs[j][tk], SUBLANES), SUBLANES), :]

    def per_token(tk, carry):
        acc = gate_refs[0][tk] * window_tile(0, tk)
        for j in range(1, TOP_K):
            acc = acc + gate_refs[j][tk] * window_tile(j, tk)
        tile(hacc, tk)[...] = acc
        return carry

    lax.fori_loop(0, tc, per_token, 0, unroll=32)
    h = h1_ref[...] + _load_token_tiles(hacc, 0, tc)
    o_ref[...] = h * lax.rsqrt(jnp.mean(h * h, axis=-1, keepdims=True) + RMS_EPS) * fg_ref[...]


def _combine(rstart, rlen, li, gates, h1, fg, yout):
    t, d = h1.shape
    tc = TC_COMBINE
    n_steps = t // tc
    n_experts = rstart.shape[0] // n_steps
    buf_rows = TOP_K * tc + n_experts * WIN_COMBINE
    runs = pl.BlockSpec((n_experts, 128), lambda i: (i, 0), memory_space=pltpu.SMEM)
    runs_next = pl.BlockSpec((n_experts, 128), lambda i: (jnp.minimum(i + 1, n_steps - 1), 0),
                             memory_space=pltpu.SMEM)
    return pl.pallas_call(
        _combine_kernel,
        grid=(n_steps,),
        in_specs=[runs, runs, runs_next, runs_next]
        + _choice_specs(n_steps, tc) + _choice_specs(n_steps, tc) + [
            pl.BlockSpec((tc, d), lambda i: (i, 0)),
            pl.BlockSpec((1, d), lambda i: (0, 0)),
            pl.BlockSpec(memory_space=pl.ANY),
        ],
        out_specs=pl.BlockSpec((tc, d), lambda i: (i, 0)),
        out_shape=jax.ShapeDtypeStruct((t, d), F32),
        scratch_shapes=[pltpu.VMEM((2, buf_rows * SUBLANES, LANES), F32),
                        pltpu.VMEM((tc * SUBLANES, LANES), F32),
                        pltpu.SemaphoreType.DMA((2,))],
        compiler_params=_params(("arbitrary",)),
        name="combine",
    )(rstart, rlen, rstart, rlen, *([li.reshape(-1)] * TOP_K), *([gates.reshape(-1)] * TOP_K),
      h1, fg, yout)


def kernel(x, norm1_g, w_in, sgu_ln_g, sgu_ln_b, sgu_w, sgu_b, mu_rkv, mu_wag, decay_w0, decay_w1,
           decay_w2, iclr_a0, iclr_a1, iclr_a2, gate_g1, gate_g2, k_k, k_a, r_k, lnx_g, lnx_b, w_out,
           norm2_g, router_w, router_b, moe_w1, moe_b1, moe_w2, moe_b2, final_g):
    batch, seq, d = x.shape
    t = batch * seq
    depth = w_in.shape[0]
    sgu_width = sgu_ln_g.shape[1]
    rw = mu_rkv.shape[2]
    n_dec, n_icl, n_gate = decay_w1.shape[2], iclr_a1.shape[2], gate_g1.shape[2]
    assert n_dec == HEAD and n_icl == HEAD and n_gate == PAIR and rw % PAIR == 0
    assert seq % TT_RWKV == 0 and seq % TT_PREP == 0 and t % TM_PROJ == 0
    assert sgu_w.shape[2] == SGU_BLOCK and TM_PROJ % SGU_BLOCK == 0
    assert t % TR_ROUTE == 0 and t % TD_DISPATCH == 0 and TR_ROUTE % TC_COMBINE == 0
    assert (t * TOP_K) % BM_MOE == 0 and router_w.shape[2] == N_EXPERTS
    assert depth == 1, "the final RMSNorm is fused into the last layer's combine kernel"
    assert d == SUBLANES * LANES, "the MoE row movers copy one (8, 128) f32 tile per token"
    assert BM_MOE + N_EXPERTS - 1 >= WIN_COMBINE

    h = x.reshape(t, d)
    for l in range(depth):
        win_bf = w_in[l].astype(BF16)
        wl = jnp.concatenate([decay_w1[l], iclr_a1[l], gate_g1[l]], axis=1)
        mucat = jnp.concatenate([
            jnp.broadcast_to(mu_wag[l, 0][:, None], (d, n_dec)),
            jnp.broadcast_to(mu_wag[l, 1][:, None], (d, n_icl)),
            jnp.broadcast_to(mu_wag[l, 2][:, None], (d, n_gate))], axis=1)
        zeros = jnp.zeros((HEAD, rw), F32)
        w2pad = jnp.concatenate([decay_w2[l], zeros], axis=0)
        a2pad = jnp.concatenate([zeros, iclr_a2[l]], axis=0)
        pvec = jnp.stack([decay_w0[l], iclr_a0[l], k_k[l], k_a[l], r_k[l].reshape(-1),
                          lnx_g[l], lnx_b[l], jnp.zeros((rw,), F32)], axis=0)
        bias2d = jnp.repeat(sgu_b[l].T, sgu_width // sgu_b.shape[1], axis=1)
        wo_bf = w_out[l].astype(BF16)
        b1p = jnp.concatenate([moe_b1[l][:, 0::2], moe_b1[l][:, 1::2]], axis=-1)[:, None, :]
        b2 = moe_b2[l][:, None, :]

        assert w_in.shape[2] == 2 * sgu_width + 3 * rw
        proj, lora, ya = _in_proj(h, norm1_g[l][None, :], win_bf, wl, mucat, sgu_ln_g[l][None, :],
                                  sgu_ln_b[l][None, :], sgu_w[l], bias2d)
        rp, yq, bonus, gate, g_m, h_m = _rwkv_prep(proj, lora, mu_rkv[l], pvec, w2pad, a2pad,
                                                   gate_g2[l], seq, 0)
        yb = _rwkv_scan(rp, yq, bonus, gate, g_m, h_m, pvec, batch)
        h1, hn2, logits_t = _out_proj(h, ya, yb, wo_bf, norm2_g[l][None, :], router_w[l].T,
                                      router_b[l][:, None])
        pos, gates, pend, li, rstart, rlen = _route(logits_t)

        pad_end = pend[:, 0]
        n_blocks = (t * TOP_K) // BM_MOE + N_EXPERTS
        n_used = pad_end[-1:] // BM_MOE

        xin = _dispatch(pad_end, n_used, pos, hn2, n_blocks * BM_MOE)
        yout = _experts(pad_end, n_used, xin, moe_w1[l], b1p, moe_w2[l], b2)
        h = _combine(rstart, rlen, li, gates, h1, final_g[None, :], yout)
    return h.reshape(batch, seq, d)
```
